```python
import math
import jax, jax.numpy as jnp
from jax import lax
import numpy as np

D_MODEL = 2048
BATCH = 4
SEQ = 2048
DEPTH = 4

RET_HEADS = 4
RET_DIM = 128
FOX_HEADS = 4
FOX_DIM = 128
DSA_HEADS = 4
DSA_DIM = 128
DSA_Q_RANK = 512
IDX_HEADS = 16
IDX_DIM = 64
DSA_TOPK = 256
SSD_HEADS = 16
SSD_HEAD_DIM = 64
SSD_GROUPS = 2
SSD_STATE = 128
SSD_CONV = 4
SSD_INNER = SSD_HEADS * SSD_HEAD_DIM
SSD_XBC = SSD_INNER + 2 * SSD_GROUPS * SSD_STATE
D_FF = 4 * D_MODEL
N_BUCKETS = 32
MAX_DISTANCE = 128
Q_BLOCK = 128
CHUNK = 128
EPS = 1e-6
N_BRANCH = 4
BRANCH_WIDTHS = (RET_HEADS * RET_DIM, FOX_HEADS * FOX_DIM, DSA_HEADS * DSA_DIM, SSD_INNER)
IN_SIZES = (
    RET_HEADS * RET_DIM, RET_HEADS * RET_DIM, RET_HEADS * RET_DIM, RET_HEADS * RET_DIM,
    FOX_HEADS * FOX_DIM, FOX_HEADS * FOX_DIM, FOX_HEADS * FOX_DIM, FOX_HEADS,
    DSA_Q_RANK, DSA_DIM, DSA_DIM, IDX_DIM, IDX_HEADS,
    SSD_INNER, SSD_XBC, SSD_HEADS,
    N_BRANCH * D_MODEL,
)
IN_TOTAL = sum(IN_SIZES)
MIX_TOTAL = sum(BRANCH_WIDTHS)

kernel_name = "hybrid_gated_parallel_mixers"


def rms_norm(x, g):
    xf = x.astype(jnp.float32)
    y = xf * lax.rsqrt(jnp.mean(xf * xf, axis=-1, keepdims=True) + EPS)
    return (y * g.astype(jnp.float32)).astype(x.dtype)


def split_axis(t, sizes, axis):
    return jnp.split(t, np.cumsum(sizes)[:-1].tolist(), axis=axis)


def heads(t, n):
    return t.reshape(t.shape[0], t.shape[1], n, -1)


def rotary(x, pos):
    half = x.shape[-1] // 2
    inv = 1.0 / (10000.0 ** (jnp.arange(half, dtype=jnp.float32) / half))
    ang = pos.astype(jnp.float32)[:, None] * inv[None, :]
    cos, sin = jnp.cos(ang)[:, None, :], jnp.sin(ang)[:, None, :]
    x1, x2 = x[..., :half], x[..., half:]
    return jnp.concatenate([x1 * cos - x2 * sin, x1 * sin + x2 * cos], axis=-1)


def t5_bucket(dist):
    max_exact = N_BUCKETS // 2
    d = jnp.maximum(dist, 0)
    log_ratio = jnp.log(jnp.maximum(d, 1).astype(jnp.float32) / max_exact) / math.log(MAX_DISTANCE / max_exact)
    large = jnp.minimum(max_exact + (log_ratio * (N_BUCKETS - max_exact)).astype(jnp.int32), N_BUCKETS - 1)
    return jnp.where(d < max_exact, d, large)


def retention(q, k, v, g):
    B, S, H, Dh = q.shape
    n = S // CHUNK
    pos = jnp.arange(S)
    q = rotary(q.astype(jnp.float32), pos)
    k = rotary(k.astype(jnp.float32), pos) * (Dh ** -0.5)
    v = v.astype(jnp.float32)
    log_gamma = jnp.log1p(-jnp.exp2(-5.0 - jnp.arange(H, dtype=jnp.float32)))
    qc = q.reshape(B, n, CHUNK, H, Dh)
    kc = k.reshape(B, n, CHUNK, H, Dh)
    vc = v.reshape(B, n, CHUNK, H, Dh)
    i = jnp.arange(CHUNK, dtype=jnp.float32)
    rel = i[:, None] - i[None, :]
    decay = jnp.where(rel >= 0, jnp.exp(log_gamma[:, None, None] * jnp.maximum(rel, 0.0)), 0.0)
    scores = jnp.einsum('bnihd,bnjhd->bnhij', qc, kc) * decay
    y_inner = jnp.einsum('bnhij,bnjhd->bnihd', scores, vc)
    k_dec = jnp.exp(log_gamma[None, :] * (CHUNK - 1.0 - i)[:, None])
    q_dec = jnp.exp(log_gamma[None, :] * (i + 1.0)[:, None])
    chunk_kv = jnp.einsum('bnjhd,jh,bnjhe->bnhde', kc, k_dec, vc)
    chunk_decay = jnp.exp(log_gamma * CHUNK)[None, :, None, None]

    def step(state, kv):
        return chunk_decay * state + kv, state

    _, prev = lax.scan(step, jnp.zeros((B, H, Dh, Dh), jnp.float32), jnp.moveaxis(chunk_kv, 1, 0))
    prev = jnp.moveaxis(prev, 0, 1)
    y_cross = jnp.einsum('bnihd,ih,bnhde->bnihe', qc, q_dec, prev)
    y = (y_inner + y_cross).reshape(B, S, H, Dh)
    yc = y - jnp.mean(y, axis=-1, keepdims=True)
    y = yc * lax.rsqrt(jnp.mean(yc * yc, axis=-1, keepdims=True) + EPS)
    out = jax.nn.silu(g.astype(jnp.float32)) * y
    return out.reshape(B, S, H * Dh)


def forgetting_attention(q, k, v, f_logit, qn_g, kn_g):
    B, S, H, Dh = q.shape
    nb = S // Q_BLOCK
    q = rms_norm(q.astype(jnp.float32), qn_g) * (Dh ** -0.5)
    k = rms_norm(k.astype(jnp.float32), kn_g)
    v = v.astype(jnp.float32)
    F = jnp.cumsum(jax.nn.log_sigmoid(f_logit.astype(jnp.float32)), axis=1)
    FT = F.transpose(0, 2, 1)
    qb = q.reshape(B, nb, Q_BLOCK, H, Dh).transpose(1, 0, 2, 3, 4)
    Fb = F.reshape(B, nb, Q_BLOCK, H).transpose(1, 0, 3, 2)
    tb = jnp.arange(S).reshape(nb, Q_BLOCK)
    s_idx = jnp.arange(S)

    def block(args):
        qi, Fi, ti = args
        logits = jnp.einsum('bqhd,bshd->bhqs', qi, k) + (Fi[..., None] - FT[:, :, None, :])
        logits = jnp.where((s_idx[None, :] <= ti[:, None])[None, None], logits, -jnp.inf)
        p = jax.nn.softmax(logits, axis=-1)
        return jnp.einsum('bhqs,bshd->bqhd', p, v)

    out = lax.map(block, (qb, Fb, tb))
    return out.transpose(1, 0, 2, 3, 4).reshape(B, S, H * Dh)


def sparse_attention(c_q, k, v, idx_k, idx_w, cq_g, w_uq, w_qidx, qn_g, kn_g, rel_bias):
    B, S, _ = c_q.shape
    nb = S // Q_BLOCK
    topk = min(DSA_TOPK, S // 4)
    cq = rms_norm(c_q, cq_g)
    q = rms_norm((cq @ w_uq).reshape(B, S, DSA_HEADS, DSA_DIM).astype(jnp.float32), qn_g) * (DSA_DIM ** -0.5)
    q_idx = (cq @ w_qidx).reshape(B, S, IDX_HEADS, IDX_DIM).astype(jnp.float32) * (IDX_DIM ** -0.5)
    w_h = idx_w.astype(jnp.float32) * (IDX_HEADS ** -0.5)
    k_idx = idx_k.astype(jnp.float32)
    k = rms_norm(k.astype(jnp.float32), kn_g)
    v = v.astype(jnp.float32)
    bias_table = rel_bias.astype(jnp.float32)
    qb = q.reshape(B, nb, Q_BLOCK, DSA_HEADS, DSA_DIM).transpose(1, 0, 2, 3, 4)
    qib = q_idx.reshape(B, nb, Q_BLOCK, IDX_HEADS, IDX_DIM).transpose(1, 0, 2, 3, 4)
    wb = w_h.reshape(B, nb, Q_BLOCK, IDX_HEADS).transpose(1, 0, 2, 3)
    tb = jnp.arange(S).reshape(nb, Q_BLOCK)
    s_idx = jnp.arange(S)
    gather = jax.vmap(lambda kb, sb: kb[sb])

    def block(args):
        qi, qidx_i, wi, ti = args
        score = jax.nn.relu(jnp.einsum('bqhd,bsd->bqhs', qidx_i, k_idx))
        score = jnp.einsum('bqh,bqhs->bqs', wi, score)
        score = jnp.where((s_idx[None, :] <= ti[:, None])[None], score, -jnp.inf)
        _, sel = lax.top_k(score, topk)
        valid = sel <= ti[None, :, None]
        k_sel = gather(k, sel)
        v_sel = gather(v, sel)
        bias = bias_table[t5_bucket(ti[None, :, None] - sel)].transpose(0, 3, 1, 2)
        logits = jnp.einsum('bqhd,bqkd->bhqk', qi, k_sel) + bias
        logits = jnp.where(valid[:, None], logits, -jnp.inf)
        p = jax.nn.softmax(logits, axis=-1)
        return jnp.einsum('bhqk,bqkd->bqhd', p, v_sel)

    out = lax.map(block, (qb, qib, wb, tb))
    return out.transpose(1, 0, 2, 3, 4).reshape(B, S, DSA_HEADS * DSA_DIM)


def segsum_exp(a_cs):
    L = a_cs.shape[-1]
    mask = jnp.tril(jnp.ones((L, L), dtype=bool))
    diff = a_cs[..., :, None] - a_cs[..., None, :]
    return jnp.where(mask, jnp.exp(jnp.where(mask, diff, 0.0)), 0.0)


def ssd_mixer(z, xbc, dt_raw, conv_w, conv_b, dt_bias, a_log, d_skip, norm_g):
    B, S, _ = xbc.shape
    G, R, P, N = SSD_GROUPS, SSD_HEADS // SSD_GROUPS, SSD_HEAD_DIM, SSD_STATE
    n = S // CHUNK
    conv = lax.conv_general_dilated(
        xbc.astype(jnp.float32), conv_w.astype(jnp.float32)[:, None, :], window_strides=(1,),
        padding=[(SSD_CONV - 1, 0)], dimension_numbers=('NWC', 'WIO', 'NWC'),
        feature_group_count=SSD_XBC)
    xbc = jax.nn.silu(conv + conv_b.astype(jnp.float32))
    xs, Bm, Cm = split_axis(xbc, (SSD_INNER, G * N, G * N), axis=-1)
    xs = xs.reshape(B, S, G, R, P)
    dt = jax.nn.softplus(dt_raw.astype(jnp.float32) + dt_bias.astype(jnp.float32)).reshape(B, S, G, R)
    A = -jnp.exp(a_log.astype(jnp.float32)).reshape(G, R)
    xc = (xs * dt[..., None]).reshape(B, n, CHUNK, G, R, P)
    Bc = Bm.reshape(B, n, CHUNK, G, N)
    Cc = Cm.reshape(B, n, CHUNK, G, N)
    a = (dt * A).reshape(B, n, CHUNK, G, R).transpose(0, 3, 4, 1, 2)
    a_cs = jnp.cumsum(a, axis=-1)
    cb = jnp.einsum('bnlgk,bnsgk->bgnls', Cc, Bc)
    y_diag = jnp.einsum('bgnls,bgrnls,bnsgrp->bnlgrp', cb, segsum_exp(a_cs), xc)
    decay_states = jnp.exp(a_cs[..., -1:] - a_cs)
    states = jnp.einsum('bnlgk,bgrnl,bnlgrp->bngrpk', Bc, decay_states, xc)
    chunk_decay = jnp.exp(a_cs[..., -1])

    def step(state, inp):
        st, dec = inp
        return dec[..., None, None] * state + st, state

    _, prev = lax.scan(step, jnp.zeros((B, G, R, P, N), jnp.float32),
                       (jnp.moveaxis(states, 1, 0), jnp.moveaxis(chunk_decay, -1, 0)))
    prev = jnp.moveaxis(prev, 0, 1)
    y_off = jnp.einsum('bnlgk,bngrpk,bgrnl->bnlgrp', Cc, prev, jnp.exp(a_cs))
    y = (y_diag + y_off).reshape(B, S, G, R, P) + d_skip.astype(jnp.float32).reshape(G, R)[..., None] * xs
    gated = y.reshape(B, S, SSD_INNER) * jax.nn.silu(z.astype(jnp.float32))
    gated = rms_norm(gated.reshape(B, S, G, SSD_INNER // G), norm_g.reshape(G, SSD_INNER // G))
    return gated.reshape(B, S, SSD_INNER)


def setup_inputs(seed: int = 0) -> dict:
    key = jax.random.key(seed)
    ks = jax.random.split(key, 32)
    f32 = jnp.float32

    def nrm(k, shape, fan_in):
        return jax.random.normal(k, shape, f32) * (fan_in ** -0.5)

    def gain(k, shape):
        return 1.0 + 0.02 * jax.random.normal(k, shape, f32)

    dt0 = jnp.exp(jax.random.uniform(ks[16], (DEPTH, SSD_HEADS), f32, math.log(1e-3), math.log(1e-1)))
    br_keys = jax.random.split(ks[20], N_BRANCH)
    w_br = jnp.concatenate([nrm(br_keys[i], (DEPTH, w, D_MODEL), w) for i, w in enumerate(BRANCH_WIDTHS)], axis=1)
    return {
        "x": jax.random.normal(ks[0], (BATCH, SEQ, D_MODEL), f32),
        "norm1_g": gain(ks[1], (DEPTH, D_MODEL)),
        "w_in": nrm(ks[2], (DEPTH, D_MODEL, IN_TOTAL), D_MODEL),
        "gate_b": 0.02 * jax.random.normal(ks[3], (DEPTH, N_BRANCH * D_MODEL), f32),
        "fox_f_b": jax.random.uniform(ks[4], (DEPTH, FOX_HEADS), f32, 1.0, 5.0),
        "fox_qn_g": gain(ks[5], (DEPTH, FOX_DIM)),
        "fox_kn_g": gain(ks[6], (DEPTH, FOX_DIM)),
        "dsa_cq_g": gain(ks[7], (DEPTH, DSA_Q_RANK)),
        "dsa_w_uq": nrm(ks[8], (DEPTH, DSA_Q_RANK, DSA_HEADS * DSA_DIM), DSA_Q_RANK),
        "dsa_w_qidx": nrm(ks[9], (DEPTH, DSA_Q_RANK, IDX_HEADS * IDX_DIM), DSA_Q_RANK),
        "dsa_qn_g": gain(ks[10], (DEPTH, DSA_DIM)),
        "dsa_kn_g": gain(ks[11], (DEPTH, DSA_DIM)),
        "rel_bias": 0.5 * jax.random.normal(ks[12], (N_BUCKETS, DSA_HEADS), f32),
        "ssd_conv_w": jax.random.uniform(ks[13], (DEPTH, SSD_CONV, SSD_XBC), f32, -1.0, 1.0) * (SSD_CONV ** -0.5),
        "ssd_conv_b": 0.02 * jax.random.normal(ks[14], (DEPTH, SSD_XBC), f32),
        "ssd_dt_bias": dt0 + jnp.log(-jnp.expm1(-dt0)),
        "ssd_a_log": jnp.log(jax.random.uniform(ks[17], (DEPTH, SSD_HEADS), f32, 1.0, 16.0)),
        "ssd_d": 1.0 + 0.1 * jax.random.normal(ks[18], (DEPTH, SSD_HEADS), f32),
        "ssd_norm_g": gain(ks[19], (DEPTH, SSD_INNER)),
        "w_br": w_br,
        "w_out": nrm(ks[21], (DEPTH, D_MODEL, D_MODEL), D_MODEL),
        "norm2_g": gain(ks[22], (DEPTH, D_MODEL)),
        "w_ff1": nrm(ks[23], (DEPTH, D_MODEL, D_FF), D_MODEL),
        "w_ff2": nrm(ks[24], (DEPTH, D_FF, D_MODEL), D_FF),
    }


def reference(x, norm1_g, w_in, gate_b, fox_f_b, fox_qn_g, fox_kn_g, dsa_cq_g, dsa_w_uq, dsa_w_qidx,
              dsa_qn_g, dsa_kn_g, rel_bias, ssd_conv_w, ssd_conv_b, ssd_dt_bias, ssd_a_log, ssd_d,
              ssd_norm_g, w_br, w_out, norm2_g, w_ff1, w_ff2):
    B, S, _ = x.shape
    for l in range(DEPTH):
        h = rms_norm(x, norm1_g[l])
        (r_q, r_k, r_v, r_g, f_q, f_k, f_v, f_f, d_cq, d_k, d_v, i_k, i_w,
         s_z, s_xbc, s_dt, g_lin) = split_axis(h @ w_in[l], IN_SIZES, axis=-1)
        o_ret = retention(heads(r_q, RET_HEADS), heads(r_k, RET_HEADS), heads(r_v, RET_HEADS), heads(r_g, RET_HEADS))
        o_fox = forgetting_attention(heads(f_q, FOX_HEADS), heads(f_k, FOX_HEADS), heads(f_v, FOX_HEADS),
                                     f_f + fox_f_b[l], fox_qn_g[l], fox_kn_g[l])
        o_dsa = sparse_attention(d_cq, d_k, d_v, i_k, i_w, dsa_cq_g[l], dsa_w_uq[l], dsa_w_qidx[l],
                                 dsa_qn_g[l], dsa_kn_g[l], rel_bias)
        o_ssd = ssd_mixer(s_z, s_xbc, s_dt, ssd_conv_w[l], ssd_conv_b[l], ssd_dt_bias[l], ssd_a_log[l],
                          ssd_d[l], ssd_norm_g[l])
        gates = jax.nn.sigmoid((g_lin + gate_b[l]).astype(jnp.float32)).astype(x.dtype)
        gates = gates.reshape(B, S, N_BRANCH, D_MODEL)
        w_branch = split_axis(w_br[l], BRANCH_WIDTHS, axis=0)
        branch_out = (o_ret, o_fox, o_dsa, o_ssd)
        merged = sum(gates[:, :, i] * (branch_out[i].astype(x.dtype) @ w_branch[i]) for i in range(N_BRANCH))
        x = x + merged @ w_out[l]
        h2 = rms_norm(x, norm2_g[l])
        x = x + jnp.square(jax.nn.relu(h2 @ w_ff1[l])) @ w_ff2[l]
    return x
```

```python
import functools
import math

import jax
import jax.numpy as jnp
from jax import lax
from jax.experimental import pallas as pl
from jax.experimental.pallas import tpu as pltpu

F32 = jnp.float32
BF16 = jnp.bfloat16

D_MODEL = 2048
DEPTH = 4
HEAD_DIM = 128
N_HEADS = 4
DSA_Q_RANK = 512
IDX_HEADS = 16
IDX_DIM = 64
DSA_TOPK = 256
SSD_HEADS = 16
SSD_HEAD_DIM = 64
SSD_GROUPS = 2
SSD_STATE = 128
SSD_CONV = 4
SSD_INNER = SSD_HEADS * SSD_HEAD_DIM
D_FF = 4 * D_MODEL
N_BUCKETS = 32
MAX_DISTANCE = 128
CHUNK = 128
EPS = 1e-6
N_BRANCH = 4
MIX_W = N_HEADS * HEAD_DIM

COL_GATE = 0
COL_RET = COL_GATE + N_BRANCH * D_MODEL
COL_Z = COL_RET + 4 * MIX_W
COL_XS = COL_Z + SSD_INNER
COL_BC = COL_XS + SSD_INNER
COL_CQ = COL_BC + 2 * SSD_GROUPS * SSD_STATE
COL_FOX = COL_CQ + DSA_Q_RANK
COL_DK = COL_FOX + 3 * MIX_W
COL_DV = COL_DK + HEAD_DIM
N_MAIN_USED = COL_DV + HEAD_DIM
N_MAIN = 15360
SM_DT = 0
SM_F = 16
SM_IW = 32
SM_IK = 64
SM_W = 128

LANES = 128
VMEM_LIMIT = 56 * 1024 * 1024
NEG_BIG = -1e30


def _cparams(sem):
    return pltpu.CompilerParams(dimension_semantics=sem, vmem_limit_bytes=VMEM_LIMIT)


def _dot(a, b):
    return jnp.dot(a, b, preferred_element_type=F32)


def _dot_nt(a, b):
    return lax.dot_general(a, b, (((1,), (1,)), ((), ())), preferred_element_type=F32)


def _dot_tn(a, b):
    return lax.dot_general(a, b, (((0,), (0,)), ((), ())), preferred_element_type=F32)


def _rms(x, g):
    return x * lax.rsqrt(jnp.mean(x * x, axis=-1, keepdims=True) + EPS) * g


def _silu(x):
    return x / (1.0 + jnp.exp(-x))


def _softplus(x):
    return jnp.maximum(x, 0.0) + jnp.log1p(jnp.exp(-jnp.abs(x)))


def _cumsum_lanes(x):
    lane = lax.broadcasted_iota(jnp.int32, x.shape, 1)
    d = 1
    while d < x.shape[1]:
        x = x + jnp.where(lane >= d, pltpu.roll(x, d, 1), 0.0)
        d *= 2
    return x


def _norm_matmul_kernel(x_ref, g_ref, w_ref, o_ref, h_ref):
    @pl.when(pl.program_id(1) == 0)
    def _():
        h_ref[...] = _rms(x_ref[...], g_ref[...]).astype(BF16)

    o_ref[...] = _dot(h_ref[...], w_ref[...]).astype(o_ref.dtype)


def norm_matmul(x, g, w, out_dtype, tm, tn):
    m, d = x.shape
    n = w.shape[1]
    return pl.pallas_call(
        _norm_matmul_kernel,
        grid=(m // tm, n // tn),
        in_specs=[
            pl.BlockSpec((tm, d), lambda i, j: (i, 0)),
            pl.BlockSpec((1, d), lambda i, j: (0, 0)),
            pl.BlockSpec((d, tn), lambda i, j: (0, j)),
        ],
        out_specs=pl.BlockSpec((tm, tn), lambda i, j: (i, j)),
        out_shape=jax.ShapeDtypeStruct((m, n), out_dtype),
        scratch_shapes=[pltpu.VMEM((tm, d), BF16)],
        compiler_params=_cparams(("parallel", "arbitrary")),
        name="norm_matmul",
    )(x, g, w)


def _retention_kernel(q_ref, k_ref, v_ref, g_ref, cos_ref, sin_ref, o_ref, state_ref):
    c = CHUNK

    @pl.when(pl.program_id(1) == 0)
    def _():
        state_ref[...] = jnp.zeros_like(state_ref)

    cos = cos_ref[...]
    sin = sin_ref[...]
    ii = lax.broadcasted_iota(jnp.int32, (c, c), 0)
    jj = lax.broadcasted_iota(jnp.int32, (c, c), 1)
    rel = (ii - jj).astype(F32)
    i_col = lax.broadcasted_iota(jnp.int32, (c, 1), 0).astype(F32)
    for h in range(N_HEADS):
        lg = math.log1p(-(2.0 ** (-5.0 - h)))
        sl = slice(h * HEAD_DIM, (h + 1) * HEAD_DIM)
        q = q_ref[:, sl].astype(F32)
        k = k_ref[:, sl].astype(F32)
        v = v_ref[:, sl]
        qr = q * cos + pltpu.roll(q, HEAD_DIM // 2, 1) * sin
        kr = (k * cos + pltpu.roll(k, HEAD_DIM // 2, 1) * sin) * (HEAD_DIM ** -0.5)
        decay = jnp.where(rel >= 0, jnp.exp(lg * jnp.maximum(rel, 0.0)), 0.0)
        scores = _dot_nt(qr.astype(BF16), kr.astype(BF16)) * decay
        y = _dot(scores.astype(BF16), v)
        q_dec = jnp.exp(lg * (i_col + 1.0))
        k_dec = jnp.exp(lg * (c - 1.0 - i_col))
        st = state_ref[h]
        y = y + _dot((qr * q_dec).astype(BF16), st.astype(BF16))
        kv = _dot_tn((kr * k_dec).astype(BF16), v)
        state_ref[h] = math.exp(lg * c) * st + kv
        yc = y - jnp.mean(y, axis=-1, keepdims=True)
        yn = yc * lax.rsqrt(jnp.mean(yc * yc, axis=-1, keepdims=True) + EPS)
        o_ref[:, sl] = (_silu(g_ref[:, sl].astype(F32)) * yn).astype(o_ref.dtype)


def retention(p, cos, sin, batch, seq):
    n = seq // CHUNK
    base = COL_RET // MIX_W

    def col(j):
        return pl.BlockSpec((CHUNK, MIX_W), lambda b, i: (b * n + i, base + j))

    tab = pl.BlockSpec((CHUNK, HEAD_DIM), lambda b, i: (i, 0))
    return pl.pallas_call(
        _retention_kernel,
        grid=(batch, n),
        in_specs=[col(0), col(1), col(2), col(3), tab, tab],
        out_specs=pl.BlockSpec((CHUNK, MIX_W), lambda b, i: (b * n + i, 0)),
        out_shape=jax.ShapeDtypeStruct((batch * seq, MIX_W), BF16),
        scratch_shapes=[pltpu.VMEM((N_HEADS, HEAD_DIM, HEAD_DIM), F32)],
        compiler_params=_cparams(("parallel", "arbitrary")),
        name="retention",
    )(p, p, p, p, cos, sin)


def _fox_prep_kernel(sm_ref, fb_ref, fcol_ref, frow_ref, carry_ref):
    @pl.when(pl.program_id(1) == 0)
    def _():
        carry_ref[...] = jnp.zeros_like(carry_ref)

    t = sm_ref[...] + fb_ref[...]
    lf = jnp.minimum(t, 0.0) - jnp.log1p(jnp.exp(-jnp.abs(t)))
    cs = _cumsum_lanes(lf.T) + carry_ref[...]
    carry_ref[...] = cs[:, LANES - 1:LANES]
    frow_ref[0, 0] = cs[SM_F:SM_F + 8, :]
    fcol_ref[...] = cs.T


def fox_prep(sm, fb_row, batch, seq):
    n = seq // CHUNK
    return pl.pallas_call(
        _fox_prep_kernel,
        grid=(batch, n),
        in_specs=[
            pl.BlockSpec((CHUNK, SM_W), lambda b, i: (b * n + i, 0)),
            pl.BlockSpec((1, SM_W), lambda b, i: (0, 0)),
        ],
        out_specs=[
            pl.BlockSpec((CHUNK, SM_W), lambda b, i: (b * n + i, 0)),
            pl.BlockSpec((1, 1, 8, CHUNK), lambda b, i: (b, i, 0, 0)),
        ],
        out_shape=[
            jax.ShapeDtypeStruct((batch * seq, SM_W), F32),
            jax.ShapeDtypeStruct((batch, n, 8, CHUNK), F32),
        ],
        scratch_shapes=[pltpu.VMEM((SM_W, 1), F32)],
        compiler_params=_cparams(("parallel", "arbitrary")),
        name="fox_prep",
    )(sm, fb_row)


FOX_TQ = 256
FOX_TK = 128


def _fox_kernel(q_ref, k_ref, v_ref, fcol_ref, frow_ref, qg_ref, kg_ref, o_ref, kn_ref):
    i = pl.program_id(1)
    tq, tk = FOX_TQ, FOX_TK

    @pl.when(i == 0)
    def _():
        for h in range(N_HEADS):
            sl = slice(h * HEAD_DIM, (h + 1) * HEAD_DIM)
            kn_ref[:, sl] = _rms(k_ref[:, sl].astype(F32), kg_ref[...]).astype(BF16)

    row = lax.broadcasted_iota(jnp.int32, (tq, tk), 0) + i * tq
    col = lax.broadcasted_iota(jnp.int32, (tq, tk), 1)
    n_kv = (i + 1) * (tq // tk)
    for h in range(N_HEADS):
        sl = slice(h * HEAD_DIM, (h + 1) * HEAD_DIM)
        qn = (_rms(q_ref[:, sl].astype(F32), qg_ref[...]) * (HEAD_DIM ** -0.5)).astype(BF16)
        fq = fcol_ref[:, SM_F + h:SM_F + h + 1]

        def body(j, carry, sl=sl, qn=qn, fq=fq, h=h):
            m, l, acc = carry
            start = pl.multiple_of(j * tk, tk)
            ks = kn_ref[pl.ds(start, tk), sl]
            vs = v_ref[pl.ds(start, tk), sl]
            fk = frow_ref[0, j, h:h + 1, :]
            s = _dot_nt(qn, ks) + (fq - fk)
            s = jnp.where(col + j * tk <= row, s, NEG_BIG)
            m_new = jnp.maximum(m, jnp.max(s, axis=1, keepdims=True))
            p = jnp.exp(s - m_new)
            alpha = jnp.exp(m - m_new)
            l = alpha * l + jnp.sum(p, axis=1, keepdims=True)
            acc = alpha * acc + _dot(p.astype(BF16), vs)
            return m_new, l, acc

        init = (jnp.full((tq, 1), NEG_BIG, F32), jnp.zeros((tq, 1), F32), jnp.zeros((tq, HEAD_DIM), F32))
        _, l, acc = lax.fori_loop(0, n_kv, body, init)
        o_ref[:, sl] = (acc / l).astype(o_ref.dtype)


def fox_attention(p, fcol, frow, qg, kg, batch, seq):
    nq = seq // FOX_TQ
    base = COL_FOX // MIX_W
    return pl.pallas_call(
        _fox_kernel,
        grid=(batch, nq),
        in_specs=[
            pl.BlockSpec((FOX_TQ, MIX_W), lambda b, i: (b * nq + i, base)),
            pl.BlockSpec((seq, MIX_W), lambda b, i: (b, base + 1)),
            pl.BlockSpec((seq, MIX_W), lambda b, i: (b, base + 2)),
            pl.BlockSpec((FOX_TQ, SM_W), lambda b, i: (b * nq + i, 0)),
            pl.BlockSpec((1, seq // FOX_TK, 8, FOX_TK), lambda b, i: (b, 0, 0, 0)),
            pl.BlockSpec((1, HEAD_DIM), lambda b, i: (0, 0)),
            pl.BlockSpec((1, HEAD_DIM), lambda b, i: (0, 0)),
        ],
        out_specs=pl.BlockSpec((FOX_TQ, MIX_W), lambda b, i: (b * nq + i, 0)),
        out_shape=jax.ShapeDtypeStruct((batch * seq, MIX_W), BF16),
        scratch_shapes=[pltpu.VMEM((seq, MIX_W), BF16)],
        compiler_params=_cparams(("parallel", "arbitrary")),
        name="fox_attention",
    )(p, p, p, fcol, frow, qg, kg)


DSA_TQ = 128
DSA_TK = 128
BAND_W = 2 * DSA_TK


def _t5_bucket(dist):
    max_exact = N_BUCKETS // 2
    d = jnp.maximum(dist, 0)
    log_ratio = jnp.log(jnp.maximum(d, 1).astype(F32) / max_exact) / math.log(MAX_DISTANCE / max_exact)
    large = jnp.minimum(max_exact + (log_ratio * (N_BUCKETS - max_exact)).astype(jnp.int32), N_BUCKETS - 1)
    return jnp.where(d < max_exact, d, large)


def _dsa_kernel(cq_ref, k_ref, v_ref, smq_ref, smk_ref, cqg_ref, wuq_ref, wqi_ref, qg_ref, kg_ref, rb_ref,
                o_ref, kn_ref, ki_ref, band_ref, sc_ref, qh_ref, *, topk):
    b = pl.program_id(0)
    i = pl.program_id(1)
    tq, tk = DSA_TQ, DSA_TK
    nkb = sc_ref.shape[0]

    @pl.when(jnp.logical_and(b == 0, i == 0))
    def _():
        r = lax.broadcasted_iota(jnp.int32, (tq, BAND_W), 0)
        c = lax.broadcasted_iota(jnp.int32, (tq, BAND_W), 1)
        bucket = _t5_bucket(tk + r - c)
        for h in range(N_HEADS):
            far = rb_ref[N_BUCKETS - 1, h]
            acc = jnp.zeros((tq, BAND_W), F32)
            for bk in range(N_BUCKETS - 1):
                acc = jnp.where(bucket == bk, rb_ref[bk, h] - far, acc)
            band_ref[h] = acc

    @pl.when(i == 0)
    def _():
        kn_ref[...] = _rms(k_ref[...].astype(F32), kg_ref[...]).astype(BF16)
        ki_ref[...] = smk_ref[:, SM_IK:SM_IK + IDX_DIM].astype(BF16)

    cq = _rms(cq_ref[...].astype(F32), cqg_ref[...]).astype(BF16)
    qf = _dot(cq, wuq_ref[...])
    for h in range(N_HEADS):
        sl = slice(h * HEAD_DIM, (h + 1) * HEAD_DIM)
        qh_ref[h] = (_rms(qf[:, sl], qg_ref[...]) * (HEAD_DIM ** -0.5)).astype(BF16)
    q_idx = (_dot(cq, wqi_ref[...]) * (IDX_DIM ** -0.5)).astype(BF16)
    w_h = smq_ref[:, SM_IW:SM_IW + IDX_HEADS] * (IDX_HEADS ** -0.5)

    row = lax.broadcasted_iota(jnp.int32, (tq, tk), 0)
    col = lax.broadcasted_iota(jnp.int32, (tq, tk), 1)

    def score_body(j, _):
        start = pl.multiple_of(j * tk, tk)
        kj = ki_ref[pl.ds(start, tk), :]
        acc = jnp.zeros((tq, tk), F32)
        for h in range(IDX_HEADS):
            r = _dot_nt(q_idx[:, h * IDX_DIM:(h + 1) * IDX_DIM], kj)
            acc = acc + w_h[:, h:h + 1] * jnp.maximum(r, 0.0)
        sc_ref[j] = jnp.where(jnp.logical_or(j < i, col <= row), acc, -jnp.inf)
        return 0

    lax.fori_loop(0, i + 1, score_body, 0)

    def fill_body(j, _):
        sc_ref[j] = jnp.full((tq, tk), -jnp.inf, F32)
        return 0

    lax.fori_loop(i + 1, nkb, fill_body, 0)

    def search():
        s = sc_ref[...]
        kf = float(topk)
        smax = jnp.max(jnp.max(s, axis=0), axis=1, keepdims=True)
        smin = jnp.min(jnp.min(jnp.where(s == -jnp.inf, jnp.inf, s), axis=0), axis=1, keepdims=True)

        def count_ge(t):
            return jnp.sum(jnp.sum(jnp.where(s >= t[None], 1.0, 0.0), axis=0), axis=1, keepdims=True)

        def midpoint(lo, hi):
            return jnp.where(hi == jnp.inf, smax, 0.5 * (lo + hi))

        def undecided(lo, hi, c_lo, mid):
            return jnp.logical_and(c_lo != kf, jnp.logical_and(mid > lo, mid < hi))

        def cond(carry):
            return jnp.logical_and(carry[0] < 400, carry[1] > 0.0)

        def body(carry):
            it, _, lo, hi, c_lo, c_hi, mid = carry
            upd = undecided(lo, hi, c_lo, mid)
            cnt = count_ge(mid)
            up = jnp.logical_and(upd, cnt >= kf)
            dn = jnp.logical_and(upd, cnt < kf)
            lo = jnp.where(up, mid, lo)
            c_lo = jnp.where(up, cnt, c_lo)
            hi = jnp.where(dn, mid, hi)
            c_hi = jnp.where(dn, cnt, c_hi)
            mid = midpoint(lo, hi)
            active = jnp.max(jnp.where(undecided(lo, hi, c_lo, mid), 1.0, 0.0))
            return it + 1, active, lo, hi, c_lo, c_hi, mid

        lo0 = smin
        hi0 = jnp.full((tq, 1), jnp.inf, F32)
        c_lo0 = count_ge(lo0)
        c_hi0 = jnp.zeros((tq, 1), F32)
        mid0 = midpoint(lo0, hi0)
        act0 = jnp.max(jnp.where(undecided(lo0, hi0, c_lo0, mid0), 1.0, 0.0))
        _, _, lo, hi, c_lo, c_hi, _ = lax.while_loop(
            cond, body, (jnp.int32(0), act0, lo0, hi0, c_lo0, c_hi0, mid0))

        kidx = (lax.broadcasted_iota(jnp.int32, s.shape, 0) * tk
                + lax.broadcasted_iota(jnp.int32, s.shape, 2))
        tie = jnp.logical_and(s >= lo[None], s < hi[None])
        need = kf - c_hi

        def tie_body(_, carry):
            jlo, jhi = carry
            jm = (jlo + jhi) // 2
            cnt = jnp.sum(jnp.sum(jnp.where(jnp.logical_and(tie, kidx <= jm[None]), 1.0, 0.0), axis=0),
                          axis=1, keepdims=True)
            ok = cnt >= need
            return jnp.where(ok, jlo, jm), jnp.where(ok, jm, jhi)

        n_bits = int(math.ceil(math.log2(nkb * tk))) + 1
        _, jmax = lax.fori_loop(0, n_bits, tie_body,
                                (jnp.full((tq, 1), -1, jnp.int32), jnp.full((tq, 1), nkb * tk - 1, jnp.int32)))
        return lo, hi, jmax

    def keep_all():
        return (jnp.full((tq, 1), -jnp.inf, F32), jnp.full((tq, 1), jnp.inf, F32),
                jnp.full((tq, 1), nkb * tk - 1, jnp.int32))

    lo, hi, jmax = lax.cond((i + 1) * tq > topk, search, keep_all)

    for h in range(N_HEADS):
        sl = slice(h * HEAD_DIM, (h + 1) * HEAD_DIM)
        qh = qh_ref[h]
        band_prev = band_ref[h, :, 0:tk]
        band_diag = band_ref[h, :, tk:2 * tk]

        def body(j, carry, qh=qh, band_prev=band_prev, band_diag=band_diag):
            m, l, acc = carry
            start = pl.multiple_of(j * tk, tk)
            ks = kn_ref[pl.ds(start, tk), :]
            vs = v_ref[pl.ds(start, tk), :]
            sc = sc_ref[j]
            bias = jnp.where(j == i, band_diag, jnp.where(j == i - 1, band_prev, 0.0))
            s = _dot_nt(qh, ks) + bias
            keep = jnp.logical_or(sc >= hi, jnp.logical_and(sc >= lo, col + j * tk <= jmax))
            keep = jnp.logical_and(keep, jnp.logical_or(j < i, col <= row))
            s = jnp.where(keep, s, NEG_BIG)
            m_new = jnp.maximum(m, jnp.max(s, axis=1, keepdims=True))
            p = jnp.where(keep, jnp.exp(s - m_new), 0.0)
            alpha = jnp.exp(m - m_new)
            l = alpha * l + jnp.sum(p, axis=1, keepdims=True)
            acc = alpha * acc + _dot(p.astype(BF16), vs)
            return m_new, l, acc

        init = (jnp.full((tq, 1), NEG_BIG, F32), jnp.zeros((tq, 1), F32), jnp.zeros((tq, HEAD_DIM), F32))
        _, l, acc = lax.fori_loop(0, i + 1, body, init)
        o_ref[:, sl] = (acc / l).astype(o_ref.dtype)


def dsa_attention(p, sm, cqg, wuq, wqi, qg, kg, rel_bias, batch, seq):
    nq = seq // DSA_TQ
    nkb = seq // DSA_TK
    topk = min(DSA_TOPK, seq // 4)
    kern = functools.partial(_dsa_kernel, topk=topk)
    return pl.pallas_call(
        kern,
        grid=(batch, nq),
        in_specs=[
            pl.BlockSpec((DSA_TQ, DSA_Q_RANK), lambda b, i: (b * nq + i, COL_CQ // DSA_Q_RANK)),
            pl.BlockSpec((seq, HEAD_DIM), lambda b, i: (b, COL_DK // HEAD_DIM)),
            pl.BlockSpec((seq, HEAD_DIM), lambda b, i: (b, COL_DV // HEAD_DIM)),
            pl.BlockSpec((DSA_TQ, SM_W), lambda b, i: (b * nq + i, 0)),
            pl.BlockSpec((seq, SM_W), lambda b, i: (b, 0)),
            pl.BlockSpec((1, DSA_Q_RANK), lambda b, i: (0, 0)),
            pl.BlockSpec((DSA_Q_RANK, N_HEADS * HEAD_DIM), lambda b, i: (0, 0)),
            pl.BlockSpec((DSA_Q_RANK, IDX_HEADS * IDX_DIM), lambda b, i: (0, 0)),
            pl.BlockSpec((1, HEAD_DIM), lambda b, i: (0, 0)),
            pl.BlockSpec((1, HEAD_DIM), lambda b, i: (0, 0)),
            pl.BlockSpec(memory_space=pltpu.SMEM),
        ],
        out_specs=pl.BlockSpec((DSA_TQ, MIX_W), lambda b, i: (b * nq + i, 0)),
        out_shape=jax.ShapeDtypeStruct((batch * seq, MIX_W), BF16),
        scratch_shapes=[
            pltpu.VMEM((seq, HEAD_DIM), BF16),
            pltpu.VMEM((seq, IDX_DIM), BF16),
            pltpu.VMEM((N_HEADS, DSA_TQ, BAND_W), F32),
            pltpu.VMEM((nkb, DSA_TQ, DSA_TK), F32),
            pltpu.VMEM((N_HEADS, DSA_TQ, HEAD_DIM), BF16),
        ],
        compiler_params=_cparams(("arbitrary", "arbitrary")),
        name="dsa_attention",
    )(p, p, p, sm, sm, cqg, wuq, wqi, qg, kg, rel_bias)


def _causal_conv(x, xp, w_ref, b_ref):
    rows = lax.broadcasted_iota(jnp.int32, x.shape, 0)
    acc = x * w_ref[SSD_CONV - 1:SSD_CONV, :] + b_ref[...]
    for d in range(1, SSD_CONV):
        shifted = jnp.where(rows < d, pltpu.roll(xp, d, 0), pltpu.roll(x, d, 0))
        acc = acc + shifted * w_ref[SSD_CONV - 1 - d:SSD_CONV - d, :]
    return _silu(acc)


def _ssd_kernel(z_ref, xs_ref, bc_ref, xsp_ref, bcp_ref, sm_ref, cwx_ref, cbx_ref, cwb_ref, cbb_ref,
                dtb_ref, alog_ref, dsk_ref, ng_ref, o_ref, prev_ref, y_ref):
    c = CHUNK
    n = pl.program_id(1)

    @pl.when(n == 0)
    def _():
        prev_ref[...] = jnp.zeros_like(prev_ref)

    first = (n > 0).astype(F32)
    xs = _causal_conv(xs_ref[...].astype(F32), xsp_ref[...].astype(F32) * first, cwx_ref, cbx_ref)
    bc = _causal_conv(bc_ref[...].astype(F32), bcp_ref[...].astype(F32) * first, cwb_ref, cbb_ref)

    dt_t = _softplus(sm_ref[...].T + dtb_ref[...])
    cs_t = _cumsum_lanes(dt_t * (-jnp.exp(alog_ref[...])))
    cs = cs_t.T
    dt = dt_t.T
    ii = lax.broadcasted_iota(jnp.int32, (c, c), 0)
    jj = lax.broadcasted_iota(jnp.int32, (c, c), 1)
    tril = ii >= jj
    gn = SSD_GROUPS * SSD_STATE
    hpg = SSD_HEADS // SSD_GROUPS
    for g in range(SSD_GROUPS):
        bg = bc[:, g * SSD_STATE:(g + 1) * SSD_STATE]
        cg = bc[:, gn + g * SSD_STATE:gn + (g + 1) * SSD_STATE].astype(BF16)
        cb = _dot_nt(cg, bg.astype(BF16))
        y_off = _dot(cg, prev_ref[g].astype(BF16))
        for r in range(hpg):
            h = g * hpg + r
            hs = slice(h * SSD_HEAD_DIM, (h + 1) * SSD_HEAD_DIM)
            rs = slice(r * SSD_HEAD_DIM, (r + 1) * SSD_HEAD_DIM)
            a_col = cs[:, SM_DT + h:SM_DT + h + 1]
            a_row = cs_t[SM_DT + h:SM_DT + h + 1, :]
            last = cs_t[SM_DT + h:SM_DT + h + 1, c - 1:c]
            seg = jnp.where(tril, jnp.exp(jnp.where(tril, a_col - a_row, 0.0)), 0.0)
            xh = xs[:, hs]
            xc = (xh * dt[:, SM_DT + h:SM_DT + h + 1]).astype(BF16)
            y_diag = _dot((cb * seg).astype(BF16), xc)
            st = _dot_tn((bg * jnp.exp(last - a_col)).astype(BF16), xc)
            y_ref[:, hs] = y_diag + y_off[:, rs] * jnp.exp(a_col) + dsk_ref[:, hs] * xh
            prev_ref[g, :, rs] = jnp.exp(last) * prev_ref[g, :, rs] + st
    gated = y_ref[...] * _silu(z_ref[...].astype(F32))
    gw = SSD_INNER // SSD_GROUPS
    for g in range(SSD_GROUPS):
        sl = slice(g * gw, (g + 1) * gw)
        o_ref[:, sl] = _rms(gated[:, sl], ng_ref[:, sl]).astype(o_ref.dtype)


def ssd_mixer(p, sm, cw, cb, dtb_col, alog_col, dskip_row, ng, batch, seq):
    n = seq // CHUNK
    bcw = 2 * SSD_GROUPS * SSD_STATE

    def cur(width, colbase):
        return pl.BlockSpec((CHUNK, width), lambda b, i: (b * n + i, colbase // width))

    def prv(width, colbase):
        return pl.BlockSpec((CHUNK, width), lambda b, i: (b * n + jnp.maximum(i - 1, 0), colbase // width))

    def const(shape):
        return pl.BlockSpec(shape, lambda b, i: (0, 0))

    return pl.pallas_call(
        _ssd_kernel,
        grid=(batch, n),
        in_specs=[
            cur(SSD_INNER, COL_Z), cur(SSD_INNER, COL_XS), cur(bcw, COL_BC),
            prv(SSD_INNER, COL_XS), prv(bcw, COL_BC),
            pl.BlockSpec((CHUNK, SM_W), lambda b, i: (b * n + i, 0)),
            const((SSD_CONV, SSD_INNER)), const((1, SSD_INNER)),
            const((SSD_CONV, bcw)), const((1, bcw)),
            const((SM_W, 1)), const((SM_W, 1)),
            const((1, SSD_INNER)), const((1, SSD_INNER)),
        ],
        out_specs=pl.BlockSpec((CHUNK, SSD_INNER), lambda b, i: (b * n + i, 0)),
        out_shape=jax.ShapeDtypeStruct((batch * seq, SSD_INNER), BF16),
        scratch_shapes=[
            pltpu.VMEM((SSD_GROUPS, SSD_STATE, SSD_INNER // SSD_GROUPS), F32),
            pltpu.VMEM((CHUNK, SSD_INNER), F32),
        ],
        compiler_params=_cparams(("parallel", "arbitrary")),
        name="ssd_mixer",
    )(p, p, p, p, p, sm, cw[:, :SSD_INNER], cb[:, :SSD_INNER], cw[:, SSD_INNER:], cb[:, SSD_INNER:],
      dtb_col, alog_col, dskip_row, ng)


MERGE_TM = 256


def _merge_kernel(x_ref, gl_ref, gb_ref, oret_ref, ofox_ref, odsa_ref, ossd_ref, wbr_ref, wout_ref, o_ref):
    branches = (oret_ref, ofox_ref, odsa_ref, ossd_ref)
    merged = None
    row0 = 0
    for bi, br in enumerate(branches):
        width = br.shape[1]
        sl = slice(bi * D_MODEL, (bi + 1) * D_MODEL)
        gate = 1.0 / (1.0 + jnp.exp(-(gl_ref[:, sl].astype(F32) + gb_ref[:, sl])))
        term = gate * _dot(br[...], wbr_ref[row0:row0 + width, :])
        merged = term if merged is None else merged + term
        row0 += width
    o_ref[...] = x_ref[...] + _dot(merged.astype(BF16), wout_ref[...])


def merge_project(x, p, gate_b, o_ret, o_fox, o_dsa, o_ssd, w_br, w_out):
    m = x.shape[0]
    tm = MERGE_TM

    def rows(width):
        return pl.BlockSpec((tm, width), lambda i: (i, 0))

    def const(shape):
        return pl.BlockSpec(shape, lambda i: (0, 0), pipeline_mode=pl.Buffered(1))

    return pl.pallas_call(
        _merge_kernel,
        grid=(m // tm,),
        in_specs=[
            rows(D_MODEL), rows(N_BRANCH * D_MODEL), const((1, N_BRANCH * D_MODEL)),
            rows(MIX_W), rows(MIX_W), rows(MIX_W), rows(SSD_INNER),
            const(w_br.shape), const(w_out.shape),
        ],
        out_specs=rows(D_MODEL),
        out_shape=jax.ShapeDtypeStruct(x.shape, x.dtype),
        compiler_params=_cparams(("parallel",)),
        name="merge_project",
    )(x, p, gate_b, o_ret, o_fox, o_dsa, o_ssd, w_br, w_out)


FFN_TM = 1024
FFN_TF = 512


def _ffn_kernel(x_ref, g_ref, w1_ref, w2_ref, o_ref, h_ref):
    @pl.when(pl.program_id(1) == 0)
    def _():
        h_ref[...] = _rms(x_ref[...], g_ref[...]).astype(BF16)
        o_ref[...] = x_ref[...]

    a = jnp.maximum(_dot(h_ref[...], w1_ref[...]), 0.0)
    o_ref[...] += _dot((a * a).astype(BF16), w2_ref[...])


def ffn(x, g, w1, w2):
    m, d = x.shape
    dff = w1.shape[1]
    tm, tf = min(FFN_TM, m), FFN_TF
    return pl.pallas_call(
        _ffn_kernel,
        grid=(m // tm, dff // tf),
        in_specs=[
            pl.BlockSpec((tm, d), lambda i, f: (i, 0), pipeline_mode=pl.Buffered(1)),
            pl.BlockSpec((1, d), lambda i, f: (0, 0)),
            pl.BlockSpec((d, tf), lambda i, f: (0, f)),
            pl.BlockSpec((tf, d), lambda i, f: (f, 0)),
        ],
        out_specs=pl.BlockSpec((tm, d), lambda i, f: (i, 0)),
        out_shape=jax.ShapeDtypeStruct(x.shape, x.dtype),
        scratch_shapes=[pltpu.VMEM((tm, d), BF16)],
        compiler_params=_cparams(("parallel", "arbitrary")),
        name="ffn",
    )(x, g, w1, w2)


def _split_cols(w, sizes):
    out, o = [], 0
    for s in sizes:
        out.append(w[:, o:o + s])
        o += s
    return out


def _layout_in_proj(w):
    (r_q, r_k, r_v, r_g, f_q, f_k, f_v, f_f, d_cq, d_k, d_v, i_k, i_w, s_z, s_xbc, s_dt, g_lin) = _split_cols(
        w, (MIX_W,) * 7 + (N_HEADS, DSA_Q_RANK, HEAD_DIM, HEAD_DIM, IDX_DIM, IDX_HEADS, SSD_INNER,
            SSD_INNER + 2 * SSD_GROUPS * SSD_STATE, SSD_HEADS, N_BRANCH * D_MODEL))
    d = w.shape[0]
    main = jnp.concatenate(
        [g_lin, r_q, r_k, r_v, r_g, s_z, s_xbc, d_cq, f_q, f_k, f_v, d_k, d_v,
         jnp.zeros((d, N_MAIN - N_MAIN_USED), w.dtype)], axis=1).astype(BF16)
    small = jnp.concatenate(
        [s_dt, f_f, jnp.zeros((d, SM_IW - SM_F - N_HEADS), w.dtype), i_w,
         jnp.zeros((d, SM_IK - SM_IW - IDX_HEADS), w.dtype), i_k], axis=1).astype(BF16)
    return main, small


def _pad_to(v, offset, total):
    return jnp.zeros((total,), v.dtype).at[offset:offset + v.shape[0]].set(v)


def _rotary_tables(seq):
    half = HEAD_DIM // 2
    inv = 1.0 / (10000.0 ** (jnp.arange(half, dtype=F32) / half))
    ang = jnp.arange(seq, dtype=F32)[:, None] * inv[None, :]
    cos, sin = jnp.cos(ang), jnp.sin(ang)
    return jnp.concatenate([cos, cos], axis=1), jnp.concatenate([-sin, sin], axis=1)


def kernel(x, norm1_g, w_in, gate_b, fox_f_b, fox_qn_g, fox_kn_g, dsa_cq_g, dsa_w_uq, dsa_w_qidx, dsa_qn_g,
           dsa_kn_g, rel_bias, ssd_conv_w, ssd_conv_b, ssd_dt_bias, ssd_a_log, ssd_d, ssd_norm_g, w_br, w_out,
           norm2_g, w_ff1, w_ff2):
    batch, seq, d = x.shape
    tokens = batch * seq
    xt = x.reshape(tokens, d)
    cos, sin = _rotary_tables(seq)
    tm = min(1024, tokens)
    for l in range(DEPTH):
        w_main, w_small = _layout_in_proj(w_in[l])
        g1 = norm1_g[l][None, :]
        p = norm_matmul(xt, g1, w_main, BF16, tm, 1024)
        sm = norm_matmul(xt, g1, w_small, F32, tm, SM_W)

        o_ret = retention(p, cos, sin, batch, seq)

        fb_row = _pad_to(fox_f_b[l], SM_F, SM_W)[None, :]
        fcol, frow = fox_prep(sm, fb_row, batch, seq)
        o_fox = fox_attention(p, fcol, frow, fox_qn_g[l][None, :], fox_kn_g[l][None, :], batch, seq)

        o_dsa = dsa_attention(p, sm, dsa_cq_g[l][None, :], dsa_w_uq[l].astype(BF16), dsa_w_qidx[l].astype(BF16),
                              dsa_qn_g[l][None, :], dsa_kn_g[l][None, :], rel_bias, batch, seq)

        o_ssd = ssd_mixer(p, sm, ssd_conv_w[l], ssd_conv_b[l][None, :],
                          _pad_to(ssd_dt_bias[l], SM_DT, SM_W)[:, None], _pad_to(ssd_a_log[l], SM_DT, SM_W)[:, None],
                          jnp.repeat(ssd_d[l], SSD_HEAD_DIM)[None, :], ssd_norm_g[l][None, :], batch, seq)

        xt = merge_project(xt, p, gate_b[l][None, :], o_ret, o_fox, o_dsa, o_ssd,
                           w_br[l].astype(BF16), w_out[l].astype(BF16))
        xt = ffn(xt, norm2_g[l][None, :], w_ff1[l].astype(BF16), w_ff2[l].astype(BF16))
    return xt.reshape(batch, seq, d)
```

```python
import functools
import math

import jax
import jax.numpy as jnp
from jax import lax
from jax.experimental import pallas as pl
from jax.experimental.pallas import tpu as pltpu

F32 = jnp.float32
BF16 = jnp.bfloat16

D_MODEL = 2048
DEPTH = 4
HEAD_DIM = 128
N_HEADS = 4
DSA_Q_RANK = 512
IDX_HEADS = 16
IDX_DIM = 64
DSA_TOPK = 256
SSD_HEADS = 16
SSD_HEAD_DIM = 64
SSD_GROUPS = 2
SSD_STATE = 128
SSD_CONV = 4
SSD_INNER = SSD_HEADS * SSD_HEAD_DIM
D_FF = 4 * D_MODEL
N_BUCKETS = 32
MAX_DISTANCE = 128
CHUNK = 128
EPS = 1e-6
N_BRANCH = 4
MIX_W = N_HEADS * HEAD_DIM

COL_GATE = 0
COL_RET = COL_GATE + N_BRANCH * D_MODEL
COL_Z = COL_RET + 4 * MIX_W
COL_XS = COL_Z + SSD_INNER
COL_BC = COL_XS + SSD_INNER
COL_CQ = COL_BC + 2 * SSD_GROUPS * SSD_STATE
COL_FOX = COL_CQ + DSA_Q_RANK
COL_DK = COL_FOX + 3 * MIX_W
COL_DV = COL_DK + HEAD_DIM
N_MAIN_USED = COL_DV + HEAD_DIM
N_MAIN = 15360
SM_DT = 0
SM_F = 16
SM_IW = 32
SM_IK = 64
SM_W = 128

LANES = 128
VMEM_LIMIT = 56 * 1024 * 1024
NEG_BIG = -1e30


def _cparams(sem):
    return pltpu.CompilerParams(dimension_semantics=sem, vmem_limit_bytes=VMEM_LIMIT)


def _dot(a, b):
    return jnp.dot(a, b, preferred_element_type=F32)


def _dot_nt(a, b):
    return lax.dot_general(a, b, (((1,), (1,)), ((), ())), preferred_element_type=F32)


def _dot_tn(a, b):
    return lax.dot_general(a, b, (((0,), (0,)), ((), ())), preferred_element_type=F32)


def _rms(x, g):
    return x * lax.rsqrt(jnp.mean(x * x, axis=-1, keepdims=True) + EPS) * g


def _silu(x):
    return x / (1.0 + jnp.exp(-x))


def _softplus(x):
    return jnp.maximum(x, 0.0) + jnp.log1p(jnp.exp(-jnp.abs(x)))


def _cumsum_lanes(x):
    lane = lax.broadcasted_iota(jnp.int32, x.shape, 1)
    d = 1
    while d < x.shape[1]:
        x = x + jnp.where(lane >= d, pltpu.roll(x, d, 1), 0.0)
        d *= 2
    return x


def _norm_matmul_kernel(x_ref, g_ref, w_ref, o_ref, h_ref):
    @pl.when(pl.program_id(1) == 0)
    def _():
        h_ref[...] = _rms(x_ref[...], g_ref[...]).astype(BF16)

    o_ref[...] = _dot(h_ref[...], w_ref[...]).astype(o_ref.dtype)


def norm_matmul(x, g, w, layer, out_dtype, tm, tn):
    m, d = x.shape
    n = w.shape[2]
    return pl.pallas_call(
        _norm_matmul_kernel,
        grid=(m // tm, n // tn),
        in_specs=[
            pl.BlockSpec((tm, d), lambda i, j: (i, 0)),
            pl.BlockSpec((1, d), lambda i, j: (0, 0)),
            pl.BlockSpec((None, d, tn), lambda i, j: (layer, 0, j)),
        ],
        out_specs=pl.BlockSpec((tm, tn), lambda i, j: (i, j)),
        out_shape=jax.ShapeDtypeStruct((m, n), out_dtype),
        scratch_shapes=[pltpu.VMEM((tm, d), BF16)],
        compiler_params=_cparams(("parallel", "arbitrary")),
        name="norm_matmul",
    )(x, g, w)


def _retention_kernel(q_ref, k_ref, v_ref, g_ref, cos_ref, sin_ref, o_ref, state_ref):
    c = CHUNK

    @pl.when(pl.program_id(1) == 0)
    def _():
        state_ref[...] = jnp.zeros_like(state_ref)

    cos = cos_ref[...]
    sin = sin_ref[...]
    ii = lax.broadcasted_iota(jnp.int32, (c, c), 0)
    jj = lax.broadcasted_iota(jnp.int32, (c, c), 1)
    rel = (ii - jj).astype(F32)
    i_col = lax.broadcasted_iota(jnp.int32, (c, 1), 0).astype(F32)
    for h in range(N_HEADS):
        lg = math.log1p(-(2.0 ** (-5.0 - h)))
        sl = slice(h * HEAD_DIM, (h + 1) * HEAD_DIM)
        q = q_ref[:, sl].astype(F32)
        k = k_ref[:, sl].astype(F32)
        v = v_ref[:, sl]
        qr = q * cos + pltpu.roll(q, HEAD_DIM // 2, 1) * sin
        kr = (k * cos + pltpu.roll(k, HEAD_DIM // 2, 1) * sin) * (HEAD_DIM ** -0.5)
        decay = jnp.where(rel >= 0, jnp.exp(lg * jnp.maximum(rel, 0.0)), 0.0)
        scores = _dot_nt(qr.astype(BF16), kr.astype(BF16)) * decay
        y = _dot(scores.astype(BF16), v)
        q_dec = jnp.exp(lg * (i_col + 1.0))
        k_dec = jnp.exp(lg * (c - 1.0 - i_col))
        st = state_ref[h]
        y = y + _dot((qr * q_dec).astype(BF16), st.astype(BF16))
        kv = _dot_tn((kr * k_dec).astype(BF16), v)
        state_ref[h] = math.exp(lg * c) * st + kv
        yc = y - jnp.mean(y, axis=-1, keepdims=True)
        yn = yc * lax.rsqrt(jnp.mean(yc * yc, axis=-1, keepdims=True) + EPS)
        o_ref[:, sl] = (_silu(g_ref[:, sl].astype(F32)) * yn).astype(o_ref.dtype)


def retention(p, cos, sin, batch, seq):
    n = seq // CHUNK
    base = COL_RET // MIX_W

    def col(j):
        return pl.BlockSpec((CHUNK, MIX_W), lambda b, i: (b * n + i, base + j))

    tab = pl.BlockSpec((CHUNK, HEAD_DIM), lambda b, i: (i, 0))
    return pl.pallas_call(
        _retention_kernel,
        grid=(batch, n),
        in_specs=[col(0), col(1), col(2), col(3), tab, tab],
        out_specs=pl.BlockSpec((CHUNK, MIX_W), lambda b, i: (b * n + i, 0)),
        out_shape=jax.ShapeDtypeStruct((batch * seq, MIX_W), BF16),
        scratch_shapes=[pltpu.VMEM((N_HEADS, HEAD_DIM, HEAD_DIM), F32)],
        compiler_params=_cparams(("parallel", "arbitrary")),
        name="retention",
    )(p, p, p, p, cos, sin)


def _fox_prep_kernel(sm_ref, fb_ref, fcol_ref, frow_ref, carry_ref):
    @pl.when(pl.program_id(1) == 0)
    def _():
        carry_ref[...] = jnp.zeros_like(carry_ref)

    t = sm_ref[...] + fb_ref[...]
    lf = jnp.minimum(t, 0.0) - jnp.log1p(jnp.exp(-jnp.abs(t)))
    cs = _cumsum_lanes(lf.T) + carry_ref[...]
    carry_ref[...] = cs[:, LANES - 1:LANES]
    frow_ref[0, 0] = cs[SM_F:SM_F + 8, :]
    fcol_ref[...] = cs.T


def fox_prep(sm, fb_row, batch, seq):
    n = seq // CHUNK
    return pl.pallas_call(
        _fox_prep_kernel,
        grid=(batch, n),
        in_specs=[
            pl.BlockSpec((CHUNK, SM_W), lambda b, i: (b * n + i, 0)),
            pl.BlockSpec((1, SM_W), lambda b, i: (0, 0)),
        ],
        out_specs=[
            pl.BlockSpec((CHUNK, SM_W), lambda b, i: (b * n + i, 0)),
            pl.BlockSpec((1, 1, 8, CHUNK), lambda b, i: (b, i, 0, 0)),
        ],
        out_shape=[
            jax.ShapeDtypeStruct((batch * seq, SM_W), F32),
            jax.ShapeDtypeStruct((batch, n, 8, CHUNK), F32),
        ],
        scratch_shapes=[pltpu.VMEM((SM_W, 1), F32)],
        compiler_params=_cparams(("parallel", "arbitrary")),
        name="fox_prep",
    )(sm, fb_row)


FOX_TQ = 256
FOX_TK = 128


def _fox_kernel(q_ref, k_ref, v_ref, fcol_ref, frow_ref, qg_ref, kg_ref, o_ref, kn_ref):
    i = pl.program_id(1)
    tq, tk = FOX_TQ, FOX_TK

    @pl.when(i == 0)
    def _():
        for h in range(N_HEADS):
            sl = slice(h * HEAD_DIM, (h + 1) * HEAD_DIM)
            kn_ref[:, sl] = _rms(k_ref[:, sl].astype(F32), kg_ref[...]).astype(BF16)

    row = lax.broadcasted_iota(jnp.int32, (tq, tk), 0) + i * tq
    col = lax.broadcasted_iota(jnp.int32, (tq, tk), 1)
    n_kv = (i + 1) * (tq // tk)
    for h in range(N_HEADS):
        sl = slice(h * HEAD_DIM, (h + 1) * HEAD_DIM)
        qn = (_rms(q_ref[:, sl].astype(F32), qg_ref[...]) * (HEAD_DIM ** -0.5)).astype(BF16)
        fq = fcol_ref[:, SM_F + h:SM_F + h + 1]

        def body(j, carry, sl=sl, qn=qn, fq=fq, h=h):
            m, l, acc = carry
            start = pl.multiple_of(j * tk, tk)
            ks = kn_ref[pl.ds(start, tk), sl]
            vs = v_ref[pl.ds(start, tk), sl]
            fk = frow_ref[0, j, h:h + 1, :]
            s = _dot_nt(qn, ks) + (fq - fk)
            s = jnp.where(col + j * tk <= row, s, NEG_BIG)
            m_new = jnp.maximum(m, jnp.max(s, axis=1, keepdims=True))
            p = jnp.exp(s - m_new)
            alpha = jnp.exp(m - m_new)
            l = alpha * l + jnp.sum(p, axis=1, keepdims=True)
            acc = alpha * acc + _dot(p.astype(BF16), vs)
            return m_new, l, acc

        init = (jnp.full((tq, 1), NEG_BIG, F32), jnp.zeros((tq, 1), F32), jnp.zeros((tq, HEAD_DIM), F32))
        _, l, acc = lax.fori_loop(0, n_kv, body, init)
        o_ref[:, sl] = (acc / l).astype(o_ref.dtype)


def fox_attention(p, fcol, frow, qg, kg, batch, seq):
    nq = seq // FOX_TQ
    base = COL_FOX // MIX_W
    return pl.pallas_call(
        _fox_kernel,
        grid=(batch, nq),
        in_specs=[
            pl.BlockSpec((FOX_TQ, MIX_W), lambda b, i: (b * nq + i, base)),
            pl.BlockSpec((seq, MIX_W), lambda b, i: (b, base + 1)),
            pl.BlockSpec((seq, MIX_W), lambda b, i: (b, base + 2)),
            pl.BlockSpec((FOX_TQ, SM_W), lambda b, i: (b * nq + i, 0)),
            pl.BlockSpec((1, seq // FOX_TK, 8, FOX_TK), lambda b, i: (b, 0, 0, 0)),
            pl.BlockSpec((1, HEAD_DIM), lambda b, i: (0, 0)),
            pl.BlockSpec((1, HEAD_DIM), lambda b, i: (0, 0)),
        ],
        out_specs=pl.BlockSpec((FOX_TQ, MIX_W), lambda b, i: (b * nq + i, 0)),
        out_shape=jax.ShapeDtypeStruct((batch * seq, MIX_W), BF16),
        scratch_shapes=[pltpu.VMEM((seq, MIX_W), BF16)],
        compiler_params=_cparams(("parallel", "arbitrary")),
        name="fox_attention",
    )(p, p, p, fcol, frow, qg, kg)


DSA_TQ = 128
DSA_TK = 128
BAND_W = 2 * DSA_TK


def _t5_bucket(dist):
    max_exact = N_BUCKETS // 2
    d = jnp.maximum(dist, 0)
    log_ratio = jnp.log(jnp.maximum(d, 1).astype(F32) / max_exact) / math.log(MAX_DISTANCE / max_exact)
    large = jnp.minimum(max_exact + (log_ratio * (N_BUCKETS - max_exact)).astype(jnp.int32), N_BUCKETS - 1)
    return jnp.where(d < max_exact, d, large)


def _dsa_kernel(cq_ref, k_ref, v_ref, smq_ref, smk_ref, cqg_ref, wuq_ref, wqi_ref, qg_ref, kg_ref, rb_ref,
                o_ref, kn_ref, ki_ref, band_ref, sc_ref, qh_ref, *, topk):
    b = pl.program_id(0)
    i = pl.program_id(1)
    tq, tk = DSA_TQ, DSA_TK
    nkb = sc_ref.shape[0]

    @pl.when(jnp.logical_and(b == 0, i == 0))
    def _():
        r = lax.broadcasted_iota(jnp.int32, (tq, BAND_W), 0)
        c = lax.broadcasted_iota(jnp.int32, (tq, BAND_W), 1)
        bucket = _t5_bucket(tk + r - c)
        for h in range(N_HEADS):
            far = rb_ref[N_BUCKETS - 1, h]
            acc = jnp.zeros((tq, BAND_W), F32)
            for bk in range(N_BUCKETS - 1):
                acc = jnp.where(bucket == bk, rb_ref[bk, h] - far, acc)
            band_ref[h] = acc

    @pl.when(i == 0)
    def _():
        kn_ref[...] = _rms(k_ref[...].astype(F32), kg_ref[...]).astype(BF16)
        ki_ref[...] = smk_ref[:, SM_IK:SM_IK + IDX_DIM].astype(BF16)

    cq = _rms(cq_ref[...].astype(F32), cqg_ref[...]).astype(BF16)
    qf = _dot(cq, wuq_ref[...])
    for h in range(N_HEADS):
        sl = slice(h * HEAD_DIM, (h + 1) * HEAD_DIM)
        qh_ref[h] = (_rms(qf[:, sl], qg_ref[...]) * (HEAD_DIM ** -0.5)).astype(BF16)
    q_idx = (_dot(cq, wqi_ref[...]) * (IDX_DIM ** -0.5)).astype(BF16)
    w_h = smq_ref[:, SM_IW:SM_IW + IDX_HEADS] * (IDX_HEADS ** -0.5)

    row = lax.broadcasted_iota(jnp.int32, (tq, tk), 0)
    col = lax.broadcasted_iota(jnp.int32, (tq, tk), 1)

    def score_body(j, _):
        start = pl.multiple_of(j * tk, tk)
        kj = ki_ref[pl.ds(start, tk), :]
        acc = jnp.zeros((tq, tk), F32)
        for h in range(IDX_HEADS):
            r = _dot_nt(q_idx[:, h * IDX_DIM:(h + 1) * IDX_DIM], kj)
            acc = acc + w_h[:, h:h + 1] * jnp.maximum(r, 0.0)
        sc_ref[j] = jnp.where(jnp.logical_or(j < i, col <= row), acc, -jnp.inf)
        return 0

    lax.fori_loop(0, i + 1, score_body, 0)

    def fill_body(j, _):
        sc_ref[j] = jnp.full((tq, tk), -jnp.inf, F32)
        return 0

    lax.fori_loop(i + 1, nkb, fill_body, 0)

    def search():
        s = sc_ref[...]
        kf = float(topk)
        smax = jnp.max(jnp.max(s, axis=0), axis=1, keepdims=True)
        smin = jnp.min(jnp.min(jnp.where(s == -jnp.inf, jnp.inf, s), axis=0), axis=1, keepdims=True)

        def count_ge(t):
            return jnp.sum(jnp.sum(jnp.where(s >= t[None], 1.0, 0.0), axis=0), axis=1, keepdims=True)

        def midpoint(lo, hi):
            return jnp.where(hi == jnp.inf, smax, 0.5 * (lo + hi))

        def undecided(lo, hi, c_lo, mid):
            return jnp.logical_and(c_lo != kf, jnp.logical_and(mid > lo, mid < hi))

        def cond(carry):
            return jnp.logical_and(carry[0] < 400, carry[1] > 0.0)

        def body(carry):
            it, _, lo, hi, c_lo, c_hi, mid = carry
            upd = undecided(lo, hi, c_lo, mid)
            cnt = count_ge(mid)
            up = jnp.logical_and(upd, cnt >= kf)
            dn = jnp.logical_and(upd, cnt < kf)
            lo = jnp.where(up, mid, lo)
            c_lo = jnp.where(up, cnt, c_lo)
            hi = jnp.where(dn, mid, hi)
            c_hi = jnp.where(dn, cnt, c_hi)
            mid = midpoint(lo, hi)
            active = jnp.max(jnp.where(undecided(lo, hi, c_lo, mid), 1.0, 0.0))
            return it + 1, active, lo, hi, c_lo, c_hi, mid

        lo0 = smin
        hi0 = jnp.full((tq, 1), jnp.inf, F32)
        c_lo0 = count_ge(lo0)
        c_hi0 = jnp.zeros((tq, 1), F32)
        mid0 = midpoint(lo0, hi0)
        act0 = jnp.max(jnp.where(undecided(lo0, hi0, c_lo0, mid0), 1.0, 0.0))
        _, _, lo, hi, c_lo, c_hi, _ = lax.while_loop(
            cond, body, (jnp.int32(0), act0, lo0, hi0, c_lo0, c_hi0, mid0))

        kidx = (lax.broadcasted_iota(jnp.int32, s.shape, 0) * tk
                + lax.broadcasted_iota(jnp.int32, s.shape, 2))
        tie = jnp.logical_and(s >= lo[None], s < hi[None])
        need = kf - c_hi

        def tie_body(_, carry):
            jlo, jhi = carry
            jm = (jlo + jhi) // 2
            cnt = jnp.sum(jnp.sum(jnp.where(jnp.logical_and(tie, kidx <= jm[None]), 1.0, 0.0), axis=0),
                          axis=1, keepdims=True)
            ok = cnt >= need
            return jnp.where(ok, jlo, jm), jnp.where(ok, jm, jhi)

        n_bits = int(math.ceil(math.log2(nkb * tk))) + 1
        _, jmax = lax.fori_loop(0, n_bits, tie_body,
                                (jnp.full((tq, 1), -1, jnp.int32), jnp.full((tq, 1), nkb * tk - 1, jnp.int32)))
        return lo, hi, jmax

    def keep_all():
        return (jnp.full((tq, 1), -jnp.inf, F32), jnp.full((tq, 1), jnp.inf, F32),
                jnp.full((tq, 1), nkb * tk - 1, jnp.int32))

    lo, hi, jmax = lax.cond((i + 1) * tq > topk, search, keep_all)

    for h in range(N_HEADS):
        sl = slice(h * HEAD_DIM, (h + 1) * HEAD_DIM)
        qh = qh_ref[h]
        band_prev = band_ref[h, :, 0:tk]
        band_diag = band_ref[h, :, tk:2 * tk]

        def body(j, carry, qh=qh, band_prev=band_prev, band_diag=band_diag):
            m, l, acc = carry
            start = pl.multiple_of(j * tk, tk)
            ks = kn_ref[pl.ds(start, tk), :]
            vs = v_ref[pl.ds(start, tk), :]
            sc = sc_ref[j]
            bias = jnp.where(j == i, band_diag, jnp.where(j == i - 1, band_prev, 0.0))
            s = _dot_nt(qh, ks) + bias
            keep = jnp.logical_or(sc >= hi, jnp.logical_and(sc >= lo, col + j * tk <= jmax))
            keep = jnp.logical_and(keep, jnp.logical_or(j < i, col <= row))
            s = jnp.where(keep, s, NEG_BIG)
            m_new = jnp.maximum(m, jnp.max(s, axis=1, keepdims=True))
            p = jnp.where(keep, jnp.exp(s - m_new), 0.0)
            alpha = jnp.exp(m - m_new)
            l = alpha * l + jnp.sum(p, axis=1, keepdims=True)
            acc = alpha * acc + _dot(p.astype(BF16), vs)
            return m_new, l, acc

        init = (jnp.full((tq, 1), NEG_BIG, F32), jnp.zeros((tq, 1), F32), jnp.zeros((tq, HEAD_DIM), F32))
        _, l, acc = lax.fori_loop(0, i + 1, body, init)
        o_ref[:, sl] = (acc / l).astype(o_ref.dtype)


def dsa_attention(p, sm, cqg, wuq, wqi, qg, kg, rel_bias, batch, seq):
    nq = seq // DSA_TQ
    nkb = seq // DSA_TK
    topk = min(DSA_TOPK, seq // 4)
    kern = functools.partial(_dsa_kernel, topk=topk)
    return pl.pallas_call(
        kern,
        grid=(batch, nq),
        in_specs=[
            pl.BlockSpec((DSA_TQ, DSA_Q_RANK), lambda b, i: (b * nq + i, COL_CQ // DSA_Q_RANK)),
            pl.BlockSpec((seq, HEAD_DIM), lambda b, i: (b, COL_DK // HEAD_DIM)),
            pl.BlockSpec((seq, HEAD_DIM), lambda b, i: (b, COL_DV // HEAD_DIM)),
            pl.BlockSpec((DSA_TQ, SM_W), lambda b, i: (b * nq + i, 0)),
            pl.BlockSpec((seq, SM_W), lambda b, i: (b, 0)),
            pl.BlockSpec((1, DSA_Q_RANK), lambda b, i: (0, 0)),
            pl.BlockSpec((DSA_Q_RANK, N_HEADS * HEAD_DIM), lambda b, i: (0, 0)),
            pl.BlockSpec((DSA_Q_RANK, IDX_HEADS * IDX_DIM), lambda b, i: (0, 0)),
            pl.BlockSpec((1, HEAD_DIM), lambda b, i: (0, 0)),
            pl.BlockSpec((1, HEAD_DIM), lambda b, i: (0, 0)),
            pl.BlockSpec(memory_space=pltpu.SMEM),
        ],
        out_specs=pl.BlockSpec((DSA_TQ, MIX_W), lambda b, i: (b * nq + i, 0)),
        out_shape=jax.ShapeDtypeStruct((batch * seq, MIX_W), BF16),
        scratch_shapes=[
            pltpu.VMEM((seq, HEAD_DIM), BF16),
            pltpu.VMEM((seq, IDX_DIM), BF16),
            pltpu.VMEM((N_HEADS, DSA_TQ, BAND_W), F32),
            pltpu.VMEM((nkb, DSA_TQ, DSA_TK), F32),
            pltpu.VMEM((N_HEADS, DSA_TQ, HEAD_DIM), BF16),
        ],
        compiler_params=_cparams(("arbitrary", "arbitrary")),
        name="dsa_attention",
    )(p, p, p, sm, sm, cqg, wuq, wqi, qg, kg, rel_bias)


def _causal_conv(x, xp, w_ref, b_ref):
    rows = lax.broadcasted_iota(jnp.int32, x.shape, 0)
    acc = x * w_ref[SSD_CONV - 1:SSD_CONV, :] + b_ref[...]
    for d in range(1, SSD_CONV):
        shifted = jnp.where(rows < d, pltpu.roll(xp, d, 0), pltpu.roll(x, d, 0))
        acc = acc + shifted * w_ref[SSD_CONV - 1 - d:SSD_CONV - d, :]
    return _silu(acc)


def _ssd_kernel(z_ref, xs_ref, bc_ref, xsp_ref, bcp_ref, sm_ref, cwx_ref, cbx_ref, cwb_ref, cbb_ref,
                dtb_ref, alog_ref, dsk_ref, ng_ref, o_ref, prev_ref, y_ref):
    c = CHUNK
    n = pl.program_id(1)

    @pl.when(n == 0)
    def _():
        prev_ref[...] = jnp.zeros_like(prev_ref)

    first = (n > 0).astype(F32)
    xs = _causal_conv(xs_ref[...].astype(F32), xsp_ref[...].astype(F32) * first, cwx_ref, cbx_ref)
    bc = _causal_conv(bc_ref[...].astype(F32), bcp_ref[...].astype(F32) * first, cwb_ref, cbb_ref)

    dt_t = _softplus(sm_ref[...].T + dtb_ref[...])
    cs_t = _cumsum_lanes(dt_t * (-jnp.exp(alog_ref[...])))
    cs = cs_t.T
    dt = dt_t.T
    ii = lax.broadcasted_iota(jnp.int32, (c, c), 0)
    jj = lax.broadcasted_iota(jnp.int32, (c, c), 1)
    tril = ii >= jj
    gn = SSD_GROUPS * SSD_STATE
    hpg = SSD_HEADS // SSD_GROUPS
    for g in range(SSD_GROUPS):
        bg = bc[:, g * SSD_STATE:(g + 1) * SSD_STATE]
        cg = bc[:, gn + g * SSD_STATE:gn + (g + 1) * SSD_STATE].astype(BF16)
        cb = _dot_nt(cg, bg.astype(BF16))
        y_off = _dot(cg, prev_ref[g].astype(BF16))
        for r in range(hpg):
            h = g * hpg + r
            hs = slice(h * SSD_HEAD_DIM, (h + 1) * SSD_HEAD_DIM)
            rs = slice(r * SSD_HEAD_DIM, (r + 1) * SSD_HEAD_DIM)
            a_col = cs[:, SM_DT + h:SM_DT + h + 1]
            a_row = cs_t[SM_DT + h:SM_DT + h + 1, :]
            last = cs_t[SM_DT + h:SM_DT + h + 1, c - 1:c]
            seg = jnp.where(tril, jnp.exp(jnp.where(tril, a_col - a_row, 0.0)), 0.0)
            xh = xs[:, hs]
            xc = (xh * dt[:, SM_DT + h:SM_DT + h + 1]).astype(BF16)
            y_diag = _dot((cb * seg).astype(BF16), xc)
            st = _dot_tn((bg * jnp.exp(last - a_col)).astype(BF16), xc)
            y_ref[:, hs] = y_diag + y_off[:, rs] * jnp.exp(a_col) + dsk_ref[:, hs] * xh
            prev_ref[g, :, rs] = jnp.exp(last) * prev_ref[g, :, rs] + st
    gated = y_ref[...] * _silu(z_ref[...].astype(F32))
    gw = SSD_INNER // SSD_GROUPS
    for g in range(SSD_GROUPS):
        sl = slice(g * gw, (g + 1) * gw)
        o_ref[:, sl] = _rms(gated[:, sl], ng_ref[:, sl]).astype(o_ref.dtype)


def ssd_mixer(p, sm, cw, cb, dtb_col, alog_col, dskip_row, ng, batch, seq):
    n = seq // CHUNK
    bcw = 2 * SSD_GROUPS * SSD_STATE

    def cur(width, colbase):
        return pl.BlockSpec((CHUNK, width), lambda b, i: (b * n + i, colbase // width))

    def prv(width, colbase):
        return pl.BlockSpec((CHUNK, width), lambda b, i: (b * n + jnp.maximum(i - 1, 0), colbase // width))

    def const(shape):
        return pl.BlockSpec(shape, lambda b, i: (0, 0))

    return pl.pallas_call(
        _ssd_kernel,
        grid=(batch, n),
        in_specs=[
            cur(SSD_INNER, COL_Z), cur(SSD_INNER, COL_XS), cur(bcw, COL_BC),
            prv(SSD_INNER, COL_XS), prv(bcw, COL_BC),
            pl.BlockSpec((CHUNK, SM_W), lambda b, i: (b * n + i, 0)),
            const((SSD_CONV, SSD_INNER)), const((1, SSD_INNER)),
            const((SSD_CONV, bcw)), const((1, bcw)),
            const((SM_W, 1)), const((SM_W, 1)),
            const((1, SSD_INNER)), const((1, SSD_INNER)),
        ],
        out_specs=pl.BlockSpec((CHUNK, SSD_INNER), lambda b, i: (b * n + i, 0)),
        out_shape=jax.ShapeDtypeStruct((batch * seq, SSD_INNER), BF16),
        scratch_shapes=[
            pltpu.VMEM((SSD_GROUPS, SSD_STATE, SSD_INNER // SSD_GROUPS), F32),
            pltpu.VMEM((CHUNK, SSD_INNER), F32),
        ],
        compiler_params=_cparams(("parallel", "arbitrary")),
        name="ssd_mixer",
    )(p, p, p, p, p, sm, cw[:, :SSD_INNER], cb[:, :SSD_INNER], cw[:, SSD_INNER:], cb[:, SSD_INNER:],
      dtb_col, alog_col, dskip_row, ng)


MERGE_TM = 256


def _merge_kernel(x_ref, gl_ref, gb_ref, oret_ref, ofox_ref, odsa_ref, ossd_ref, wbr_ref, wout_ref, o_ref):
    branches = (oret_ref, ofox_ref, odsa_ref, ossd_ref)
    merged = None
    row0 = 0
    for bi, br in enumerate(branches):
        width = br.shape[1]
        sl = slice(bi * D_MODEL, (bi + 1) * D_MODEL)
        gate = 1.0 / (1.0 + jnp.exp(-(gl_ref[:, sl].astype(F32) + gb_ref[:, sl])))
        term = gate * _dot(br[...], wbr_ref[row0:row0 + width, :])
        merged = term if merged is None else merged + term
        row0 += width
    o_ref[...] = x_ref[...] + _dot(merged.astype(BF16), wout_ref[...])


def merge_project(x, p, gate_b, o_ret, o_fox, o_dsa, o_ssd, w_br, w_out):
    m = x.shape[0]
    tm = MERGE_TM

    def rows(width):
        return pl.BlockSpec((tm, width), lambda i: (i, 0))

    def const(shape):
        return pl.BlockSpec(shape, lambda i: (0, 0), pipeline_mode=pl.Buffered(1))

    return pl.pallas_call(
        _merge_kernel,
        grid=(m // tm,),
        in_specs=[
            rows(D_MODEL), rows(N_BRANCH * D_MODEL), const((1, N_BRANCH * D_MODEL)),
            rows(MIX_W), rows(MIX_W), rows(MIX_W), rows(SSD_INNER),
            const(w_br.shape), const(w_out.shape),
        ],
        out_specs=rows(D_MODEL),
        out_shape=jax.ShapeDtypeStruct(x.shape, x.dtype),
        compiler_params=_cparams(("parallel",)),
        name="merge_project",
    )(x, p, gate_b, o_ret, o_fox, o_dsa, o_ssd, w_br, w_out)


FFN_TM = 1024
FFN_TF = 512


def _ffn_kernel(x_ref, g_ref, w1_ref, w2_ref, o_ref, h_ref):
    @pl.when(pl.program_id(1) == 0)
    def _():
        h_ref[...] = _rms(x_ref[...], g_ref[...]).astype(BF16)
        o_ref[...] = x_ref[...]

    a = jnp.maximum(_dot(h_ref[...], w1_ref[...]), 0.0)
    o_ref[...] += _dot((a * a).astype(BF16), w2_ref[...])


def ffn(x, g, w1, w2):
    m, d = x.shape
    dff = w1.shape[1]
    tm, tf = min(FFN_TM, m), FFN_TF
    return pl.pallas_call(
        _ffn_kernel,
        grid=(m // tm, dff // tf),
        in_specs=[
            pl.BlockSpec((tm, d), lambda i, f: (i, 0), pipeline_mode=pl.Buffered(1)),
            pl.BlockSpec((1, d), lambda i, f: (0, 0)),
            pl.BlockSpec((d, tf), lambda i, f: (0, f)),
            pl.BlockSpec((tf, d), lambda i, f: (f, 0)),
        ],
        out_specs=pl.BlockSpec((tm, d), lambda i, f: (i, 0)),
        out_shape=jax.ShapeDtypeStruct(x.shape, x.dtype),
        scratch_shapes=[pltpu.VMEM((tm, d), BF16)],
        compiler_params=_cparams(("parallel", "arbitrary")),
        name="ffn",
    )(x, g, w1, w2)


SRC_RET = 0
SRC_FOX = SRC_RET + 4 * MIX_W
SRC_FF = SRC_FOX + 3 * MIX_W
SRC_CQ = SRC_FF + N_HEADS
SRC_DK = SRC_CQ + DSA_Q_RANK
SRC_IK = SRC_DK + 2 * HEAD_DIM
SRC_IW = SRC_IK + IDX_DIM
SRC_Z = SRC_IW + IDX_HEADS
SRC_DT = SRC_Z + 2 * SSD_INNER + 2 * SSD_GROUPS * SSD_STATE
SRC_GATE = SRC_DT + SSD_HEADS
IN_TOTAL = SRC_GATE + N_BRANCH * D_MODEL
MAIN_RUNS = ((COL_GATE, SRC_GATE), (COL_RET, SRC_RET), (COL_Z, SRC_Z), (COL_CQ, SRC_CQ),
             (COL_FOX, SRC_FOX), (COL_DK, SRC_DK))
RELAYOUT_W = 512
RELAYOUT_TILES = RELAYOUT_W // LANES


def _relayout_tables():
    starts, shifts = [], []
    for blk in range(N_MAIN // RELAYOUT_W):
        o = blk * RELAYOUT_W
        dst, src = [r for r in MAIN_RUNS if r[0] <= o][-1]
        col = src + (o - dst)
        starts.append(col // LANES)
        shifts.append(col % LANES)
    return jnp.asarray(starts, jnp.int32), jnp.asarray(shifts, jnp.int32)


def _relayout_kernel(start_ref, shift_ref, *refs):
    del start_ref
    tiles, o_ref = refs[:-1], refs[-1]
    shift = shift_ref[pl.program_id(1)]
    amount = lax.rem(LANES - shift, LANES)
    lane = lax.broadcasted_iota(jnp.int32, tiles[0].shape, 1)
    rolled = [pltpu.roll(t[...], amount, 1) for t in tiles]
    for k in range(RELAYOUT_TILES):
        piece = jnp.where(lane < LANES - shift, rolled[k], rolled[k + 1])
        o_ref[:, k * LANES:(k + 1) * LANES] = piece.astype(o_ref.dtype)


def relayout_main(w_in):
    depth, d, n_src = w_in.shape
    last = (n_src - 1) // LANES
    starts, shifts = _relayout_tables()

    def tile(k):
        return pl.BlockSpec((None, d, LANES), lambda l, b, st, sh: (l, 0, jnp.minimum(st[b] + k, last)))

    return pl.pallas_call(
        _relayout_kernel,
        grid_spec=pltpu.PrefetchScalarGridSpec(
            num_scalar_prefetch=2,
            grid=(depth, N_MAIN // RELAYOUT_W),
            in_specs=[tile(k) for k in range(RELAYOUT_TILES + 1)],
            out_specs=pl.BlockSpec((None, d, RELAYOUT_W), lambda l, b, st, sh: (l, 0, b)),
        ),
        out_shape=jax.ShapeDtypeStruct((depth, d, N_MAIN), BF16),
        compiler_params=_cparams(("parallel", "arbitrary")),
        name="relayout_main",
    )(starts, shifts, *([w_in] * (RELAYOUT_TILES + 1)))


SMALL_PIECES = ((SRC_DT, SM_DT, SSD_HEADS), (SRC_FF, SM_F, N_HEADS), (SRC_IW, SM_IW, IDX_HEADS),
                (SRC_IK, SM_IK, IDX_DIM))


def _relayout_small_kernel(*refs):
    tiles, o_ref = refs[:-1], refs[-1]
    lane = lax.broadcasted_iota(jnp.int32, o_ref.shape, 1)
    out = jnp.zeros(o_ref.shape, F32)
    for t, (src, dst, width) in zip(tiles, SMALL_PIECES):
        moved = pltpu.roll(t[...], (dst - src % LANES) % LANES, 1)
        out = jnp.where(jnp.logical_and(lane >= dst, lane < dst + width), moved, out)
    o_ref[...] = out.astype(o_ref.dtype)


def relayout_small(w_in):
    depth, d, _ = w_in.shape
    for src, _, width in SMALL_PIECES:
        assert src // LANES == (src + width - 1) // LANES

    def tile(src):
        return pl.BlockSpec((None, d, LANES), lambda l: (l, 0, src // LANES))

    return pl.pallas_call(
        _relayout_small_kernel,
        grid=(depth,),
        in_specs=[tile(src) for src, _, _ in SMALL_PIECES],
        out_specs=pl.BlockSpec((None, d, SM_W), lambda l: (l, 0, 0)),
        out_shape=jax.ShapeDtypeStruct((depth, d, SM_W), BF16),
        compiler_params=_cparams(("parallel",)),
        name="relayout_small",
    )(*([w_in] * len(SMALL_PIECES)))


def _pad_to(v, offset, total):
    return jnp.zeros((total,), v.dtype).at[offset:offset + v.shape[0]].set(v)


def _rotary_tables(seq):
    half = HEAD_DIM // 2
    inv = 1.0 / (10000.0 ** (jnp.arange(half, dtype=F32) / half))
    ang = jnp.arange(seq, dtype=F32)[:, None] * inv[None, :]
    cos, sin = jnp.cos(ang), jnp.sin(ang)
    return jnp.concatenate([cos, cos], axis=1), jnp.concatenate([-sin, sin], axis=1)


def kernel(x, norm1_g, w_in, gate_b, fox_f_b, fox_qn_g, fox_kn_g, dsa_cq_g, dsa_w_uq, dsa_w_qidx, dsa_qn_g,
           dsa_kn_g, rel_bias, ssd_conv_w, ssd_conv_b, ssd_dt_bias, ssd_a_log, ssd_d, ssd_norm_g, w_br, w_out,
           norm2_g, w_ff1, w_ff2):
    batch, seq, d = x.shape
    tokens = batch * seq
    xt = x.reshape(tokens, d)
    cos, sin = _rotary_tables(seq)
    tm = min(1024, tokens)
    w_main = relayout_main(w_in)
    w_small = relayout_small(w_in)
    for l in range(DEPTH):
        g1 = norm1_g[l][None, :]
        p = norm_matmul(xt, g1, w_main, l, BF16, tm, 1024)
        sm = norm_matmul(xt, g1, w_small, l, F32, tm, SM_W)

        o_ret = retention(p, cos, sin, batch, seq)

        fb_row = _pad_to(fox_f_b[l], SM_F, SM_W)[None, :]
        fcol, frow = fox_prep(sm, fb_row, batch, seq)
        o_fox = fox_attention(p, fcol, frow, fox_qn_g[l][None, :], fox_kn_g[l][None, :], batch, seq)

        o_dsa = dsa_attention(p, sm, dsa_cq_g[l][None, :], dsa_w_uq[l].astype(BF16), dsa_w_qidx[l].astype(BF16),
                              dsa_qn_g[l][None, :], dsa_kn_g[l][None, :], rel_bias, batch, seq)

        o_ssd = ssd_mixer(p, sm, ssd_conv_w[l], ssd_conv_b[l][None, :],
                          _pad_to(ssd_dt_bias[l], SM_DT, SM_W)[:, None], _pad_to(ssd_a_log[l], SM_DT, SM_W)[:, None],
                          jnp.repeat(ssd_d[l], SSD_HEAD_DIM)[None, :], ssd_norm_g[l][None, :], batch, seq)

        xt = merge_project(xt, p, gate_b[l][None, :], o_ret, o_fox, o_dsa, o_ssd,
                           w_br[l].astype(BF16), w_out[l].astype(BF16))
        xt = ffn(xt, norm2_g[l][None, :], w_ff1[l].astype(BF16), w_ff2[l].astype(BF16))
    return xt.reshape(batch, seq, d)
```

```python
import functools
import math

import jax
import jax.numpy as jnp
from jax import lax
from jax.experimental import pallas as pl
from jax.experimental.pallas import tpu as pltpu

F32 = jnp.float32
BF16 = jnp.bfloat16

D_MODEL = 2048
DEPTH = 4
HEAD_DIM = 128
N_HEADS = 4
DSA_Q_RANK = 512
IDX_HEADS = 16
IDX_DIM = 64
DSA_TOPK = 256
SSD_HEADS = 16
SSD_HEAD_DIM = 64
SSD_GROUPS = 2
SSD_STATE = 128
SSD_CONV = 4
SSD_INNER = SSD_HEADS * SSD_HEAD_DIM
D_FF = 4 * D_MODEL
N_BUCKETS = 32
MAX_DISTANCE = 128
CHUNK = 128
EPS = 1e-6
N_BRANCH = 4
MIX_W = N_HEADS * HEAD_DIM

COL_GATE = 0
COL_RET = COL_GATE + N_BRANCH * D_MODEL
COL_Z = COL_RET + 4 * MIX_W
COL_XS = COL_Z + SSD_INNER
COL_BC = COL_XS + SSD_INNER
COL_CQ = COL_BC + 2 * SSD_GROUPS * SSD_STATE
COL_FOX = COL_CQ + DSA_Q_RANK
COL_DK = COL_FOX + 3 * MIX_W
COL_DV = COL_DK + HEAD_DIM
N_MAIN_USED = COL_DV + HEAD_DIM
N_MAIN = 15360
SM_DT = 0
SM_F = 16
SM_IW = 32
SM_IK = 64
SM_W = 128

LANES = 128
VMEM_LIMIT = 56 * 1024 * 1024
NEG_BIG = -1e30


def _cparams(sem):
    return pltpu.CompilerParams(dimension_semantics=sem, vmem_limit_bytes=VMEM_LIMIT)


def _dot(a, b):
    return jnp.dot(a, b, preferred_element_type=F32)


def _dot_nt(a, b):
    return lax.dot_general(a, b, (((1,), (1,)), ((), ())), preferred_element_type=F32)


def _dot_tn(a, b):
    return lax.dot_general(a, b, (((0,), (0,)), ((), ())), preferred_element_type=F32)


def _rms(x, g):
    return x * lax.rsqrt(jnp.mean(x * x, axis=-1, keepdims=True) + EPS) * g


def _silu(x):
    return x / (1.0 + jnp.exp(-x))


def _softplus(x):
    return jnp.maximum(x, 0.0) + jnp.log1p(jnp.exp(-jnp.abs(x)))


def _cumsum_lanes(x):
    lane = lax.broadcasted_iota(jnp.int32, x.shape, 1)
    d = 1
    while d < x.shape[1]:
        x = x + jnp.where(lane >= d, pltpu.roll(x, d, 1), 0.0)
        d *= 2
    return x


def _norm_matmul_kernel(x_ref, g_ref, w_ref, o_ref, h_ref):
    @pl.when(pl.program_id(1) == 0)
    def _():
        h_ref[...] = _rms(x_ref[...], g_ref[...]).astype(BF16)

    o_ref[...] = _dot(h_ref[...], w_ref[...]).astype(o_ref.dtype)


def norm_matmul(x, g, w, layer, out_dtype, tm, tn):
    m, d = x.shape
    n = w.shape[2]
    return pl.pallas_call(
        _norm_matmul_kernel,
        grid=(m // tm, n // tn),
        in_specs=[
            pl.BlockSpec((tm, d), lambda i, j: (i, 0)),
            pl.BlockSpec((1, d), lambda i, j: (0, 0)),
            pl.BlockSpec((None, d, tn), lambda i, j: (layer, 0, j)),
        ],
        out_specs=pl.BlockSpec((tm, tn), lambda i, j: (i, j)),
        out_shape=jax.ShapeDtypeStruct((m, n), out_dtype),
        scratch_shapes=[pltpu.VMEM((tm, d), BF16)],
        compiler_params=_cparams(("parallel", "arbitrary")),
        name="norm_matmul",
    )(x, g, w)


def _retention_kernel(q_ref, k_ref, v_ref, g_ref, cos_ref, sin_ref, o_ref, state_ref):
    c = CHUNK

    @pl.when(pl.program_id(1) == 0)
    def _():
        state_ref[...] = jnp.zeros_like(state_ref)

    cos = cos_ref[...]
    sin = sin_ref[...]
    ii = lax.broadcasted_iota(jnp.int32, (c, c), 0)
    jj = lax.broadcasted_iota(jnp.int32, (c, c), 1)
    rel = (ii - jj).astype(F32)
    i_col = lax.broadcasted_iota(jnp.int32, (c, 1), 0).astype(F32)
    for h in range(N_HEADS):
        lg = math.log1p(-(2.0 ** (-5.0 - h)))
        sl = slice(h * HEAD_DIM, (h + 1) * HEAD_DIM)
        q = q_ref[:, sl].astype(F32)
        k = k_ref[:, sl].astype(F32)
        v = v_ref[:, sl]
        qr = q * cos + pltpu.roll(q, HEAD_DIM // 2, 1) * sin
        kr = (k * cos + pltpu.roll(k, HEAD_DIM // 2, 1) * sin) * (HEAD_DIM ** -0.5)
        decay = jnp.where(rel >= 0, jnp.exp(lg * jnp.maximum(rel, 0.0)), 0.0)
        scores = _dot_nt(qr.astype(BF16), kr.astype(BF16)) * decay
        y = _dot(scores.astype(BF16), v)
        q_dec = jnp.exp(lg * (i_col + 1.0))
        k_dec = jnp.exp(lg * (c - 1.0 - i_col))
        st = state_ref[h]
        y = y + _dot((qr * q_dec).astype(BF16), st.astype(BF16))
        kv = _dot_tn((kr * k_dec).astype(BF16), v)
        state_ref[h] = math.exp(lg * c) * st + kv
        yc = y - jnp.mean(y, axis=-1, keepdims=True)
        yn = yc * lax.rsqrt(jnp.mean(yc * yc, axis=-1, keepdims=True) + EPS)
        o_ref[:, sl] = (_silu(g_ref[:, sl].astype(F32)) * yn).astype(o_ref.dtype)


def retention(p, cos, sin, batch, seq):
    n = seq // CHUNK
    base = COL_RET // MIX_W

    def col(j):
        return pl.BlockSpec((CHUNK, MIX_W), lambda b, i: (b * n + i, base + j))

    tab = pl.BlockSpec((CHUNK, HEAD_DIM), lambda b, i: (i, 0))
    return pl.pallas_call(
        _retention_kernel,
        grid=(batch, n),
        in_specs=[col(0), col(1), col(2), col(3), tab, tab],
        out_specs=pl.BlockSpec((CHUNK, MIX_W), lambda b, i: (b * n + i, 0)),
        out_shape=jax.ShapeDtypeStruct((batch * seq, MIX_W), BF16),
        scratch_shapes=[pltpu.VMEM((N_HEADS, HEAD_DIM, HEAD_DIM), F32)],
        compiler_params=_cparams(("parallel", "arbitrary")),
        name="retention",
    )(p, p, p, p, cos, sin)


def _fox_prep_kernel(sm_ref, fb_ref, fcol_ref, frow_ref, carry_ref):
    @pl.when(pl.program_id(1) == 0)
    def _():
        carry_ref[...] = jnp.zeros_like(carry_ref)

    t = sm_ref[...] + fb_ref[...]
    lf = jnp.minimum(t, 0.0) - jnp.log1p(jnp.exp(-jnp.abs(t)))
    cs = _cumsum_lanes(lf.T) + carry_ref[...]
    carry_ref[...] = cs[:, LANES - 1:LANES]
    frow_ref[0, 0] = cs[SM_F:SM_F + 8, :]
    fcol_ref[...] = cs.T


def fox_prep(sm, fb_row, batch, seq):
    n = seq // CHUNK
    return pl.pallas_call(
        _fox_prep_kernel,
        grid=(batch, n),
        in_specs=[
            pl.BlockSpec((CHUNK, SM_W), lambda b, i: (b * n + i, 0)),
            pl.BlockSpec((1, SM_W), lambda b, i: (0, 0)),
        ],
        out_specs=[
            pl.BlockSpec((CHUNK, SM_W), lambda b, i: (b * n + i, 0)),
            pl.BlockSpec((1, 1, 8, CHUNK), lambda b, i: (b, i, 0, 0)),
        ],
        out_shape=[
            jax.ShapeDtypeStruct((batch * seq, SM_W), F32),
            jax.ShapeDtypeStruct((batch, n, 8, CHUNK), F32),
        ],
        scratch_shapes=[pltpu.VMEM((SM_W, 1), F32)],
        compiler_params=_cparams(("parallel", "arbitrary")),
        name="fox_prep",
    )(sm, fb_row)


FOX_TQ = 256
FOX_TK = 128


def _fox_kernel(q_ref, k_ref, v_ref, fcol_ref, frow_ref, qg_ref, kg_ref, o_ref, kn_ref):
    i = pl.program_id(1)
    tq, tk = FOX_TQ, FOX_TK

    @pl.when(i == 0)
    def _():
        for h in range(N_HEADS):
            sl = slice(h * HEAD_DIM, (h + 1) * HEAD_DIM)
            kn_ref[:, sl] = _rms(k_ref[:, sl].astype(F32), kg_ref[...]).astype(BF16)

    row = lax.broadcasted_iota(jnp.int32, (tq, tk), 0) + i * tq
    col = lax.broadcasted_iota(jnp.int32, (tq, tk), 1)
    n_kv = (i + 1) * (tq // tk)
    for h in range(N_HEADS):
        sl = slice(h * HEAD_DIM, (h + 1) * HEAD_DIM)
        qn = (_rms(q_ref[:, sl].astype(F32), qg_ref[...]) * (HEAD_DIM ** -0.5)).astype(BF16)
        fq = fcol_ref[:, SM_F + h:SM_F + h + 1]

        def body(j, carry, sl=sl, qn=qn, fq=fq, h=h):
            m, l, acc = carry
            start = pl.multiple_of(j * tk, tk)
            ks = kn_ref[pl.ds(start, tk), sl]
            vs = v_ref[pl.ds(start, tk), sl]
            fk = frow_ref[0, j, h:h + 1, :]
            s = _dot_nt(qn, ks) + (fq - fk)
            s = jnp.where(col + j * tk <= row, s, NEG_BIG)
            m_new = jnp.maximum(m, jnp.max(s, axis=1, keepdims=True))
            p = jnp.exp(s - m_new)
            alpha = jnp.exp(m - m_new)
            l = alpha * l + jnp.sum(p, axis=1, keepdims=True)
            acc = alpha * acc + _dot(p.astype(BF16), vs)
            return m_new, l, acc

        init = (jnp.full((tq, 1), NEG_BIG, F32), jnp.zeros((tq, 1), F32), jnp.zeros((tq, HEAD_DIM), F32))
        _, l, acc = lax.fori_loop(0, n_kv, body, init)
        o_ref[:, sl] = (acc / l).astype(o_ref.dtype)


def fox_attention(p, fcol, frow, qg, kg, batch, seq):
    nq = seq // FOX_TQ
    base = COL_FOX // MIX_W
    return pl.pallas_call(
        _fox_kernel,
        grid=(batch, nq),
        in_specs=[
            pl.BlockSpec((FOX_TQ, MIX_W), lambda b, i: (b * nq + i, base)),
            pl.BlockSpec((seq, MIX_W), lambda b, i: (b, base + 1)),
            pl.BlockSpec((seq, MIX_W), lambda b, i: (b, base + 2)),
            pl.BlockSpec((FOX_TQ, SM_W), lambda b, i: (b * nq + i, 0)),
            pl.BlockSpec((1, seq // FOX_TK, 8, FOX_TK), lambda b, i: (b, 0, 0, 0)),
            pl.BlockSpec((1, HEAD_DIM), lambda b, i: (0, 0)),
            pl.BlockSpec((1, HEAD_DIM), lambda b, i: (0, 0)),
        ],
        out_specs=pl.BlockSpec((FOX_TQ, MIX_W), lambda b, i: (b * nq + i, 0)),
        out_shape=jax.ShapeDtypeStruct((batch * seq, MIX_W), BF16),
        scratch_shapes=[pltpu.VMEM((seq, MIX_W), BF16)],
        compiler_params=_cparams(("parallel", "arbitrary")),
        name="fox_attention",
    )(p, p, p, fcol, frow, qg, kg)


DSA_TQ = 128
DSA_TK = 256
BAND_W = 2 * DSA_TQ


def _t5_bucket(dist):
    max_exact = N_BUCKETS // 2
    d = jnp.maximum(dist, 0)
    log_ratio = jnp.log(jnp.maximum(d, 1).astype(F32) / max_exact) / math.log(MAX_DISTANCE / max_exact)
    large = jnp.minimum(max_exact + (log_ratio * (N_BUCKETS - max_exact)).astype(jnp.int32), N_BUCKETS - 1)
    return jnp.where(d < max_exact, d, large)


def _dsa_kernel(cq_ref, k_ref, v_ref, smq_ref, smk_ref, cqg_ref, wuq_ref, wqi_ref, qg_ref, kg_ref, rb_ref,
                o_ref, kn_ref, ki_ref, band_ref, sc_ref, qh_ref, qi_ref, m_ref, l_ref, acc_ref, *, topk):
    b = pl.program_id(0)
    i = pl.program_id(1)
    tq, tk = DSA_TQ, DSA_TK
    nkc = sc_ref.shape[0]
    seq = nkc * tk
    tiles = tk // tq
    nb = ((i + 1) * tq + tk - 1) // tk

    @pl.when(jnp.logical_and(b == 0, i == 0))
    def _():
        r = lax.broadcasted_iota(jnp.int32, (tq, BAND_W), 0)
        c = lax.broadcasted_iota(jnp.int32, (tq, BAND_W), 1)
        bucket = _t5_bucket(tq + r - c)
        for h in range(N_HEADS):
            far = rb_ref[N_BUCKETS - 1, h]
            acc = jnp.zeros((tq, BAND_W), F32)
            for bk in range(N_BUCKETS - 1):
                acc = jnp.where(bucket == bk, rb_ref[bk, h] - far, acc)
            band_ref[h] = acc

    @pl.when(i == 0)
    def _():
        kn_ref[...] = _rms(k_ref[...].astype(F32), kg_ref[...]).astype(BF16)
        ki_ref[...] = smk_ref[:, SM_IK:SM_IK + IDX_DIM].astype(BF16)

    cq = _rms(cq_ref[...].astype(F32), cqg_ref[...]).astype(BF16)
    qf = _dot(cq, wuq_ref[...])
    for h in range(N_HEADS):
        sl = slice(h * HEAD_DIM, (h + 1) * HEAD_DIM)
        qh_ref[h] = (_rms(qf[:, sl], qg_ref[...]) * (HEAD_DIM ** -0.5)).astype(BF16)
    q_idx = (_dot(cq, wqi_ref[...]) * (IDX_DIM ** -0.5)).astype(BF16)
    for h in range(IDX_HEADS):
        qi_ref[h] = q_idx[:, h * IDX_DIM:(h + 1) * IDX_DIM]

    qpos = lax.broadcasted_iota(jnp.int32, (tq, tk), 0) + i * tq
    col = lax.broadcasted_iota(jnp.int32, (tq, tk), 1)

    def score_body(j, _):
        start = pl.multiple_of(j * tk, tk)
        kj = ki_ref[pl.ds(start, tk), :]
        w_h = smq_ref[:, SM_IW:SM_IW + IDX_HEADS] * (IDX_HEADS ** -0.5)
        acc = jnp.zeros((tq, tk), F32)
        for h in range(IDX_HEADS):
            acc = acc + w_h[:, h:h + 1] * jnp.maximum(_dot_nt(qi_ref[h], kj), 0.0)
        sc_ref[j] = jnp.where(col + start <= qpos, acc, -jnp.inf)
        return 0

    lax.fori_loop(0, nb, score_body, 0)

    def over_chunks(fn, init):
        acc = init
        for j in range(nkc):
            acc = lax.cond(j < nb, functools.partial(fn, j), lambda a: a, acc)
        return acc

    def lane_tiles(x):
        return [x[:, t * tq:(t + 1) * tq] for t in range(tiles)]

    def row_total(x):
        return jnp.sum(x, axis=1, keepdims=True)

    def search():
        kf = float(topk)

        def max_fn(j, a):
            for x in lane_tiles(sc_ref[j]):
                a = jnp.maximum(a, x)
            return a

        def min_fn(j, a):
            for x in lane_tiles(sc_ref[j]):
                a = jnp.minimum(a, jnp.where(x == -jnp.inf, jnp.inf, x))
            return a

        smax = jnp.max(over_chunks(max_fn, jnp.full((tq, tq), -jnp.inf, F32)), axis=1, keepdims=True)
        smin = jnp.min(over_chunks(min_fn, jnp.full((tq, tq), jnp.inf, F32)), axis=1, keepdims=True)

        def count_ge(t):
            tb = jnp.broadcast_to(t, (tq, tq))

            def fn(j, a):
                for x in lane_tiles(sc_ref[j]):
                    a = a + jnp.where(x >= tb, 1.0, 0.0)
                return a

            return row_total(over_chunks(fn, jnp.zeros((tq, tq), F32)))

        def midpoint(lo, hi):
            return jnp.where(hi == jnp.inf, smax, 0.5 * (lo + hi))

        def undecided(lo, hi, c_lo, mid):
            return jnp.logical_and(c_lo != kf, jnp.logical_and(mid > lo, mid < hi))

        def cond(carry):
            return jnp.logical_and(carry[0] < 400, carry[1] > 0.0)

        def body(carry):
            it, _, lo, hi, c_lo, c_hi, mid = carry
            upd = undecided(lo, hi, c_lo, mid)
            cnt = count_ge(mid)
            up = jnp.logical_and(upd, cnt >= kf)
            dn = jnp.logical_and(upd, cnt < kf)
            lo = jnp.where(up, mid, lo)
            c_lo = jnp.where(up, cnt, c_lo)
            hi = jnp.where(dn, mid, hi)
            c_hi = jnp.where(dn, cnt, c_hi)
            mid = midpoint(lo, hi)
            active = jnp.max(jnp.where(undecided(lo, hi, c_lo, mid), 1.0, 0.0))
            return it + 1, active, lo, hi, c_lo, c_hi, mid

        lo0 = smin
        hi0 = jnp.full((tq, 1), jnp.inf, F32)
        c_lo0 = count_ge(lo0)
        c_hi0 = jnp.zeros((tq, 1), F32)
        mid0 = midpoint(lo0, hi0)
        act0 = jnp.max(jnp.where(undecided(lo0, hi0, c_lo0, mid0), 1.0, 0.0))
        _, _, lo, hi, c_lo, c_hi, _ = lax.while_loop(
            cond, body, (jnp.int32(0), act0, lo0, hi0, c_lo0, c_hi0, mid0))

        def tie_search():
            need = kf - c_hi
            lo_b = jnp.broadcast_to(lo, (tq, tq))
            hi_b = jnp.broadcast_to(hi, (tq, tq))
            lane = lax.broadcasted_iota(jnp.int32, (tq, tq), 1)

            def tie_body(_, carry):
                jlo, jhi = carry
                jm = (jlo + jhi) // 2
                jm_b = jnp.broadcast_to(jm, (tq, tq))

                def fn(j, a):
                    for t, x in enumerate(lane_tiles(sc_ref[j])):
                        hit = jnp.logical_and(jnp.logical_and(x >= lo_b, x < hi_b), lane + (j * tk + t * tq) <= jm_b)
                        a = a + jnp.where(hit, 1.0, 0.0)
                    return a

                ok = row_total(over_chunks(fn, jnp.zeros((tq, tq), F32))) >= need
                return jnp.where(ok, jlo, jm), jnp.where(ok, jm, jhi)

            n_bits = int(math.ceil(math.log2(seq))) + 1
            _, jmax = lax.fori_loop(0, n_bits, tie_body,
                                    (jnp.full((tq, 1), -1, jnp.int32), jnp.full((tq, 1), seq - 1, jnp.int32)))
            return jmax

        any_tie = jnp.max(jnp.where(c_lo != kf, 1.0, 0.0)) > 0.0
        jmax = lax.cond(any_tie, tie_search, lambda: jnp.full((tq, 1), seq - 1, jnp.int32))
        return lo, hi, jmax

    def keep_all():
        return (jnp.full((tq, 1), -jnp.inf, F32), jnp.full((tq, 1), jnp.inf, F32),
                jnp.full((tq, 1), seq - 1, jnp.int32))

    lo, hi, jmax = lax.cond((i + 1) * tq > topk, search, keep_all)

    m_ref[...] = jnp.full(m_ref.shape, NEG_BIG, F32)
    l_ref[...] = jnp.zeros(l_ref.shape, F32)
    acc_ref[...] = jnp.zeros(acc_ref.shape, F32)

    def attend_body(j, _):
        start = pl.multiple_of(j * tk, tk)
        ks = kn_ref[pl.ds(start, tk), :]
        vs = v_ref[pl.ds(start, tk), :]
        sc = sc_ref[j]
        kpos = col + start
        keep = jnp.logical_or(sc >= hi, jnp.logical_and(sc >= lo, kpos <= jmax))
        keep = jnp.logical_and(keep, kpos <= qpos)
        for h in range(N_HEADS):
            bias = jnp.concatenate(
                [jnp.where(j * tiles + t == i, band_ref[h, :, tq:2 * tq],
                           jnp.where(j * tiles + t == i - 1, band_ref[h, :, 0:tq], 0.0))
                 for t in range(tiles)], axis=1)
            s = jnp.where(keep, _dot_nt(qh_ref[h], ks) + bias, NEG_BIG)
            m_old = m_ref[h]
            m_new = jnp.maximum(m_old, jnp.max(s, axis=1, keepdims=True))
            p = jnp.exp(s - m_new)
            alpha = jnp.exp(m_old - m_new)
            l_ref[h] = alpha * l_ref[h] + jnp.sum(p, axis=1, keepdims=True)
            acc_ref[h] = alpha * acc_ref[h] + _dot(p.astype(BF16), vs)
            m_ref[h] = m_new
        return 0

    lax.fori_loop(0, nb, attend_body, 0)
    for h in range(N_HEADS):
        o_ref[:, h * HEAD_DIM:(h + 1) * HEAD_DIM] = (acc_ref[h] / l_ref[h]).astype(o_ref.dtype)


def dsa_attention(p, sm, cqg, wuq, wqi, qg, kg, rel_bias, batch, seq):
    nq = seq // DSA_TQ
    nkb = seq // DSA_TK
    topk = min(DSA_TOPK, seq // 4)
    kern = functools.partial(_dsa_kernel, topk=topk)
    return pl.pallas_call(
        kern,
        grid=(batch, nq),
        in_specs=[
            pl.BlockSpec((DSA_TQ, DSA_Q_RANK), lambda b, i: (b * nq + i, COL_CQ // DSA_Q_RANK)),
            pl.BlockSpec((seq, HEAD_DIM), lambda b, i: (b, COL_DK // HEAD_DIM)),
            pl.BlockSpec((seq, HEAD_DIM), lambda b, i: (b, COL_DV // HEAD_DIM)),
            pl.BlockSpec((DSA_TQ, SM_W), lambda b, i: (b * nq + i, 0)),
            pl.BlockSpec((seq, SM_W), lambda b, i: (b, 0)),
            pl.BlockSpec((1, DSA_Q_RANK), lambda b, i: (0, 0)),
            pl.BlockSpec((DSA_Q_RANK, N_HEADS * HEAD_DIM), lambda b, i: (0, 0)),
            pl.BlockSpec((DSA_Q_RANK, IDX_HEADS * IDX_DIM), lambda b, i: (0, 0)),
            pl.BlockSpec((1, HEAD_DIM), lambda b, i: (0, 0)),
            pl.BlockSpec((1, HEAD_DIM), lambda b, i: (0, 0)),
            pl.BlockSpec(memory_space=pltpu.SMEM),
        ],
        out_specs=pl.BlockSpec((DSA_TQ, MIX_W), lambda b, i: (b * nq + i, 0)),
        out_shape=jax.ShapeDtypeStruct((batch * seq, MIX_W), BF16),
        scratch_shapes=[
            pltpu.VMEM((seq, HEAD_DIM), BF16),
            pltpu.VMEM((seq, IDX_DIM), BF16),
            pltpu.VMEM((N_HEADS, DSA_TQ, BAND_W), F32),
            pltpu.VMEM((nkb, DSA_TQ, DSA_TK), F32),
            pltpu.VMEM((N_HEADS, DSA_TQ, HEAD_DIM), BF16),
            pltpu.VMEM((IDX_HEADS, DSA_TQ, IDX_DIM), BF16),
            pltpu.VMEM((N_HEADS, DSA_TQ, 1), F32),
            pltpu.VMEM((N_HEADS, DSA_TQ, 1), F32),
            pltpu.VMEM((N_HEADS, DSA_TQ, HEAD_DIM), F32),
        ],
        compiler_params=_cparams(("arbitrary", "arbitrary")),
        name="dsa_attention",
    )(p, p, p, sm, sm, cqg, wuq, wqi, qg, kg, rel_bias)


def _causal_conv(x, xp, w_ref, b_ref):
    rows = lax.broadcasted_iota(jnp.int32, x.shape, 0)
    acc = x * w_ref[SSD_CONV - 1:SSD_CONV, :] + b_ref[...]
    for d in range(1, SSD_CONV):
        shifted = jnp.where(rows < d, pltpu.roll(xp, d, 0), pltpu.roll(x, d, 0))
        acc = acc + shifted * w_ref[SSD_CONV - 1 - d:SSD_CONV - d, :]
    return _silu(acc)


def _ssd_kernel(z_ref, xs_ref, bc_ref, xsp_ref, bcp_ref, sm_ref, cwx_ref, cbx_ref, cwb_ref, cbb_ref,
                dtb_ref, alog_ref, dsk_ref, ng_ref, o_ref, prev_ref, y_ref):
    c = CHUNK
    n = pl.program_id(1)

    @pl.when(n == 0)
    def _():
        prev_ref[...] = jnp.zeros_like(prev_ref)

    first = (n > 0).astype(F32)
    xs = _causal_conv(xs_ref[...].astype(F32), xsp_ref[...].astype(F32) * first, cwx_ref, cbx_ref)
    bc = _causal_conv(bc_ref[...].astype(F32), bcp_ref[...].astype(F32) * first, cwb_ref, cbb_ref)

    dt_t = _softplus(sm_ref[...].T + dtb_ref[...])
    cs_t = _cumsum_lanes(dt_t * (-jnp.exp(alog_ref[...])))
    cs = cs_t.T
    dt = dt_t.T
    ii = lax.broadcasted_iota(jnp.int32, (c, c), 0)
    jj = lax.broadcasted_iota(jnp.int32, (c, c), 1)
    tril = ii >= jj
    gn = SSD_GROUPS * SSD_STATE
    hpg = SSD_HEADS // SSD_GROUPS
    for g in range(SSD_GROUPS):
        bg = bc[:, g * SSD_STATE:(g + 1) * SSD_STATE]
        cg = bc[:, gn + g * SSD_STATE:gn + (g + 1) * SSD_STATE].astype(BF16)
        cb = _dot_nt(cg, bg.astype(BF16))
        y_off = _dot(cg, prev_ref[g].astype(BF16))
        for r in range(hpg):
            h = g * hpg + r
            hs = slice(h * SSD_HEAD_DIM, (h + 1) * SSD_HEAD_DIM)
            rs = slice(r * SSD_HEAD_DIM, (r + 1) * SSD_HEAD_DIM)
            a_col = cs[:, SM_DT + h:SM_DT + h + 1]
            a_row = cs_t[SM_DT + h:SM_DT + h + 1, :]
            last = cs_t[SM_DT + h:SM_DT + h + 1, c - 1:c]
            seg = jnp.where(tril, jnp.exp(jnp.where(tril, a_col - a_row, 0.0)), 0.0)
            xh = xs[:, hs]
            xc = (xh * dt[:, SM_DT + h:SM_DT + h + 1]).astype(BF16)
            y_diag = _dot((cb * seg).astype(BF16), xc)
            st = _dot_tn((bg * jnp.exp(last - a_col)).astype(BF16), xc)
            y_ref[:, hs] = y_diag + y_off[:, rs] * jnp.exp(a_col) + dsk_ref[:, hs] * xh
            prev_ref[g, :, rs] = jnp.exp(last) * prev_ref[g, :, rs] + st
    gated = y_ref[...] * _silu(z_ref[...].astype(F32))
    gw = SSD_INNER // SSD_GROUPS
    for g in range(SSD_GROUPS):
        sl = slice(g * gw, (g + 1) * gw)
        o_ref[:, sl] = _rms(gated[:, sl], ng_ref[:, sl]).astype(o_ref.dtype)


def ssd_mixer(p, sm, cw, cb, dtb_col, alog_col, dskip_row, ng, batch, seq):
    n = seq // CHUNK
    bcw = 2 * SSD_GROUPS * SSD_STATE

    def cur(width, colbase):
        return pl.BlockSpec((CHUNK, width), lambda b, i: (b * n + i, colbase // width))

    def prv(width, colbase):
        return pl.BlockSpec((CHUNK, width), lambda b, i: (b * n + jnp.maximum(i - 1, 0), colbase // width))

    def const(shape):
        return pl.BlockSpec(shape, lambda b, i: (0, 0))

    return pl.pallas_call(
        _ssd_kernel,
        grid=(batch, n),
        in_specs=[
            cur(SSD_INNER, COL_Z), cur(SSD_INNER, COL_XS), cur(bcw, COL_BC),
            prv(SSD_INNER, COL_XS), prv(bcw, COL_BC),
            pl.BlockSpec((CHUNK, SM_W), lambda b, i: (b * n + i, 0)),
            const((SSD_CONV, SSD_INNER)), const((1, SSD_INNER)),
            const((SSD_CONV, bcw)), const((1, bcw)),
            const((SM_W, 1)), const((SM_W, 1)),
            const((1, SSD_INNER)), const((1, SSD_INNER)),
        ],
        out_specs=pl.BlockSpec((CHUNK, SSD_INNER), lambda b, i: (b * n + i, 0)),
        out_shape=jax.ShapeDtypeStruct((batch * seq, SSD_INNER), BF16),
        scratch_shapes=[
            pltpu.VMEM((SSD_GROUPS, SSD_STATE, SSD_INNER // SSD_GROUPS), F32),
            pltpu.VMEM((CHUNK, SSD_INNER), F32),
        ],
        compiler_params=_cparams(("parallel", "arbitrary")),
        name="ssd_mixer",
    )(p, p, p, p, p, sm, cw[:, :SSD_INNER], cb[:, :SSD_INNER], cw[:, SSD_INNER:], cb[:, SSD_INNER:],
      dtb_col, alog_col, dskip_row, ng)


MERGE_TM = 256


def _merge_kernel(x_ref, gl_ref, gb_ref, oret_ref, ofox_ref, odsa_ref, ossd_ref, wbr_ref, wout_ref, o_ref):
    branches = (oret_ref, ofox_ref, odsa_ref, ossd_ref)
    merged = None
    row0 = 0
    for bi, br in enumerate(branches):
        width = br.shape[1]
        sl = slice(bi * D_MODEL, (bi + 1) * D_MODEL)
        gate = 1.0 / (1.0 + jnp.exp(-(gl_ref[:, sl].astype(F32) + gb_ref[:, sl])))
        term = gate * _dot(br[...], wbr_ref[row0:row0 + width, :])
        merged = term if merged is None else merged + term
        row0 += width
    o_ref[...] = x_ref[...] + _dot(merged.astype(BF16), wout_ref[...])


def merge_project(x, p, gate_b, o_ret, o_fox, o_dsa, o_ssd, w_br, w_out):
    m = x.shape[0]
    tm = MERGE_TM

    def rows(width):
        return pl.BlockSpec((tm, width), lambda i: (i, 0))

    def const(shape):
        return pl.BlockSpec(shape, lambda i: (0, 0), pipeline_mode=pl.Buffered(1))

    return pl.pallas_call(
        _merge_kernel,
        grid=(m // tm,),
        in_specs=[
            rows(D_MODEL), rows(N_BRANCH * D_MODEL), const((1, N_BRANCH * D_MODEL)),
            rows(MIX_W), rows(MIX_W), rows(MIX_W), rows(SSD_INNER),
            const(w_br.shape), const(w_out.shape),
        ],
        out_specs=rows(D_MODEL),
        out_shape=jax.ShapeDtypeStruct(x.shape, x.dtype),
        compiler_params=_cparams(("parallel",)),
        name="merge_project",
    )(x, p, gate_b, o_ret, o_fox, o_dsa, o_ssd, w_br, w_out)


FFN_TM = 1024
FFN_TF = 512


def _ffn_kernel(x_ref, g_ref, w1_ref, w2_ref, o_ref, h_ref):
    @pl.when(pl.program_id(1) == 0)
    def _():
        h_ref[...] = _rms(x_ref[...], g_ref[...]).astype(BF16)
        o_ref[...] = x_ref[...]

    a = jnp.maximum(_dot(h_ref[...], w1_ref[...]), 0.0)
    o_ref[...] += _dot((a * a).astype(BF16), w2_ref[...])


def ffn(x, g, w1, w2):
    m, d = x.shape
    dff = w1.shape[1]
    tm, tf = min(FFN_TM, m), FFN_TF
    return pl.pallas_call(
        _ffn_kernel,
        grid=(m // tm, dff // tf),
        in_specs=[
            pl.BlockSpec((tm, d), lambda i, f: (i, 0), pipeline_mode=pl.Buffered(1)),
            pl.BlockSpec((1, d), lambda i, f: (0, 0)),
            pl.BlockSpec((d, tf), lambda i, f: (0, f)),
            pl.BlockSpec((tf, d), lambda i, f: (f, 0)),
        ],
        out_specs=pl.BlockSpec((tm, d), lambda i, f: (i, 0)),
        out_shape=jax.ShapeDtypeStruct(x.shape, x.dtype),
        scratch_shapes=[pltpu.VMEM((tm, d), BF16)],
        compiler_params=_cparams(("parallel", "arbitrary")),
        name="ffn",
    )(x, g, w1, w2)


SRC_RET = 0
SRC_FOX = SRC_RET + 4 * MIX_W
SRC_FF = SRC_FOX + 3 * MIX_W
SRC_CQ = SRC_FF + N_HEADS
SRC_DK = SRC_CQ + DSA_Q_RANK
SRC_IK = SRC_DK + 2 * HEAD_DIM
SRC_IW = SRC_IK + IDX_DIM
SRC_Z = SRC_IW + IDX_HEADS
SRC_DT = SRC_Z + 2 * SSD_INNER + 2 * SSD_GROUPS * SSD_STATE
SRC_GATE = SRC_DT + SSD_HEADS
IN_TOTAL = SRC_GATE + N_BRANCH * D_MODEL
MAIN_RUNS = ((COL_GATE, SRC_GATE), (COL_RET, SRC_RET), (COL_Z, SRC_Z), (COL_CQ, SRC_CQ),
             (COL_FOX, SRC_FOX), (COL_DK, SRC_DK))
RELAYOUT_W = 512
RELAYOUT_TILES = RELAYOUT_W // LANES


def _relayout_tables():
    starts, shifts = [], []
    for blk in range(N_MAIN // RELAYOUT_W):
        o = blk * RELAYOUT_W
        dst, src = [r for r in MAIN_RUNS if r[0] <= o][-1]
        col = src + (o - dst)
        starts.append(col // LANES)
        shifts.append(col % LANES)
    return jnp.asarray(starts, jnp.int32), jnp.asarray(shifts, jnp.int32)


def _relayout_kernel(start_ref, shift_ref, *refs):
    del start_ref
    tiles, o_ref = refs[:-1], refs[-1]
    shift = shift_ref[pl.program_id(1)]
    amount = lax.rem(LANES - shift, LANES)
    lane = lax.broadcasted_iota(jnp.int32, tiles[0].shape, 1)
    rolled = [pltpu.roll(t[...], amount, 1) for t in tiles]
    for k in range(RELAYOUT_TILES):
        piece = jnp.where(lane < LANES - shift, rolled[k], rolled[k + 1])
        o_ref[:, k * LANES:(k + 1) * LANES] = piece.astype(o_ref.dtype)


def relayout_main(w_in):
    depth, d, n_src = w_in.shape
    last = (n_src - 1) // LANES
    starts, shifts = _relayout_tables()

    def tile(k):
        return pl.BlockSpec((None, d, LANES), lambda l, b, st, sh: (l, 0, jnp.minimum(st[b] + k, last)))

    return pl.pallas_call(
        _relayout_kernel,
        grid_spec=pltpu.PrefetchScalarGridSpec(
            num_scalar_prefetch=2,
            grid=(depth, N_MAIN // RELAYOUT_W),
            in_specs=[tile(k) for k in range(RELAYOUT_TILES + 1)],
            out_specs=pl.BlockSpec((None, d, RELAYOUT_W), lambda l, b, st, sh: (l, 0, b)),
        ),
        out_shape=jax.ShapeDtypeStruct((depth, d, N_MAIN), BF16),
        compiler_params=_cparams(("parallel", "arbitrary")),
        name="relayout_main",
    )(starts, shifts, *([w_in] * (RELAYOUT_TILES + 1)))


SMALL_PIECES = ((SRC_DT, SM_DT, SSD_HEADS), (SRC_FF, SM_F, N_HEADS), (SRC_IW, SM_IW, IDX_HEADS),
                (SRC_IK, SM_IK, IDX_DIM))


def _relayout_small_kernel(*refs):
    tiles, o_ref = refs[:-1], refs[-1]
    lane = lax.broadcasted_iota(jnp.int32, o_ref.shape, 1)
    out = jnp.zeros(o_ref.shape, F32)
    for t, (src, dst, width) in zip(tiles, SMALL_PIECES):
        moved = pltpu.roll(t[...], (dst - src % LANES) % LANES, 1)
        out = jnp.where(jnp.logical_and(lane >= dst, lane < dst + width), moved, out)
    o_ref[...] = out.astype(o_ref.dtype)


def relayout_small(w_in):
    depth, d, _ = w_in.shape
    for src, _, width in SMALL_PIECES:
        assert src // LANES == (src + width - 1) // LANES

    def tile(src):
        return pl.BlockSpec((None, d, LANES), lambda l: (l, 0, src // LANES))

    return pl.pallas_call(
        _relayout_small_kernel,
        grid=(depth,),
        in_specs=[tile(src) for src, _, _ in SMALL_PIECES],
        out_specs=pl.BlockSpec((None, d, SM_W), lambda l: (l, 0, 0)),
        out_shape=jax.ShapeDtypeStruct((depth, d, SM_W), BF16),
        compiler_params=_cparams(("parallel",)),
        name="relayout_small",
    )(*([w_in] * len(SMALL_PIECES)))


def _pad_to(v, offset, total):
    return jnp.zeros((total,), v.dtype).at[offset:offset + v.shape[0]].set(v)


def _rotary_tables(seq):
    half = HEAD_DIM // 2
    inv = 1.0 / (10000.0 ** (jnp.arange(half, dtype=F32) / half))
    ang = jnp.arange(seq, dtype=F32)[:, None] * inv[None, :]
    cos, sin = jnp.cos(ang), jnp.sin(ang)
    return jnp.concatenate([cos, cos], axis=1), jnp.concatenate([-sin, sin], axis=1)


def kernel(x, norm1_g, w_in, gate_b, fox_f_b, fox_qn_g, fox_kn_g, dsa_cq_g, dsa_w_uq, dsa_w_qidx, dsa_qn_g,
           dsa_kn_g, rel_bias, ssd_conv_w, ssd_conv_b, ssd_dt_bias, ssd_a_log, ssd_d, ssd_norm_g, w_br, w_out,
           norm2_g, w_ff1, w_ff2):
    batch, seq, d = x.shape
    tokens = batch * seq
    xt = x.reshape(tokens, d)
    cos, sin = _rotary_tables(seq)
    tm = min(1024, tokens)
    w_main = relayout_main(w_in)
    w_small = relayout_small(w_in)
    for l in range(DEPTH):
        g1 = norm1_g[l][None, :]
        p = norm_matmul(xt, g1, w_main, l, BF16, tm, 1024)
        sm = norm_matmul(xt, g1, w_small, l, F32, tm, SM_W)

        o_ret = retention(p, cos, sin, batch, seq)

        fb_row = _pad_to(fox_f_b[l], SM_F, SM_W)[None, :]
        fcol, frow = fox_prep(sm, fb_row, batch, seq)
        o_fox = fox_attention(p, fcol, frow, fox_qn_g[l][None, :], fox_kn_g[l][None, :], batch, seq)

        o_dsa = dsa_attention(p, sm, dsa_cq_g[l][None, :], dsa_w_uq[l].astype(BF16), dsa_w_qidx[l].astype(BF16),
                              dsa_qn_g[l][None, :], dsa_kn_g[l][None, :], rel_bias, batch, seq)

        o_ssd = ssd_mixer(p, sm, ssd_conv_w[l], ssd_conv_b[l][None, :],
                          _pad_to(ssd_dt_bias[l], SM_DT, SM_W)[:, None], _pad_to(ssd_a_log[l], SM_DT, SM_W)[:, None],
                          jnp.repeat(ssd_d[l], SSD_HEAD_DIM)[None, :], ssd_norm_g[l][None, :], batch, seq)

        xt = merge_project(xt, p, gate_b[l][None, :], o_ret, o_fox, o_dsa, o_ssd,
                           w_br[l].astype(BF16), w_out[l].astype(BF16))
        xt = ffn(xt, norm2_g[l][None, :], w_ff1[l].astype(BF16), w_ff2[l].astype(BF16))
    return xt.reshape(batch, seq, d)
```

```python
import functools
import math

import jax
import jax.numpy as jnp
from jax import lax
from jax.experimental import pallas as pl
from jax.experimental.pallas import tpu as pltpu

F32 = jnp.float32
BF16 = jnp.bfloat16

D_MODEL = 2048
DEPTH = 4
HEAD_DIM = 128
N_HEADS = 4
DSA_Q_RANK = 512
IDX_HEADS = 16
IDX_DIM = 64
DSA_TOPK = 256
SSD_HEADS = 16
SSD_HEAD_DIM = 64
SSD_GROUPS = 2
SSD_STATE = 128
SSD_CONV = 4
SSD_INNER = SSD_HEADS * SSD_HEAD_DIM
D_FF = 4 * D_MODEL
N_BUCKETS = 32
MAX_DISTANCE = 128
CHUNK = 128
EPS = 1e-6
N_BRANCH = 4
MIX_W = N_HEADS * HEAD_DIM

COL_GATE = 0
COL_RET = COL_GATE + N_BRANCH * D_MODEL
COL_Z = COL_RET + 4 * MIX_W
COL_XS = COL_Z + SSD_INNER
COL_BC = COL_XS + SSD_INNER
COL_CQ = COL_BC + 2 * SSD_GROUPS * SSD_STATE
COL_FOX = COL_CQ + DSA_Q_RANK
COL_DK = COL_FOX + 3 * MIX_W
COL_DV = COL_DK + HEAD_DIM
N_MAIN_USED = COL_DV + HEAD_DIM
N_MAIN = 15360
SM_DT = 0
SM_F = 16
SM_IW = 32
SM_IK = 64
SM_W = 128

LANES = 128
VMEM_LIMIT = 56 * 1024 * 1024
NEG_BIG = -1e30


def _cparams(sem):
    return pltpu.CompilerParams(dimension_semantics=sem, vmem_limit_bytes=VMEM_LIMIT)


def _dot(a, b):
    return jnp.dot(a, b, preferred_element_type=F32)


def _dot_nt(a, b):
    return lax.dot_general(a, b, (((1,), (1,)), ((), ())), preferred_element_type=F32)


def _dot_tn(a, b):
    return lax.dot_general(a, b, (((0,), (0,)), ((), ())), preferred_element_type=F32)


def _rms(x, g):
    return x * lax.rsqrt(jnp.mean(x * x, axis=-1, keepdims=True) + EPS) * g


def _silu(x):
    return x / (1.0 + jnp.exp(-x))


def _softplus(x):
    return jnp.maximum(x, 0.0) + jnp.log1p(jnp.exp(-jnp.abs(x)))


def _cumsum_lanes(x):
    lane = lax.broadcasted_iota(jnp.int32, x.shape, 1)
    d = 1
    while d < x.shape[1]:
        x = x + jnp.where(lane >= d, pltpu.roll(x, d, 1), 0.0)
        d *= 2
    return x


def _norm_matmul_kernel(x_ref, g_ref, w_ref, o_ref, h_ref):
    @pl.when(pl.program_id(1) == 0)
    def _():
        h_ref[...] = _rms(x_ref[...], g_ref[...]).astype(BF16)

    o_ref[...] = _dot(h_ref[...], w_ref[...]).astype(o_ref.dtype)


def norm_matmul(x, g, w, layer, out_dtype, tm, tn):
    m, d = x.shape
    n = w.shape[2]
    return pl.pallas_call(
        _norm_matmul_kernel,
        grid=(m // tm, n // tn),
        in_specs=[
            pl.BlockSpec((tm, d), lambda i, j: (i, 0)),
            pl.BlockSpec((1, d), lambda i, j: (0, 0)),
            pl.BlockSpec((None, d, tn), lambda i, j: (layer, 0, j)),
        ],
        out_specs=pl.BlockSpec((tm, tn), lambda i, j: (i, j)),
        out_shape=jax.ShapeDtypeStruct((m, n), out_dtype),
        scratch_shapes=[pltpu.VMEM((tm, d), BF16)],
        compiler_params=_cparams(("parallel", "arbitrary")),
        name="norm_matmul",
    )(x, g, w)


def _retention_kernel(q_ref, k_ref, v_ref, g_ref, cos_ref, sin_ref, o_ref, state_ref):
    c = CHUNK

    @pl.when(pl.program_id(1) == 0)
    def _():
        state_ref[...] = jnp.zeros_like(state_ref)

    cos = cos_ref[...]
    sin = sin_ref[...]
    ii = lax.broadcasted_iota(jnp.int32, (c, c), 0)
    jj = lax.broadcasted_iota(jnp.int32, (c, c), 1)
    rel = (ii - jj).astype(F32)
    i_col = lax.broadcasted_iota(jnp.int32, (c, 1), 0).astype(F32)
    for h in range(N_HEADS):
        lg = math.log1p(-(2.0 ** (-5.0 - h)))
        sl = slice(h * HEAD_DIM, (h + 1) * HEAD_DIM)
        q = q_ref[:, sl].astype(F32)
        k = k_ref[:, sl].astype(F32)
        v = v_ref[:, sl]
        qr = q * cos + pltpu.roll(q, HEAD_DIM // 2, 1) * sin
        kr = (k * cos + pltpu.roll(k, HEAD_DIM // 2, 1) * sin) * (HEAD_DIM ** -0.5)
        decay = jnp.where(rel >= 0, jnp.exp(lg * jnp.maximum(rel, 0.0)), 0.0)
        scores = _dot_nt(qr.astype(BF16), kr.astype(BF16)) * decay
        y = _dot(scores.astype(BF16), v)
        q_dec = jnp.exp(lg * (i_col + 1.0))
        k_dec = jnp.exp(lg * (c - 1.0 - i_col))
        st = state_ref[h]
        y = y + _dot((qr * q_dec).astype(BF16), st.astype(BF16))
        kv = _dot_tn((kr * k_dec).astype(BF16), v)
        state_ref[h] = math.exp(lg * c) * st + kv
        yc = y - jnp.mean(y, axis=-1, keepdims=True)
        yn = yc * lax.rsqrt(jnp.mean(yc * yc, axis=-1, keepdims=True) + EPS)
        o_ref[:, sl] = (_silu(g_ref[:, sl].astype(F32)) * yn).astype(o_ref.dtype)


def retention(p, cos, sin, batch, seq):
    n = seq // CHUNK
    base = COL_RET // MIX_W

    def col(j):
        return pl.BlockSpec((CHUNK, MIX_W), lambda b, i: (b * n + i, base + j))

    tab = pl.BlockSpec((CHUNK, HEAD_DIM), lambda b, i: (i, 0))
    return pl.pallas_call(
        _retention_kernel,
        grid=(batch, n),
        in_specs=[col(0), col(1), col(2), col(3), tab, tab],
        out_specs=pl.BlockSpec((CHUNK, MIX_W), lambda b, i: (b * n + i, 0)),
        out_shape=jax.ShapeDtypeStruct((batch * seq, MIX_W), BF16),
        scratch_shapes=[pltpu.VMEM((N_HEADS, HEAD_DIM, HEAD_DIM), F32)],
        compiler_params=_cparams(("parallel", "arbitrary")),
        name="retention",
    )(p, p, p, p, cos, sin)


def _fox_prep_kernel(sm_ref, fb_ref, fcol_ref, frow_ref, carry_ref):
    @pl.when(pl.program_id(1) == 0)
    def _():
        carry_ref[...] = jnp.zeros_like(carry_ref)

    t = sm_ref[...] + fb_ref[...]
    lf = jnp.minimum(t, 0.0) - jnp.log1p(jnp.exp(-jnp.abs(t)))
    cs = _cumsum_lanes(lf.T) + carry_ref[...]
    carry_ref[...] = cs[:, LANES - 1:LANES]
    frow_ref[0, 0] = cs[SM_F:SM_F + 8, :]
    fcol_ref[...] = cs.T


def fox_prep(sm, fb_row, batch, seq):
    n = seq // CHUNK
    return pl.pallas_call(
        _fox_prep_kernel,
        grid=(batch, n),
        in_specs=[
            pl.BlockSpec((CHUNK, SM_W), lambda b, i: (b * n + i, 0)),
            pl.BlockSpec((1, SM_W), lambda b, i: (0, 0)),
        ],
        out_specs=[
            pl.BlockSpec((CHUNK, SM_W), lambda b, i: (b * n + i, 0)),
            pl.BlockSpec((1, 1, 8, CHUNK), lambda b, i: (b, i, 0, 0)),
        ],
        out_shape=[
            jax.ShapeDtypeStruct((batch * seq, SM_W), F32),
            jax.ShapeDtypeStruct((batch, n, 8, CHUNK), F32),
        ],
        scratch_shapes=[pltpu.VMEM((SM_W, 1), F32)],
        compiler_params=_cparams(("parallel", "arbitrary")),
        name="fox_prep",
    )(sm, fb_row)


FOX_TQ = 256
FOX_TK = 128


def _fox_kernel(q_ref, k_ref, v_ref, fcol_ref, frow_ref, qg_ref, kg_ref, o_ref, kn_ref):
    i = pl.program_id(1)
    tq, tk = FOX_TQ, FOX_TK

    @pl.when(i == 0)
    def _():
        for h in range(N_HEADS):
            sl = slice(h * HEAD_DIM, (h + 1) * HEAD_DIM)
            kn_ref[:, sl] = _rms(k_ref[:, sl].astype(F32), kg_ref[...]).astype(BF16)

    row = lax.broadcasted_iota(jnp.int32, (tq, tk), 0) + i * tq
    col = lax.broadcasted_iota(jnp.int32, (tq, tk), 1)
    n_kv = (i + 1) * (tq // tk)
    for h in range(N_HEADS):
        sl = slice(h * HEAD_DIM, (h + 1) * HEAD_DIM)
        qn = (_rms(q_ref[:, sl].astype(F32), qg_ref[...]) * (HEAD_DIM ** -0.5)).astype(BF16)
        fq = fcol_ref[:, SM_F + h:SM_F + h + 1]

        def body(j, carry, sl=sl, qn=qn, fq=fq, h=h):
            m, l, acc = carry
            start = pl.multiple_of(j * tk, tk)
            ks = kn_ref[pl.ds(start, tk), sl]
            vs = v_ref[pl.ds(start, tk), sl]
            fk = frow_ref[0, j, h:h + 1, :]
            s = _dot_nt(qn, ks) + (fq - fk)
            s = jnp.where(col + j * tk <= row, s, NEG_BIG)
            m_new = jnp.maximum(m, jnp.max(s, axis=1, keepdims=True))
            p = jnp.exp(s - m_new)
            alpha = jnp.exp(m - m_new)
            l = alpha * l + jnp.sum(p, axis=1, keepdims=True)
            acc = alpha * acc + _dot(p.astype(BF16), vs)
            return m_new, l, acc

        init = (jnp.full((tq, 1), NEG_BIG, F32), jnp.zeros((tq, 1), F32), jnp.zeros((tq, HEAD_DIM), F32))
        _, l, acc = lax.fori_loop(0, n_kv, body, init)
        o_ref[:, sl] = (acc / l).astype(o_ref.dtype)


def fox_attention(p, fcol, frow, qg, kg, batch, seq):
    nq = seq // FOX_TQ
    base = COL_FOX // MIX_W
    return pl.pallas_call(
        _fox_kernel,
        grid=(batch, nq),
        in_specs=[
            pl.BlockSpec((FOX_TQ, MIX_W), lambda b, i: (b * nq + i, base)),
            pl.BlockSpec((seq, MIX_W), lambda b, i: (b, base + 1)),
            pl.BlockSpec((seq, MIX_W), lambda b, i: (b, base + 2)),
            pl.BlockSpec((FOX_TQ, SM_W), lambda b, i: (b * nq + i, 0)),
            pl.BlockSpec((1, seq // FOX_TK, 8, FOX_TK), lambda b, i: (b, 0, 0, 0)),
            pl.BlockSpec((1, HEAD_DIM), lambda b, i: (0, 0)),
            pl.BlockSpec((1, HEAD_DIM), lambda b, i: (0, 0)),
        ],
        out_specs=pl.BlockSpec((FOX_TQ, MIX_W), lambda b, i: (b * nq + i, 0)),
        out_shape=jax.ShapeDtypeStruct((batch * seq, MIX_W), BF16),
        scratch_shapes=[pltpu.VMEM((seq, MIX_W), BF16)],
        compiler_params=_cparams(("parallel", "arbitrary")),
        name="fox_attention",
    )(p, p, p, fcol, frow, qg, kg)


DSA_TQ = 128
DSA_TK = 256
BAND_W = 2 * DSA_TQ


def _t5_bucket(dist):
    max_exact = N_BUCKETS // 2
    d = jnp.maximum(dist, 0)
    log_ratio = jnp.log(jnp.maximum(d, 1).astype(F32) / max_exact) / math.log(MAX_DISTANCE / max_exact)
    large = jnp.minimum(max_exact + (log_ratio * (N_BUCKETS - max_exact)).astype(jnp.int32), N_BUCKETS - 1)
    return jnp.where(d < max_exact, d, large)


def _dsa_kernel(cq_ref, k_ref, v_ref, smq_ref, smk_ref, cqg_ref, wuq_ref, wqi_ref, qg_ref, kg_ref, rb_ref,
                o_ref, kn_ref, ki_ref, band_ref, sc_ref, qh_ref, qi_ref, m_ref, l_ref, acc_ref, *, topk):
    b = pl.program_id(0)
    i = pl.program_id(1)
    tq, tk = DSA_TQ, DSA_TK
    nkc = sc_ref.shape[0]
    seq = nkc * tk
    tiles = tk // tq
    nb = ((i + 1) * tq + tk - 1) // tk

    @pl.when(jnp.logical_and(b == 0, i == 0))
    def _():
        r = lax.broadcasted_iota(jnp.int32, (tq, BAND_W), 0)
        c = lax.broadcasted_iota(jnp.int32, (tq, BAND_W), 1)
        bucket = _t5_bucket(tq + r - c)
        for h in range(N_HEADS):
            far = rb_ref[N_BUCKETS - 1, h]
            acc = jnp.zeros((tq, BAND_W), F32)
            for bk in range(N_BUCKETS - 1):
                acc = jnp.where(bucket == bk, rb_ref[bk, h] - far, acc)
            band_ref[h] = acc

    @pl.when(i == 0)
    def _():
        kn_ref[...] = _rms(k_ref[...].astype(F32), kg_ref[...]).astype(BF16)
        ki_ref[...] = smk_ref[:, SM_IK:SM_IK + IDX_DIM].astype(BF16)

    cq = _rms(cq_ref[...].astype(F32), cqg_ref[...]).astype(BF16)
    qf = _dot(cq, wuq_ref[...])
    for h in range(N_HEADS):
        sl = slice(h * HEAD_DIM, (h + 1) * HEAD_DIM)
        qh_ref[h] = (_rms(qf[:, sl], qg_ref[...]) * (HEAD_DIM ** -0.5)).astype(BF16)
    q_idx = (_dot(cq, wqi_ref[...]) * (IDX_DIM ** -0.5)).astype(BF16)
    for h in range(IDX_HEADS):
        qi_ref[h] = q_idx[:, h * IDX_DIM:(h + 1) * IDX_DIM]

    qpos = lax.broadcasted_iota(jnp.int32, (tq, tk), 0) + i * tq
    col = lax.broadcasted_iota(jnp.int32, (tq, tk), 1)

    def score_body(j, _):
        start = pl.multiple_of(j * tk, tk)
        kj = ki_ref[pl.ds(start, tk), :]
        w_h = smq_ref[:, SM_IW:SM_IW + IDX_HEADS] * (IDX_HEADS ** -0.5)
        acc = jnp.zeros((tq, tk), F32)
        for h in range(IDX_HEADS):
            acc = acc + w_h[:, h:h + 1] * jnp.maximum(_dot_nt(qi_ref[h], kj), 0.0)
        sc_ref[j] = jnp.where(col + start <= qpos, acc, -jnp.inf)
        return 0

    lax.fori_loop(0, nb, score_body, 0)

    def over_chunks(fn, init):
        acc = init
        for j in range(nkc):
            acc = lax.cond(j < nb, functools.partial(fn, j), lambda a: a, acc)
        return acc

    def lane_tiles(x):
        return [x[:, t * tq:(t + 1) * tq] for t in range(tiles)]

    def row_total(x):
        return jnp.sum(x, axis=1, keepdims=True)

    def search():
        kf = float(topk)

        def max_fn(j, a):
            for x in lane_tiles(sc_ref[j]):
                a = jnp.maximum(a, x)
            return a

        def min_fn(j, a):
            for x in lane_tiles(sc_ref[j]):
                a = jnp.minimum(a, jnp.where(x == -jnp.inf, jnp.inf, x))
            return a

        smax = jnp.max(over_chunks(max_fn, jnp.full((tq, tq), -jnp.inf, F32)), axis=1, keepdims=True)
        smin = jnp.min(over_chunks(min_fn, jnp.full((tq, tq), jnp.inf, F32)), axis=1, keepdims=True)

        def count_ge(t):
            tb = jnp.broadcast_to(t, (tq, tq))

            def fn(j, a):
                for x in lane_tiles(sc_ref[j]):
                    a = a + jnp.where(x >= tb, 1.0, 0.0)
                return a

            return row_total(over_chunks(fn, jnp.zeros((tq, tq), F32)))

        def midpoint(lo, hi):
            return jnp.where(hi == jnp.inf, smax, 0.5 * (lo + hi))

        def undecided(lo, hi, c_lo, mid):
            return jnp.logical_and(c_lo != kf, jnp.logical_and(mid > lo, mid < hi))

        def cond(carry):
            return jnp.logical_and(carry[0] < 400, carry[1] > 0.0)

        def body(carry):
            it, _, lo, hi, c_lo, c_hi, mid = carry
            upd = undecided(lo, hi, c_lo, mid)
            cnt = count_ge(mid)
            up = jnp.logical_and(upd, cnt >= kf)
            dn = jnp.logical_and(upd, cnt < kf)
            lo = jnp.where(up, mid, lo)
            c_lo = jnp.where(up, cnt, c_lo)
            hi = jnp.where(dn, mid, hi)
            c_hi = jnp.where(dn, cnt, c_hi)
            mid = midpoint(lo, hi)
            active = jnp.max(jnp.where(undecided(lo, hi, c_lo, mid), 1.0, 0.0))
            return it + 1, active, lo, hi, c_lo, c_hi, mid

        lo0 = smin
        hi0 = jnp.full((tq, 1), jnp.inf, F32)
        c_lo0 = count_ge(lo0)
        c_hi0 = jnp.zeros((tq, 1), F32)
        mid0 = midpoint(lo0, hi0)
        act0 = jnp.max(jnp.where(undecided(lo0, hi0, c_lo0, mid0), 1.0, 0.0))
        _, _, lo, hi, c_lo, c_hi, _ = lax.while_loop(
            cond, body, (jnp.int32(0), act0, lo0, hi0, c_lo0, c_hi0, mid0))

        def tie_search():
            need = kf - c_hi
            lo_b = jnp.broadcast_to(lo, (tq, tq))
            hi_b = jnp.broadcast_to(hi, (tq, tq))
            lane = lax.broadcasted_iota(jnp.int32, (tq, tq), 1)

            def tie_body(_, carry):
                jlo, jhi = carry
                jm = (jlo + jhi) // 2
                jm_b = jnp.broadcast_to(jm, (tq, tq))

                def fn(j, a):
                    for t, x in enumerate(lane_tiles(sc_ref[j])):
                        hit = jnp.logical_and(jnp.logical_and(x >= lo_b, x < hi_b), lane + (j * tk + t * tq) <= jm_b)
                        a = a + jnp.where(hit, 1.0, 0.0)
                    return a

                ok = row_total(over_chunks(fn, jnp.zeros((tq, tq), F32))) >= need
                return jnp.where(ok, jlo, jm), jnp.where(ok, jm, jhi)

            n_bits = int(math.ceil(math.log2(seq))) + 1
            _, jmax = lax.fori_loop(0, n_bits, tie_body,
                                    (jnp.full((tq, 1), -1, jnp.int32), jnp.full((tq, 1), seq - 1, jnp.int32)))
            return jmax

        any_tie = jnp.max(jnp.where(c_lo != kf, 1.0, 0.0)) > 0.0
        jmax = lax.cond(any_tie, tie_search, lambda: jnp.full((tq, 1), seq - 1, jnp.int32))
        return lo, hi, jmax

    def keep_all():
        return (jnp.full((tq, 1), -jnp.inf, F32), jnp.full((tq, 1), jnp.inf, F32),
                jnp.full((tq, 1), seq - 1, jnp.int32))

    lo, hi, jmax = lax.cond((i + 1) * tq > topk, search, keep_all)

    m_ref[...] = jnp.full(m_ref.shape, NEG_BIG, F32)
    l_ref[...] = jnp.zeros(l_ref.shape, F32)
    acc_ref[...] = jnp.zeros(acc_ref.shape, F32)

    def attend_body(j, _):
        start = pl.multiple_of(j * tk, tk)
        ks = kn_ref[pl.ds(start, tk), :]
        vs = v_ref[pl.ds(start, tk), :]
        sc = sc_ref[j]
        kpos = col + start
        keep = jnp.logical_or(sc >= hi, jnp.logical_and(sc >= lo, kpos <= jmax))
        keep = jnp.logical_and(keep, kpos <= qpos)
        for h in range(N_HEADS):
            bias = jnp.concatenate(
                [jnp.where(j * tiles + t == i, band_ref[h, :, tq:2 * tq],
                           jnp.where(j * tiles + t == i - 1, band_ref[h, :, 0:tq], 0.0))
                 for t in range(tiles)], axis=1)
            s = jnp.where(keep, _dot_nt(qh_ref[h], ks) + bias, NEG_BIG)
            m_old = m_ref[h]
            m_new = jnp.maximum(m_old, jnp.max(s, axis=1, keepdims=True))
            p = jnp.exp(s - m_new)
            alpha = jnp.exp(m_old - m_new)
            l_ref[h] = alpha * l_ref[h] + jnp.sum(p, axis=1, keepdims=True)
            acc_ref[h] = alpha * acc_ref[h] + _dot(p.astype(BF16), vs)
            m_ref[h] = m_new
        return 0

    lax.fori_loop(0, nb, attend_body, 0)
    for h in range(N_HEADS):
        o_ref[:, h * HEAD_DIM:(h + 1) * HEAD_DIM] = (acc_ref[h] / l_ref[h]).astype(o_ref.dtype)


SUBLANES = 8


def _fold_rows(x, op):
    return op(x.reshape(x.shape[0] // SUBLANES, SUBLANES, x.shape[1]), axis=0)


def _dsa_t_kernel(cq_ref, k_ref, v_ref, smq_ref, smk_ref, cqg_ref, wuq_ref, wqi_ref, qg_ref, kg_ref, rb_ref,
                  o_ref, kn_ref, ki_ref, vt_ref, band_ref, sc_ref, qt_ref, xi_ref, acc_ref, *, topk):
    b = pl.program_id(0)
    i = pl.program_id(1)
    tq, tk = DSA_TQ, DSA_TK
    nkc = sc_ref.shape[0]
    seq = nkc * tk
    tiles = tk // tq
    nb = ((i + 1) * tq + tk - 1) // tk

    @pl.when(jnp.logical_and(b == 0, i == 0))
    def _():
        c = lax.broadcasted_iota(jnp.int32, (BAND_W, tq), 0)
        r = lax.broadcasted_iota(jnp.int32, (BAND_W, tq), 1)
        bucket = _t5_bucket(tq + r - c)
        for h in range(N_HEADS):
            far = rb_ref[N_BUCKETS - 1, h]
            acc = jnp.zeros((BAND_W, tq), F32)
            for bk in range(N_BUCKETS - 1):
                acc = jnp.where(bucket == bk, rb_ref[bk, h] - far, acc)
            band_ref[h] = acc

    @pl.when(i == 0)
    def _():
        kn_ref[...] = _rms(k_ref[...].astype(F32), kg_ref[...]).astype(BF16)
        ki_ref[...] = smk_ref[:, SM_IK:SM_IK + IDX_DIM].astype(BF16)
        for j in range(nkc):
            for t in range(tiles):
                rows = slice(j * tk + t * tq, j * tk + (t + 1) * tq)
                vt_ref[j, :, t * tq:(t + 1) * tq] = v_ref[rows, :].astype(F32).T.astype(BF16)

    cq_t = _rms(cq_ref[...].astype(F32), cqg_ref[...]).T.astype(BF16)
    q_t = _dot(wuq_ref[...], cq_t)
    g_col = jnp.broadcast_to(qg_ref[...], (HEAD_DIM, tq))
    for h in range(N_HEADS):
        x = q_t[h * HEAD_DIM:(h + 1) * HEAD_DIM, :]
        inv = lax.rsqrt(jnp.mean(x * x, axis=0, keepdims=True) + EPS)
        qt_ref[:, h * tq:(h + 1) * tq] = (x * inv * g_col * (HEAD_DIM ** -0.5)).astype(BF16)
    qi_t = (_dot(wqi_ref[...], cq_t) * (IDX_DIM ** -0.5)).astype(BF16)
    for h in range(IDX_HEADS):
        xi_ref[:, h * tq:(h + 1) * tq] = qi_t[h * IDX_DIM:(h + 1) * IDX_DIM, :]
    w_rows = smq_ref[...].T[SM_IW:SM_IW + IDX_HEADS, :] * (IDX_HEADS ** -0.5)

    kofs = lax.broadcasted_iota(jnp.int32, (tk, tq), 0)
    qpos = lax.broadcasted_iota(jnp.int32, (tk, tq), 1) + i * tq

    def score_body(j, _):
        start = pl.multiple_of(j * tk, tk)
        kj = ki_ref[pl.ds(start, tk), :]
        acc = jnp.zeros((tk, tq), F32)
        for h in range(IDX_HEADS):
            r = _dot(kj, xi_ref[:, h * tq:(h + 1) * tq])
            acc = acc + w_rows[h:h + 1, :] * jnp.maximum(r, 0.0)
        sc_ref[j] = jnp.where(kofs + start <= qpos, acc, -jnp.inf)
        return 0

    lax.fori_loop(0, nb, score_body, 0)

    def over_chunks(fn, init):
        acc = init
        for j in range(nkc):
            acc = lax.cond(j < nb, functools.partial(fn, j), lambda a: a, acc)
        return acc

    def count_where(pred_fn):
        def fn(j, a):
            return a + _fold_rows(jnp.where(pred_fn(j, sc_ref[j]), 1.0, 0.0), jnp.sum)

        return jnp.sum(over_chunks(fn, jnp.zeros((SUBLANES, tq), F32)), axis=0, keepdims=True)

    def search():
        kf = float(topk)
        smax = jnp.max(over_chunks(lambda j, a: jnp.maximum(a, _fold_rows(sc_ref[j], jnp.max)),
                                   jnp.full((SUBLANES, tq), -jnp.inf, F32)), axis=0, keepdims=True)
        smin = jnp.min(over_chunks(
            lambda j, a: jnp.minimum(a, _fold_rows(jnp.where(sc_ref[j] == -jnp.inf, jnp.inf, sc_ref[j]), jnp.min)),
            jnp.full((SUBLANES, tq), jnp.inf, F32)), axis=0, keepdims=True)

        def count_ge(t):
            return count_where(lambda j, x: x >= t)

        def midpoint(lo, hi):
            return jnp.where(hi == jnp.inf, smax, 0.5 * (lo + hi))

        def undecided(lo, hi, c_lo, mid):
            return jnp.logical_and(c_lo != kf, jnp.logical_and(mid > lo, mid < hi))

        def cond(carry):
            return jnp.logical_and(carry[0] < 400, carry[1] > 0.0)

        def body(carry):
            it, _, lo, hi, c_lo, c_hi, mid = carry
            upd = undecided(lo, hi, c_lo, mid)
            cnt = count_ge(mid)
            up = jnp.logical_and(upd, cnt >= kf)
            dn = jnp.logical_and(upd, cnt < kf)
            lo = jnp.where(up, mid, lo)
            c_lo = jnp.where(up, cnt, c_lo)
            hi = jnp.where(dn, mid, hi)
            c_hi = jnp.where(dn, cnt, c_hi)
            mid = midpoint(lo, hi)
            active = jnp.max(jnp.where(undecided(lo, hi, c_lo, mid), 1.0, 0.0))
            return it + 1, active, lo, hi, c_lo, c_hi, mid

        lo0 = smin
        hi0 = jnp.full((1, tq), jnp.inf, F32)
        c_lo0 = count_ge(lo0)
        c_hi0 = jnp.zeros((1, tq), F32)
        mid0 = midpoint(lo0, hi0)
        act0 = jnp.max(jnp.where(undecided(lo0, hi0, c_lo0, mid0), 1.0, 0.0))
        _, _, lo, hi, c_lo, c_hi, _ = lax.while_loop(
            cond, body, (jnp.int32(0), act0, lo0, hi0, c_lo0, c_hi0, mid0))

        def tie_search():
            need = kf - c_hi

            def tie_body(_, carry):
                jlo, jhi = carry
                jm = (jlo + jhi) // 2
                cnt = count_where(lambda j, x: jnp.logical_and(jnp.logical_and(x >= lo, x < hi),
                                                               kofs + j * tk <= jm))
                ok = cnt >= need
                return jnp.where(ok, jlo, jm), jnp.where(ok, jm, jhi)

            n_bits = int(math.ceil(math.log2(seq))) + 1
            _, jmax = lax.fori_loop(0, n_bits, tie_body,
                                    (jnp.full((1, tq), -1, jnp.int32), jnp.full((1, tq), seq - 1, jnp.int32)))
            return jmax

        any_tie = jnp.max(jnp.where(c_lo != kf, 1.0, 0.0)) > 0.0
        jmax = lax.cond(any_tie, tie_search, lambda: jnp.full((1, tq), seq - 1, jnp.int32))
        return lo, hi, jmax

    def keep_all():
        return (jnp.full((1, tq), -jnp.inf, F32), jnp.full((1, tq), jnp.inf, F32),
                jnp.full((1, tq), seq - 1, jnp.int32))

    lo, hi, jmax = lax.cond((i + 1) * tq > topk, search, keep_all)

    acc_ref[...] = jnp.zeros(acc_ref.shape, F32)

    def attend_body(j, carry):
        ms, ls = carry
        start = pl.multiple_of(j * tk, tk)
        kc = kn_ref[pl.ds(start, tk), :]
        vt = vt_ref[j]
        sc = sc_ref[j]
        kpos = kofs + start
        keep = jnp.logical_or(sc >= hi, jnp.logical_and(sc >= lo, kpos <= jmax))
        keep = jnp.logical_and(keep, kpos <= qpos)
        new_ms, new_ls = [], []
        for h in range(N_HEADS):
            hs = slice(h * tq, (h + 1) * tq)
            bias = jnp.concatenate(
                [jnp.where(j * tiles + t == i, band_ref[h, tq:2 * tq, :],
                           jnp.where(j * tiles + t == i - 1, band_ref[h, 0:tq, :], 0.0))
                 for t in range(tiles)], axis=0)
            s = jnp.where(keep, _dot(kc, qt_ref[:, hs]) + bias, NEG_BIG)
            m_new = jnp.maximum(ms[h], jnp.max(s, axis=0, keepdims=True))
            p = jnp.exp(s - m_new)
            alpha = jnp.exp(ms[h] - m_new)
            new_ls.append(alpha * ls[h] + jnp.sum(p, axis=0, keepdims=True))
            acc_ref[:, hs] = alpha * acc_ref[:, hs] + _dot(vt, p.astype(BF16))
            new_ms.append(m_new)
        return tuple(new_ms), tuple(new_ls)

    init = (tuple(jnp.full((1, tq), NEG_BIG, F32) for _ in range(N_HEADS)),
            tuple(jnp.zeros((1, tq), F32) for _ in range(N_HEADS)))
    _, ls = lax.fori_loop(0, nb, attend_body, init)
    for h in range(N_HEADS):
        out_t = acc_ref[:, h * tq:(h + 1) * tq] / ls[h]
        o_ref[:, h * HEAD_DIM:(h + 1) * HEAD_DIM] = out_t.T.astype(o_ref.dtype)


def dsa_attention(p, sm, cqg, wuq_t, wqi_t, qg_col, kg, rel_bias, batch, seq):
    nq = seq // DSA_TQ
    nkc = seq // DSA_TK
    topk = min(DSA_TOPK, seq // 4)
    kern = functools.partial(_dsa_t_kernel, topk=topk)
    return pl.pallas_call(
        kern,
        grid=(batch, nq),
        in_specs=[
            pl.BlockSpec((DSA_TQ, DSA_Q_RANK), lambda b, i: (b * nq + i, COL_CQ // DSA_Q_RANK)),
            pl.BlockSpec((seq, HEAD_DIM), lambda b, i: (b, COL_DK // HEAD_DIM)),
            pl.BlockSpec((seq, HEAD_DIM), lambda b, i: (b, COL_DV // HEAD_DIM)),
            pl.BlockSpec((DSA_TQ, SM_W), lambda b, i: (b * nq + i, 0)),
            pl.BlockSpec((seq, SM_W), lambda b, i: (b, 0)),
            pl.BlockSpec((1, DSA_Q_RANK), lambda b, i: (0, 0)),
            pl.BlockSpec((N_HEADS * HEAD_DIM, DSA_Q_RANK), lambda b, i: (0, 0)),
            pl.BlockSpec((IDX_HEADS * IDX_DIM, DSA_Q_RANK), lambda b, i: (0, 0)),
            pl.BlockSpec((HEAD_DIM, 1), lambda b, i: (0, 0)),
            pl.BlockSpec((1, HEAD_DIM), lambda b, i: (0, 0)),
            pl.BlockSpec(memory_space=pltpu.SMEM),
        ],
        out_specs=pl.BlockSpec((DSA_TQ, MIX_W), lambda b, i: (b * nq + i, 0)),
        out_shape=jax.ShapeDtypeStruct((batch * seq, MIX_W), BF16),
        scratch_shapes=[
            pltpu.VMEM((seq, HEAD_DIM), BF16),
            pltpu.VMEM((seq, IDX_DIM), BF16),
            pltpu.VMEM((nkc, HEAD_DIM, DSA_TK), BF16),
            pltpu.VMEM((N_HEADS, BAND_W, DSA_TQ), F32),
            pltpu.VMEM((nkc, DSA_TK, DSA_TQ), F32),
            pltpu.VMEM((HEAD_DIM, N_HEADS * DSA_TQ), BF16),
            pltpu.VMEM((IDX_DIM, IDX_HEADS * DSA_TQ), BF16),
            pltpu.VMEM((HEAD_DIM, N_HEADS * DSA_TQ), F32),
        ],
        compiler_params=_cparams(("arbitrary", "arbitrary")),
        name="dsa_attention",
    )(p, p, p, sm, sm, cqg, wuq_t, wqi_t, qg_col, kg, rel_bias)


def _causal_conv(x, xp, w_ref, b_ref):
    rows = lax.broadcasted_iota(jnp.int32, x.shape, 0)
    acc = x * w_ref[SSD_CONV - 1:SSD_CONV, :] + b_ref[...]
    for d in range(1, SSD_CONV):
        shifted = jnp.where(rows < d, pltpu.roll(xp, d, 0), pltpu.roll(x, d, 0))
        acc = acc + shifted * w_ref[SSD_CONV - 1 - d:SSD_CONV - d, :]
    return _silu(acc)


def _ssd_kernel(z_ref, xs_ref, bc_ref, xsp_ref, bcp_ref, sm_ref, cwx_ref, cbx_ref, cwb_ref, cbb_ref,
                dtb_ref, alog_ref, dsk_ref, ng_ref, o_ref, prev_ref, y_ref):
    c = CHUNK
    n = pl.program_id(1)

    @pl.when(n == 0)
    def _():
        prev_ref[...] = jnp.zeros_like(prev_ref)

    first = (n > 0).astype(F32)
    xs = _causal_conv(xs_ref[...].astype(F32), xsp_ref[...].astype(F32) * first, cwx_ref, cbx_ref)
    bc = _causal_conv(bc_ref[...].astype(F32), bcp_ref[...].astype(F32) * first, cwb_ref, cbb_ref)

    dt_t = _softplus(sm_ref[...].T + dtb_ref[...])
    cs_t = _cumsum_lanes(dt_t * (-jnp.exp(alog_ref[...])))
    cs = cs_t.T
    dt = dt_t.T
    ii = lax.broadcasted_iota(jnp.int32, (c, c), 0)
    jj = lax.broadcasted_iota(jnp.int32, (c, c), 1)
    tril = ii >= jj
    gn = SSD_GROUPS * SSD_STATE
    hpg = SSD_HEADS // SSD_GROUPS
    for g in range(SSD_GROUPS):
        bg = bc[:, g * SSD_STATE:(g + 1) * SSD_STATE]
        cg = bc[:, gn + g * SSD_STATE:gn + (g + 1) * SSD_STATE].astype(BF16)
        cb = _dot_nt(cg, bg.astype(BF16))
        y_off = _dot(cg, prev_ref[g].astype(BF16))
        for r in range(hpg):
            h = g * hpg + r
            hs = slice(h * SSD_HEAD_DIM, (h + 1) * SSD_HEAD_DIM)
            rs = slice(r * SSD_HEAD_DIM, (r + 1) * SSD_HEAD_DIM)
            a_col = cs[:, SM_DT + h:SM_DT + h + 1]
            a_row = cs_t[SM_DT + h:SM_DT + h + 1, :]
            last = cs_t[SM_DT + h:SM_DT + h + 1, c - 1:c]
            seg = jnp.where(tril, jnp.exp(jnp.where(tril, a_col - a_row, 0.0)), 0.0)
            xh = xs[:, hs]
            xc = (xh * dt[:, SM_DT + h:SM_DT + h + 1]).astype(BF16)
            y_diag = _dot((cb * seg).astype(BF16), xc)
            st = _dot_tn((bg * jnp.exp(last - a_col)).astype(BF16), xc)
            y_ref[:, hs] = y_diag + y_off[:, rs] * jnp.exp(a_col) + dsk_ref[:, hs] * xh
            prev_ref[g, :, rs] = jnp.exp(last) * prev_ref[g, :, rs] + st
    gated = y_ref[...] * _silu(z_ref[...].astype(F32))
    gw = SSD_INNER // SSD_GROUPS
    for g in range(SSD_GROUPS):
        sl = slice(g * gw, (g + 1) * gw)
        o_ref[:, sl] = _rms(gated[:, sl], ng_ref[:, sl]).astype(o_ref.dtype)


def ssd_mixer(p, sm, cw, cb, dtb_col, alog_col, dskip_row, ng, batch, seq):
    n = seq // CHUNK
    bcw = 2 * SSD_GROUPS * SSD_STATE

    def cur(width, colbase):
        return pl.BlockSpec((CHUNK, width), lambda b, i: (b * n + i, colbase // width))

    def prv(width, colbase):
        return pl.BlockSpec((CHUNK, width), lambda b, i: (b * n + jnp.maximum(i - 1, 0), colbase // width))

    def const(shape):
        return pl.BlockSpec(shape, lambda b, i: (0, 0))

    return pl.pallas_call(
        _ssd_kernel,
        grid=(batch, n),
        in_specs=[
            cur(SSD_INNER, COL_Z), cur(SSD_INNER, COL_XS), cur(bcw, COL_BC),
            prv(SSD_INNER, COL_XS), prv(bcw, COL_BC),
            pl.BlockSpec((CHUNK, SM_W), lambda b, i: (b * n + i, 0)),
            const((SSD_CONV, SSD_INNER)), const((1, SSD_INNER)),
            const((SSD_CONV, bcw)), const((1, bcw)),
            const((SM_W, 1)), const((SM_W, 1)),
            const((1, SSD_INNER)), const((1, SSD_INNER)),
        ],
        out_specs=pl.BlockSpec((CHUNK, SSD_INNER), lambda b, i: (b * n + i, 0)),
        out_shape=jax.ShapeDtypeStruct((batch * seq, SSD_INNER), BF16),
        scratch_shapes=[
            pltpu.VMEM((SSD_GROUPS, SSD_STATE, SSD_INNER // SSD_GROUPS), F32),
            pltpu.VMEM((CHUNK, SSD_INNER), F32),
        ],
        compiler_params=_cparams(("parallel", "arbitrary")),
        name="ssd_mixer",
    )(p, p, p, p, p, sm, cw[:, :SSD_INNER], cb[:, :SSD_INNER], cw[:, SSD_INNER:], cb[:, SSD_INNER:],
      dtb_col, alog_col, dskip_row, ng)


MERGE_TM = 256


def _merge_kernel(x_ref, gl_ref, gb_ref, oret_ref, ofox_ref, odsa_ref, ossd_ref, wbr_ref, wout_ref, o_ref):
    branches = (oret_ref, ofox_ref, odsa_ref, ossd_ref)
    merged = None
    row0 = 0
    for bi, br in enumerate(branches):
        width = br.shape[1]
        sl = slice(bi * D_MODEL, (bi + 1) * D_MODEL)
        gate = 1.0 / (1.0 + jnp.exp(-(gl_ref[:, sl].astype(F32) + gb_ref[:, sl])))
        term = gate * _dot(br[...], wbr_ref[row0:row0 + width, :])
        merged = term if merged is None else merged + term
        row0 += width
    o_ref[...] = x_ref[...] + _dot(merged.astype(BF16), wout_ref[...])


def merge_project(x, p, gate_b, o_ret, o_fox, o_dsa, o_ssd, w_br, w_out):
    m = x.shape[0]
    tm = MERGE_TM

    def rows(width):
        return pl.BlockSpec((tm, width), lambda i: (i, 0))

    def const(shape):
        return pl.BlockSpec(shape, lambda i: (0, 0), pipeline_mode=pl.Buffered(1))

    return pl.pallas_call(
        _merge_kernel,
        grid=(m // tm,),
        in_specs=[
            rows(D_MODEL), rows(N_BRANCH * D_MODEL), const((1, N_BRANCH * D_MODEL)),
            rows(MIX_W), rows(MIX_W), rows(MIX_W), rows(SSD_INNER),
            const(w_br.shape), const(w_out.shape),
        ],
        out_specs=rows(D_MODEL),
        out_shape=jax.ShapeDtypeStruct(x.shape, x.dtype),
        compiler_params=_cparams(("parallel",)),
        name="merge_project",
    )(x, p, gate_b, o_ret, o_fox, o_dsa, o_ssd, w_br, w_out)


FFN_TM = 1024
FFN_TF = 512


def _ffn_kernel(x_ref, g_ref, w1_ref, w2_ref, o_ref, h_ref):
    @pl.when(pl.program_id(1) == 0)
    def _():
        h_ref[...] = _rms(x_ref[...], g_ref[...]).astype(BF16)
        o_ref[...] = x_ref[...]

    a = jnp.maximum(_dot(h_ref[...], w1_ref[...]), 0.0)
    o_ref[...] += _dot((a * a).astype(BF16), w2_ref[...])


def ffn(x, g, w1, w2):
    m, d = x.shape
    dff = w1.shape[1]
    tm, tf = min(FFN_TM, m), FFN_TF
    return pl.pallas_call(
        _ffn_kernel,
        grid=(m // tm, dff // tf),
        in_specs=[
            pl.BlockSpec((tm, d), lambda i, f: (i, 0), pipeline_mode=pl.Buffered(1)),
            pl.BlockSpec((1, d), lambda i, f: (0, 0)),
            pl.BlockSpec((d, tf), lambda i, f: (0, f)),
            pl.BlockSpec((tf, d), lambda i, f: (f, 0)),
        ],
        out_specs=pl.BlockSpec((tm, d), lambda i, f: (i, 0)),
        out_shape=jax.ShapeDtypeStruct(x.shape, x.dtype),
        scratch_shapes=[pltpu.VMEM((tm, d), BF16)],
        compiler_params=_cparams(("parallel", "arbitrary")),
        name="ffn",
    )(x, g, w1, w2)


SRC_RET = 0
SRC_FOX = SRC_RET + 4 * MIX_W
SRC_FF = SRC_FOX + 3 * MIX_W
SRC_CQ = SRC_FF + N_HEADS
SRC_DK = SRC_CQ + DSA_Q_RANK
SRC_IK = SRC_DK + 2 * HEAD_DIM
SRC_IW = SRC_IK + IDX_DIM
SRC_Z = SRC_IW + IDX_HEADS
SRC_DT = SRC_Z + 2 * SSD_INNER + 2 * SSD_GROUPS * SSD_STATE
SRC_GATE = SRC_DT + SSD_HEADS
IN_TOTAL = SRC_GATE + N_BRANCH * D_MODEL
MAIN_RUNS = ((COL_GATE, SRC_GATE), (COL_RET, SRC_RET), (COL_Z, SRC_Z), (COL_CQ, SRC_CQ),
             (COL_FOX, SRC_FOX), (COL_DK, SRC_DK))
RELAYOUT_W = 512
RELAYOUT_TILES = RELAYOUT_W // LANES


def _relayout_tables():
    starts, shifts = [], []
    for blk in range(N_MAIN // RELAYOUT_W):
        o = blk * RELAYOUT_W
        dst, src = [r for r in MAIN_RUNS if r[0] <= o][-1]
        col = src + (o - dst)
        starts.append(col // LANES)
        shifts.append(col % LANES)
    return jnp.asarray(starts, jnp.int32), jnp.asarray(shifts, jnp.int32)


def _relayout_kernel(start_ref, shift_ref, *refs):
    del start_ref
    tiles, o_ref = refs[:-1], refs[-1]
    shift = shift_ref[pl.program_id(1)]
    amount = lax.rem(LANES - shift, LANES)
    lane = lax.broadcasted_iota(jnp.int32, tiles[0].shape, 1)
    rolled = [pltpu.roll(t[...], amount, 1) for t in tiles]
    for k in range(RELAYOUT_TILES):
        piece = jnp.where(lane < LANES - shift, rolled[k], rolled[k + 1])
        o_ref[:, k * LANES:(k + 1) * LANES] = piece.astype(o_ref.dtype)


def relayout_main(w_in):
    depth, d, n_src = w_in.shape
    last = (n_src - 1) // LANES
    starts, shifts = _relayout_tables()

    def tile(k):
        return pl.BlockSpec((None, d, LANES), lambda l, b, st, sh: (l, 0, jnp.minimum(st[b] + k, last)))

    return pl.pallas_call(
        _relayout_kernel,
        grid_spec=pltpu.PrefetchScalarGridSpec(
            num_scalar_prefetch=2,
            grid=(depth, N_MAIN // RELAYOUT_W),
            in_specs=[tile(k) for k in range(RELAYOUT_TILES + 1)],
            out_specs=pl.BlockSpec((None, d, RELAYOUT_W), lambda l, b, st, sh: (l, 0, b)),
        ),
        out_shape=jax.ShapeDtypeStruct((depth, d, N_MAIN), BF16),
        compiler_params=_cparams(("parallel", "arbitrary")),
        name="relayout_main",
    )(starts, shifts, *([w_in] * (RELAYOUT_TILES + 1)))


SMALL_PIECES = ((SRC_DT, SM_DT, SSD_HEADS), (SRC_FF, SM_F, N_HEADS), (SRC_IW, SM_IW, IDX_HEADS),
                (SRC_IK, SM_IK, IDX_DIM))


def _relayout_small_kernel(*refs):
    tiles, o_ref = refs[:-1], refs[-1]
    lane = lax.broadcasted_iota(jnp.int32, o_ref.shape, 1)
    out = jnp.zeros(o_ref.shape, F32)
    for t, (src, dst, width) in zip(tiles, SMALL_PIECES):
        moved = pltpu.roll(t[...], (dst - src % LANES) % LANES, 1)
        out = jnp.where(jnp.logical_and(lane >= dst, lane < dst + width), moved, out)
    o_ref[...] = out.astype(o_ref.dtype)


def relayout_small(w_in):
    depth, d, _ = w_in.shape
    for src, _, width in SMALL_PIECES:
        assert src // LANES == (src + width - 1) // LANES

    def tile(src):
        return pl.BlockSpec((None, d, LANES), lambda l: (l, 0, src // LANES))

    return pl.pallas_call(
        _relayout_small_kernel,
        grid=(depth,),
        in_specs=[tile(src) for src, _, _ in SMALL_PIECES],
        out_specs=pl.BlockSpec((None, d, SM_W), lambda l: (l, 0, 0)),
        out_shape=jax.ShapeDtypeStruct((depth, d, SM_W), BF16),
        compiler_params=_cparams(("parallel",)),
        name="relayout_small",
    )(*([w_in] * len(SMALL_PIECES)))


def _pad_to(v, offset, total):
    return jnp.zeros((total,), v.dtype).at[offset:offset + v.shape[0]].set(v)


def _rotary_tables(seq):
    half = HEAD_DIM // 2
    inv = 1.0 / (10000.0 ** (jnp.arange(half, dtype=F32) / half))
    ang = jnp.arange(seq, dtype=F32)[:, None] * inv[None, :]
    cos, sin = jnp.cos(ang), jnp.sin(ang)
    return jnp.concatenate([cos, cos], axis=1), jnp.concatenate([-sin, sin], axis=1)


def kernel(x, norm1_g, w_in, gate_b, fox_f_b, fox_qn_g, fox_kn_g, dsa_cq_g, dsa_w_uq, dsa_w_qidx, dsa_qn_g,
           dsa_kn_g, rel_bias, ssd_conv_w, ssd_conv_b, ssd_dt_bias, ssd_a_log, ssd_d, ssd_norm_g, w_br, w_out,
           norm2_g, w_ff1, w_ff2):
    batch, seq, d = x.shape
    tokens = batch * seq
    xt = x.reshape(tokens, d)
    cos, sin = _rotary_tables(seq)
    tm = min(1024, tokens)
    w_main = relayout_main(w_in)
    w_small = relayout_small(w_in)
    for l in range(DEPTH):
        g1 = norm1_g[l][None, :]
        p = norm_matmul(xt, g1, w_main, l, BF16, tm, 1024)
        sm = norm_matmul(xt, g1, w_small, l, F32, tm, SM_W)

        o_ret = retention(p, cos, sin, batch, seq)

        fb_row = _pad_to(fox_f_b[l], SM_F, SM_W)[None, :]
        fcol, frow = fox_prep(sm, fb_row, batch, seq)
        o_fox = fox_attention(p, fcol, frow, fox_qn_g[l][None, :], fox_kn_g[l][None, :], batch, seq)

        o_dsa = dsa_attention(p, sm, dsa_cq_g[l][None, :], dsa_w_uq[l].T.astype(BF16), dsa_w_qidx[l].T.astype(BF16),
                              dsa_qn_g[l][:, None], dsa_kn_g[l][None, :], rel_bias, batch, seq)

        o_ssd = ssd_mixer(p, sm, ssd_conv_w[l], ssd_conv_b[l][None, :],
                          _pad_to(ssd_dt_bias[l], SM_DT, SM_W)[:, None], _pad_to(ssd_a_log[l], SM_DT, SM_W)[:, None],
                          jnp.repeat(ssd_d[l], SSD_HEAD_DIM)[None, :], ssd_norm_g[l][None, :], batch, seq)

        xt = merge_project(xt, p, gate_b[l][None, :], o_ret, o_fox, o_dsa, o_ssd,
                           w_br[l].astype(BF16), w_out[l].astype(BF16))
        xt = ffn(xt, norm2_g[l][None, :], w_ff1[l].astype(BF16), w_ff2[l].astype(BF16))
    return xt.reshape(batch, seq, d)
```

```python
import functools
import math

import jax
import jax.numpy as jnp
from jax import lax
from jax.experimental import pallas as pl
from jax.experimental.pallas import tpu as pltpu

F32 = jnp.float32
BF16 = jnp.bfloat16

D_MODEL = 2048
DEPTH = 4
HEAD_DIM = 128
N_HEADS = 4
DSA_Q_RANK = 512
IDX_HEADS = 16
IDX_DIM = 64
DSA_TOPK = 256
SSD_HEADS = 16
SSD_HEAD_DIM = 64
SSD_GROUPS = 2
SSD_STATE = 128
SSD_CONV = 4
SSD_INNER = SSD_HEADS * SSD_HEAD_DIM
D_FF = 4 * D_MODEL
N_BUCKETS = 32
MAX_DISTANCE = 128
CHUNK = 128
EPS = 1e-6
N_BRANCH = 4
MIX_W = N_HEADS * HEAD_DIM

COL_GATE = 0
COL_RET = COL_GATE + N_BRANCH * D_MODEL
COL_Z = COL_RET + 4 * MIX_W
COL_XS = COL_Z + SSD_INNER
COL_BC = COL_XS + SSD_INNER
COL_CQ = COL_BC + 2 * SSD_GROUPS * SSD_STATE
COL_FOX = COL_CQ + DSA_Q_RANK
COL_DK = COL_FOX + 3 * MIX_W
COL_DV = COL_DK + HEAD_DIM
N_MAIN_USED = COL_DV + HEAD_DIM
N_MAIN = 15360
SM_DT = 0
SM_F = 16
SM_IW = 32
SM_IK = 64
SM_W = 128

LANES = 128
VMEM_LIMIT = 56 * 1024 * 1024
NEG_BIG = -1e30


def _cparams(sem):
    return pltpu.CompilerParams(dimension_semantics=sem, vmem_limit_bytes=VMEM_LIMIT)


def _dot(a, b):
    return jnp.dot(a, b, preferred_element_type=F32)


def _dot_nt(a, b):
    return lax.dot_general(a, b, (((1,), (1,)), ((), ())), preferred_element_type=F32)


def _dot_tn(a, b):
    return lax.dot_general(a, b, (((0,), (0,)), ((), ())), preferred_element_type=F32)


def _rms(x, g):
    return x * lax.rsqrt(jnp.mean(x * x, axis=-1, keepdims=True) + EPS) * g


def _silu(x):
    return x / (1.0 + jnp.exp(-x))


def _softplus(x):
    return jnp.maximum(x, 0.0) + jnp.log1p(jnp.exp(-jnp.abs(x)))


def _cumsum_lanes(x):
    lane = lax.broadcasted_iota(jnp.int32, x.shape, 1)
    d = 1
    while d < x.shape[1]:
        x = x + jnp.where(lane >= d, pltpu.roll(x, d, 1), 0.0)
        d *= 2
    return x


def _norm_matmul_kernel(x_ref, g_ref, w_ref, o_ref, h_ref):
    @pl.when(pl.program_id(1) == 0)
    def _():
        h_ref[...] = _rms(x_ref[...], g_ref[...]).astype(BF16)

    o_ref[...] = _dot(h_ref[...], w_ref[...]).astype(o_ref.dtype)


def norm_matmul(x, g, w, layer, out_dtype, tm, tn):
    m, d = x.shape
    n = w.shape[2]
    return pl.pallas_call(
        _norm_matmul_kernel,
        grid=(m // tm, n // tn),
        in_specs=[
            pl.BlockSpec((tm, d), lambda i, j: (i, 0)),
            pl.BlockSpec((1, d), lambda i, j: (0, 0)),
            pl.BlockSpec((None, d, tn), lambda i, j: (layer, 0, j)),
        ],
        out_specs=pl.BlockSpec((tm, tn), lambda i, j: (i, j)),
        out_shape=jax.ShapeDtypeStruct((m, n), out_dtype),
        scratch_shapes=[pltpu.VMEM((tm, d), BF16)],
        compiler_params=_cparams(("parallel", "arbitrary")),
        name="norm_matmul",
    )(x, g, w)


def _retention_kernel(q_ref, k_ref, v_ref, g_ref, cos_ref, sin_ref, o_ref, state_ref):
    c = CHUNK

    @pl.when(pl.program_id(1) == 0)
    def _():
        state_ref[...] = jnp.zeros_like(state_ref)

    cos = cos_ref[...]
    sin = sin_ref[...]
    ii = lax.broadcasted_iota(jnp.int32, (c, c), 0)
    jj = lax.broadcasted_iota(jnp.int32, (c, c), 1)
    rel = (ii - jj).astype(F32)
    i_col = lax.broadcasted_iota(jnp.int32, (c, 1), 0).astype(F32)
    for h in range(N_HEADS):
        lg = math.log1p(-(2.0 ** (-5.0 - h)))
        sl = slice(h * HEAD_DIM, (h + 1) * HEAD_DIM)
        q = q_ref[:, sl].astype(F32)
        k = k_ref[:, sl].astype(F32)
        v = v_ref[:, sl]
        qr = q * cos + pltpu.roll(q, HEAD_DIM // 2, 1) * sin
        kr = (k * cos + pltpu.roll(k, HEAD_DIM // 2, 1) * sin) * (HEAD_DIM ** -0.5)
        decay = jnp.where(rel >= 0, jnp.exp(lg * jnp.maximum(rel, 0.0)), 0.0)
        scores = _dot_nt(qr.astype(BF16), kr.astype(BF16)) * decay
        y = _dot(scores.astype(BF16), v)
        q_dec = jnp.exp(lg * (i_col + 1.0))
        k_dec = jnp.exp(lg * (c - 1.0 - i_col))
        st = state_ref[h]
        y = y + _dot((qr * q_dec).astype(BF16), st.astype(BF16))
        kv = _dot_tn((kr * k_dec).astype(BF16), v)
        state_ref[h] = math.exp(lg * c) * st + kv
        yc = y - jnp.mean(y, axis=-1, keepdims=True)
        yn = yc * lax.rsqrt(jnp.mean(yc * yc, axis=-1, keepdims=True) + EPS)
        o_ref[:, sl] = (_silu(g_ref[:, sl].astype(F32)) * yn).astype(o_ref.dtype)


def retention(p, cos, sin, batch, seq):
    n = seq // CHUNK
    base = COL_RET // MIX_W

    def col(j):
        return pl.BlockSpec((CHUNK, MIX_W), lambda b, i: (b * n + i, base + j))

    tab = pl.BlockSpec((CHUNK, HEAD_DIM), lambda b, i: (i, 0))
    return pl.pallas_call(
        _retention_kernel,
        grid=(batch, n),
        in_specs=[col(0), col(1), col(2), col(3), tab, tab],
        out_specs=pl.BlockSpec((CHUNK, MIX_W), lambda b, i: (b * n + i, 0)),
        out_shape=jax.ShapeDtypeStruct((batch * seq, MIX_W), BF16),
        scratch_shapes=[pltpu.VMEM((N_HEADS, HEAD_DIM, HEAD_DIM), F32)],
        compiler_params=_cparams(("parallel", "arbitrary")),
        name="retention",
    )(p, p, p, p, cos, sin)


def _fox_prep_kernel(sm_ref, fb_ref, fcol_ref, frow_ref, carry_ref):
    @pl.when(pl.program_id(1) == 0)
    def _():
        carry_ref[...] = jnp.zeros_like(carry_ref)

    t = sm_ref[...] + fb_ref[...]
    lf = jnp.minimum(t, 0.0) - jnp.log1p(jnp.exp(-jnp.abs(t)))
    cs = _cumsum_lanes(lf.T) + carry_ref[...]
    carry_ref[...] = cs[:, LANES - 1:LANES]
    frow_ref[0, 0] = cs[SM_F:SM_F + 8, :]
    fcol_ref[...] = cs.T


def fox_prep(sm, fb_row, batch, seq):
    n = seq // CHUNK
    return pl.pallas_call(
        _fox_prep_kernel,
        grid=(batch, n),
        in_specs=[
            pl.BlockSpec((CHUNK, SM_W), lambda b, i: (b * n + i, 0)),
            pl.BlockSpec((1, SM_W), lambda b, i: (0, 0)),
        ],
        out_specs=[
            pl.BlockSpec((CHUNK, SM_W), lambda b, i: (b * n + i, 0)),
            pl.BlockSpec((1, 1, 8, CHUNK), lambda b, i: (b, i, 0, 0)),
        ],
        out_shape=[
            jax.ShapeDtypeStruct((batch * seq, SM_W), F32),
            jax.ShapeDtypeStruct((batch, n, 8, CHUNK), F32),
        ],
        scratch_shapes=[pltpu.VMEM((SM_W, 1), F32)],
        compiler_params=_cparams(("parallel", "arbitrary")),
        name="fox_prep",
    )(sm, fb_row)


FOX_TQ = 256
FOX_TK = 128


def _fox_kernel(q_ref, k_ref, v_ref, fcol_ref, frow_ref, qg_ref, kg_ref, o_ref, kn_ref):
    i = pl.program_id(1)
    tq, tk = FOX_TQ, FOX_TK

    @pl.when(i == 0)
    def _():
        for h in range(N_HEADS):
            sl = slice(h * HEAD_DIM, (h + 1) * HEAD_DIM)
            kn_ref[:, sl] = _rms(k_ref[:, sl].astype(F32), kg_ref[...]).astype(BF16)

    row = lax.broadcasted_iota(jnp.int32, (tq, tk), 0) + i * tq
    col = lax.broadcasted_iota(jnp.int32, (tq, tk), 1)
    n_kv = (i + 1) * (tq // tk)
    for h in range(N_HEADS):
        sl = slice(h * HEAD_DIM, (h + 1) * HEAD_DIM)
        qn = (_rms(q_ref[:, sl].astype(F32), qg_ref[...]) * (HEAD_DIM ** -0.5)).astype(BF16)
        fq = fcol_ref[:, SM_F + h:SM_F + h + 1]

        def body(j, carry, sl=sl, qn=qn, fq=fq, h=h):
            m, l, acc = carry
            start = pl.multiple_of(j * tk, tk)
            ks = kn_ref[pl.ds(start, tk), sl]
            vs = v_ref[pl.ds(start, tk), sl]
            fk = frow_ref[0, j, h:h + 1, :]
            s = _dot_nt(qn, ks) + (fq - fk)
            s = jnp.where(col + j * tk <= row, s, NEG_BIG)
            m_new = jnp.maximum(m, jnp.max(s, axis=1, keepdims=True))
            p = jnp.exp(s - m_new)
            alpha = jnp.exp(m - m_new)
            l = alpha * l + jnp.sum(p, axis=1, keepdims=True)
            acc = alpha * acc + _dot(p.astype(BF16), vs)
            return m_new, l, acc

        init = (jnp.full((tq, 1), NEG_BIG, F32), jnp.zeros((tq, 1), F32), jnp.zeros((tq, HEAD_DIM), F32))
        _, l, acc = lax.fori_loop(0, n_kv, body, init)
        o_ref[:, sl] = (acc / l).astype(o_ref.dtype)


def fox_attention(p, fcol, frow, qg, kg, batch, seq):
    nq = seq // FOX_TQ
    base = COL_FOX // MIX_W
    return pl.pallas_call(
        _fox_kernel,
        grid=(batch, nq),
        in_specs=[
            pl.BlockSpec((FOX_TQ, MIX_W), lambda b, i: (b * nq + i, base)),
            pl.BlockSpec((seq, MIX_W), lambda b, i: (b, base + 1)),
            pl.BlockSpec((seq, MIX_W), lambda b, i: (b, base + 2)),
            pl.BlockSpec((FOX_TQ, SM_W), lambda b, i: (b * nq + i, 0)),
            pl.BlockSpec((1, seq // FOX_TK, 8, FOX_TK), lambda b, i: (b, 0, 0, 0)),
            pl.BlockSpec((1, HEAD_DIM), lambda b, i: (0, 0)),
            pl.BlockSpec((1, HEAD_DIM), lambda b, i: (0, 0)),
        ],
        out_specs=pl.BlockSpec((FOX_TQ, MIX_W), lambda b, i: (b * nq + i, 0)),
        out_shape=jax.ShapeDtypeStruct((batch * seq, MIX_W), BF16),
        scratch_shapes=[pltpu.VMEM((seq, MIX_W), BF16)],
        compiler_params=_cparams(("parallel", "arbitrary")),
        name="fox_attention",
    )(p, p, p, fcol, frow, qg, kg)


DSA_TQ = 128
DSA_TK = 256
BAND_W = 2 * DSA_TQ


def _t5_bucket(dist):
    max_exact = N_BUCKETS // 2
    d = jnp.maximum(dist, 0)
    log_ratio = jnp.log(jnp.maximum(d, 1).astype(F32) / max_exact) / math.log(MAX_DISTANCE / max_exact)
    large = jnp.minimum(max_exact + (log_ratio * (N_BUCKETS - max_exact)).astype(jnp.int32), N_BUCKETS - 1)
    return jnp.where(d < max_exact, d, large)


def _dsa_kernel(cq_ref, k_ref, v_ref, smq_ref, smk_ref, cqg_ref, wuq_ref, wqi_ref, qg_ref, kg_ref, rb_ref,
                o_ref, kn_ref, ki_ref, band_ref, sc_ref, qh_ref, qi_ref, m_ref, l_ref, acc_ref, *, topk):
    b = pl.program_id(0)
    i = pl.program_id(1)
    tq, tk = DSA_TQ, DSA_TK
    nkc = sc_ref.shape[0]
    seq = nkc * tk
    tiles = tk // tq
    nb = ((i + 1) * tq + tk - 1) // tk

    @pl.when(jnp.logical_and(b == 0, i == 0))
    def _():
        r = lax.broadcasted_iota(jnp.int32, (tq, BAND_W), 0)
        c = lax.broadcasted_iota(jnp.int32, (tq, BAND_W), 1)
        bucket = _t5_bucket(tq + r - c)
        for h in range(N_HEADS):
            far = rb_ref[N_BUCKETS - 1, h]
            acc = jnp.zeros((tq, BAND_W), F32)
            for bk in range(N_BUCKETS - 1):
                acc = jnp.where(bucket == bk, rb_ref[bk, h] - far, acc)
            band_ref[h] = acc

    @pl.when(i == 0)
    def _():
        kn_ref[...] = _rms(k_ref[...].astype(F32), kg_ref[...]).astype(BF16)
        ki_ref[...] = smk_ref[:, SM_IK:SM_IK + IDX_DIM].astype(BF16)

    cq = _rms(cq_ref[...].astype(F32), cqg_ref[...]).astype(BF16)
    qf = _dot(cq, wuq_ref[...])
    for h in range(N_HEADS):
        sl = slice(h * HEAD_DIM, (h + 1) * HEAD_DIM)
        qh_ref[h] = (_rms(qf[:, sl], qg_ref[...]) * (HEAD_DIM ** -0.5)).astype(BF16)
    q_idx = (_dot(cq, wqi_ref[...]) * (IDX_DIM ** -0.5)).astype(BF16)
    for h in range(IDX_HEADS):
        qi_ref[h] = q_idx[:, h * IDX_DIM:(h + 1) * IDX_DIM]

    qpos = lax.broadcasted_iota(jnp.int32, (tq, tk), 0) + i * tq
    col = lax.broadcasted_iota(jnp.int32, (tq, tk), 1)

    def score_body(j, _):
        start = pl.multiple_of(j * tk, tk)
        kj = ki_ref[pl.ds(start, tk), :]
        w_h = smq_ref[:, SM_IW:SM_IW + IDX_HEADS] * (IDX_HEADS ** -0.5)
        acc = jnp.zeros((tq, tk), F32)
        for h in range(IDX_HEADS):
            acc = acc + w_h[:, h:h + 1] * jnp.maximum(_dot_nt(qi_ref[h], kj), 0.0)
        sc_ref[j] = jnp.where(col + start <= qpos, acc, -jnp.inf)
        return 0

    lax.fori_loop(0, nb, score_body, 0)

    def over_chunks(fn, init):
        acc = init
        for j in range(nkc):
            acc = lax.cond(j < nb, functools.partial(fn, j), lambda a: a, acc)
        return acc

    def lane_tiles(x):
        return [x[:, t * tq:(t + 1) * tq] for t in range(tiles)]

    def row_total(x):
        return jnp.sum(x, axis=1, keepdims=True)

    def search():
        kf = float(topk)

        def max_fn(j, a):
            for x in lane_tiles(sc_ref[j]):
                a = jnp.maximum(a, x)
            return a

        def min_fn(j, a):
            for x in lane_tiles(sc_ref[j]):
                a = jnp.minimum(a, jnp.where(x == -jnp.inf, jnp.inf, x))
            return a

        smax = jnp.max(over_chunks(max_fn, jnp.full((tq, tq), -jnp.inf, F32)), axis=1, keepdims=True)
        smin = jnp.min(over_chunks(min_fn, jnp.full((tq, tq), jnp.inf, F32)), axis=1, keepdims=True)

        def count_ge(t):
            tb = jnp.broadcast_to(t, (tq, tq))

            def fn(j, a):
                for x in lane_tiles(sc_ref[j]):
                    a = a + jnp.where(x >= tb, 1.0, 0.0)
                return a

            return row_total(over_chunks(fn, jnp.zeros((tq, tq), F32)))

        def midpoint(lo, hi):
            return jnp.where(hi == jnp.inf, smax, 0.5 * (lo + hi))

        def undecided(lo, hi, c_lo, mid):
            return jnp.logical_and(c_lo != kf, jnp.logical_and(mid > lo, mid < hi))

        def cond(carry):
            return jnp.logical_and(carry[0] < 400, carry[1] > 0.0)

        def body(carry):
            it, _, lo, hi, c_lo, c_hi, mid = carry
            upd = undecided(lo, hi, c_lo, mid)
            cnt = count_ge(mid)
            up = jnp.logical_and(upd, cnt >= kf)
            dn = jnp.logical_and(upd, cnt < kf)
            lo = jnp.where(up, mid, lo)
            c_lo = jnp.where(up, cnt, c_lo)
            hi = jnp.where(dn, mid, hi)
            c_hi = jnp.where(dn, cnt, c_hi)
            mid = midpoint(lo, hi)
            active = jnp.max(jnp.where(undecided(lo, hi, c_lo, mid), 1.0, 0.0))
            return it + 1, active, lo, hi, c_lo, c_hi, mid

        lo0 = smin
        hi0 = jnp.full((tq, 1), jnp.inf, F32)
        c_lo0 = count_ge(lo0)
        c_hi0 = jnp.zeros((tq, 1), F32)
        mid0 = midpoint(lo0, hi0)
        act0 = jnp.max(jnp.where(undecided(lo0, hi0, c_lo0, mid0), 1.0, 0.0))
        _, _, lo, hi, c_lo, c_hi, _ = lax.while_loop(
            cond, body, (jnp.int32(0), act0, lo0, hi0, c_lo0, c_hi0, mid0))

        def tie_search():
            need = kf - c_hi
            lo_b = jnp.broadcast_to(lo, (tq, tq))
            hi_b = jnp.broadcast_to(hi, (tq, tq))
            lane = lax.broadcasted_iota(jnp.int32, (tq, tq), 1)

            def tie_body(_, carry):
                jlo, jhi = carry
                jm = (jlo + jhi) // 2
                jm_b = jnp.broadcast_to(jm, (tq, tq))

                def fn(j, a):
                    for t, x in enumerate(lane_tiles(sc_ref[j])):
                        hit = jnp.logical_and(jnp.logical_and(x >= lo_b, x < hi_b), lane + (j * tk + t * tq) <= jm_b)
                        a = a + jnp.where(hit, 1.0, 0.0)
                    return a

                ok = row_total(over_chunks(fn, jnp.zeros((tq, tq), F32))) >= need
                return jnp.where(ok, jlo, jm), jnp.where(ok, jm, jhi)

            n_bits = int(math.ceil(math.log2(seq))) + 1
            _, jmax = lax.fori_loop(0, n_bits, tie_body,
                                    (jnp.full((tq, 1), -1, jnp.int32), jnp.full((tq, 1), seq - 1, jnp.int32)))
            return jmax

        any_tie = jnp.max(jnp.where(c_lo != kf, 1.0, 0.0)) > 0.0
        jmax = lax.cond(any_tie, tie_search, lambda: jnp.full((tq, 1), seq - 1, jnp.int32))
        return lo, hi, jmax

    def keep_all():
        return (jnp.full((tq, 1), -jnp.inf, F32), jnp.full((tq, 1), jnp.inf, F32),
                jnp.full((tq, 1), seq - 1, jnp.int32))

    lo, hi, jmax = lax.cond((i + 1) * tq > topk, search, keep_all)

    m_ref[...] = jnp.full(m_ref.shape, NEG_BIG, F32)
    l_ref[...] = jnp.zeros(l_ref.shape, F32)
    acc_ref[...] = jnp.zeros(acc_ref.shape, F32)

    def attend_body(j, _):
        start = pl.multiple_of(j * tk, tk)
        ks = kn_ref[pl.ds(start, tk), :]
        vs = v_ref[pl.ds(start, tk), :]
        sc = sc_ref[j]
        kpos = col + start
        keep = jnp.logical_or(sc >= hi, jnp.logical_and(sc >= lo, kpos <= jmax))
        keep = jnp.logical_and(keep, kpos <= qpos)
        for h in range(N_HEADS):
            bias = jnp.concatenate(
                [jnp.where(j * tiles + t == i, band_ref[h, :, tq:2 * tq],
                           jnp.where(j * tiles + t == i - 1, band_ref[h, :, 0:tq], 0.0))
                 for t in range(tiles)], axis=1)
            s = jnp.where(keep, _dot_nt(qh_ref[h], ks) + bias, NEG_BIG)
            m_old = m_ref[h]
            m_new = jnp.maximum(m_old, jnp.max(s, axis=1, keepdims=True))
            p = jnp.exp(s - m_new)
            alpha = jnp.exp(m_old - m_new)
            l_ref[h] = alpha * l_ref[h] + jnp.sum(p, axis=1, keepdims=True)
            acc_ref[h] = alpha * acc_ref[h] + _dot(p.astype(BF16), vs)
            m_ref[h] = m_new
        return 0

    lax.fori_loop(0, nb, attend_body, 0)
    for h in range(N_HEADS):
        o_ref[:, h * HEAD_DIM:(h + 1) * HEAD_DIM] = (acc_ref[h] / l_ref[h]).astype(o_ref.dtype)


SUBLANES = 8
COUNT_ROWS = 64


def _fold_rows(x, op):
    return op(x.reshape(x.shape[0] // SUBLANES, SUBLANES, x.shape[1]), axis=0)


def _dsa_t_kernel(cq_ref, k_ref, v_ref, smq_ref, smk_ref, cqg_ref, wuq_ref, wqi_ref, qg_ref, kg_ref, rb_ref,
                  o_ref, kn_ref, ki_ref, vt_ref, band_ref, sc_ref, qt_ref, xi_ref, acc_ref, *, topk):
    b = pl.program_id(0)
    i = pl.program_id(1)
    tq, tk = DSA_TQ, DSA_TK
    nkc = sc_ref.shape[0]
    seq = nkc * tk
    tiles = tk // tq
    nb = ((i + 1) * tq + tk - 1) // tk

    @pl.when(jnp.logical_and(b == 0, i == 0))
    def _():
        c = lax.broadcasted_iota(jnp.int32, (BAND_W, tq), 0)
        r = lax.broadcasted_iota(jnp.int32, (BAND_W, tq), 1)
        bucket = _t5_bucket(tq + r - c)
        for h in range(N_HEADS):
            far = rb_ref[N_BUCKETS - 1, h]
            acc = jnp.zeros((BAND_W, tq), F32)
            for bk in range(N_BUCKETS - 1):
                acc = jnp.where(bucket == bk, rb_ref[bk, h] - far, acc)
            band_ref[h] = acc

    @pl.when(i == 0)
    def _():
        kn_ref[...] = _rms(k_ref[...].astype(F32), kg_ref[...]).astype(BF16)
        ki_ref[...] = smk_ref[:, SM_IK:SM_IK + IDX_DIM].astype(BF16)
        for j in range(nkc):
            for t in range(tiles):
                rows = slice(j * tk + t * tq, j * tk + (t + 1) * tq)
                vt_ref[j, :, t * tq:(t + 1) * tq] = v_ref[rows, :].astype(F32).T.astype(BF16)

    cq_t = _rms(cq_ref[...].astype(F32), cqg_ref[...]).T.astype(BF16)
    q_t = _dot(wuq_ref[...], cq_t)
    g_col = jnp.broadcast_to(qg_ref[...], (HEAD_DIM, tq))
    for h in range(N_HEADS):
        x = q_t[h * HEAD_DIM:(h + 1) * HEAD_DIM, :]
        inv = lax.rsqrt(jnp.mean(x * x, axis=0, keepdims=True) + EPS)
        qt_ref[:, h * tq:(h + 1) * tq] = (x * inv * g_col * (HEAD_DIM ** -0.5)).astype(BF16)
    qi_t = (_dot(wqi_ref[...], cq_t) * (IDX_DIM ** -0.5)).astype(BF16)
    for h in range(IDX_HEADS):
        xi_ref[:, h * tq:(h + 1) * tq] = qi_t[h * IDX_DIM:(h + 1) * IDX_DIM, :]
    w_rows = smq_ref[...].T[SM_IW:SM_IW + IDX_HEADS, :] * (IDX_HEADS ** -0.5)

    kofs = lax.broadcasted_iota(jnp.int32, (tk, tq), 0)
    qpos = lax.broadcasted_iota(jnp.int32, (tk, tq), 1) + i * tq

    def score_body(j, _):
        start = pl.multiple_of(j * tk, tk)
        kj = ki_ref[pl.ds(start, tk), :]
        acc = jnp.zeros((tk, tq), F32)
        for h2 in range(IDX_HEADS // 2):
            r = _dot(kj, xi_ref[:, 2 * h2 * tq:(2 * h2 + 2) * tq])
            for h in (2 * h2, 2 * h2 + 1):
                acc = acc + w_rows[h:h + 1, :] * jnp.maximum(r[:, (h - 2 * h2) * tq:(h - 2 * h2 + 1) * tq], 0.0)
        sc_ref[j] = jnp.where(kofs + start <= qpos, acc, -jnp.inf)
        return 0

    lax.fori_loop(0, nb, score_body, 0)

    @pl.when(nb % 2 == 1)
    def _():
        sc_ref[jnp.minimum(nb, nkc - 1)] = jnp.full((tk, tq), -jnp.inf, F32)

    def over_chunks(fn, init):
        acc = init
        for j in range(nkc):
            acc = lax.cond(j < nb, functools.partial(fn, j), lambda a: a, acc)
        return acc

    def count_where(pred_fn):
        def walk(n_chunks):
            def run():
                a = jnp.zeros((COUNT_ROWS, tq), F32)
                for j in range(n_chunks):
                    hit = jnp.where(pred_fn(j, sc_ref[j]), 1.0, 0.0)
                    a = a + jnp.sum(hit.reshape(tk // COUNT_ROWS, COUNT_ROWS, tq), axis=0)
                return a
            return run

        extents = sorted({min(n, nkc) for n in range(2, nkc + 2, 2)})
        a = lax.switch((nb - 1) // 2, [walk(n) for n in extents])
        return jnp.sum(a, axis=0, keepdims=True)

    def search():
        kf = float(topk)
        smax = jnp.max(over_chunks(lambda j, a: jnp.maximum(a, _fold_rows(sc_ref[j], jnp.max)),
                                   jnp.full((SUBLANES, tq), -jnp.inf, F32)), axis=0, keepdims=True)
        smin = jnp.min(over_chunks(
            lambda j, a: jnp.minimum(a, _fold_rows(jnp.where(sc_ref[j] == -jnp.inf, jnp.inf, sc_ref[j]), jnp.min)),
            jnp.full((SUBLANES, tq), jnp.inf, F32)), axis=0, keepdims=True)

        def count_ge(t):
            return count_where(lambda j, x: x >= t)

        def midpoint(lo, hi):
            return jnp.where(hi == jnp.inf, smax, 0.5 * (lo + hi))

        def undecided(lo, hi, c_lo, mid):
            return jnp.logical_and(c_lo != kf, jnp.logical_and(mid > lo, mid < hi))

        def cond(carry):
            return jnp.logical_and(carry[0] < 400, carry[1] > 0.0)

        def body(carry):
            it, _, lo, hi, c_lo, c_hi, mid = carry
            upd = undecided(lo, hi, c_lo, mid)
            cnt = count_ge(mid)
            up = jnp.logical_and(upd, cnt >= kf)
            dn = jnp.logical_and(upd, cnt < kf)
            lo = jnp.where(up, mid, lo)
            c_lo = jnp.where(up, cnt, c_lo)
            hi = jnp.where(dn, mid, hi)
            c_hi = jnp.where(dn, cnt, c_hi)
            mid = midpoint(lo, hi)
            active = jnp.max(jnp.where(undecided(lo, hi, c_lo, mid), 1.0, 0.0))
            return it + 1, active, lo, hi, c_lo, c_hi, mid

        lo0 = smin
        hi0 = jnp.full((1, tq), jnp.inf, F32)
        c_lo0 = count_ge(lo0)
        c_hi0 = jnp.zeros((1, tq), F32)
        mid0 = midpoint(lo0, hi0)
        act0 = jnp.max(jnp.where(undecided(lo0, hi0, c_lo0, mid0), 1.0, 0.0))
        _, _, lo, hi, c_lo, c_hi, _ = lax.while_loop(
            cond, body, (jnp.int32(0), act0, lo0, hi0, c_lo0, c_hi0, mid0))

        def tie_search():
            need = kf - c_hi

            def tie_body(_, carry):
                jlo, jhi = carry
                jm = (jlo + jhi) // 2
                cnt = count_where(lambda j, x: jnp.logical_and(jnp.logical_and(x >= lo, x < hi),
                                                               kofs + j * tk <= jm))
                ok = cnt >= need
                return jnp.where(ok, jlo, jm), jnp.where(ok, jm, jhi)

            n_bits = int(math.ceil(math.log2(seq))) + 1
            _, jmax = lax.fori_loop(0, n_bits, tie_body,
                                    (jnp.full((1, tq), -1, jnp.int32), jnp.full((1, tq), seq - 1, jnp.int32)))
            return jmax

        any_tie = jnp.max(jnp.where(c_lo != kf, 1.0, 0.0)) > 0.0
        jmax = lax.cond(any_tie, tie_search, lambda: jnp.full((1, tq), seq - 1, jnp.int32))
        return lo, hi, jmax

    def keep_all():
        return (jnp.full((1, tq), -jnp.inf, F32), jnp.full((1, tq), jnp.inf, F32),
                jnp.full((1, tq), seq - 1, jnp.int32))

    lo, hi, jmax = lax.cond((i + 1) * tq > topk, search, keep_all)

    acc_ref[...] = jnp.zeros(acc_ref.shape, F32)

    def attend_body(near, j, carry):
        ms, ls = carry
        start = pl.multiple_of(j * tk, tk)
        kc = kn_ref[pl.ds(start, tk), :]
        vt = vt_ref[j]
        sc = sc_ref[j]
        kpos = kofs + start
        keep = jnp.logical_or(sc >= hi, jnp.logical_and(sc >= lo, kpos <= jmax))
        if near:
            keep = jnp.logical_and(keep, kpos <= qpos)
        new_ms, new_ls = [], []
        for h in range(N_HEADS):
            hs = slice(h * tq, (h + 1) * tq)
            s = _dot(kc, qt_ref[:, hs])
            if near:
                s = s + jnp.concatenate(
                    [jnp.where(j * tiles + t == i, band_ref[h, tq:2 * tq, :],
                               jnp.where(j * tiles + t == i - 1, band_ref[h, 0:tq, :], 0.0))
                     for t in range(tiles)], axis=0)
            s = jnp.where(keep, s, NEG_BIG)
            m_new = jnp.maximum(ms[h], jnp.max(s, axis=0, keepdims=True))
            p = jnp.exp(s - m_new)
            alpha = jnp.exp(ms[h] - m_new)
            new_ls.append(alpha * ls[h] + jnp.sum(p, axis=0, keepdims=True))
            acc_ref[:, hs] = alpha * acc_ref[:, hs] + _dot(vt, p.astype(BF16))
            new_ms.append(m_new)
        return tuple(new_ms), tuple(new_ls)

    init = (tuple(jnp.full((1, tq), NEG_BIG, F32) for _ in range(N_HEADS)),
            tuple(jnp.zeros((1, tq), F32) for _ in range(N_HEADS)))
    n_far = jnp.maximum((i - 1) * tq // tk, 0)
    carry = lax.fori_loop(0, n_far, functools.partial(attend_body, False), init)
    _, ls = lax.fori_loop(n_far, nb, functools.partial(attend_body, True), carry)
    for h in range(N_HEADS):
        out_t = acc_ref[:, h * tq:(h + 1) * tq] / ls[h]
        o_ref[:, h * HEAD_DIM:(h + 1) * HEAD_DIM] = out_t.T.astype(o_ref.dtype)


def dsa_attention(p, sm, cqg, wuq_t, wqi_t, qg_col, kg, rel_bias, batch, seq):
    nq = seq // DSA_TQ
    nkc = seq // DSA_TK
    topk = min(DSA_TOPK, seq // 4)
    kern = functools.partial(_dsa_t_kernel, topk=topk)
    return pl.pallas_call(
        kern,
        grid=(batch, nq),
        in_specs=[
            pl.BlockSpec((DSA_TQ, DSA_Q_RANK), lambda b, i: (b * nq + i, COL_CQ // DSA_Q_RANK)),
            pl.BlockSpec((seq, HEAD_DIM), lambda b, i: (b, COL_DK // HEAD_DIM)),
            pl.BlockSpec((seq, HEAD_DIM), lambda b, i: (b, COL_DV // HEAD_DIM)),
            pl.BlockSpec((DSA_TQ, SM_W), lambda b, i: (b * nq + i, 0)),
            pl.BlockSpec((seq, SM_W), lambda b, i: (b, 0)),
            pl.BlockSpec((1, DSA_Q_RANK), lambda b, i: (0, 0)),
            pl.BlockSpec((N_HEADS * HEAD_DIM, DSA_Q_RANK), lambda b, i: (0, 0)),
            pl.BlockSpec((IDX_HEADS * IDX_DIM, DSA_Q_RANK), lambda b, i: (0, 0)),
            pl.BlockSpec((HEAD_DIM, 1), lambda b, i: (0, 0)),
            pl.BlockSpec((1, HEAD_DIM), lambda b, i: (0, 0)),
            pl.BlockSpec(memory_space=pltpu.SMEM),
        ],
        out_specs=pl.BlockSpec((DSA_TQ, MIX_W), lambda b, i: (b * nq + i, 0)),
        out_shape=jax.ShapeDtypeStruct((batch * seq, MIX_W), BF16),
        scratch_shapes=[
            pltpu.VMEM((seq, HEAD_DIM), BF16),
            pltpu.VMEM((seq, IDX_DIM), BF16),
            pltpu.VMEM((nkc, HEAD_DIM, DSA_TK), BF16),
            pltpu.VMEM((N_HEADS, BAND_W, DSA_TQ), F32),
            pltpu.VMEM((nkc, DSA_TK, DSA_TQ), F32),
            pltpu.VMEM((HEAD_DIM, N_HEADS * DSA_TQ), BF16),
            pltpu.VMEM((IDX_DIM, IDX_HEADS * DSA_TQ), BF16),
            pltpu.VMEM((HEAD_DIM, N_HEADS * DSA_TQ), F32),
        ],
        compiler_params=_cparams(("arbitrary", "arbitrary")),
        name="dsa_attention",
    )(p, p, p, sm, sm, cqg, wuq_t, wqi_t, qg_col, kg, rel_bias)


def _causal_conv(x, xp, w_ref, b_ref):
    rows = lax.broadcasted_iota(jnp.int32, x.shape, 0)
    acc = x * w_ref[SSD_CONV - 1:SSD_CONV, :] + b_ref[...]
    for d in range(1, SSD_CONV):
        shifted = jnp.where(rows < d, pltpu.roll(xp, d, 0), pltpu.roll(x, d, 0))
        acc = acc + shifted * w_ref[SSD_CONV - 1 - d:SSD_CONV - d, :]
    return _silu(acc)


def _ssd_kernel(z_ref, xs_ref, bc_ref, xsp_ref, bcp_ref, sm_ref, cwx_ref, cbx_ref, cwb_ref, cbb_ref,
                dtb_ref, alog_ref, dsk_ref, ng_ref, o_ref, prev_ref, y_ref):
    c = CHUNK
    n = pl.program_id(1)

    @pl.when(n == 0)
    def _():
        prev_ref[...] = jnp.zeros_like(prev_ref)

    first = (n > 0).astype(F32)
    xs = _causal_conv(xs_ref[...].astype(F32), xsp_ref[...].astype(F32) * first, cwx_ref, cbx_ref)
    bc = _causal_conv(bc_ref[...].astype(F32), bcp_ref[...].astype(F32) * first, cwb_ref, cbb_ref)

    dt_t = _softplus(sm_ref[...].T + dtb_ref[...])
    cs_t = _cumsum_lanes(dt_t * (-jnp.exp(alog_ref[...])))
    cs = cs_t.T
    dt = dt_t.T
    ii = lax.broadcasted_iota(jnp.int32, (c, c), 0)
    jj = lax.broadcasted_iota(jnp.int32, (c, c), 1)
    tril = ii >= jj
    gn = SSD_GROUPS * SSD_STATE
    hpg = SSD_HEADS // SSD_GROUPS
    for g in range(SSD_GROUPS):
        bg = bc[:, g * SSD_STATE:(g + 1) * SSD_STATE]
        cg = bc[:, gn + g * SSD_STATE:gn + (g + 1) * SSD_STATE].astype(BF16)
        cb = _dot_nt(cg, bg.astype(BF16))
        y_off = _dot(cg, prev_ref[g].astype(BF16))
        for r in range(hpg):
            h = g * hpg + r
            hs = slice(h * SSD_HEAD_DIM, (h + 1) * SSD_HEAD_DIM)
            rs = slice(r * SSD_HEAD_DIM, (r + 1) * SSD_HEAD_DIM)
            a_col = cs[:, SM_DT + h:SM_DT + h + 1]
            a_row = cs_t[SM_DT + h:SM_DT + h + 1, :]
            last = cs_t[SM_DT + h:SM_DT + h + 1, c - 1:c]
            seg = jnp.where(tril, jnp.exp(jnp.where(tril, a_col - a_row, 0.0)), 0.0)
            xh = xs[:, hs]
            xc = (xh * dt[:, SM_DT + h:SM_DT + h + 1]).astype(BF16)
            y_diag = _dot((cb * seg).astype(BF16), xc)
            st = _dot_tn((bg * jnp.exp(last - a_col)).astype(BF16), xc)
            y_ref[:, hs] = y_diag + y_off[:, rs] * jnp.exp(a_col) + dsk_ref[:, hs] * xh
            prev_ref[g, :, rs] = jnp.exp(last) * prev_ref[g, :, rs] + st
    gated = y_ref[...] * _silu(z_ref[...].astype(F32))
    gw = SSD_INNER // SSD_GROUPS
    for g in range(SSD_GROUPS):
        sl = slice(g * gw, (g + 1) * gw)
        o_ref[:, sl] = _rms(gated[:, sl], ng_ref[:, sl]).astype(o_ref.dtype)


def ssd_mixer(p, sm, cw, cb, dtb_col, alog_col, dskip_row, ng, batch, seq):
    n = seq // CHUNK
    bcw = 2 * SSD_GROUPS * SSD_STATE

    def cur(width, colbase):
        return pl.BlockSpec((CHUNK, width), lambda b, i: (b * n + i, colbase // width))

    def prv(width, colbase):
        return pl.BlockSpec((CHUNK, width), lambda b, i: (b * n + jnp.maximum(i - 1, 0), colbase // width))

    def const(shape):
        return pl.BlockSpec(shape, lambda b, i: (0, 0))

    return pl.pallas_call(
        _ssd_kernel,
        grid=(batch, n),
        in_specs=[
            cur(SSD_INNER, COL_Z), cur(SSD_INNER, COL_XS), cur(bcw, COL_BC),
            prv(SSD_INNER, COL_XS), prv(bcw, COL_BC),
            pl.BlockSpec((CHUNK, SM_W), lambda b, i: (b * n + i, 0)),
            const((SSD_CONV, SSD_INNER)), const((1, SSD_INNER)),
            const((SSD_CONV, bcw)), const((1, bcw)),
            const((SM_W, 1)), const((SM_W, 1)),
            const((1, SSD_INNER)), const((1, SSD_INNER)),
        ],
        out_specs=pl.BlockSpec((CHUNK, SSD_INNER), lambda b, i: (b * n + i, 0)),
        out_shape=jax.ShapeDtypeStruct((batch * seq, SSD_INNER), BF16),
        scratch_shapes=[
            pltpu.VMEM((SSD_GROUPS, SSD_STATE, SSD_INNER // SSD_GROUPS), F32),
            pltpu.VMEM((CHUNK, SSD_INNER), F32),
        ],
        compiler_params=_cparams(("parallel", "arbitrary")),
        name="ssd_mixer",
    )(p, p, p, p, p, sm, cw[:, :SSD_INNER], cb[:, :SSD_INNER], cw[:, SSD_INNER:], cb[:, SSD_INNER:],
      dtb_col, alog_col, dskip_row, ng)


MERGE_TM = 256


def _merge_kernel(x_ref, gl_ref, gb_ref, oret_ref, ofox_ref, odsa_ref, ossd_ref, wbr_ref, wout_ref, o_ref):
    branches = (oret_ref, ofox_ref, odsa_ref, ossd_ref)
    merged = None
    row0 = 0
    for bi, br in enumerate(branches):
        width = br.shape[1]
        sl = slice(bi * D_MODEL, (bi + 1) * D_MODEL)
        gate = 1.0 / (1.0 + jnp.exp(-(gl_ref[:, sl].astype(F32) + gb_ref[:, sl])))
        term = gate * _dot(br[...], wbr_ref[row0:row0 + width, :])
        merged = term if merged is None else merged + term
        row0 += width
    o_ref[...] = x_ref[...] + _dot(merged.astype(BF16), wout_ref[...])


def merge_project(x, p, gate_b, o_ret, o_fox, o_dsa, o_ssd, w_br, w_out):
    m = x.shape[0]
    tm = MERGE_TM

    def rows(width):
        return pl.BlockSpec((tm, width), lambda i: (i, 0))

    def const(shape):
        return pl.BlockSpec(shape, lambda i: (0, 0), pipeline_mode=pl.Buffered(1))

    return pl.pallas_call(
        _merge_kernel,
        grid=(m // tm,),
        in_specs=[
            rows(D_MODEL), rows(N_BRANCH * D_MODEL), const((1, N_BRANCH * D_MODEL)),
            rows(MIX_W), rows(MIX_W), rows(MIX_W), rows(SSD_INNER),
            const(w_br.shape), const(w_out.shape),
        ],
        out_specs=rows(D_MODEL),
        out_shape=jax.ShapeDtypeStruct(x.shape, x.dtype),
        compiler_params=_cparams(("parallel",)),
        name="merge_project",
    )(x, p, gate_b, o_ret, o_fox, o_dsa, o_ssd, w_br, w_out)


FFN_TM = 1024
FFN_TF = 512


def _ffn_kernel(x_ref, g_ref, w1_ref, w2_ref, o_ref, h_ref):
    @pl.when(pl.program_id(1) == 0)
    def _():
        h_ref[...] = _rms(x_ref[...], g_ref[...]).astype(BF16)
        o_ref[...] = x_ref[...]

    a = jnp.maximum(_dot(h_ref[...], w1_ref[...]), 0.0)
    o_ref[...] += _dot((a * a).astype(BF16), w2_ref[...])


def ffn(x, g, w1, w2):
    m, d = x.shape
    dff = w1.shape[1]
    tm, tf = min(FFN_TM, m), FFN_TF
    return pl.pallas_call(
        _ffn_kernel,
        grid=(m // tm, dff // tf),
        in_specs=[
            pl.BlockSpec((tm, d), lambda i, f: (i, 0), pipeline_mode=pl.Buffered(1)),
            pl.BlockSpec((1, d), lambda i, f: (0, 0)),
            pl.BlockSpec((d, tf), lambda i, f: (0, f)),
            pl.BlockSpec((tf, d), lambda i, f: (f, 0)),
        ],
        out_specs=pl.BlockSpec((tm, d), lambda i, f: (i, 0)),
        out_shape=jax.ShapeDtypeStruct(x.shape, x.dtype),
        scratch_shapes=[pltpu.VMEM((tm, d), BF16)],
        compiler_params=_cparams(("parallel", "arbitrary")),
        name="ffn",
    )(x, g, w1, w2)


SRC_RET = 0
SRC_FOX = SRC_RET + 4 * MIX_W
SRC_FF = SRC_FOX + 3 * MIX_W
SRC_CQ = SRC_FF + N_HEADS
SRC_DK = SRC_CQ + DSA_Q_RANK
SRC_IK = SRC_DK + 2 * HEAD_DIM
SRC_IW = SRC_IK + IDX_DIM
SRC_Z = SRC_IW + IDX_HEADS
SRC_DT = SRC_Z + 2 * SSD_INNER + 2 * SSD_GROUPS * SSD_STATE
SRC_GATE = SRC_DT + SSD_HEADS
IN_TOTAL = SRC_GATE + N_BRANCH * D_MODEL
MAIN_RUNS = ((COL_GATE, SRC_GATE), (COL_RET, SRC_RET), (COL_Z, SRC_Z), (COL_CQ, SRC_CQ),
             (COL_FOX, SRC_FOX), (COL_DK, SRC_DK))
RELAYOUT_W = 512
RELAYOUT_TILES = RELAYOUT_W // LANES


def _relayout_tables():
    starts, shifts = [], []
    for blk in range(N_MAIN // RELAYOUT_W):
        o = blk * RELAYOUT_W
        dst, src = [r for r in MAIN_RUNS if r[0] <= o][-1]
        col = src + (o - dst)
        starts.append(col // LANES)
        shifts.append(col % LANES)
    return jnp.asarray(starts, jnp.int32), jnp.asarray(shifts, jnp.int32)


def _relayout_kernel(start_ref, shift_ref, *refs):
    del start_ref
    tiles, o_ref = refs[:-1], refs[-1]
    shift = shift_ref[pl.program_id(1)]
    amount = lax.rem(LANES - shift, LANES)
    lane = lax.broadcasted_iota(jnp.int32, tiles[0].shape, 1)
    rolled = [pltpu.roll(t[...], amount, 1) for t in tiles]
    for k in range(RELAYOUT_TILES):
        piece = jnp.where(lane < LANES - shift, rolled[k], rolled[k + 1])
        o_ref[:, k * LANES:(k + 1) * LANES] = piece.astype(o_ref.dtype)


def relayout_main(w_in):
    depth, d, n_src = w_in.shape
    last = (n_src - 1) // LANES
    starts, shifts = _relayout_tables()

    def tile(k):
        return pl.BlockSpec((None, d, LANES), lambda l, b, st, sh: (l, 0, jnp.minimum(st[b] + k, last)))

    return pl.pallas_call(
        _relayout_kernel,
        grid_spec=pltpu.PrefetchScalarGridSpec(
            num_scalar_prefetch=2,
            grid=(depth, N_MAIN // RELAYOUT_W),
            in_specs=[tile(k) for k in range(RELAYOUT_TILES + 1)],
            out_specs=pl.BlockSpec((None, d, RELAYOUT_W), lambda l, b, st, sh: (l, 0, b)),
        ),
        out_shape=jax.ShapeDtypeStruct((depth, d, N_MAIN), BF16),
        compiler_params=_cparams(("parallel", "arbitrary")),
        name="relayout_main",
    )(starts, shifts, *([w_in] * (RELAYOUT_TILES + 1)))


SMALL_PIECES = ((SRC_DT, SM_DT, SSD_HEADS), (SRC_FF, SM_F, N_HEADS), (SRC_IW, SM_IW, IDX_HEADS),
                (SRC_IK, SM_IK, IDX_DIM))


def _relayout_small_kernel(*refs):
    tiles, o_ref = refs[:-1], refs[-1]
    lane = lax.broadcasted_iota(jnp.int32, o_ref.shape, 1)
    out = jnp.zeros(o_ref.shape, F32)
    for t, (src, dst, width) in zip(tiles, SMALL_PIECES):
        moved = pltpu.roll(t[...], (dst - src % LANES) % LANES, 1)
        out = jnp.where(jnp.logical_and(lane >= dst, lane < dst + width), moved, out)
    o_ref[...] = out.astype(o_ref.dtype)


def relayout_small(w_in):
    depth, d, _ = w_in.shape
    for src, _, width in SMALL_PIECES:
        assert src // LANES == (src + width - 1) // LANES

    def tile(src):
        return pl.BlockSpec((None, d, LANES), lambda l: (l, 0, src // LANES))

    return pl.pallas_call(
        _relayout_small_kernel,
        grid=(depth,),
        in_specs=[tile(src) for src, _, _ in SMALL_PIECES],
        out_specs=pl.BlockSpec((None, d, SM_W), lambda l: (l, 0, 0)),
        out_shape=jax.ShapeDtypeStruct((depth, d, SM_W), BF16),
        compiler_params=_cparams(("parallel",)),
        name="relayout_small",
    )(*([w_in] * len(SMALL_PIECES)))


def _pad_to(v, offset, total):
    return jnp.zeros((total,), v.dtype).at[offset:offset + v.shape[0]].set(v)


def _rotary_tables(seq):
    half = HEAD_DIM // 2
    inv = 1.0 / (10000.0 ** (jnp.arange(half, dtype=F32) / half))
    ang = jnp.arange(seq, dtype=F32)[:, None] * inv[None, :]
    cos, sin = jnp.cos(ang), jnp.sin(ang)
    return jnp.concatenate([cos, cos], axis=1), jnp.concatenate([-sin, sin], axis=1)


def kernel(x, norm1_g, w_in, gate_b, fox_f_b, fox_qn_g, fox_kn_g, dsa_cq_g, dsa_w_uq, dsa_w_qidx, dsa_qn_g,
           dsa_kn_g, rel_bias, ssd_conv_w, ssd_conv_b, ssd_dt_bias, ssd_a_log, ssd_d, ssd_norm_g, w_br, w_out,
           norm2_g, w_ff1, w_ff2):
    batch, seq, d = x.shape
    tokens = batch * seq
    xt = x.reshape(tokens, d)
    cos, sin = _rotary_tables(seq)
    tm = min(1024, tokens)
    w_main = relayout_main(w_in)
    w_small = relayout_small(w_in)
    for l in range(DEPTH):
        g1 = norm1_g[l][None, :]
        p = norm_matmul(xt, g1, w_main, l, BF16, tm, 1024)
        sm = norm_matmul(xt, g1, w_small, l, F32, tm, SM_W)

        o_ret = retention(p, cos, sin, batch, seq)

        fb_row = _pad_to(fox_f_b[l], SM_F, SM_W)[None, :]
        fcol, frow = fox_prep(sm, fb_row, batch, seq)
        o_fox = fox_attention(p, fcol, frow, fox_qn_g[l][None, :], fox_kn_g[l][None, :], batch, seq)

        o_dsa = dsa_attention(p, sm, dsa_cq_g[l][None, :], dsa_w_uq[l].T.astype(BF16), dsa_w_qidx[l].T.astype(BF16),
                              dsa_qn_g[l][:, None], dsa_kn_g[l][None, :], rel_bias, batch, seq)

        o_ssd = ssd_mixer(p, sm, ssd_conv_w[l], ssd_conv_b[l][None, :],
                          _pad_to(ssd_dt_bias[l], SM_DT, SM_W)[:, None], _pad_to(ssd_a_log[l], SM_DT, SM_W)[:, None],
                          jnp.repeat(ssd_d[l], SSD_HEAD_DIM)[None, :], ssd_norm_g[l][None, :], batch, seq)

        xt = merge_project(xt, p, gate_b[l][None, :], o_ret, o_fox, o_dsa, o_ssd,
                           w_br[l].astype(BF16), w_out[l].astype(BF16))
        xt = ffn(xt, norm2_g[l][None, :], w_ff1[l].astype(BF16), w_ff2[l].astype(BF16))
    return xt.reshape(batch, seq, d)
```

```python
import functools
import math

import jax
import jax.numpy as jnp
from jax import lax
from jax.experimental import pallas as pl
from jax.experimental.pallas import tpu as pltpu

F32 = jnp.float32
BF16 = jnp.bfloat16

D_MODEL = 2048
DEPTH = 4
HEAD_DIM = 128
N_HEADS = 4
DSA_Q_RANK = 512
IDX_HEADS = 16
IDX_DIM = 64
DSA_TOPK = 256
SSD_HEADS = 16
SSD_HEAD_DIM = 64
SSD_GROUPS = 2
SSD_STATE = 128
SSD_CONV = 4
SSD_INNER = SSD_HEADS * SSD_HEAD_DIM
D_FF = 4 * D_MODEL
N_BUCKETS = 32
MAX_DISTANCE = 128
CHUNK = 128
EPS = 1e-6
N_BRANCH = 4
MIX_W = N_HEADS * HEAD_DIM

COL_GATE = 0
COL_RET = COL_GATE + N_BRANCH * D_MODEL
COL_Z = COL_RET + 4 * MIX_W
COL_XS = COL_Z + SSD_INNER
COL_BC = COL_XS + SSD_INNER
COL_CQ = COL_BC + 2 * SSD_GROUPS * SSD_STATE
COL_FOX = COL_CQ + DSA_Q_RANK
COL_DK = COL_FOX + 3 * MIX_W
COL_DV = COL_DK + HEAD_DIM
N_MAIN_USED = COL_DV + HEAD_DIM
N_MAIN = 15360
SM_DT = 0
SM_F = 16
SM_IW = 32
SM_IK = 64
SM_W = 128

LANES = 128
VMEM_LIMIT = 56 * 1024 * 1024
NEG_BIG = -1e30


def _cparams(sem):
    return pltpu.CompilerParams(dimension_semantics=sem, vmem_limit_bytes=VMEM_LIMIT)


def _dot(a, b):
    return jnp.dot(a, b, preferred_element_type=F32)


def _dot_nt(a, b):
    return lax.dot_general(a, b, (((1,), (1,)), ((), ())), preferred_element_type=F32)


def _dot_tn(a, b):
    return lax.dot_general(a, b, (((0,), (0,)), ((), ())), preferred_element_type=F32)


def _rms(x, g):
    return x * lax.rsqrt(jnp.mean(x * x, axis=-1, keepdims=True) + EPS) * g


def _silu(x):
    return x / (1.0 + jnp.exp(-x))


def _softplus(x):
    return jnp.maximum(x, 0.0) + jnp.log1p(jnp.exp(-jnp.abs(x)))


def _cumsum_lanes(x):
    lane = lax.broadcasted_iota(jnp.int32, x.shape, 1)
    d = 1
    while d < x.shape[1]:
        x = x + jnp.where(lane >= d, pltpu.roll(x, d, 1), 0.0)
        d *= 2
    return x


def _norm_matmul_kernel(x_ref, g_ref, w_ref, o_ref, h_ref):
    @pl.when(pl.program_id(1) == 0)
    def _():
        h_ref[...] = _rms(x_ref[...], g_ref[...]).astype(BF16)

    o_ref[...] = _dot(h_ref[...], w_ref[...]).astype(o_ref.dtype)


def norm_matmul(x, g, w, layer, out_dtype, tm, tn):
    m, d = x.shape
    n = w.shape[2]
    return pl.pallas_call(
        _norm_matmul_kernel,
        grid=(m // tm, n // tn),
        in_specs=[
            pl.BlockSpec((tm, d), lambda i, j: (i, 0)),
            pl.BlockSpec((1, d), lambda i, j: (0, 0)),
            pl.BlockSpec((None, d, tn), lambda i, j: (layer, 0, j)),
        ],
        out_specs=pl.BlockSpec((tm, tn), lambda i, j: (i, j)),
        out_shape=jax.ShapeDtypeStruct((m, n), out_dtype),
        scratch_shapes=[pltpu.VMEM((tm, d), BF16)],
        compiler_params=_cparams(("parallel", "arbitrary")),
        name="norm_matmul",
    )(x, g, w)


def _retention_kernel(q_ref, k_ref, v_ref, g_ref, cos_ref, sin_ref, o_ref, state_ref):
    c = CHUNK

    @pl.when(pl.program_id(1) == 0)
    def _():
        state_ref[...] = jnp.zeros_like(state_ref)

    cos = cos_ref[...]
    sin = sin_ref[...]
    ii = lax.broadcasted_iota(jnp.int32, (c, c), 0)
    jj = lax.broadcasted_iota(jnp.int32, (c, c), 1)
    rel = (ii - jj).astype(F32)
    i_col = lax.broadcasted_iota(jnp.int32, (c, 1), 0).astype(F32)
    for h in range(N_HEADS):
        lg = math.log1p(-(2.0 ** (-5.0 - h)))
        sl = slice(h * HEAD_DIM, (h + 1) * HEAD_DIM)
        q = q_ref[:, sl].astype(F32)
        k = k_ref[:, sl].astype(F32)
        v = v_ref[:, sl]
        qr = q * cos + pltpu.roll(q, HEAD_DIM // 2, 1) * sin
        kr = (k * cos + pltpu.roll(k, HEAD_DIM // 2, 1) * sin) * (HEAD_DIM ** -0.5)
        decay = jnp.where(rel >= 0, jnp.exp(lg * jnp.maximum(rel, 0.0)), 0.0)
        scores = _dot_nt(qr.astype(BF16), kr.astype(BF16)) * decay
        y = _dot(scores.astype(BF16), v)
        q_dec = jnp.exp(lg * (i_col + 1.0))
        k_dec = jnp.exp(lg * (c - 1.0 - i_col))
        st = state_ref[h]
        y = y + _dot((qr * q_dec).astype(BF16), st.astype(BF16))
        kv = _dot_tn((kr * k_dec).astype(BF16), v)
        state_ref[h] = math.exp(lg * c) * st + kv
        yc = y - jnp.mean(y, axis=-1, keepdims=True)
        yn = yc * lax.rsqrt(jnp.mean(yc * yc, axis=-1, keepdims=True) + EPS)
        o_ref[:, sl] = (_silu(g_ref[:, sl].astype(F32)) * yn).astype(o_ref.dtype)


def retention(p, cos, sin, batch, seq):
    n = seq // CHUNK
    base = COL_RET // MIX_W

    def col(j):
        return pl.BlockSpec((CHUNK, MIX_W), lambda b, i: (b * n + i, base + j))

    tab = pl.BlockSpec((CHUNK, HEAD_DIM), lambda b, i: (i, 0))
    return pl.pallas_call(
        _retention_kernel,
        grid=(batch, n),
        in_specs=[col(0), col(1), col(2), col(3), tab, tab],
        out_specs=pl.BlockSpec((CHUNK, MIX_W), lambda b, i: (b * n + i, 0)),
        out_shape=jax.ShapeDtypeStruct((batch * seq, MIX_W), BF16),
        scratch_shapes=[pltpu.VMEM((N_HEADS, HEAD_DIM, HEAD_DIM), F32)],
        compiler_params=_cparams(("parallel", "arbitrary")),
        name="retention",
    )(p, p, p, p, cos, sin)


def _fox_prep_kernel(sm_ref, fb_ref, fcol_ref, frow_ref, carry_ref):
    @pl.when(pl.program_id(1) == 0)
    def _():
        carry_ref[...] = jnp.zeros_like(carry_ref)

    t = sm_ref[...] + fb_ref[...]
    lf = jnp.minimum(t, 0.0) - jnp.log1p(jnp.exp(-jnp.abs(t)))
    cs = _cumsum_lanes(lf.T) + carry_ref[...]
    carry_ref[...] = cs[:, LANES - 1:LANES]
    frow_ref[0, 0] = cs[SM_F:SM_F + 8, :]
    fcol_ref[...] = cs.T


def fox_prep(sm, fb_row, batch, seq):
    n = seq // CHUNK
    return pl.pallas_call(
        _fox_prep_kernel,
        grid=(batch, n),
        in_specs=[
            pl.BlockSpec((CHUNK, SM_W), lambda b, i: (b * n + i, 0)),
            pl.BlockSpec((1, SM_W), lambda b, i: (0, 0)),
        ],
        out_specs=[
            pl.BlockSpec((CHUNK, SM_W), lambda b, i: (b * n + i, 0)),
            pl.BlockSpec((1, 1, 8, CHUNK), lambda b, i: (b, i, 0, 0)),
        ],
        out_shape=[
            jax.ShapeDtypeStruct((batch * seq, SM_W), F32),
            jax.ShapeDtypeStruct((batch, n, 8, CHUNK), F32),
        ],
        scratch_shapes=[pltpu.VMEM((SM_W, 1), F32)],
        compiler_params=_cparams(("parallel", "arbitrary")),
        name="fox_prep",
    )(sm, fb_row)


FOX_T = 256


def _fox_kernel(q_ref, k_ref, v_ref, fcol_ref, frow_ref, qg_ref, kg_ref, o_ref,
                kn_ref, vt_ref, fb_ref, qt_ref, acc_ref):
    i = pl.program_id(1)
    t = FOX_T
    nkc = vt_ref.shape[0]
    sub = t // CHUNK

    @pl.when(i == 0)
    def _():
        for h in range(N_HEADS):
            sl = slice(h * HEAD_DIM, (h + 1) * HEAD_DIM)
            kn_ref[:, sl] = _rms(k_ref[:, sl].astype(F32), kg_ref[...]).astype(BF16)
            fb_ref[h] = jnp.broadcast_to(fcol_ref[:, SM_F + h:SM_F + h + 1], fb_ref.shape[1:])
            for j in range(nkc):
                for c in range(sub):
                    rows = slice(j * t + c * CHUNK, j * t + (c + 1) * CHUNK)
                    vt_ref[j, h, :, c * CHUNK:(c + 1) * CHUNK] = v_ref[rows, sl].astype(F32).T.astype(BF16)

    fqs = []
    for h in range(N_HEADS):
        sl = slice(h * HEAD_DIM, (h + 1) * HEAD_DIM)
        qn = _rms(q_ref[:, sl].astype(F32), qg_ref[...]) * (HEAD_DIM ** -0.5)
        qt_ref[h] = jnp.concatenate([qn[c * CHUNK:(c + 1) * CHUNK, :].T for c in range(sub)], axis=1).astype(BF16)
        fqs.append(jnp.concatenate([frow_ref[0, i * sub + c, h:h + 1, :] for c in range(sub)], axis=1))
        acc_ref[h] = jnp.zeros(acc_ref.shape[1:], F32)

    kofs = lax.broadcasted_iota(jnp.int32, (t, t), 0)
    qofs = lax.broadcasted_iota(jnp.int32, (t, t), 1)

    def body(diag, j, carry):
        ms, ls = carry
        start = pl.multiple_of(j * t, t)
        scores = [_dot(kn_ref[pl.ds(start, t), h * HEAD_DIM:(h + 1) * HEAD_DIM], qt_ref[h])
                  for h in range(N_HEADS)]
        new_ms, new_ls, ps, alphas = [], [], [], []
        for h in range(N_HEADS):
            fk = fb_ref[h, pl.ds(start, t), :]
            s = scores[h] + fqs[h] - jnp.concatenate([fk] * (t // LANES), axis=1)
            if diag:
                s = jnp.where(kofs <= qofs, s, NEG_BIG)
            m_new = jnp.maximum(ms[h], jnp.max(s, axis=0, keepdims=True))
            p = jnp.exp(s - m_new)
            alpha = jnp.exp(ms[h] - m_new)
            new_ls.append(alpha * ls[h] + jnp.sum(p, axis=0, keepdims=True))
            ps.append(p.astype(BF16))
            alphas.append(alpha)
            new_ms.append(m_new)
        for h in range(N_HEADS):
            acc_ref[h] = alphas[h] * acc_ref[h] + _dot(vt_ref[j, h], ps[h])
        return tuple(new_ms), tuple(new_ls)

    init = (tuple(jnp.full((1, t), NEG_BIG, F32) for _ in range(N_HEADS)),
            tuple(jnp.zeros((1, t), F32) for _ in range(N_HEADS)))
    carry = lax.fori_loop(0, i, functools.partial(body, False), init)
    _, ls = body(True, i, carry)
    for h in range(N_HEADS):
        out_t = acc_ref[h] / ls[h]
        for c in range(sub):
            o_ref[c * CHUNK:(c + 1) * CHUNK, h * HEAD_DIM:(h + 1) * HEAD_DIM] = (
                out_t[:, c * CHUNK:(c + 1) * CHUNK].T.astype(o_ref.dtype))


def fox_attention(p, fcol, frow, qg, kg, batch, seq):
    nq = seq // FOX_T
    base = COL_FOX // MIX_W
    return pl.pallas_call(
        _fox_kernel,
        grid=(batch, nq),
        in_specs=[
            pl.BlockSpec((FOX_T, MIX_W), lambda b, i: (b * nq + i, base)),
            pl.BlockSpec((seq, MIX_W), lambda b, i: (b, base + 1)),
            pl.BlockSpec((seq, MIX_W), lambda b, i: (b, base + 2)),
            pl.BlockSpec((seq, SM_W), lambda b, i: (b, 0)),
            pl.BlockSpec((1, seq // CHUNK, 8, CHUNK), lambda b, i: (b, 0, 0, 0)),
            pl.BlockSpec((1, HEAD_DIM), lambda b, i: (0, 0)),
            pl.BlockSpec((1, HEAD_DIM), lambda b, i: (0, 0)),
        ],
        out_specs=pl.BlockSpec((FOX_T, MIX_W), lambda b, i: (b * nq + i, 0)),
        out_shape=jax.ShapeDtypeStruct((batch * seq, MIX_W), BF16),
        scratch_shapes=[
            pltpu.VMEM((seq, MIX_W), BF16),
            pltpu.VMEM((nq, N_HEADS, HEAD_DIM, FOX_T), BF16),
            pltpu.VMEM((N_HEADS, seq, LANES), F32),
            pltpu.VMEM((N_HEADS, HEAD_DIM, FOX_T), BF16),
            pltpu.VMEM((N_HEADS, HEAD_DIM, FOX_T), F32),
        ],
        compiler_params=_cparams(("parallel", "arbitrary")),
        name="fox_attention",
    )(p, p, p, fcol, frow, qg, kg)


DSA_TQ = 128
DSA_TK = 256
BAND_W = 2 * DSA_TQ


def _t5_bucket(dist):
    max_exact = N_BUCKETS // 2
    d = jnp.maximum(dist, 0)
    log_ratio = jnp.log(jnp.maximum(d, 1).astype(F32) / max_exact) / math.log(MAX_DISTANCE / max_exact)
    large = jnp.minimum(max_exact + (log_ratio * (N_BUCKETS - max_exact)).astype(jnp.int32), N_BUCKETS - 1)
    return jnp.where(d < max_exact, d, large)


def _dsa_kernel(cq_ref, k_ref, v_ref, smq_ref, smk_ref, cqg_ref, wuq_ref, wqi_ref, qg_ref, kg_ref, rb_ref,
                o_ref, kn_ref, ki_ref, band_ref, sc_ref, qh_ref, qi_ref, m_ref, l_ref, acc_ref, *, topk):
    b = pl.program_id(0)
    i = pl.program_id(1)
    tq, tk = DSA_TQ, DSA_TK
    nkc = sc_ref.shape[0]
    seq = nkc * tk
    tiles = tk // tq
    nb = ((i + 1) * tq + tk - 1) // tk

    @pl.when(jnp.logical_and(b == 0, i == 0))
    def _():
        r = lax.broadcasted_iota(jnp.int32, (tq, BAND_W), 0)
        c = lax.broadcasted_iota(jnp.int32, (tq, BAND_W), 1)
        bucket = _t5_bucket(tq + r - c)
        for h in range(N_HEADS):
            far = rb_ref[N_BUCKETS - 1, h]
            acc = jnp.zeros((tq, BAND_W), F32)
            for bk in range(N_BUCKETS - 1):
                acc = jnp.where(bucket == bk, rb_ref[bk, h] - far, acc)
            band_ref[h] = acc

    @pl.when(i == 0)
    def _():
        kn_ref[...] = _rms(k_ref[...].astype(F32), kg_ref[...]).astype(BF16)
        ki_ref[...] = smk_ref[:, SM_IK:SM_IK + IDX_DIM].astype(BF16)

    cq = _rms(cq_ref[...].astype(F32), cqg_ref[...]).astype(BF16)
    qf = _dot(cq, wuq_ref[...])
    for h in range(N_HEADS):
        sl = slice(h * HEAD_DIM, (h + 1) * HEAD_DIM)
        qh_ref[h] = (_rms(qf[:, sl], qg_ref[...]) * (HEAD_DIM ** -0.5)).astype(BF16)
    q_idx = (_dot(cq, wqi_ref[...]) * (IDX_DIM ** -0.5)).astype(BF16)
    for h in range(IDX_HEADS):
        qi_ref[h] = q_idx[:, h * IDX_DIM:(h + 1) * IDX_DIM]

    qpos = lax.broadcasted_iota(jnp.int32, (tq, tk), 0) + i * tq
    col = lax.broadcasted_iota(jnp.int32, (tq, tk), 1)

    def score_body(j, _):
        start = pl.multiple_of(j * tk, tk)
        kj = ki_ref[pl.ds(start, tk), :]
        w_h = smq_ref[:, SM_IW:SM_IW + IDX_HEADS] * (IDX_HEADS ** -0.5)
        acc = jnp.zeros((tq, tk), F32)
        for h in range(IDX_HEADS):
            acc = acc + w_h[:, h:h + 1] * jnp.maximum(_dot_nt(qi_ref[h], kj), 0.0)
        sc_ref[j] = jnp.where(col + start <= qpos, acc, -jnp.inf)
        return 0

    lax.fori_loop(0, nb, score_body, 0)

    def over_chunks(fn, init):
        acc = init
        for j in range(nkc):
            acc = lax.cond(j < nb, functools.partial(fn, j), lambda a: a, acc)
        return acc

    def lane_tiles(x):
        return [x[:, t * tq:(t + 1) * tq] for t in range(tiles)]

    def row_total(x):
        return jnp.sum(x, axis=1, keepdims=True)

    def search():
        kf = float(topk)

        def max_fn(j, a):
            for x in lane_tiles(sc_ref[j]):
                a = jnp.maximum(a, x)
            return a

        def min_fn(j, a):
            for x in lane_tiles(sc_ref[j]):
                a = jnp.minimum(a, jnp.where(x == -jnp.inf, jnp.inf, x))
            return a

        smax = jnp.max(over_chunks(max_fn, jnp.full((tq, tq), -jnp.inf, F32)), axis=1, keepdims=True)
        smin = jnp.min(over_chunks(min_fn, jnp.full((tq, tq), jnp.inf, F32)), axis=1, keepdims=True)

        def count_ge(t):
            tb = jnp.broadcast_to(t, (tq, tq))

            def fn(j, a):
                for x in lane_tiles(sc_ref[j]):
                    a = a + jnp.where(x >= tb, 1.0, 0.0)
                return a

            return row_total(over_chunks(fn, jnp.zeros((tq, tq), F32)))

        def midpoint(lo, hi):
            return jnp.where(hi == jnp.inf, smax, 0.5 * (lo + hi))

        def undecided(lo, hi, c_lo, mid):
            return jnp.logical_and(c_lo != kf, jnp.logical_and(mid > lo, mid < hi))

        def cond(carry):
            return jnp.logical_and(carry[0] < 400, carry[1] > 0.0)

        def body(carry):
            it, _, lo, hi, c_lo, c_hi, mid = carry
            upd = undecided(lo, hi, c_lo, mid)
            cnt = count_ge(mid)
            up = jnp.logical_and(upd, cnt >= kf)
            dn = jnp.logical_and(upd, cnt < kf)
            lo = jnp.where(up, mid, lo)
            c_lo = jnp.where(up, cnt, c_lo)
            hi = jnp.where(dn, mid, hi)
            c_hi = jnp.where(dn, cnt, c_hi)
            mid = midpoint(lo, hi)
            active = jnp.max(jnp.where(undecided(lo, hi, c_lo, mid), 1.0, 0.0))
            return it + 1, active, lo, hi, c_lo, c_hi, mid

        lo0 = smin
        hi0 = jnp.full((tq, 1), jnp.inf, F32)
        c_lo0 = count_ge(lo0)
        c_hi0 = jnp.zeros((tq, 1), F32)
        mid0 = midpoint(lo0, hi0)
        act0 = jnp.max(jnp.where(undecided(lo0, hi0, c_lo0, mid0), 1.0, 0.0))
        _, _, lo, hi, c_lo, c_hi, _ = lax.while_loop(
            cond, body, (jnp.int32(0), act0, lo0, hi0, c_lo0, c_hi0, mid0))

        def tie_search():
            need = kf - c_hi
            lo_b = jnp.broadcast_to(lo, (tq, tq))
            hi_b = jnp.broadcast_to(hi, (tq, tq))
            lane = lax.broadcasted_iota(jnp.int32, (tq, tq), 1)

            def tie_body(_, carry):
                jlo, jhi = carry
                jm = (jlo + jhi) // 2
                jm_b = jnp.broadcast_to(jm, (tq, tq))

                def fn(j, a):
                    for t, x in enumerate(lane_tiles(sc_ref[j])):
                        hit = jnp.logical_and(jnp.logical_and(x >= lo_b, x < hi_b), lane + (j * tk + t * tq) <= jm_b)
                        a = a + jnp.where(hit, 1.0, 0.0)
                    return a

                ok = row_total(over_chunks(fn, jnp.zeros((tq, tq), F32))) >= need
                return jnp.where(ok, jlo, jm), jnp.where(ok, jm, jhi)

            n_bits = int(math.ceil(math.log2(seq))) + 1
            _, jmax = lax.fori_loop(0, n_bits, tie_body,
                                    (jnp.full((tq, 1), -1, jnp.int32), jnp.full((tq, 1), seq - 1, jnp.int32)))
            return jmax

        any_tie = jnp.max(jnp.where(c_lo != kf, 1.0, 0.0)) > 0.0
        jmax = lax.cond(any_tie, tie_search, lambda: jnp.full((tq, 1), seq - 1, jnp.int32))
        return lo, hi, jmax

    def keep_all():
        return (jnp.full((tq, 1), -jnp.inf, F32), jnp.full((tq, 1), jnp.inf, F32),
                jnp.full((tq, 1), seq - 1, jnp.int32))

    lo, hi, jmax = lax.cond((i + 1) * tq > topk, search, keep_all)

    m_ref[...] = jnp.full(m_ref.shape, NEG_BIG, F32)
    l_ref[...] = jnp.zeros(l_ref.shape, F32)
    acc_ref[...] = jnp.zeros(acc_ref.shape, F32)

    def attend_body(j, _):
        start = pl.multiple_of(j * tk, tk)
        ks = kn_ref[pl.ds(start, tk), :]
        vs = v_ref[pl.ds(start, tk), :]
        sc = sc_ref[j]
        kpos = col + start
        keep = jnp.logical_or(sc >= hi, jnp.logical_and(sc >= lo, kpos <= jmax))
        keep = jnp.logical_and(keep, kpos <= qpos)
        for h in range(N_HEADS):
            bias = jnp.concatenate(
                [jnp.where(j * tiles + t == i, band_ref[h, :, tq:2 * tq],
                           jnp.where(j * tiles + t == i - 1, band_ref[h, :, 0:tq], 0.0))
                 for t in range(tiles)], axis=1)
            s = jnp.where(keep, _dot_nt(qh_ref[h], ks) + bias, NEG_BIG)
            m_old = m_ref[h]
            m_new = jnp.maximum(m_old, jnp.max(s, axis=1, keepdims=True))
            p = jnp.exp(s - m_new)
            alpha = jnp.exp(m_old - m_new)
            l_ref[h] = alpha * l_ref[h] + jnp.sum(p, axis=1, keepdims=True)
            acc_ref[h] = alpha * acc_ref[h] + _dot(p.astype(BF16), vs)
            m_ref[h] = m_new
        return 0

    lax.fori_loop(0, nb, attend_body, 0)
    for h in range(N_HEADS):
        o_ref[:, h * HEAD_DIM:(h + 1) * HEAD_DIM] = (acc_ref[h] / l_ref[h]).astype(o_ref.dtype)


SUBLANES = 8
COUNT_ROWS = 64


def _fold_rows(x, op):
    return op(x.reshape(x.shape[0] // SUBLANES, SUBLANES, x.shape[1]), axis=0)


def _dsa_t_kernel(cq_ref, k_ref, v_ref, smq_ref, smk_ref, cqg_ref, wuq_ref, wqi_ref, qg_ref, kg_ref, rb_ref,
                  o_ref, kn_ref, ki_ref, vt_ref, band_ref, sc_ref, qt_ref, xi_ref, acc_ref, *, topk):
    b = pl.program_id(0)
    i = pl.program_id(1)
    tq, tk = DSA_TQ, DSA_TK
    nkc = sc_ref.shape[0]
    seq = nkc * tk
    tiles = tk // tq
    nb = ((i + 1) * tq + tk - 1) // tk

    @pl.when(jnp.logical_and(b == 0, i == 0))
    def _():
        c = lax.broadcasted_iota(jnp.int32, (BAND_W, tq), 0)
        r = lax.broadcasted_iota(jnp.int32, (BAND_W, tq), 1)
        bucket = _t5_bucket(tq + r - c)
        for h in range(N_HEADS):
            far = rb_ref[N_BUCKETS - 1, h]
            acc = jnp.zeros((BAND_W, tq), F32)
            for bk in range(N_BUCKETS - 1):
                acc = jnp.where(bucket == bk, rb_ref[bk, h] - far, acc)
            band_ref[h] = acc

    @pl.when(i == 0)
    def _():
        kn_ref[...] = _rms(k_ref[...].astype(F32), kg_ref[...]).astype(BF16)
        ki_ref[...] = smk_ref[:, SM_IK:SM_IK + IDX_DIM].astype(BF16)
        for j in range(nkc):
            for t in range(tiles):
                rows = slice(j * tk + t * tq, j * tk + (t + 1) * tq)
                vt_ref[j, :, t * tq:(t + 1) * tq] = v_ref[rows, :].astype(F32).T.astype(BF16)

    cq_t = _rms(cq_ref[...].astype(F32), cqg_ref[...]).T.astype(BF16)
    q_t = _dot(wuq_ref[...], cq_t)
    g_col = jnp.broadcast_to(qg_ref[...], (HEAD_DIM, tq))
    for h in range(N_HEADS):
        x = q_t[h * HEAD_DIM:(h + 1) * HEAD_DIM, :]
        inv = lax.rsqrt(jnp.mean(x * x, axis=0, keepdims=True) + EPS)
        qt_ref[:, h * tq:(h + 1) * tq] = (x * inv * g_col * (HEAD_DIM ** -0.5)).astype(BF16)
    qi_t = (_dot(wqi_ref[...], cq_t) * (IDX_DIM ** -0.5)).astype(BF16)
    for h in range(IDX_HEADS):
        xi_ref[:, h * tq:(h + 1) * tq] = qi_t[h * IDX_DIM:(h + 1) * IDX_DIM, :]
    w_rows = smq_ref[...].T[SM_IW:SM_IW + IDX_HEADS, :] * (IDX_HEADS ** -0.5)

    kofs = lax.broadcasted_iota(jnp.int32, (tk, tq), 0)
    qpos = lax.broadcasted_iota(jnp.int32, (tk, tq), 1) + i * tq

    def score_body(j, _):
        start = pl.multiple_of(j * tk, tk)
        kj = ki_ref[pl.ds(start, tk), :]
        acc = jnp.zeros((tk, tq), F32)
        for h2 in range(IDX_HEADS // 2):
            r = _dot(kj, xi_ref[:, 2 * h2 * tq:(2 * h2 + 2) * tq])
            for h in (2 * h2, 2 * h2 + 1):
                acc = acc + w_rows[h:h + 1, :] * jnp.maximum(r[:, (h - 2 * h2) * tq:(h - 2 * h2 + 1) * tq], 0.0)
        sc_ref[j] = jnp.where(kofs + start <= qpos, acc, -jnp.inf)
        return 0

    lax.fori_loop(0, nb, score_body, 0)

    @pl.when(nb % 2 == 1)
    def _():
        sc_ref[jnp.minimum(nb, nkc - 1)] = jnp.full((tk, tq), -jnp.inf, F32)

    def over_chunks(fn, init):
        acc = init
        for j in range(nkc):
            acc = lax.cond(j < nb, functools.partial(fn, j), lambda a: a, acc)
        return acc

    def count_where(pred_fn):
        def walk(n_chunks):
            def run():
                a = jnp.zeros((COUNT_ROWS, tq), F32)
                for j in range(n_chunks):
                    hit = jnp.where(pred_fn(j, sc_ref[j]), 1.0, 0.0)
                    a = a + jnp.sum(hit.reshape(tk // COUNT_ROWS, COUNT_ROWS, tq), axis=0)
                return a
            return run

        extents = sorted({min(n, nkc) for n in range(2, nkc + 2, 2)})
        a = lax.switch((nb - 1) // 2, [walk(n) for n in extents])
        return jnp.sum(a, axis=0, keepdims=True)

    def search():
        kf = float(topk)
        smax = jnp.max(over_chunks(lambda j, a: jnp.maximum(a, _fold_rows(sc_ref[j], jnp.max)),
                                   jnp.full((SUBLANES, tq), -jnp.inf, F32)), axis=0, keepdims=True)
        smin = jnp.min(over_chunks(
            lambda j, a: jnp.minimum(a, _fold_rows(jnp.where(sc_ref[j] == -jnp.inf, jnp.inf, sc_ref[j]), jnp.min)),
            jnp.full((SUBLANES, tq), jnp.inf, F32)), axis=0, keepdims=True)

        def count_ge(t):
            return count_where(lambda j, x: x >= t)

        def midpoint(lo, hi):
            return jnp.where(hi == jnp.inf, smax, 0.5 * (lo + hi))

        def undecided(lo, hi, c_lo, mid):
            return jnp.logical_and(c_lo != kf, jnp.logical_and(mid > lo, mid < hi))

        def cond(carry):
            return jnp.logical_and(carry[0] < 400, carry[1] > 0.0)

        def body(carry):
            it, _, lo, hi, c_lo, c_hi, mid = carry
            upd = undecided(lo, hi, c_lo, mid)
            cnt = count_ge(mid)
            up = jnp.logical_and(upd, cnt >= kf)
            dn = jnp.logical_and(upd, cnt < kf)
            lo = jnp.where(up, mid, lo)
            c_lo = jnp.where(up, cnt, c_lo)
            hi = jnp.where(dn, mid, hi)
            c_hi = jnp.where(dn, cnt, c_hi)
            mid = midpoint(lo, hi)
            active = jnp.max(jnp.where(undecided(lo, hi, c_lo, mid), 1.0, 0.0))
            return it + 1, active, lo, hi, c_lo, c_hi, mid

        lo0 = smin
        hi0 = jnp.full((1, tq), jnp.inf, F32)
        c_lo0 = count_ge(lo0)
        c_hi0 = jnp.zeros((1, tq), F32)
        mid0 = midpoint(lo0, hi0)
        act0 = jnp.max(jnp.where(undecided(lo0, hi0, c_lo0, mid0), 1.0, 0.0))
        _, _, lo, hi, c_lo, c_hi, _ = lax.while_loop(
            cond, body, (jnp.int32(0), act0, lo0, hi0, c_lo0, c_hi0, mid0))

        def tie_search():
            need = kf - c_hi

            def tie_body(_, carry):
                jlo, jhi = carry
                jm = (jlo + jhi) // 2
                cnt = count_where(lambda j, x: jnp.logical_and(jnp.logical_and(x >= lo, x < hi),
                                                               kofs + j * tk <= jm))
                ok = cnt >= need
                return jnp.where(ok, jlo, jm), jnp.where(ok, jm, jhi)

            n_bits = int(math.ceil(math.log2(seq))) + 1
            _, jmax = lax.fori_loop(0, n_bits, tie_body,
                                    (jnp.full((1, tq), -1, jnp.int32), jnp.full((1, tq), seq - 1, jnp.int32)))
            return jmax

        any_tie = jnp.max(jnp.where(c_lo != kf, 1.0, 0.0)) > 0.0
        jmax = lax.cond(any_tie, tie_search, lambda: jnp.full((1, tq), seq - 1, jnp.int32))
        return lo, hi, jmax

    def keep_all():
        return (jnp.full((1, tq), -jnp.inf, F32), jnp.full((1, tq), jnp.inf, F32),
                jnp.full((1, tq), seq - 1, jnp.int32))

    lo, hi, jmax = lax.cond((i + 1) * tq > topk, search, keep_all)

    acc_ref[...] = jnp.zeros(acc_ref.shape, F32)

    def attend_body(near, j, carry):
        ms, ls = carry
        start = pl.multiple_of(j * tk, tk)
        kc = kn_ref[pl.ds(start, tk), :]
        vt = vt_ref[j]
        sc = sc_ref[j]
        kpos = kofs + start
        keep = jnp.logical_or(sc >= hi, jnp.logical_and(sc >= lo, kpos <= jmax))
        if near:
            keep = jnp.logical_and(keep, kpos <= qpos)
        s_all = _dot(kc, qt_ref[...])
        new_ms, new_ls, ps, alphas = [], [], [], []
        for h in range(N_HEADS):
            s = s_all[:, h * tq:(h + 1) * tq]
            if near:
                s = s + jnp.concatenate(
                    [jnp.where(j * tiles + t == i, band_ref[h, tq:2 * tq, :],
                               jnp.where(j * tiles + t == i - 1, band_ref[h, 0:tq, :], 0.0))
                     for t in range(tiles)], axis=0)
            s = jnp.where(keep, s, NEG_BIG)
            m_new = jnp.maximum(ms[h], jnp.max(s, axis=0, keepdims=True))
            p = jnp.exp(s - m_new)
            alphas.append(jnp.exp(ms[h] - m_new))
            new_ls.append(alphas[h] * ls[h] + jnp.sum(p, axis=0, keepdims=True))
            ps.append(p.astype(BF16))
            new_ms.append(m_new)
        acc_ref[...] = (jnp.concatenate(alphas, axis=1) * acc_ref[...]
                        + _dot(vt, jnp.concatenate(ps, axis=1)))
        return tuple(new_ms), tuple(new_ls)

    init = (tuple(jnp.full((1, tq), NEG_BIG, F32) for _ in range(N_HEADS)),
            tuple(jnp.zeros((1, tq), F32) for _ in range(N_HEADS)))
    n_far = jnp.maximum((i - 1) * tq // tk, 0)
    carry = lax.fori_loop(0, n_far, functools.partial(attend_body, False), init)
    _, ls = lax.fori_loop(n_far, nb, functools.partial(attend_body, True), carry)
    for h in range(N_HEADS):
        out_t = acc_ref[:, h * tq:(h + 1) * tq] / ls[h]
        o_ref[:, h * HEAD_DIM:(h + 1) * HEAD_DIM] = out_t.T.astype(o_ref.dtype)


def dsa_attention(p, sm, cqg, wuq_t, wqi_t, qg_col, kg, rel_bias, batch, seq):
    nq = seq // DSA_TQ
    nkc = seq // DSA_TK
    topk = min(DSA_TOPK, seq // 4)
    kern = functools.partial(_dsa_t_kernel, topk=topk)
    return pl.pallas_call(
        kern,
        grid=(batch, nq),
        in_specs=[
            pl.BlockSpec((DSA_TQ, DSA_Q_RANK), lambda b, i: (b * nq + i, COL_CQ // DSA_Q_RANK)),
            pl.BlockSpec((seq, HEAD_DIM), lambda b, i: (b, COL_DK // HEAD_DIM)),
            pl.BlockSpec((seq, HEAD_DIM), lambda b, i: (b, COL_DV // HEAD_DIM)),
            pl.BlockSpec((DSA_TQ, SM_W), lambda b, i: (b * nq + i, 0)),
            pl.BlockSpec((seq, SM_W), lambda b, i: (b, 0)),
            pl.BlockSpec((1, DSA_Q_RANK), lambda b, i: (0, 0)),
            pl.BlockSpec((N_HEADS * HEAD_DIM, DSA_Q_RANK), lambda b, i: (0, 0)),
            pl.BlockSpec((IDX_HEADS * IDX_DIM, DSA_Q_RANK), lambda b, i: (0, 0)),
            pl.BlockSpec((HEAD_DIM, 1), lambda b, i: (0, 0)),
            pl.BlockSpec((1, HEAD_DIM), lambda b, i: (0, 0)),
            pl.BlockSpec(memory_space=pltpu.SMEM),
        ],
        out_specs=pl.BlockSpec((DSA_TQ, MIX_W), lambda b, i: (b * nq + i, 0)),
        out_shape=jax.ShapeDtypeStruct((batch * seq, MIX_W), BF16),
        scratch_shapes=[
            pltpu.VMEM((seq, HEAD_DIM), BF16),
            pltpu.VMEM((seq, IDX_DIM), BF16),
            pltpu.VMEM((nkc, HEAD_DIM, DSA_TK), BF16),
            pltpu.VMEM((N_HEADS, BAND_W, DSA_TQ), F32),
            pltpu.VMEM((nkc, DSA_TK, DSA_TQ), F32),
            pltpu.VMEM((HEAD_DIM, N_HEADS * DSA_TQ), BF16),
            pltpu.VMEM((IDX_DIM, IDX_HEADS * DSA_TQ), BF16),
            pltpu.VMEM((HEAD_DIM, N_HEADS * DSA_TQ), F32),
        ],
        compiler_params=_cparams(("arbitrary", "arbitrary")),
        name="dsa_attention",
    )(p, p, p, sm, sm, cqg, wuq_t, wqi_t, qg_col, kg, rel_bias)


def _causal_conv(x, xp, w_ref, b_ref):
    rows = lax.broadcasted_iota(jnp.int32, x.shape, 0)
    acc = x * w_ref[SSD_CONV - 1:SSD_CONV, :] + b_ref[...]
    for d in range(1, SSD_CONV):
        shifted = jnp.where(rows < d, pltpu.roll(xp, d, 0), pltpu.roll(x, d, 0))
        acc = acc + shifted * w_ref[SSD_CONV - 1 - d:SSD_CONV - d, :]
    return _silu(acc)


def _ssd_kernel(z_ref, xs_ref, bc_ref, xsp_ref, bcp_ref, sm_ref, cwx_ref, cbx_ref, cwb_ref, cbb_ref,
                dtb_ref, alog_ref, dsk_ref, ng_ref, o_ref, prev_ref, y_ref):
    c = CHUNK
    n = pl.program_id(1)

    @pl.when(n == 0)
    def _():
        prev_ref[...] = jnp.zeros_like(prev_ref)

    first = (n > 0).astype(F32)
    xs = _causal_conv(xs_ref[...].astype(F32), xsp_ref[...].astype(F32) * first, cwx_ref, cbx_ref)
    bc = _causal_conv(bc_ref[...].astype(F32), bcp_ref[...].astype(F32) * first, cwb_ref, cbb_ref)

    dt_t = _softplus(sm_ref[...].T + dtb_ref[...])
    cs_t = _cumsum_lanes(dt_t * (-jnp.exp(alog_ref[...])))
    cs = cs_t.T
    dt = dt_t.T
    ii = lax.broadcasted_iota(jnp.int32, (c, c), 0)
    jj = lax.broadcasted_iota(jnp.int32, (c, c), 1)
    tril = ii >= jj
    gn = SSD_GROUPS * SSD_STATE
    hpg = SSD_HEADS // SSD_GROUPS
    for g in range(SSD_GROUPS):
        bg = bc[:, g * SSD_STATE:(g + 1) * SSD_STATE]
        cg = bc[:, gn + g * SSD_STATE:gn + (g + 1) * SSD_STATE].astype(BF16)
        cb = _dot_nt(cg, bg.astype(BF16))
        y_off = _dot(cg, prev_ref[g].astype(BF16))
        for r in range(hpg):
            h = g * hpg + r
            hs = slice(h * SSD_HEAD_DIM, (h + 1) * SSD_HEAD_DIM)
            rs = slice(r * SSD_HEAD_DIM, (r + 1) * SSD_HEAD_DIM)
            a_col = cs[:, SM_DT + h:SM_DT + h + 1]
            a_row = cs_t[SM_DT + h:SM_DT + h + 1, :]
            last = cs_t[SM_DT + h:SM_DT + h + 1, c - 1:c]
            seg = jnp.where(tril, jnp.exp(jnp.where(tril, a_col - a_row, 0.0)), 0.0)
            xh = xs[:, hs]
            xc = (xh * dt[:, SM_DT + h:SM_DT + h + 1]).astype(BF16)
            y_diag = _dot((cb * seg).astype(BF16), xc)
            st = _dot_tn((bg * jnp.exp(last - a_col)).astype(BF16), xc)
            y_ref[:, hs] = y_diag + y_off[:, rs] * jnp.exp(a_col) + dsk_ref[:, hs] * xh
            prev_ref[g, :, rs] = jnp.exp(last) * prev_ref[g, :, rs] + st
    gated = y_ref[...] * _silu(z_ref[...].astype(F32))
    gw = SSD_INNER // SSD_GROUPS
    for g in range(SSD_GROUPS):
        sl = slice(g * gw, (g + 1) * gw)
        o_ref[:, sl] = _rms(gated[:, sl], ng_ref[:, sl]).astype(o_ref.dtype)


def ssd_mixer(p, sm, cw, cb, dtb_col, alog_col, dskip_row, ng, batch, seq):
    n = seq // CHUNK
    bcw = 2 * SSD_GROUPS * SSD_STATE

    def cur(width, colbase):
        return pl.BlockSpec((CHUNK, width), lambda b, i: (b * n + i, colbase // width))

    def prv(width, colbase):
        return pl.BlockSpec((CHUNK, width), lambda b, i: (b * n + jnp.maximum(i - 1, 0), colbase // width))

    def const(shape):
        return pl.BlockSpec(shape, lambda b, i: (0, 0))

    return pl.pallas_call(
        _ssd_kernel,
        grid=(batch, n),
        in_specs=[
            cur(SSD_INNER, COL_Z), cur(SSD_INNER, COL_XS), cur(bcw, COL_BC),
            prv(SSD_INNER, COL_XS), prv(bcw, COL_BC),
            pl.BlockSpec((CHUNK, SM_W), lambda b, i: (b * n + i, 0)),
            const((SSD_CONV, SSD_INNER)), const((1, SSD_INNER)),
            const((SSD_CONV, bcw)), const((1, bcw)),
            const((SM_W, 1)), const((SM_W, 1)),
            const((1, SSD_INNER)), const((1, SSD_INNER)),
        ],
        out_specs=pl.BlockSpec((CHUNK, SSD_INNER), lambda b, i: (b * n + i, 0)),
        out_shape=jax.ShapeDtypeStruct((batch * seq, SSD_INNER), BF16),
        scratch_shapes=[
            pltpu.VMEM((SSD_GROUPS, SSD_STATE, SSD_INNER // SSD_GROUPS), F32),
            pltpu.VMEM((CHUNK, SSD_INNER), F32),
        ],
        compiler_params=_cparams(("parallel", "arbitrary")),
        name="ssd_mixer",
    )(p, p, p, p, p, sm, cw[:, :SSD_INNER], cb[:, :SSD_INNER], cw[:, SSD_INNER:], cb[:, SSD_INNER:],
      dtb_col, alog_col, dskip_row, ng)


MERGE_TM = 256


def _merge_kernel(x_ref, gl_ref, gb_ref, oret_ref, ofox_ref, odsa_ref, ossd_ref, wbr_ref, wout_ref, o_ref):
    branches = (oret_ref, ofox_ref, odsa_ref, ossd_ref)
    merged = None
    row0 = 0
    for bi, br in enumerate(branches):
        width = br.shape[1]
        sl = slice(bi * D_MODEL, (bi + 1) * D_MODEL)
        gate = 1.0 / (1.0 + jnp.exp(-(gl_ref[:, sl].astype(F32) + gb_ref[:, sl])))
        term = gate * _dot(br[...], wbr_ref[row0:row0 + width, :])
        merged = term if merged is None else merged + term
        row0 += width
    o_ref[...] = x_ref[...] + _dot(merged.astype(BF16), wout_ref[...])


def merge_project(x, p, gate_b, o_ret, o_fox, o_dsa, o_ssd, w_br, w_out):
    m = x.shape[0]
    tm = MERGE_TM

    def rows(width):
        return pl.BlockSpec((tm, width), lambda i: (i, 0))

    def const(shape):
        return pl.BlockSpec(shape, lambda i: (0, 0), pipeline_mode=pl.Buffered(1))

    return pl.pallas_call(
        _merge_kernel,
        grid=(m // tm,),
        in_specs=[
            rows(D_MODEL), rows(N_BRANCH * D_MODEL), const((1, N_BRANCH * D_MODEL)),
            rows(MIX_W), rows(MIX_W), rows(MIX_W), rows(SSD_INNER),
            const(w_br.shape), const(w_out.shape),
        ],
        out_specs=rows(D_MODEL),
        out_shape=jax.ShapeDtypeStruct(x.shape, x.dtype),
        compiler_params=_cparams(("parallel",)),
        name="merge_project",
    )(x, p, gate_b, o_ret, o_fox, o_dsa, o_ssd, w_br, w_out)


FFN_TM = 1024
FFN_TF = 512


def _ffn_kernel(x_ref, g_ref, w1_ref, w2_ref, o_ref, h_ref):
    @pl.when(pl.program_id(1) == 0)
    def _():
        h_ref[...] = _rms(x_ref[...], g_ref[...]).astype(BF16)
        o_ref[...] = x_ref[...]

    a = jnp.maximum(_dot(h_ref[...], w1_ref[...]), 0.0)
    o_ref[...] += _dot((a * a).astype(BF16), w2_ref[...])


def ffn(x, g, w1, w2):
    m, d = x.shape
    dff = w1.shape[1]
    tm, tf = min(FFN_TM, m), FFN_TF
    return pl.pallas_call(
        _ffn_kernel,
        grid=(m // tm, dff // tf),
        in_specs=[
            pl.BlockSpec((tm, d), lambda i, f: (i, 0), pipeline_mode=pl.Buffered(1)),
            pl.BlockSpec((1, d), lambda i, f: (0, 0)),
            pl.BlockSpec((d, tf), lambda i, f: (0, f)),
            pl.BlockSpec((tf, d), lambda i, f: (f, 0)),
        ],
        out_specs=pl.BlockSpec((tm, d), lambda i, f: (i, 0)),
        out_shape=jax.ShapeDtypeStruct(x.shape, x.dtype),
        scratch_shapes=[pltpu.VMEM((tm, d), BF16)],
        compiler_params=_cparams(("parallel", "arbitrary")),
        name="ffn",
    )(x, g, w1, w2)


SRC_RET = 0
SRC_FOX = SRC_RET + 4 * MIX_W
SRC_FF = SRC_FOX + 3 * MIX_W
SRC_CQ = SRC_FF + N_HEADS
SRC_DK = SRC_CQ + DSA_Q_RANK
SRC_IK = SRC_DK + 2 * HEAD_DIM
SRC_IW = SRC_IK + IDX_DIM
SRC_Z = SRC_IW + IDX_HEADS
SRC_DT = SRC_Z + 2 * SSD_INNER + 2 * SSD_GROUPS * SSD_STATE
SRC_GATE = SRC_DT + SSD_HEADS
IN_TOTAL = SRC_GATE + N_BRANCH * D_MODEL
MAIN_RUNS = ((COL_GATE, SRC_GATE), (COL_RET, SRC_RET), (COL_Z, SRC_Z), (COL_CQ, SRC_CQ),
             (COL_FOX, SRC_FOX), (COL_DK, SRC_DK))
RELAYOUT_W = 512
RELAYOUT_TILES = RELAYOUT_W // LANES


def _relayout_tables():
    starts, shifts = [], []
    for blk in range(N_MAIN // RELAYOUT_W):
        o = blk * RELAYOUT_W
        dst, src = [r for r in MAIN_RUNS if r[0] <= o][-1]
        col = src + (o - dst)
        starts.append(col // LANES)
        shifts.append(col % LANES)
    return jnp.asarray(starts, jnp.int32), jnp.asarray(shifts, jnp.int32)


def _relayout_kernel(start_ref, shift_ref, *refs):
    del start_ref
    tiles, o_ref = refs[:-1], refs[-1]
    shift = shift_ref[pl.program_id(1)]
    amount = lax.rem(LANES - shift, LANES)
    lane = lax.broadcasted_iota(jnp.int32, tiles[0].shape, 1)
    rolled = [pltpu.roll(t[...], amount, 1) for t in tiles]
    for k in range(RELAYOUT_TILES):
        piece = jnp.where(lane < LANES - shift, rolled[k], rolled[k + 1])
        o_ref[:, k * LANES:(k + 1) * LANES] = piece.astype(o_ref.dtype)


def relayout_main(w_in):
    depth, d, n_src = w_in.shape
    last = (n_src - 1) // LANES
    starts, shifts = _relayout_tables()

    def tile(k):
        return pl.BlockSpec((None, d, LANES), lambda l, b, st, sh: (l, 0, jnp.minimum(st[b] + k, last)))

    return pl.pallas_call(
        _relayout_kernel,
        grid_spec=pltpu.PrefetchScalarGridSpec(
            num_scalar_prefetch=2,
            grid=(depth, N_MAIN // RELAYOUT_W),
            in_specs=[tile(k) for k in range(RELAYOUT_TILES + 1)],
            out_specs=pl.BlockSpec((None, d, RELAYOUT_W), lambda l, b, st, sh: (l, 0, b)),
        ),
        out_shape=jax.ShapeDtypeStruct((depth, d, N_MAIN), BF16),
        compiler_params=_cparams(("parallel", "arbitrary")),
        name="relayout_main",
    )(starts, shifts, *([w_in] * (RELAYOUT_TILES + 1)))


SMALL_PIECES = ((SRC_DT, SM_DT, SSD_HEADS), (SRC_FF, SM_F, N_HEADS), (SRC_IW, SM_IW, IDX_HEADS),
                (SRC_IK, SM_IK, IDX_DIM))


def _relayout_small_kernel(*refs):
    tiles, o_ref = refs[:-1], refs[-1]
    lane = lax.broadcasted_iota(jnp.int32, o_ref.shape, 1)
    out = jnp.zeros(o_ref.shape, F32)
    for t, (src, dst, width) in zip(tiles, SMALL_PIECES):
        moved = pltpu.roll(t[...], (dst - src % LANES) % LANES, 1)
        out = jnp.where(jnp.logical_and(lane >= dst, lane < dst + width), moved, out)
    o_ref[...] = out.astype(o_ref.dtype)


def relayout_small(w_in):
    depth, d, _ = w_in.shape
    for src, _, width in SMALL_PIECES:
        assert src // LANES == (src + width - 1) // LANES

    def tile(src):
        return pl.BlockSpec((None, d, LANES), lambda l: (l, 0, src // LANES))

    return pl.pallas_call(
        _relayout_small_kernel,
        grid=(depth,),
        in_specs=[tile(src) for src, _, _ in SMALL_PIECES],
        out_specs=pl.BlockSpec((None, d, SM_W), lambda l: (l, 0, 0)),
        out_shape=jax.ShapeDtypeStruct((depth, d, SM_W), BF16),
        compiler_params=_cparams(("parallel",)),
        name="relayout_small",
    )(*([w_in] * len(SMALL_PIECES)))


def _pad_to(v, offset, total):
    return jnp.zeros((total,), v.dtype).at[offset:offset + v.shape[0]].set(v)


def _rotary_tables(seq):
    half = HEAD_DIM // 2
    inv = 1.0 / (10000.0 ** (jnp.arange(half, dtype=F32) / half))
    ang = jnp.arange(seq, dtype=F32)[:, None] * inv[None, :]
    cos, sin = jnp.cos(ang), jnp.sin(ang)
    return jnp.concatenate([cos, cos], axis=1), jnp.concatenate([-sin, sin], axis=1)


def kernel(x, norm1_g, w_in, gate_b, fox_f_b, fox_qn_g, fox_kn_g, dsa_cq_g, dsa_w_uq, dsa_w_qidx, dsa_qn_g,
           dsa_kn_g, rel_bias, ssd_conv_w, ssd_conv_b, ssd_dt_bias, ssd_a_log, ssd_d, ssd_norm_g, w_br, w_out,
           norm2_g, w_ff1, w_ff2):
    batch, seq, d = x.shape
    tokens = batch * seq
    xt = x.reshape(tokens, d)
    cos, sin = _rotary_tables(seq)
    tm = min(1024, tokens)
    w_main = relayout_main(w_in)
    w_small = relayout_small(w_in)
    for l in range(DEPTH):
        g1 = norm1_g[l][None, :]
        p = norm_matmul(xt, g1, w_main, l, BF16, tm, 1024)
        sm = norm_matmul(xt, g1, w_small, l, F32, tm, SM_W)

        o_ret = retention(p, cos, sin, batch, seq)

        fb_row = _pad_to(fox_f_b[l], SM_F, SM_W)[None, :]
        fcol, frow = fox_prep(sm, fb_row, batch, seq)
        o_fox = fox_attention(p, fcol, frow, fox_qn_g[l][None, :], fox_kn_g[l][None, :], batch, seq)

        o_dsa = dsa_attention(p, sm, dsa_cq_g[l][None, :], dsa_w_uq[l].T.astype(BF16), dsa_w_qidx[l].T.astype(BF16),
                              dsa_qn_g[l][:, None], dsa_kn_g[l][None, :], rel_bias, batch, seq)

        o_ssd = ssd_mixer(p, sm, ssd_conv_w[l], ssd_conv_b[l][None, :],
                          _pad_to(ssd_dt_bias[l], SM_DT, SM_W)[:, None], _pad_to(ssd_a_log[l], SM_DT, SM_W)[:, None],
                          jnp.repeat(ssd_d[l], SSD_HEAD_DIM)[None, :], ssd_norm_g[l][None, :], batch, seq)

        xt = merge_project(xt, p, gate_b[l][None, :], o_ret, o_fox, o_dsa, o_ssd,
                           w_br[l].astype(BF16), w_out[l].astype(BF16))
        xt = ffn(xt, norm2_g[l][None, :], w_ff1[l].astype(BF16), w_ff2[l].astype(BF16))
    return xt.reshape(batch, seq, d)
```

```python
import functools
import math

import jax
import jax.numpy as jnp
from jax import lax
from jax.experimental import pallas as pl
from jax.experimental.pallas import tpu as pltpu

F32 = jnp.float32
BF16 = jnp.bfloat16

D_MODEL = 2048
DEPTH = 4
HEAD_DIM = 128
N_HEADS = 4
DSA_Q_RANK = 512
IDX_HEADS = 16
IDX_DIM = 64
DSA_TOPK = 256
SSD_HEADS = 16
SSD_HEAD_DIM = 64
SSD_GROUPS = 2
SSD_STATE = 128
SSD_CONV = 4
SSD_INNER = SSD_HEADS * SSD_HEAD_DIM
D_FF = 4 * D_MODEL
N_BUCKETS = 32
MAX_DISTANCE = 128
CHUNK = 128
EPS = 1e-6
N_BRANCH = 4
MIX_W = N_HEADS * HEAD_DIM

COL_GATE = 0
COL_RET = COL_GATE + N_BRANCH * D_MODEL
COL_Z = COL_RET + 4 * MIX_W
COL_XS = COL_Z + SSD_INNER
COL_BC = COL_XS + SSD_INNER
COL_CQ = COL_BC + 2 * SSD_GROUPS * SSD_STATE
COL_FOX = COL_CQ + DSA_Q_RANK
COL_DK = COL_FOX + 3 * MIX_W
COL_DV = COL_DK + HEAD_DIM
N_MAIN_USED = COL_DV + HEAD_DIM
N_MAIN = 15360
SM_DT = 0
SM_F = 16
SM_IW = 32
SM_IK = 64
SM_W = 128

LANES = 128
VMEM_LIMIT = 56 * 1024 * 1024
NEG_BIG = -1e30


def _cparams(sem):
    return pltpu.CompilerParams(dimension_semantics=sem, vmem_limit_bytes=VMEM_LIMIT)


def _dot(a, b):
    return jnp.dot(a, b, preferred_element_type=F32)


def _dot_nt(a, b):
    return lax.dot_general(a, b, (((1,), (1,)), ((), ())), preferred_element_type=F32)


def _dot_tn(a, b):
    return lax.dot_general(a, b, (((0,), (0,)), ((), ())), preferred_element_type=F32)


def _rms(x, g):
    return x * lax.rsqrt(jnp.mean(x * x, axis=-1, keepdims=True) + EPS) * g


def _silu(x):
    return x / (1.0 + jnp.exp(-x))


def _softplus(x):
    return jnp.maximum(x, 0.0) + jnp.log1p(jnp.exp(-jnp.abs(x)))


def _cumsum_lanes(x):
    lane = lax.broadcasted_iota(jnp.int32, x.shape, 1)
    d = 1
    while d < x.shape[1]:
        x = x + jnp.where(lane >= d, pltpu.roll(x, d, 1), 0.0)
        d *= 2
    return x


def _norm_matmul_kernel(x_ref, g_ref, w_ref, o_ref, h_ref):
    @pl.when(pl.program_id(1) == 0)
    def _():
        h_ref[...] = _rms(x_ref[...], g_ref[...]).astype(BF16)

    o_ref[...] = _dot(h_ref[...], w_ref[...]).astype(o_ref.dtype)


def norm_matmul(x, g, w, layer, out_dtype, tm, tn):
    m, d = x.shape
    n = w.shape[2]
    return pl.pallas_call(
        _norm_matmul_kernel,
        grid=(m // tm, n // tn),
        in_specs=[
            pl.BlockSpec((tm, d), lambda i, j: (i, 0)),
            pl.BlockSpec((1, d), lambda i, j: (0, 0)),
            pl.BlockSpec((None, d, tn), lambda i, j: (layer, 0, j)),
        ],
        out_specs=pl.BlockSpec((tm, tn), lambda i, j: (i, j)),
        out_shape=jax.ShapeDtypeStruct((m, n), out_dtype),
        scratch_shapes=[pltpu.VMEM((tm, d), BF16)],
        compiler_params=_cparams(("parallel", "arbitrary")),
        name="norm_matmul",
    )(x, g, w)


def _retention_kernel(q_ref, k_ref, v_ref, g_ref, cos_ref, sin_ref, o_ref, state_ref):
    c = CHUNK

    @pl.when(pl.program_id(1) == 0)
    def _():
        state_ref[...] = jnp.zeros_like(state_ref)

    cos = cos_ref[...]
    sin = sin_ref[...]
    ii = lax.broadcasted_iota(jnp.int32, (c, c), 0)
    jj = lax.broadcasted_iota(jnp.int32, (c, c), 1)
    rel = (ii - jj).astype(F32)
    i_col = lax.broadcasted_iota(jnp.int32, (c, 1), 0).astype(F32)
    for h in range(N_HEADS):
        lg = math.log1p(-(2.0 ** (-5.0 - h)))
        sl = slice(h * HEAD_DIM, (h + 1) * HEAD_DIM)
        q = q_ref[:, sl].astype(F32)
        k = k_ref[:, sl].astype(F32)
        v = v_ref[:, sl]
        qr = q * cos + pltpu.roll(q, HEAD_DIM // 2, 1) * sin
        kr = (k * cos + pltpu.roll(k, HEAD_DIM // 2, 1) * sin) * (HEAD_DIM ** -0.5)
        decay = jnp.where(rel >= 0, jnp.exp(lg * jnp.maximum(rel, 0.0)), 0.0)
        scores = _dot_nt(qr.astype(BF16), kr.astype(BF16)) * decay
        y = _dot(scores.astype(BF16), v)
        q_dec = jnp.exp(lg * (i_col + 1.0))
        k_dec = jnp.exp(lg * (c - 1.0 - i_col))
        st = state_ref[h]
        y = y + _dot((qr * q_dec).astype(BF16), st.astype(BF16))
        kv = _dot_tn((kr * k_dec).astype(BF16), v)
        state_ref[h] = math.exp(lg * c) * st + kv
        yc = y - jnp.mean(y, axis=-1, keepdims=True)
        yn = yc * lax.rsqrt(jnp.mean(yc * yc, axis=-1, keepdims=True) + EPS)
        o_ref[:, sl] = (_silu(g_ref[:, sl].astype(F32)) * yn).astype(o_ref.dtype)


def retention(p, cos, sin, batch, seq):
    n = seq // CHUNK
    base = COL_RET // MIX_W

    def col(j):
        return pl.BlockSpec((CHUNK, MIX_W), lambda b, i: (b * n + i, base + j))

    tab = pl.BlockSpec((CHUNK, HEAD_DIM), lambda b, i: (i, 0))
    return pl.pallas_call(
        _retention_kernel,
        grid=(batch, n),
        in_specs=[col(0), col(1), col(2), col(3), tab, tab],
        out_specs=pl.BlockSpec((CHUNK, MIX_W), lambda b, i: (b * n + i, 0)),
        out_shape=jax.ShapeDtypeStruct((batch * seq, MIX_W), BF16),
        scratch_shapes=[pltpu.VMEM((N_HEADS, HEAD_DIM, HEAD_DIM), F32)],
        compiler_params=_cparams(("parallel", "arbitrary")),
        name="retention",
    )(p, p, p, p, cos, sin)


def _fox_prep_kernel(sm_ref, fb_ref, fcol_ref, frow_ref, carry_ref):
    @pl.when(pl.program_id(1) == 0)
    def _():
        carry_ref[...] = jnp.zeros_like(carry_ref)

    t = sm_ref[...] + fb_ref[...]
    lf = jnp.minimum(t, 0.0) - jnp.log1p(jnp.exp(-jnp.abs(t)))
    cs = _cumsum_lanes(lf.T) + carry_ref[...]
    carry_ref[...] = cs[:, LANES - 1:LANES]
    frow_ref[0, 0] = cs[SM_F:SM_F + 8, :]
    fcol_ref[...] = cs.T


def fox_prep(sm, fb_row, batch, seq):
    n = seq // CHUNK
    return pl.pallas_call(
        _fox_prep_kernel,
        grid=(batch, n),
        in_specs=[
            pl.BlockSpec((CHUNK, SM_W), lambda b, i: (b * n + i, 0)),
            pl.BlockSpec((1, SM_W), lambda b, i: (0, 0)),
        ],
        out_specs=[
            pl.BlockSpec((CHUNK, SM_W), lambda b, i: (b * n + i, 0)),
            pl.BlockSpec((1, 1, 8, CHUNK), lambda b, i: (b, i, 0, 0)),
        ],
        out_shape=[
            jax.ShapeDtypeStruct((batch * seq, SM_W), F32),
            jax.ShapeDtypeStruct((batch, n, 8, CHUNK), F32),
        ],
        scratch_shapes=[pltpu.VMEM((SM_W, 1), F32)],
        compiler_params=_cparams(("parallel", "arbitrary")),
        name="fox_prep",
    )(sm, fb_row)


FOX_T = 256


def _fox_kernel(q_ref, k_ref, v_ref, fcol_ref, frow_ref, qg_ref, kg_ref, o_ref,
                kn_ref, vt_ref, fb_ref, qt_ref, acc_ref):
    i = pl.program_id(1)
    t = FOX_T
    nkc = vt_ref.shape[0]
    sub = t // CHUNK

    @pl.when(i == 0)
    def _():
        for h in range(N_HEADS):
            sl = slice(h * HEAD_DIM, (h + 1) * HEAD_DIM)
            kn_ref[:, sl] = _rms(k_ref[:, sl].astype(F32), kg_ref[...]).astype(BF16)
            fb_ref[h] = jnp.broadcast_to(fcol_ref[:, SM_F + h:SM_F + h + 1], fb_ref.shape[1:])
            for j in range(nkc):
                for c in range(sub):
                    rows = slice(j * t + c * CHUNK, j * t + (c + 1) * CHUNK)
                    vt_ref[j, h, :, c * CHUNK:(c + 1) * CHUNK] = v_ref[rows, sl].astype(F32).T.astype(BF16)

    fqs = []
    for h in range(N_HEADS):
        sl = slice(h * HEAD_DIM, (h + 1) * HEAD_DIM)
        qn = _rms(q_ref[:, sl].astype(F32), qg_ref[...]) * (HEAD_DIM ** -0.5)
        qt_ref[h] = jnp.concatenate([qn[c * CHUNK:(c + 1) * CHUNK, :].T for c in range(sub)], axis=1).astype(BF16)
        fqs.append(jnp.concatenate([frow_ref[0, i * sub + c, h:h + 1, :] for c in range(sub)], axis=1))
        acc_ref[h] = jnp.zeros(acc_ref.shape[1:], F32)

    kofs = lax.broadcasted_iota(jnp.int32, (t, t), 0)
    qofs = lax.broadcasted_iota(jnp.int32, (t, t), 1)

    def body(diag, j, carry):
        ms, ls = carry
        start = pl.multiple_of(j * t, t)
        scores = [_dot(kn_ref[pl.ds(start, t), h * HEAD_DIM:(h + 1) * HEAD_DIM], qt_ref[h])
                  for h in range(N_HEADS)]
        new_ms, new_ls, ps, alphas = [], [], [], []
        for h in range(N_HEADS):
            fk = fb_ref[h, pl.ds(start, t), :]
            s = scores[h] + fqs[h] - jnp.concatenate([fk] * (t // LANES), axis=1)
            if diag:
                s = jnp.where(kofs <= qofs, s, NEG_BIG)
            m_new = jnp.maximum(ms[h], jnp.max(s, axis=0, keepdims=True))
            p = jnp.exp(s - m_new)
            alpha = jnp.exp(ms[h] - m_new)
            new_ls.append(alpha * ls[h] + jnp.sum(p, axis=0, keepdims=True))
            ps.append(p.astype(BF16))
            alphas.append(alpha)
            new_ms.append(m_new)
        for h in range(N_HEADS):
            acc_ref[h] = alphas[h] * acc_ref[h] + _dot(vt_ref[j, h], ps[h])
        return tuple(new_ms), tuple(new_ls)

    init = (tuple(jnp.full((1, t), NEG_BIG, F32) for _ in range(N_HEADS)),
            tuple(jnp.zeros((1, t), F32) for _ in range(N_HEADS)))
    carry = lax.fori_loop(0, i, functools.partial(body, False), init)
    _, ls = body(True, i, carry)
    for h in range(N_HEADS):
        out_t = acc_ref[h] / ls[h]
        for c in range(sub):
            o_ref[c * CHUNK:(c + 1) * CHUNK, h * HEAD_DIM:(h + 1) * HEAD_DIM] = (
                out_t[:, c * CHUNK:(c + 1) * CHUNK].T.astype(o_ref.dtype))


def fox_attention(p, fcol, frow, qg, kg, batch, seq):
    nq = seq // FOX_T
    base = COL_FOX // MIX_W
    return pl.pallas_call(
        _fox_kernel,
        grid=(batch, nq),
        in_specs=[
            pl.BlockSpec((FOX_T, MIX_W), lambda b, i: (b * nq + i, base)),
            pl.BlockSpec((seq, MIX_W), lambda b, i: (b, base + 1)),
            pl.BlockSpec((seq, MIX_W), lambda b, i: (b, base + 2)),
            pl.BlockSpec((seq, SM_W), lambda b, i: (b, 0)),
            pl.BlockSpec((1, seq // CHUNK, 8, CHUNK), lambda b, i: (b, 0, 0, 0)),
            pl.BlockSpec((1, HEAD_DIM), lambda b, i: (0, 0)),
            pl.BlockSpec((1, HEAD_DIM), lambda b, i: (0, 0)),
        ],
        out_specs=pl.BlockSpec((FOX_T, MIX_W), lambda b, i: (b * nq + i, 0)),
        out_shape=jax.ShapeDtypeStruct((batch * seq, MIX_W), BF16),
        scratch_shapes=[
            pltpu.VMEM((seq, MIX_W), BF16),
            pltpu.VMEM((nq, N_HEADS, HEAD_DIM, FOX_T), BF16),
            pltpu.VMEM((N_HEADS, seq, LANES), F32),
            pltpu.VMEM((N_HEADS, HEAD_DIM, FOX_T), BF16),
            pltpu.VMEM((N_HEADS, HEAD_DIM, FOX_T), F32),
        ],
        compiler_params=_cparams(("parallel", "arbitrary")),
        name="fox_attention",
    )(p, p, p, fcol, frow, qg, kg)


DSA_TQ = 128
DSA_TK = 256
BAND_W = 2 * DSA_TQ


def _t5_bucket(dist):
    max_exact = N_BUCKETS // 2
    d = jnp.maximum(dist, 0)
    log_ratio = jnp.log(jnp.maximum(d, 1).astype(F32) / max_exact) / math.log(MAX_DISTANCE / max_exact)
    large = jnp.minimum(max_exact + (log_ratio * (N_BUCKETS - max_exact)).astype(jnp.int32), N_BUCKETS - 1)
    return jnp.where(d < max_exact, d, large)


def _dsa_kernel(cq_ref, k_ref, v_ref, smq_ref, smk_ref, cqg_ref, wuq_ref, wqi_ref, qg_ref, kg_ref, rb_ref,
                o_ref, kn_ref, ki_ref, band_ref, sc_ref, qh_ref, qi_ref, m_ref, l_ref, acc_ref, *, topk):
    b = pl.program_id(0)
    i = pl.program_id(1)
    tq, tk = DSA_TQ, DSA_TK
    nkc = sc_ref.shape[0]
    seq = nkc * tk
    tiles = tk // tq
    nb = ((i + 1) * tq + tk - 1) // tk

    @pl.when(jnp.logical_and(b == 0, i == 0))
    def _():
        r = lax.broadcasted_iota(jnp.int32, (tq, BAND_W), 0)
        c = lax.broadcasted_iota(jnp.int32, (tq, BAND_W), 1)
        bucket = _t5_bucket(tq + r - c)
        for h in range(N_HEADS):
            far = rb_ref[N_BUCKETS - 1, h]
            acc = jnp.zeros((tq, BAND_W), F32)
            for bk in range(N_BUCKETS - 1):
                acc = jnp.where(bucket == bk, rb_ref[bk, h] - far, acc)
            band_ref[h] = acc

    @pl.when(i == 0)
    def _():
        kn_ref[...] = _rms(k_ref[...].astype(F32), kg_ref[...]).astype(BF16)
        ki_ref[...] = smk_ref[:, SM_IK:SM_IK + IDX_DIM].astype(BF16)

    cq = _rms(cq_ref[...].astype(F32), cqg_ref[...]).astype(BF16)
    qf = _dot(cq, wuq_ref[...])
    for h in range(N_HEADS):
        sl = slice(h * HEAD_DIM, (h + 1) * HEAD_DIM)
        qh_ref[h] = (_rms(qf[:, sl], qg_ref[...]) * (HEAD_DIM ** -0.5)).astype(BF16)
    q_idx = (_dot(cq, wqi_ref[...]) * (IDX_DIM ** -0.5)).astype(BF16)
    for h in range(IDX_HEADS):
        qi_ref[h] = q_idx[:, h * IDX_DIM:(h + 1) * IDX_DIM]

    qpos = lax.broadcasted_iota(jnp.int32, (tq, tk), 0) + i * tq
    col = lax.broadcasted_iota(jnp.int32, (tq, tk), 1)

    def score_body(j, _):
        start = pl.multiple_of(j * tk, tk)
        kj = ki_ref[pl.ds(start, tk), :]
        w_h = smq_ref[:, SM_IW:SM_IW + IDX_HEADS] * (IDX_HEADS ** -0.5)
        acc = jnp.zeros((tq, tk), F32)
        for h in range(IDX_HEADS):
            acc = acc + w_h[:, h:h + 1] * jnp.maximum(_dot_nt(qi_ref[h], kj), 0.0)
        sc_ref[j] = jnp.where(col + start <= qpos, acc, -jnp.inf)
        return 0

    lax.fori_loop(0, nb, score_body, 0)

    def over_chunks(fn, init):
        acc = init
        for j in range(nkc):
            acc = lax.cond(j < nb, functools.partial(fn, j), lambda a: a, acc)
        return acc

    def lane_tiles(x):
        return [x[:, t * tq:(t + 1) * tq] for t in range(tiles)]

    def row_total(x):
        return jnp.sum(x, axis=1, keepdims=True)

    def search():
        kf = float(topk)

        def max_fn(j, a):
            for x in lane_tiles(sc_ref[j]):
                a = jnp.maximum(a, x)
            return a

        def min_fn(j, a):
            for x in lane_tiles(sc_ref[j]):
                a = jnp.minimum(a, jnp.where(x == -jnp.inf, jnp.inf, x))
            return a

        smax = jnp.max(over_chunks(max_fn, jnp.full((tq, tq), -jnp.inf, F32)), axis=1, keepdims=True)
        smin = jnp.min(over_chunks(min_fn, jnp.full((tq, tq), jnp.inf, F32)), axis=1, keepdims=True)

        def count_ge(t):
            tb = jnp.broadcast_to(t, (tq, tq))

            def fn(j, a):
                for x in lane_tiles(sc_ref[j]):
                    a = a + jnp.where(x >= tb, 1.0, 0.0)
                return a

            return row_total(over_chunks(fn, jnp.zeros((tq, tq), F32)))

        def midpoint(lo, hi):
            return jnp.where(hi == jnp.inf, smax, 0.5 * (lo + hi))

        def undecided(lo, hi, c_lo, mid):
            return jnp.logical_and(c_lo != kf, jnp.logical_and(mid > lo, mid < hi))

        def cond(carry):
            return jnp.logical_and(carry[0] < 400, carry[1] > 0.0)

        def body(carry):
            it, _, lo, hi, c_lo, c_hi, mid = carry
            upd = undecided(lo, hi, c_lo, mid)
            cnt = count_ge(mid)
            up = jnp.logical_and(upd, cnt >= kf)
            dn = jnp.logical_and(upd, cnt < kf)
            lo = jnp.where(up, mid, lo)
            c_lo = jnp.where(up, cnt, c_lo)
            hi = jnp.where(dn, mid, hi)
            c_hi = jnp.where(dn, cnt, c_hi)
            mid = midpoint(lo, hi)
            active = jnp.max(jnp.where(undecided(lo, hi, c_lo, mid), 1.0, 0.0))
            return it + 1, active, lo, hi, c_lo, c_hi, mid

        lo0 = smin
        hi0 = jnp.full((tq, 1), jnp.inf, F32)
        c_lo0 = count_ge(lo0)
        c_hi0 = jnp.zeros((tq, 1), F32)
        mid0 = midpoint(lo0, hi0)
        act0 = jnp.max(jnp.where(undecided(lo0, hi0, c_lo0, mid0), 1.0, 0.0))
        _, _, lo, hi, c_lo, c_hi, _ = lax.while_loop(
            cond, body, (jnp.int32(0), act0, lo0, hi0, c_lo0, c_hi0, mid0))

        def tie_search():
            need = kf - c_hi
            lo_b = jnp.broadcast_to(lo, (tq, tq))
            hi_b = jnp.broadcast_to(hi, (tq, tq))
            lane = lax.broadcasted_iota(jnp.int32, (tq, tq), 1)

            def tie_body(_, carry):
                jlo, jhi = carry
                jm = (jlo + jhi) // 2
                jm_b = jnp.broadcast_to(jm, (tq, tq))

                def fn(j, a):
                    for t, x in enumerate(lane_tiles(sc_ref[j])):
                        hit = jnp.logical_and(jnp.logical_and(x >= lo_b, x < hi_b), lane + (j * tk + t * tq) <= jm_b)
                        a = a + jnp.where(hit, 1.0, 0.0)
                    return a

                ok = row_total(over_chunks(fn, jnp.zeros((tq, tq), F32))) >= need
                return jnp.where(ok, jlo, jm), jnp.where(ok, jm, jhi)

            n_bits = int(math.ceil(math.log2(seq))) + 1
            _, jmax = lax.fori_loop(0, n_bits, tie_body,
                                    (jnp.full((tq, 1), -1, jnp.int32), jnp.full((tq, 1), seq - 1, jnp.int32)))
            return jmax

        any_tie = jnp.max(jnp.where(c_lo != kf, 1.0, 0.0)) > 0.0
        jmax = lax.cond(any_tie, tie_search, lambda: jnp.full((tq, 1), seq - 1, jnp.int32))
        return lo, hi, jmax

    def keep_all():
        return (jnp.full((tq, 1), -jnp.inf, F32), jnp.full((tq, 1), jnp.inf, F32),
                jnp.full((tq, 1), seq - 1, jnp.int32))

    lo, hi, jmax = lax.cond((i + 1) * tq > topk, search, keep_all)

    m_ref[...] = jnp.full(m_ref.shape, NEG_BIG, F32)
    l_ref[...] = jnp.zeros(l_ref.shape, F32)
    acc_ref[...] = jnp.zeros(acc_ref.shape, F32)

    def attend_body(j, _):
        start = pl.multiple_of(j * tk, tk)
        ks = kn_ref[pl.ds(start, tk), :]
        vs = v_ref[pl.ds(start, tk), :]
        sc = sc_ref[j]
        kpos = col + start
        keep = jnp.logical_or(sc >= hi, jnp.logical_and(sc >= lo, kpos <= jmax))
        keep = jnp.logical_and(keep, kpos <= qpos)
        for h in range(N_HEADS):
            bias = jnp.concatenate(
                [jnp.where(j * tiles + t == i, band_ref[h, :, tq:2 * tq],
                           jnp.where(j * tiles + t == i - 1, band_ref[h, :, 0:tq], 0.0))
                 for t in range(tiles)], axis=1)
            s = jnp.where(keep, _dot_nt(qh_ref[h], ks) + bias, NEG_BIG)
            m_old = m_ref[h]
            m_new = jnp.maximum(m_old, jnp.max(s, axis=1, keepdims=True))
            p = jnp.exp(s - m_new)
            alpha = jnp.exp(m_old - m_new)
            l_ref[h] = alpha * l_ref[h] + jnp.sum(p, axis=1, keepdims=True)
            acc_ref[h] = alpha * acc_ref[h] + _dot(p.astype(BF16), vs)
            m_ref[h] = m_new
        return 0

    lax.fori_loop(0, nb, attend_body, 0)
    for h in range(N_HEADS):
        o_ref[:, h * HEAD_DIM:(h + 1) * HEAD_DIM] = (acc_ref[h] / l_ref[h]).astype(o_ref.dtype)


SUBLANES = 8
COUNT_ROWS = 64


def _fold_rows(x, op):
    return op(x.reshape(x.shape[0] // SUBLANES, SUBLANES, x.shape[1]), axis=0)


def _dsa_t_kernel(cq_ref, k_ref, v_ref, smq_ref, smk_ref, cqg_ref, wuq_ref, wqi_ref, qg_ref, kg_ref, rb_ref,
                  o_ref, kn_ref, ki_ref, vt_ref, band_ref, sc_ref, qt_ref, xi_ref, acc_ref, *, topk):
    b = pl.program_id(0)
    i = pl.program_id(1)
    tq, tk = DSA_TQ, DSA_TK
    nkc = sc_ref.shape[0]
    seq = nkc * tk
    tiles = tk // tq
    nb = ((i + 1) * tq + tk - 1) // tk

    @pl.when(jnp.logical_and(b == 0, i == 0))
    def _():
        c = lax.broadcasted_iota(jnp.int32, (BAND_W, tq), 0)
        r = lax.broadcasted_iota(jnp.int32, (BAND_W, tq), 1)
        bucket = _t5_bucket(tq + r - c)
        for h in range(N_HEADS):
            far = rb_ref[N_BUCKETS - 1, h]
            acc = jnp.zeros((BAND_W, tq), F32)
            for bk in range(N_BUCKETS - 1):
                acc = jnp.where(bucket == bk, rb_ref[bk, h] - far, acc)
            band_ref[h] = acc

    @pl.when(i == 0)
    def _():
        kn_ref[...] = _rms(k_ref[...].astype(F32), kg_ref[...]).astype(BF16)
        ki_ref[...] = smk_ref[:, SM_IK:SM_IK + IDX_DIM].astype(BF16)
        for j in range(nkc):
            for t in range(tiles):
                rows = slice(j * tk + t * tq, j * tk + (t + 1) * tq)
                vt_ref[j, :, t * tq:(t + 1) * tq] = v_ref[rows, :].astype(F32).T.astype(BF16)

    cq_t = _rms(cq_ref[...].astype(F32), cqg_ref[...]).T.astype(BF16)
    q_t = _dot(wuq_ref[...], cq_t)
    g_col = jnp.broadcast_to(qg_ref[...], (HEAD_DIM, tq))
    for h in range(N_HEADS):
        x = q_t[h * HEAD_DIM:(h + 1) * HEAD_DIM, :]
        inv = lax.rsqrt(jnp.mean(x * x, axis=0, keepdims=True) + EPS)
        qt_ref[:, h * tq:(h + 1) * tq] = (x * inv * g_col * (HEAD_DIM ** -0.5)).astype(BF16)
    qi_t = (_dot(wqi_ref[...], cq_t) * (IDX_DIM ** -0.5)).astype(BF16)
    for h in range(IDX_HEADS):
        xi_ref[:, h * tq:(h + 1) * tq] = qi_t[h * IDX_DIM:(h + 1) * IDX_DIM, :]
    w_rows = smq_ref[...].T[SM_IW:SM_IW + IDX_HEADS, :] * (IDX_HEADS ** -0.5)

    kofs = lax.broadcasted_iota(jnp.int32, (tk, tq), 0)
    qpos = lax.broadcasted_iota(jnp.int32, (tk, tq), 1) + i * tq

    def score_body(j, _):
        start = pl.multiple_of(j * tk, tk)
        kj = ki_ref[pl.ds(start, tk), :]
        acc = jnp.zeros((tk, tq), F32)
        for h2 in range(IDX_HEADS // 2):
            r = _dot(kj, xi_ref[:, 2 * h2 * tq:(2 * h2 + 2) * tq])
            for h in (2 * h2, 2 * h2 + 1):
                acc = acc + w_rows[h:h + 1, :] * jnp.maximum(r[:, (h - 2 * h2) * tq:(h - 2 * h2 + 1) * tq], 0.0)
        sc_ref[j] = jnp.where(kofs + start <= qpos, acc, -jnp.inf)
        return 0

    lax.fori_loop(0, nb, score_body, 0)

    @pl.when(nb % 2 == 1)
    def _():
        sc_ref[jnp.minimum(nb, nkc - 1)] = jnp.full((tk, tq), -jnp.inf, F32)

    def over_chunks(fn, init):
        acc = init
        for j in range(nkc):
            acc = lax.cond(j < nb, functools.partial(fn, j), lambda a: a, acc)
        return acc

    def count_where(pred_fn):
        def walk(n_chunks):
            def run():
                a = jnp.zeros((COUNT_ROWS, tq), F32)
                for j in range(n_chunks):
                    hit = jnp.where(pred_fn(j, sc_ref[j]), 1.0, 0.0)
                    a = a + jnp.sum(hit.reshape(tk // COUNT_ROWS, COUNT_ROWS, tq), axis=0)
                return a
            return run

        extents = sorted({min(n, nkc) for n in range(2, nkc + 2, 2)})
        a = lax.switch((nb - 1) // 2, [walk(n) for n in extents])
        return jnp.sum(a, axis=0, keepdims=True)

    def search():
        kf = float(topk)
        smax = jnp.max(over_chunks(lambda j, a: jnp.maximum(a, _fold_rows(sc_ref[j], jnp.max)),
                                   jnp.full((SUBLANES, tq), -jnp.inf, F32)), axis=0, keepdims=True)
        smin = jnp.min(over_chunks(
            lambda j, a: jnp.minimum(a, _fold_rows(jnp.where(sc_ref[j] == -jnp.inf, jnp.inf, sc_ref[j]), jnp.min)),
            jnp.full((SUBLANES, tq), jnp.inf, F32)), axis=0, keepdims=True)

        def count_ge(t):
            return count_where(lambda j, x: x >= t)

        def midpoint(lo, hi):
            return jnp.where(hi == jnp.inf, smax, 0.5 * (lo + hi))

        def undecided(lo, hi, c_lo, mid):
            return jnp.logical_and(c_lo != kf, jnp.logical_and(mid > lo, mid < hi))

        def cond(carry):
            return jnp.logical_and(carry[0] < 400, carry[1] > 0.0)

        def body(carry):
            it, _, lo, hi, c_lo, c_hi, mid = carry
            upd = undecided(lo, hi, c_lo, mid)
            cnt = count_ge(mid)
            up = jnp.logical_and(upd, cnt >= kf)
            dn = jnp.logical_and(upd, cnt < kf)
            lo = jnp.where(up, mid, lo)
            c_lo = jnp.where(up, cnt, c_lo)
            hi = jnp.where(dn, mid, hi)
            c_hi = jnp.where(dn, cnt, c_hi)
            mid = midpoint(lo, hi)
            active = jnp.max(jnp.where(undecided(lo, hi, c_lo, mid), 1.0, 0.0))
            return it + 1, active, lo, hi, c_lo, c_hi, mid

        lo0 = smin
        hi0 = jnp.full((1, tq), jnp.inf, F32)
        c_lo0 = count_ge(lo0)
        c_hi0 = jnp.zeros((1, tq), F32)
        mid0 = midpoint(lo0, hi0)
        act0 = jnp.max(jnp.where(undecided(lo0, hi0, c_lo0, mid0), 1.0, 0.0))
        _, _, lo, hi, c_lo, c_hi, _ = lax.while_loop(
            cond, body, (jnp.int32(0), act0, lo0, hi0, c_lo0, c_hi0, mid0))

        def tie_search():
            need = kf - c_hi

            def tie_body(_, carry):
                jlo, jhi = carry
                jm = (jlo + jhi) // 2
                cnt = count_where(lambda j, x: jnp.logical_and(jnp.logical_and(x >= lo, x < hi),
                                                               kofs + j * tk <= jm))
                ok = cnt >= need
                return jnp.where(ok, jlo, jm), jnp.where(ok, jm, jhi)

            n_bits = int(math.ceil(math.log2(seq))) + 1
            _, jmax = lax.fori_loop(0, n_bits, tie_body,
                                    (jnp.full((1, tq), -1, jnp.int32), jnp.full((1, tq), seq - 1, jnp.int32)))
            return jmax

        any_tie = jnp.max(jnp.where(c_lo != kf, 1.0, 0.0)) > 0.0
        jmax = lax.cond(any_tie, tie_search, lambda: jnp.full((1, tq), seq - 1, jnp.int32))
        return lo, hi, jmax

    def keep_all():
        return (jnp.full((1, tq), -jnp.inf, F32), jnp.full((1, tq), jnp.inf, F32),
                jnp.full((1, tq), seq - 1, jnp.int32))

    lo, hi, jmax = lax.cond((i + 1) * tq > topk, search, keep_all)

    acc_ref[...] = jnp.zeros(acc_ref.shape, F32)

    def attend_body(near, j, carry):
        ms, ls = carry
        start = pl.multiple_of(j * tk, tk)
        kc = kn_ref[pl.ds(start, tk), :]
        vt = vt_ref[j]
        sc = sc_ref[j]
        kpos = kofs + start
        keep = jnp.logical_or(sc >= hi, jnp.logical_and(sc >= lo, kpos <= jmax))
        if near:
            keep = jnp.logical_and(keep, kpos <= qpos)
        s_all = _dot(kc, qt_ref[...])
        new_ms, new_ls, ps, alphas = [], [], [], []
        for h in range(N_HEADS):
            s = s_all[:, h * tq:(h + 1) * tq]
            if near:
                s = s + jnp.concatenate(
                    [jnp.where(j * tiles + t == i, band_ref[h, tq:2 * tq, :],
                               jnp.where(j * tiles + t == i - 1, band_ref[h, 0:tq, :], 0.0))
                     for t in range(tiles)], axis=0)
            s = jnp.where(keep, s, NEG_BIG)
            m_new = jnp.maximum(ms[h], jnp.max(s, axis=0, keepdims=True))
            p = jnp.exp(s - m_new)
            alphas.append(jnp.exp(ms[h] - m_new))
            new_ls.append(alphas[h] * ls[h] + jnp.sum(p, axis=0, keepdims=True))
            ps.append(p.astype(BF16))
            new_ms.append(m_new)
        acc_ref[...] = (jnp.concatenate(alphas, axis=1) * acc_ref[...]
                        + _dot(vt, jnp.concatenate(ps, axis=1)))
        return tuple(new_ms), tuple(new_ls)

    init = (tuple(jnp.full((1, tq), NEG_BIG, F32) for _ in range(N_HEADS)),
            tuple(jnp.zeros((1, tq), F32) for _ in range(N_HEADS)))
    n_far = jnp.maximum((i - 1) * tq // tk, 0)
    carry = lax.fori_loop(0, n_far, functools.partial(attend_body, False), init)
    _, ls = lax.fori_loop(n_far, nb, functools.partial(attend_body, True), carry)
    for h in range(N_HEADS):
        out_t = acc_ref[:, h * tq:(h + 1) * tq] / ls[h]
        o_ref[:, h * HEAD_DIM:(h + 1) * HEAD_DIM] = out_t.T.astype(o_ref.dtype)


def dsa_attention(p, sm, cqg, wuq_t, wqi_t, qg_col, kg, rel_bias, batch, seq):
    nq = seq // DSA_TQ
    nkc = seq // DSA_TK
    topk = min(DSA_TOPK, seq // 4)
    kern = functools.partial(_dsa_t_kernel, topk=topk)
    return pl.pallas_call(
        kern,
        grid=(batch, nq),
        in_specs=[
            pl.BlockSpec((DSA_TQ, DSA_Q_RANK), lambda b, i: (b * nq + i, COL_CQ // DSA_Q_RANK)),
            pl.BlockSpec((seq, HEAD_DIM), lambda b, i: (b, COL_DK // HEAD_DIM)),
            pl.BlockSpec((seq, HEAD_DIM), lambda b, i: (b, COL_DV // HEAD_DIM)),
            pl.BlockSpec((DSA_TQ, SM_W), lambda b, i: (b * nq + i, 0)),
            pl.BlockSpec((seq, SM_W), lambda b, i: (b, 0)),
            pl.BlockSpec((1, DSA_Q_RANK), lambda b, i: (0, 0)),
            pl.BlockSpec((N_HEADS * HEAD_DIM, DSA_Q_RANK), lambda b, i: (0, 0)),
            pl.BlockSpec((IDX_HEADS * IDX_DIM, DSA_Q_RANK), lambda b, i: (0, 0)),
            pl.BlockSpec((HEAD_DIM, 1), lambda b, i: (0, 0)),
            pl.BlockSpec((1, HEAD_DIM), lambda b, i: (0, 0)),
            pl.BlockSpec(memory_space=pltpu.SMEM),
        ],
        out_specs=pl.BlockSpec((DSA_TQ, MIX_W), lambda b, i: (b * nq + i, 0)),
        out_shape=jax.ShapeDtypeStruct((batch * seq, MIX_W), BF16),
        scratch_shapes=[
            pltpu.VMEM((seq, HEAD_DIM), BF16),
            pltpu.VMEM((seq, IDX_DIM), BF16),
            pltpu.VMEM((nkc, HEAD_DIM, DSA_TK), BF16),
            pltpu.VMEM((N_HEADS, BAND_W, DSA_TQ), F32),
            pltpu.VMEM((nkc, DSA_TK, DSA_TQ), F32),
            pltpu.VMEM((HEAD_DIM, N_HEADS * DSA_TQ), BF16),
            pltpu.VMEM((IDX_DIM, IDX_HEADS * DSA_TQ), BF16),
            pltpu.VMEM((HEAD_DIM, N_HEADS * DSA_TQ), F32),
        ],
        compiler_params=_cparams(("arbitrary", "arbitrary")),
        name="dsa_attention",
    )(p, p, p, sm, sm, cqg, wuq_t, wqi_t, qg_col, kg, rel_bias)


CONV_PAD = 8


def _causal_conv(x_ref, xp_ref, first, w_ref, b_ref, ext_ref):
    ext_ref[0:CONV_PAD, :] = xp_ref[CHUNK - CONV_PAD:CHUNK, :].astype(F32) * first
    ext_ref[CONV_PAD:CONV_PAD + CHUNK, :] = x_ref[...].astype(F32)
    acc = b_ref[...] + ext_ref[CONV_PAD:CONV_PAD + CHUNK, :] * w_ref[SSD_CONV - 1:SSD_CONV, :]
    for d in range(1, SSD_CONV):
        acc = acc + ext_ref[CONV_PAD - d:CONV_PAD - d + CHUNK, :] * w_ref[SSD_CONV - 1 - d:SSD_CONV - d, :]
    return _silu(acc)


def _ssd_kernel(z_ref, xs_ref, bc_ref, xsp_ref, bcp_ref, sm_ref, cwx_ref, cbx_ref, cwb_ref, cbb_ref,
                dtb_ref, alog_ref, dsk_ref, ng_ref, o_ref, prev_ref, y_ref, extx_ref, extb_ref):
    c = CHUNK
    n = pl.program_id(1)

    @pl.when(n == 0)
    def _():
        prev_ref[...] = jnp.zeros_like(prev_ref)

    first = (n > 0).astype(F32)
    xs = _causal_conv(xs_ref, xsp_ref, first, cwx_ref, cbx_ref, extx_ref)
    bc = _causal_conv(bc_ref, bcp_ref, first, cwb_ref, cbb_ref, extb_ref)

    dt_t = _softplus(sm_ref[...].T + dtb_ref[...])
    cs_t = _cumsum_lanes(dt_t * (-jnp.exp(alog_ref[...])))
    cs = cs_t.T
    ii = lax.broadcasted_iota(jnp.int32, (c, c), 0)
    jj = lax.broadcasted_iota(jnp.int32, (c, c), 1)
    tril = ii >= jj
    pair_w = 2 * SSD_HEAD_DIM
    first_head = jj < SSD_HEAD_DIM
    first_head_row = lax.broadcasted_iota(jnp.int32, (1, pair_w), 1) < SSD_HEAD_DIM
    gn = SSD_GROUPS * SSD_STATE
    hpg = SSD_HEADS // SSD_GROUPS
    for g in range(SSD_GROUPS):
        bg = bc[:, g * SSD_STATE:(g + 1) * SSD_STATE]
        cg = bc[:, gn + g * SSD_STATE:gn + (g + 1) * SSD_STATE].astype(BF16)
        cb = _dot_nt(cg, bg.astype(BF16))
        bg_t = bg.T
        y_off = _dot(cg, prev_ref[g].astype(BF16))
        for pr in range(hpg // 2):
            cols = slice((g * hpg + 2 * pr) * SSD_HEAD_DIM, (g * hpg + 2 * pr + 2) * SSD_HEAD_DIM)
            rcols = slice(2 * pr * SSD_HEAD_DIM, (2 * pr + 2) * SSD_HEAD_DIM)
            x_pair = xs[:, cols]
            x_bf = x_pair.astype(BF16)
            y_diag, st, exp_a, exp_last = [], [], [], []
            for k in range(2):
                row = SM_DT + g * hpg + 2 * pr + k
                a_row = cs_t[row:row + 1, :]
                dt_row = dt_t[row:row + 1, :]
                last = cs_t[row:row + 1, c - 1:c]
                a_col = jnp.broadcast_to(cs[:, row:row + 1], (c, c))
                seg = jnp.where(tril, jnp.exp(jnp.where(tril, a_col - a_row, 0.0)), 0.0)
                y_diag.append(_dot((cb * seg * dt_row).astype(BF16), x_bf))
                st.append(_dot((bg_t * (dt_row * jnp.exp(last - a_row))).astype(BF16), x_bf))
                exp_a.append(jnp.exp(a_col))
                exp_last.append(jnp.exp(last))
            y_ref[:, cols] = (jnp.where(first_head, y_diag[0], y_diag[1])
                              + y_off[:, rcols] * jnp.where(first_head, exp_a[0], exp_a[1])
                              + dsk_ref[:, cols] * x_pair)
            prev_ref[g, :, rcols] = (jnp.where(first_head_row, exp_last[0], exp_last[1]) * prev_ref[g, :, rcols]
                                     + jnp.where(first_head, st[0], st[1]))
    gated = y_ref[...] * _silu(z_ref[...].astype(F32))
    gw = SSD_INNER // SSD_GROUPS
    for g in range(SSD_GROUPS):
        sl = slice(g * gw, (g + 1) * gw)
        o_ref[:, sl] = _rms(gated[:, sl], ng_ref[:, sl]).astype(o_ref.dtype)


def ssd_mixer(p, sm, cw, cb, dtb_col, alog_col, dskip_row, ng, batch, seq):
    n = seq // CHUNK
    bcw = 2 * SSD_GROUPS * SSD_STATE

    def cur(width, colbase):
        return pl.BlockSpec((CHUNK, width), lambda b, i: (b * n + i, colbase // width))

    def prv(width, colbase):
        return pl.BlockSpec((CHUNK, width), lambda b, i: (b * n + jnp.maximum(i - 1, 0), colbase // width))

    def const(shape):
        return pl.BlockSpec(shape, lambda b, i: (0, 0))

    return pl.pallas_call(
        _ssd_kernel,
        grid=(batch, n),
        in_specs=[
            cur(SSD_INNER, COL_Z), cur(SSD_INNER, COL_XS), cur(bcw, COL_BC),
            prv(SSD_INNER, COL_XS), prv(bcw, COL_BC),
            pl.BlockSpec((CHUNK, SM_W), lambda b, i: (b * n + i, 0)),
            const((SSD_CONV, SSD_INNER)), const((1, SSD_INNER)),
            const((SSD_CONV, bcw)), const((1, bcw)),
            const((SM_W, 1)), const((SM_W, 1)),
            const((1, SSD_INNER)), const((1, SSD_INNER)),
        ],
        out_specs=pl.BlockSpec((CHUNK, SSD_INNER), lambda b, i: (b * n + i, 0)),
        out_shape=jax.ShapeDtypeStruct((batch * seq, SSD_INNER), BF16),
        scratch_shapes=[
            pltpu.VMEM((SSD_GROUPS, SSD_STATE, SSD_INNER // SSD_GROUPS), F32),
            pltpu.VMEM((CHUNK, SSD_INNER), F32),
            pltpu.VMEM((CONV_PAD + CHUNK, SSD_INNER), F32),
            pltpu.VMEM((CONV_PAD + CHUNK, bcw), F32),
        ],
        compiler_params=_cparams(("parallel", "arbitrary")),
        name="ssd_mixer",
    )(p, p, p, p, p, sm, cw[:, :SSD_INNER], cb[:, :SSD_INNER], cw[:, SSD_INNER:], cb[:, SSD_INNER:],
      dtb_col, alog_col, dskip_row, ng)


MERGE_TM = 256


def _merge_kernel(x_ref, gl_ref, gb_ref, oret_ref, ofox_ref, odsa_ref, ossd_ref, wbr_ref, wout_ref, o_ref):
    branches = (oret_ref, ofox_ref, odsa_ref, ossd_ref)
    merged = None
    row0 = 0
    for bi, br in enumerate(branches):
        width = br.shape[1]
        sl = slice(bi * D_MODEL, (bi + 1) * D_MODEL)
        gate = 1.0 / (1.0 + jnp.exp(-(gl_ref[:, sl].astype(F32) + gb_ref[:, sl])))
        term = gate * _dot(br[...], wbr_ref[row0:row0 + width, :])
        merged = term if merged is None else merged + term
        row0 += width
    o_ref[...] = x_ref[...] + _dot(merged.astype(BF16), wout_ref[...])


def merge_project(x, p, gate_b, o_ret, o_fox, o_dsa, o_ssd, w_br, w_out):
    m = x.shape[0]
    tm = MERGE_TM

    def rows(width):
        return pl.BlockSpec((tm, width), lambda i: (i, 0))

    def const(shape):
        return pl.BlockSpec(shape, lambda i: (0, 0), pipeline_mode=pl.Buffered(1))

    return pl.pallas_call(
        _merge_kernel,
        grid=(m // tm,),
        in_specs=[
            rows(D_MODEL), rows(N_BRANCH * D_MODEL), const((1, N_BRANCH * D_MODEL)),
            rows(MIX_W), rows(MIX_W), rows(MIX_W), rows(SSD_INNER),
            const(w_br.shape), const(w_out.shape),
        ],
        out_specs=rows(D_MODEL),
        out_shape=jax.ShapeDtypeStruct(x.shape, x.dtype),
        compiler_params=_cparams(("parallel",)),
        name="merge_project",
    )(x, p, gate_b, o_ret, o_fox, o_dsa, o_ssd, w_br, w_out)


FFN_TM = 1024
FFN_TF = 512


def _ffn_kernel(x_ref, g_ref, w1_ref, w2_ref, o_ref, h_ref):
    @pl.when(pl.program_id(1) == 0)
    def _():
        h_ref[...] = _rms(x_ref[...], g_ref[...]).astype(BF16)
        o_ref[...] = x_ref[...]

    a = jnp.maximum(_dot(h_ref[...], w1_ref[...]), 0.0)
    o_ref[...] += _dot((a * a).astype(BF16), w2_ref[...])


def ffn(x, g, w1, w2):
    m, d = x.shape
    dff = w1.shape[1]
    tm, tf = min(FFN_TM, m), FFN_TF
    return pl.pallas_call(
        _ffn_kernel,
        grid=(m // tm, dff // tf),
        in_specs=[
            pl.BlockSpec((tm, d), lambda i, f: (i, 0), pipeline_mode=pl.Buffered(1)),
            pl.BlockSpec((1, d), lambda i, f: (0, 0)),
            pl.BlockSpec((d, tf), lambda i, f: (0, f)),
            pl.BlockSpec((tf, d), lambda i, f: (f, 0)),
        ],
        out_specs=pl.BlockSpec((tm, d), lambda i, f: (i, 0)),
        out_shape=jax.ShapeDtypeStruct(x.shape, x.dtype),
        scratch_shapes=[pltpu.VMEM((tm, d), BF16)],
        compiler_params=_cparams(("parallel", "arbitrary")),
        name="ffn",
    )(x, g, w1, w2)


SRC_RET = 0
SRC_FOX = SRC_RET + 4 * MIX_W
SRC_FF = SRC_FOX + 3 * MIX_W
SRC_CQ = SRC_FF + N_HEADS
SRC_DK = SRC_CQ + DSA_Q_RANK
SRC_IK = SRC_DK + 2 * HEAD_DIM
SRC_IW = SRC_IK + IDX_DIM
SRC_Z = SRC_IW + IDX_HEADS
SRC_DT = SRC_Z + 2 * SSD_INNER + 2 * SSD_GROUPS * SSD_STATE
SRC_GATE = SRC_DT + SSD_HEADS
IN_TOTAL = SRC_GATE + N_BRANCH * D_MODEL
MAIN_RUNS = ((COL_GATE, SRC_GATE), (COL_RET, SRC_RET), (COL_Z, SRC_Z), (COL_CQ, SRC_CQ),
             (COL_FOX, SRC_FOX), (COL_DK, SRC_DK))
RELAYOUT_W = 512
RELAYOUT_TILES = RELAYOUT_W // LANES


def _relayout_tables():
    starts, shifts = [], []
    for blk in range(N_MAIN // RELAYOUT_W):
        o = blk * RELAYOUT_W
        dst, src = [r for r in MAIN_RUNS if r[0] <= o][-1]
        col = src + (o - dst)
        starts.append(col // LANES)
        shifts.append(col % LANES)
    return jnp.asarray(starts, jnp.int32), jnp.asarray(shifts, jnp.int32)


def _relayout_kernel(start_ref, shift_ref, *refs):
    del start_ref
    tiles, o_ref = refs[:-1], refs[-1]
    shift = shift_ref[pl.program_id(1)]
    amount = lax.rem(LANES - shift, LANES)
    lane = lax.broadcasted_iota(jnp.int32, tiles[0].shape, 1)
    rolled = [pltpu.roll(t[...], amount, 1) for t in tiles]
    for k in range(RELAYOUT_TILES):
        piece = jnp.where(lane < LANES - shift, rolled[k], rolled[k + 1])
        o_ref[:, k * LANES:(k + 1) * LANES] = piece.astype(o_ref.dtype)


def relayout_main(w_in):
    depth, d, n_src = w_in.shape
    last = (n_src - 1) // LANES
    starts, shifts = _relayout_tables()

    def tile(k):
        return pl.BlockSpec((None, d, LANES), lambda l, b, st, sh: (l, 0, jnp.minimum(st[b] + k, last)))

    return pl.pallas_call(
        _relayout_kernel,
        grid_spec=pltpu.PrefetchScalarGridSpec(
            num_scalar_prefetch=2,
            grid=(depth, N_MAIN // RELAYOUT_W),
            in_specs=[tile(k) for k in range(RELAYOUT_TILES + 1)],
            out_specs=pl.BlockSpec((None, d, RELAYOUT_W), lambda l, b, st, sh: (l, 0, b)),
        ),
        out_shape=jax.ShapeDtypeStruct((depth, d, N_MAIN), BF16),
        compiler_params=_cparams(("parallel", "arbitrary")),
        name="relayout_main",
    )(starts, shifts, *([w_in] * (RELAYOUT_TILES + 1)))


SMALL_PIECES = ((SRC_DT, SM_DT, SSD_HEADS), (SRC_FF, SM_F, N_HEADS), (SRC_IW, SM_IW, IDX_HEADS),
                (SRC_IK, SM_IK, IDX_DIM))


def _relayout_small_kernel(*refs):
    tiles, o_ref = refs[:-1], refs[-1]
    lane = lax.broadcasted_iota(jnp.int32, o_ref.shape, 1)
    out = jnp.zeros(o_ref.shape, F32)
    for t, (src, dst, width) in zip(tiles, SMALL_PIECES):
        moved = pltpu.roll(t[...], (dst - src % LANES) % LANES, 1)
        out = jnp.where(jnp.logical_and(lane >= dst, lane < dst + width), moved, out)
    o_ref[...] = out.astype(o_ref.dtype)


def relayout_small(w_in):
    depth, d, _ = w_in.shape
    for src, _, width in SMALL_PIECES:
        assert src // LANES == (src + width - 1) // LANES

    def tile(src):
        return pl.BlockSpec((None, d, LANES), lambda l: (l, 0, src // LANES))

    return pl.pallas_call(
        _relayout_small_kernel,
        grid=(depth,),
        in_specs=[tile(src) for src, _, _ in SMALL_PIECES],
        out_specs=pl.BlockSpec((None, d, SM_W), lambda l: (l, 0, 0)),
        out_shape=jax.ShapeDtypeStruct((depth, d, SM_W), BF16),
        compiler_params=_cparams(("parallel",)),
        name="relayout_small",
    )(*([w_in] * len(SMALL_PIECES)))


def _pad_to(v, offset, total):
    return jnp.zeros((total,), v.dtype).at[offset:offset + v.shape[0]].set(v)


def _rotary_tables(seq):
    half = HEAD_DIM // 2
    inv = 1.0 / (10000.0 ** (jnp.arange(half, dtype=F32) / half))
    ang = jnp.arange(seq, dtype=F32)[:, None] * inv[None, :]
    cos, sin = jnp.cos(ang), jnp.sin(ang)
    return jnp.concatenate([cos, cos], axis=1), jnp.concatenate([-sin, sin], axis=1)


def kernel(x, norm1_g, w_in, gate_b, fox_f_b, fox_qn_g, fox_kn_g, dsa_cq_g, dsa_w_uq, dsa_w_qidx, dsa_qn_g,
           dsa_kn_g, rel_bias, ssd_conv_w, ssd_conv_b, ssd_dt_bias, ssd_a_log, ssd_d, ssd_norm_g, w_br, w_out,
           norm2_g, w_ff1, w_ff2):
    batch, seq, d = x.shape
    tokens = batch * seq
    xt = x.reshape(tokens, d)
    cos, sin = _rotary_tables(seq)
    tm = min(1024, tokens)
    w_main = relayout_main(w_in)
    w_small = relayout_small(w_in)
    for l in range(DEPTH):
        g1 = norm1_g[l][None, :]
        p = norm_matmul(xt, g1, w_main, l, BF16, tm, 1024)
        sm = norm_matmul(xt, g1, w_small, l, F32, tm, SM_W)

        o_ret = retention(p, cos, sin, batch, seq)

        fb_row = _pad_to(fox_f_b[l], SM_F, SM_W)[None, :]
        fcol, frow = fox_prep(sm, fb_row, batch, seq)
        o_fox = fox_attention(p, fcol, frow, fox_qn_g[l][None, :], fox_kn_g[l][None, :], batch, seq)

        o_dsa = dsa_attention(p, sm, dsa_cq_g[l][None, :], dsa_w_uq[l].T.astype(BF16), dsa_w_qidx[l].T.astype(BF16),
                              dsa_qn_g[l][:, None], dsa_kn_g[l][None, :], rel_bias, batch, seq)

        o_ssd = ssd_mixer(p, sm, ssd_conv_w[l], ssd_conv_b[l][None, :],
                          _pad_to(ssd_dt_bias[l], SM_DT, SM_W)[:, None], _pad_to(ssd_a_log[l], SM_DT, SM_W)[:, None],
                          jnp.repeat(ssd_d[l], SSD_HEAD_DIM)[None, :], ssd_norm_g[l][None, :], batch, seq)

        xt = merge_project(xt, p, gate_b[l][None, :], o_ret, o_fox, o_dsa, o_ssd,
                           w_br[l].astype(BF16), w_out[l].astype(BF16))
        xt = ffn(xt, norm2_g[l][None, :], w_ff1[l].astype(BF16), w_ff2[l].astype(BF16))
    return xt.reshape(batch, seq, d)
```

```python
import functools
import math

import jax
import jax.numpy as jnp
from jax import lax
from jax.experimental import pallas as pl
from jax.experimental.pallas import tpu as pltpu

F32 = jnp.float32
BF16 = jnp.bfloat16

D_MODEL = 2048
DEPTH = 4
HEAD_DIM = 128
N_HEADS = 4
DSA_Q_RANK = 512
IDX_HEADS = 16
IDX_DIM = 64
DSA_TOPK = 256
SSD_HEADS = 16
SSD_HEAD_DIM = 64
SSD_GROUPS = 2
SSD_STATE = 128
SSD_CONV = 4
SSD_INNER = SSD_HEADS * SSD_HEAD_DIM
D_FF = 4 * D_MODEL
N_BUCKETS = 32
MAX_DISTANCE = 128
CHUNK = 128
EPS = 1e-6
N_BRANCH = 4
MIX_W = N_HEADS * HEAD_DIM

COL_GATE = 0
COL_RET = COL_GATE + N_BRANCH * D_MODEL
COL_Z = COL_RET + 4 * MIX_W
COL_XS = COL_Z + SSD_INNER
COL_BC = COL_XS + SSD_INNER
COL_CQ = COL_BC + 2 * SSD_GROUPS * SSD_STATE
COL_FOX = COL_CQ + DSA_Q_RANK
COL_DK = COL_FOX + 3 * MIX_W
COL_DV = COL_DK + HEAD_DIM
N_MAIN_USED = COL_DV + HEAD_DIM
N_MAIN = 15360
SM_DT = 0
SM_F = 16
SM_IW = 32
SM_IK = 64
SM_W = 128

LANES = 128
VMEM_LIMIT = 56 * 1024 * 1024
NEG_BIG = -1e30


def _cparams(sem):
    return pltpu.CompilerParams(dimension_semantics=sem, vmem_limit_bytes=VMEM_LIMIT)


def _dot(a, b):
    return jnp.dot(a, b, preferred_element_type=F32)


def _dot_nt(a, b):
    return lax.dot_general(a, b, (((1,), (1,)), ((), ())), preferred_element_type=F32)


def _dot_tn(a, b):
    return lax.dot_general(a, b, (((0,), (0,)), ((), ())), preferred_element_type=F32)


def _rms(x, g):
    return x * lax.rsqrt(jnp.mean(x * x, axis=-1, keepdims=True) + EPS) * g


def _silu(x):
    return x / (1.0 + jnp.exp(-x))


def _softplus(x):
    return jnp.maximum(x, 0.0) + jnp.log1p(jnp.exp(-jnp.abs(x)))


def _cumsum_lanes(x):
    lane = lax.broadcasted_iota(jnp.int32, x.shape, 1)
    d = 1
    while d < x.shape[1]:
        x = x + jnp.where(lane >= d, pltpu.roll(x, d, 1), 0.0)
        d *= 2
    return x


def _norm_matmul_kernel(x_ref, g_ref, w_ref, o_ref, h_ref):
    @pl.when(pl.program_id(1) == 0)
    def _():
        h_ref[...] = _rms(x_ref[...], g_ref[...]).astype(BF16)

    o_ref[...] = _dot(h_ref[...], w_ref[...]).astype(o_ref.dtype)


def norm_matmul(x, g, w, layer, out_dtype, tm, tn):
    m, d = x.shape
    n = w.shape[2]
    return pl.pallas_call(
        _norm_matmul_kernel,
        grid=(m // tm, n // tn),
        in_specs=[
            pl.BlockSpec((tm, d), lambda i, j: (i, 0)),
            pl.BlockSpec((1, d), lambda i, j: (0, 0)),
            pl.BlockSpec((None, d, tn), lambda i, j: (layer, 0, j)),
        ],
        out_specs=pl.BlockSpec((tm, tn), lambda i, j: (i, j)),
        out_shape=jax.ShapeDtypeStruct((m, n), out_dtype),
        scratch_shapes=[pltpu.VMEM((tm, d), BF16)],
        compiler_params=_cparams(("parallel", "arbitrary")),
        name="norm_matmul",
    )(x, g, w)


def _retention_kernel(q_ref, k_ref, v_ref, g_ref, cos_ref, sin_ref, o_ref, state_ref):
    c = CHUNK

    @pl.when(pl.program_id(1) == 0)
    def _():
        state_ref[...] = jnp.zeros_like(state_ref)

    cos = cos_ref[...]
    sin = sin_ref[...]
    ii = lax.broadcasted_iota(jnp.int32, (c, c), 0)
    jj = lax.broadcasted_iota(jnp.int32, (c, c), 1)
    rel = (ii - jj).astype(F32)
    i_col = lax.broadcasted_iota(jnp.int32, (c, 1), 0).astype(F32)
    for h in range(N_HEADS):
        lg = math.log1p(-(2.0 ** (-5.0 - h)))
        sl = slice(h * HEAD_DIM, (h + 1) * HEAD_DIM)
        q = q_ref[:, sl].astype(F32)
        k = k_ref[:, sl].astype(F32)
        v = v_ref[:, sl]
        qr = q * cos + pltpu.roll(q, HEAD_DIM // 2, 1) * sin
        kr = (k * cos + pltpu.roll(k, HEAD_DIM // 2, 1) * sin) * (HEAD_DIM ** -0.5)
        decay = jnp.where(rel >= 0, jnp.exp(lg * jnp.maximum(rel, 0.0)), 0.0)
        scores = _dot_nt(qr.astype(BF16), kr.astype(BF16)) * decay
        y = _dot(scores.astype(BF16), v)
        q_dec = jnp.exp(lg * (i_col + 1.0))
        k_dec = jnp.exp(lg * (c - 1.0 - i_col))
        st = state_ref[h]
        y = y + _dot((qr * q_dec).astype(BF16), st.astype(BF16))
        kv = _dot_tn((kr * k_dec).astype(BF16), v)
        state_ref[h] = math.exp(lg * c) * st + kv
        yc = y - jnp.mean(y, axis=-1, keepdims=True)
        yn = yc * lax.rsqrt(jnp.mean(yc * yc, axis=-1, keepdims=True) + EPS)
        o_ref[:, sl] = (_silu(g_ref[:, sl].astype(F32)) * yn).astype(o_ref.dtype)


def retention(p, cos, sin, batch, seq):
    n = seq // CHUNK
    base = COL_RET // MIX_W

    def col(j):
        return pl.BlockSpec((CHUNK, MIX_W), lambda b, i: (b * n + i, base + j))

    tab = pl.BlockSpec((CHUNK, HEAD_DIM), lambda b, i: (i, 0))
    return pl.pallas_call(
        _retention_kernel,
        grid=(batch, n),
        in_specs=[col(0), col(1), col(2), col(3), tab, tab],
        out_specs=pl.BlockSpec((CHUNK, MIX_W), lambda b, i: (b * n + i, 0)),
        out_shape=jax.ShapeDtypeStruct((batch * seq, MIX_W), BF16),
        scratch_shapes=[pltpu.VMEM((N_HEADS, HEAD_DIM, HEAD_DIM), F32)],
        compiler_params=_cparams(("parallel", "arbitrary")),
        name="retention",
    )(p, p, p, p, cos, sin)


def _fox_prep_kernel(sm_ref, fb_ref, fcol_ref, frow_ref, carry_ref):
    @pl.when(pl.program_id(1) == 0)
    def _():
        carry_ref[...] = jnp.zeros_like(carry_ref)

    t = sm_ref[...] + fb_ref[...]
    lf = jnp.minimum(t, 0.0) - jnp.log1p(jnp.exp(-jnp.abs(t)))
    cs = _cumsum_lanes(lf.T) + carry_ref[...]
    carry_ref[...] = cs[:, LANES - 1:LANES]
    frow_ref[0, 0] = cs[SM_F:SM_F + 8, :]
    fcol_ref[...] = cs.T


def fox_prep(sm, fb_row, batch, seq):
    n = seq // CHUNK
    return pl.pallas_call(
        _fox_prep_kernel,
        grid=(batch, n),
        in_specs=[
            pl.BlockSpec((CHUNK, SM_W), lambda b, i: (b * n + i, 0)),
            pl.BlockSpec((1, SM_W), lambda b, i: (0, 0)),
        ],
        out_specs=[
            pl.BlockSpec((CHUNK, SM_W), lambda b, i: (b * n + i, 0)),
            pl.BlockSpec((1, 1, 8, CHUNK), lambda b, i: (b, i, 0, 0)),
        ],
        out_shape=[
            jax.ShapeDtypeStruct((batch * seq, SM_W), F32),
            jax.ShapeDtypeStruct((batch, n, 8, CHUNK), F32),
        ],
        scratch_shapes=[pltpu.VMEM((SM_W, 1), F32)],
        compiler_params=_cparams(("parallel", "arbitrary")),
        name="fox_prep",
    )(sm, fb_row)


FOX_T = 256


def _fox_kernel(q_ref, k_ref, v_ref, fcol_ref, frow_ref, qg_ref, kg_ref, o_ref,
                kn_ref, vt_ref, fb_ref, qt_ref, acc_ref):
    i = pl.program_id(1)
    t = FOX_T
    nkc = vt_ref.shape[0]
    sub = t // CHUNK

    @pl.when(i == 0)
    def _():
        for h in range(N_HEADS):
            sl = slice(h * HEAD_DIM, (h + 1) * HEAD_DIM)
            kn_ref[:, sl] = _rms(k_ref[:, sl].astype(F32), kg_ref[...]).astype(BF16)
            fb_ref[h] = jnp.broadcast_to(fcol_ref[:, SM_F + h:SM_F + h + 1], fb_ref.shape[1:])
            for j in range(nkc):
                for c in range(sub):
                    rows = slice(j * t + c * CHUNK, j * t + (c + 1) * CHUNK)
                    vt_ref[j, h, :, c * CHUNK:(c + 1) * CHUNK] = v_ref[rows, sl].astype(F32).T.astype(BF16)

    fqs = []
    for h in range(N_HEADS):
        sl = slice(h * HEAD_DIM, (h + 1) * HEAD_DIM)
        qn = _rms(q_ref[:, sl].astype(F32), qg_ref[...]) * (HEAD_DIM ** -0.5)
        qt_ref[h] = jnp.concatenate([qn[c * CHUNK:(c + 1) * CHUNK, :].T for c in range(sub)], axis=1).astype(BF16)
        fqs.append(jnp.concatenate([frow_ref[0, i * sub + c, h:h + 1, :] for c in range(sub)], axis=1))
        acc_ref[h] = jnp.zeros(acc_ref.shape[1:], F32)

    kofs = lax.broadcasted_iota(jnp.int32, (t, t), 0)
    qofs = lax.broadcasted_iota(jnp.int32, (t, t), 1)

    def body(diag, j, carry):
        ms, ls = carry
        start = pl.multiple_of(j * t, t)
        scores = [_dot(kn_ref[pl.ds(start, t), h * HEAD_DIM:(h + 1) * HEAD_DIM], qt_ref[h])
                  for h in range(N_HEADS)]
        new_ms, new_ls, ps, alphas = [], [], [], []
        for h in range(N_HEADS):
            fk = fb_ref[h, pl.ds(start, t), :]
            s = scores[h] + fqs[h] - jnp.concatenate([fk] * (t // LANES), axis=1)
            if diag:
                s = jnp.where(kofs <= qofs, s, NEG_BIG)
            m_new = jnp.maximum(ms[h], jnp.max(s, axis=0, keepdims=True))
            p = jnp.exp(s - m_new)
            alpha = jnp.exp(ms[h] - m_new)
            new_ls.append(alpha * ls[h] + jnp.sum(p, axis=0, keepdims=True))
            ps.append(p.astype(BF16))
            alphas.append(alpha)
            new_ms.append(m_new)
        for h in range(N_HEADS):
            acc_ref[h] = alphas[h] * acc_ref[h] + _dot(vt_ref[j, h], ps[h])
        return tuple(new_ms), tuple(new_ls)

    init = (tuple(jnp.full((1, t), NEG_BIG, F32) for _ in range(N_HEADS)),
            tuple(jnp.zeros((1, t), F32) for _ in range(N_HEADS)))
    carry = lax.fori_loop(0, i, functools.partial(body, False), init)
    _, ls = body(True, i, carry)
    for h in range(N_HEADS):
        out_t = acc_ref[h] / ls[h]
        for c in range(sub):
            o_ref[c * CHUNK:(c + 1) * CHUNK, h * HEAD_DIM:(h + 1) * HEAD_DIM] = (
                out_t[:, c * CHUNK:(c + 1) * CHUNK].T.astype(o_ref.dtype))


def fox_attention(p, fcol, frow, qg, kg, batch, seq):
    nq = seq // FOX_T
    base = COL_FOX // MIX_W
    return pl.pallas_call(
        _fox_kernel,
        grid=(batch, nq),
        in_specs=[
            pl.BlockSpec((FOX_T, MIX_W), lambda b, i: (b * nq + i, base)),
            pl.BlockSpec((seq, MIX_W), lambda b, i: (b, base + 1)),
            pl.BlockSpec((seq, MIX_W), lambda b, i: (b, base + 2)),
            pl.BlockSpec((seq, SM_W), lambda b, i: (b, 0)),
            pl.BlockSpec((1, seq // CHUNK, 8, CHUNK), lambda b, i: (b, 0, 0, 0)),
            pl.BlockSpec((1, HEAD_DIM), lambda b, i: (0, 0)),
            pl.BlockSpec((1, HEAD_DIM), lambda b, i: (0, 0)),
        ],
        out_specs=pl.BlockSpec((FOX_T, MIX_W), lambda b, i: (b * nq + i, 0)),
        out_shape=jax.ShapeDtypeStruct((batch * seq, MIX_W), BF16),
        scratch_shapes=[
            pltpu.VMEM((seq, MIX_W), BF16),
            pltpu.VMEM((nq, N_HEADS, HEAD_DIM, FOX_T), BF16),
            pltpu.VMEM((N_HEADS, seq, LANES), F32),
            pltpu.VMEM((N_HEADS, HEAD_DIM, FOX_T), BF16),
            pltpu.VMEM((N_HEADS, HEAD_DIM, FOX_T), F32),
        ],
        compiler_params=_cparams(("parallel", "arbitrary")),
        name="fox_attention",
    )(p, p, p, fcol, frow, qg, kg)


DSA_TQ = 128
DSA_TK = 256
BAND_W = 2 * DSA_TQ


def _t5_bucket(dist):
    max_exact = N_BUCKETS // 2
    d = jnp.maximum(dist, 0)
    log_ratio = jnp.log(jnp.maximum(d, 1).astype(F32) / max_exact) / math.log(MAX_DISTANCE / max_exact)
    large = jnp.minimum(max_exact + (log_ratio * (N_BUCKETS - max_exact)).astype(jnp.int32), N_BUCKETS - 1)
    return jnp.where(d < max_exact, d, large)


def _dsa_kernel(cq_ref, k_ref, v_ref, smq_ref, smk_ref, cqg_ref, wuq_ref, wqi_ref, qg_ref, kg_ref, rb_ref,
                o_ref, kn_ref, ki_ref, band_ref, sc_ref, qh_ref, qi_ref, m_ref, l_ref, acc_ref, *, topk):
    b = pl.program_id(0)
    i = pl.program_id(1)
    tq, tk = DSA_TQ, DSA_TK
    nkc = sc_ref.shape[0]
    seq = nkc * tk
    tiles = tk // tq
    nb = ((i + 1) * tq + tk - 1) // tk

    @pl.when(jnp.logical_and(b == 0, i == 0))
    def _():
        r = lax.broadcasted_iota(jnp.int32, (tq, BAND_W), 0)
        c = lax.broadcasted_iota(jnp.int32, (tq, BAND_W), 1)
        bucket = _t5_bucket(tq + r - c)
        for h in range(N_HEADS):
            far = rb_ref[N_BUCKETS - 1, h]
            acc = jnp.zeros((tq, BAND_W), F32)
            for bk in range(N_BUCKETS - 1):
                acc = jnp.where(bucket == bk, rb_ref[bk, h] - far, acc)
            band_ref[h] = acc

    @pl.when(i == 0)
    def _():
        kn_ref[...] = _rms(k_ref[...].astype(F32), kg_ref[...]).astype(BF16)
        ki_ref[...] = smk_ref[:, SM_IK:SM_IK + IDX_DIM].astype(BF16)

    cq = _rms(cq_ref[...].astype(F32), cqg_ref[...]).astype(BF16)
    qf = _dot(cq, wuq_ref[...])
    for h in range(N_HEADS):
        sl = slice(h * HEAD_DIM, (h + 1) * HEAD_DIM)
        qh_ref[h] = (_rms(qf[:, sl], qg_ref[...]) * (HEAD_DIM ** -0.5)).astype(BF16)
    q_idx = (_dot(cq, wqi_ref[...]) * (IDX_DIM ** -0.5)).astype(BF16)
    for h in range(IDX_HEADS):
        qi_ref[h] = q_idx[:, h * IDX_DIM:(h + 1) * IDX_DIM]

    qpos = lax.broadcasted_iota(jnp.int32, (tq, tk), 0) + i * tq
    col = lax.broadcasted_iota(jnp.int32, (tq, tk), 1)

    def score_body(j, _):
        start = pl.multiple_of(j * tk, tk)
        kj = ki_ref[pl.ds(start, tk), :]
        w_h = smq_ref[:, SM_IW:SM_IW + IDX_HEADS] * (IDX_HEADS ** -0.5)
        acc = jnp.zeros((tq, tk), F32)
        for h in range(IDX_HEADS):
            acc = acc + w_h[:, h:h + 1] * jnp.maximum(_dot_nt(qi_ref[h], kj), 0.0)
        sc_ref[j] = jnp.where(col + start <= qpos, acc, -jnp.inf)
        return 0

    lax.fori_loop(0, nb, score_body, 0)

    def over_chunks(fn, init):
        acc = init
        for j in range(nkc):
            acc = lax.cond(j < nb, functools.partial(fn, j), lambda a: a, acc)
        return acc

    def lane_tiles(x):
        return [x[:, t * tq:(t + 1) * tq] for t in range(tiles)]

    def row_total(x):
        return jnp.sum(x, axis=1, keepdims=True)

    def search():
        kf = float(topk)

        def max_fn(j, a):
            for x in lane_tiles(sc_ref[j]):
                a = jnp.maximum(a, x)
            return a

        def min_fn(j, a):
            for x in lane_tiles(sc_ref[j]):
                a = jnp.minimum(a, jnp.where(x == -jnp.inf, jnp.inf, x))
            return a

        smax = jnp.max(over_chunks(max_fn, jnp.full((tq, tq), -jnp.inf, F32)), axis=1, keepdims=True)
        smin = jnp.min(over_chunks(min_fn, jnp.full((tq, tq), jnp.inf, F32)), axis=1, keepdims=True)

        def count_ge(t):
            tb = jnp.broadcast_to(t, (tq, tq))

            def fn(j, a):
                for x in lane_tiles(sc_ref[j]):
                    a = a + jnp.where(x >= tb, 1.0, 0.0)
                return a

            return row_total(over_chunks(fn, jnp.zeros((tq, tq), F32)))

        def midpoint(lo, hi):
            return jnp.where(hi == jnp.inf, smax, 0.5 * (lo + hi))

        def undecided(lo, hi, c_lo, mid):
            return jnp.logical_and(c_lo != kf, jnp.logical_and(mid > lo, mid < hi))

        def cond(carry):
            return jnp.logical_and(carry[0] < 400, carry[1] > 0.0)

        def body(carry):
            it, _, lo, hi, c_lo, c_hi, mid = carry
            upd = undecided(lo, hi, c_lo, mid)
            cnt = count_ge(mid)
            up = jnp.logical_and(upd, cnt >= kf)
            dn = jnp.logical_and(upd, cnt < kf)
            lo = jnp.where(up, mid, lo)
            c_lo = jnp.where(up, cnt, c_lo)
            hi = jnp.where(dn, mid, hi)
            c_hi = jnp.where(dn, cnt, c_hi)
            mid = midpoint(lo, hi)
            active = jnp.max(jnp.where(undecided(lo, hi, c_lo, mid), 1.0, 0.0))
            return it + 1, active, lo, hi, c_lo, c_hi, mid

        lo0 = smin
        hi0 = jnp.full((tq, 1), jnp.inf, F32)
        c_lo0 = count_ge(lo0)
        c_hi0 = jnp.zeros((tq, 1), F32)
        mid0 = midpoint(lo0, hi0)
        act0 = jnp.max(jnp.where(undecided(lo0, hi0, c_lo0, mid0), 1.0, 0.0))
        _, _, lo, hi, c_lo, c_hi, _ = lax.while_loop(
            cond, body, (jnp.int32(0), act0, lo0, hi0, c_lo0, c_hi0, mid0))

        def tie_search():
            need = kf - c_hi
            lo_b = jnp.broadcast_to(lo, (tq, tq))
            hi_b = jnp.broadcast_to(hi, (tq, tq))
            lane = lax.broadcasted_iota(jnp.int32, (tq, tq), 1)

            def tie_body(_, carry):
                jlo, jhi = carry
                jm = (jlo + jhi) // 2
                jm_b = jnp.broadcast_to(jm, (tq, tq))

                def fn(j, a):
                    for t, x in enumerate(lane_tiles(sc_ref[j])):
                        hit = jnp.logical_and(jnp.logical_and(x >= lo_b, x < hi_b), lane + (j * tk + t * tq) <= jm_b)
                        a = a + jnp.where(hit, 1.0, 0.0)
                    return a

                ok = row_total(over_chunks(fn, jnp.zeros((tq, tq), F32))) >= need
                return jnp.where(ok, jlo, jm), jnp.where(ok, jm, jhi)

            n_bits = int(math.ceil(math.log2(seq))) + 1
            _, jmax = lax.fori_loop(0, n_bits, tie_body,
                                    (jnp.full((tq, 1), -1, jnp.int32), jnp.full((tq, 1), seq - 1, jnp.int32)))
            return jmax

        any_tie = jnp.max(jnp.where(c_lo != kf, 1.0, 0.0)) > 0.0
        jmax = lax.cond(any_tie, tie_search, lambda: jnp.full((tq, 1), seq - 1, jnp.int32))
        return lo, hi, jmax

    def keep_all():
        return (jnp.full((tq, 1), -jnp.inf, F32), jnp.full((tq, 1), jnp.inf, F32),
                jnp.full((tq, 1), seq - 1, jnp.int32))

    lo, hi, jmax = lax.cond((i + 1) * tq > topk, search, keep_all)

    m_ref[...] = jnp.full(m_ref.shape, NEG_BIG, F32)
    l_ref[...] = jnp.zeros(l_ref.shape, F32)
    acc_ref[...] = jnp.zeros(acc_ref.shape, F32)

    def attend_body(j, _):
        start = pl.multiple_of(j * tk, tk)
        ks = kn_ref[pl.ds(start, tk), :]
        vs = v_ref[pl.ds(start, tk), :]
        sc = sc_ref[j]
        kpos = col + start
        keep = jnp.logical_or(sc >= hi, jnp.logical_and(sc >= lo, kpos <= jmax))
        keep = jnp.logical_and(keep, kpos <= qpos)
        for h in range(N_HEADS):
            bias = jnp.concatenate(
                [jnp.where(j * tiles + t == i, band_ref[h, :, tq:2 * tq],
                           jnp.where(j * tiles + t == i - 1, band_ref[h, :, 0:tq], 0.0))
                 for t in range(tiles)], axis=1)
            s = jnp.where(keep, _dot_nt(qh_ref[h], ks) + bias, NEG_BIG)
            m_old = m_ref[h]
            m_new = jnp.maximum(m_old, jnp.max(s, axis=1, keepdims=True))
            p = jnp.exp(s - m_new)
            alpha = jnp.exp(m_old - m_new)
            l_ref[h] = alpha * l_ref[h] + jnp.sum(p, axis=1, keepdims=True)
            acc_ref[h] = alpha * acc_ref[h] + _dot(p.astype(BF16), vs)
            m_ref[h] = m_new
        return 0

    lax.fori_loop(0, nb, attend_body, 0)
    for h in range(N_HEADS):
        o_ref[:, h * HEAD_DIM:(h + 1) * HEAD_DIM] = (acc_ref[h] / l_ref[h]).astype(o_ref.dtype)


SUBLANES = 8
COUNT_ROWS = 64
BISECT_FIXED_STEPS = 16


def _fold_rows(x, op):
    return op(x.reshape(x.shape[0] // SUBLANES, SUBLANES, x.shape[1]), axis=0)


def _dsa_t_kernel(cq_ref, k_ref, v_ref, smq_ref, smk_ref, cqg_ref, wuq_ref, wqi_ref, qg_ref, kg_ref, rb_ref,
                  o_ref, kn_ref, ki_ref, vt_ref, band_ref, sc_ref, qt_ref, xi_ref, acc_ref, *, topk):
    b = pl.program_id(0)
    i = pl.program_id(1)
    tq, tk = DSA_TQ, DSA_TK
    nkc = sc_ref.shape[0]
    seq = nkc * tk
    tiles = tk // tq
    nb = ((i + 1) * tq + tk - 1) // tk

    @pl.when(jnp.logical_and(b == 0, i == 0))
    def _():
        c = lax.broadcasted_iota(jnp.int32, (BAND_W, tq), 0)
        r = lax.broadcasted_iota(jnp.int32, (BAND_W, tq), 1)
        bucket = _t5_bucket(tq + r - c)
        for h in range(N_HEADS):
            far = rb_ref[N_BUCKETS - 1, h]
            acc = jnp.zeros((BAND_W, tq), F32)
            for bk in range(N_BUCKETS - 1):
                acc = jnp.where(bucket == bk, rb_ref[bk, h] - far, acc)
            band_ref[h] = acc

    @pl.when(i == 0)
    def _():
        kn_ref[...] = _rms(k_ref[...].astype(F32), kg_ref[...]).astype(BF16)
        ki_ref[...] = smk_ref[:, SM_IK:SM_IK + IDX_DIM].astype(BF16)
        for j in range(nkc):
            for t in range(tiles):
                rows = slice(j * tk + t * tq, j * tk + (t + 1) * tq)
                vt_ref[j, :, t * tq:(t + 1) * tq] = v_ref[rows, :].astype(F32).T.astype(BF16)

    cq_t = _rms(cq_ref[...].astype(F32), cqg_ref[...]).T.astype(BF16)
    q_t = _dot(wuq_ref[...], cq_t)
    g_col = jnp.broadcast_to(qg_ref[...], (HEAD_DIM, tq))
    for h in range(N_HEADS):
        x = q_t[h * HEAD_DIM:(h + 1) * HEAD_DIM, :]
        inv = lax.rsqrt(jnp.mean(x * x, axis=0, keepdims=True) + EPS)
        qt_ref[:, h * tq:(h + 1) * tq] = (x * inv * g_col * (HEAD_DIM ** -0.5)).astype(BF16)
    qi_t = (_dot(wqi_ref[...], cq_t) * (IDX_DIM ** -0.5)).astype(BF16)
    for h in range(IDX_HEADS):
        xi_ref[:, h * tq:(h + 1) * tq] = qi_t[h * IDX_DIM:(h + 1) * IDX_DIM, :]
    w_rows = smq_ref[...].T[SM_IW:SM_IW + IDX_HEADS, :] * (IDX_HEADS ** -0.5)

    kofs = lax.broadcasted_iota(jnp.int32, (tk, tq), 0)
    qpos = lax.broadcasted_iota(jnp.int32, (tk, tq), 1) + i * tq

    def score_body(j, _):
        start = pl.multiple_of(j * tk, tk)
        kj = ki_ref[pl.ds(start, tk), :]
        acc = jnp.zeros((tk, tq), F32)
        for h2 in range(IDX_HEADS // 2):
            r = _dot(kj, xi_ref[:, 2 * h2 * tq:(2 * h2 + 2) * tq])
            for h in (2 * h2, 2 * h2 + 1):
                acc = acc + w_rows[h:h + 1, :] * jnp.maximum(r[:, (h - 2 * h2) * tq:(h - 2 * h2 + 1) * tq], 0.0)
        sc_ref[j] = jnp.where(kofs + start <= qpos, acc, -jnp.inf)
        return 0

    lax.fori_loop(0, nb, score_body, 0)

    @pl.when(nb % 2 == 1)
    def _():
        sc_ref[jnp.minimum(nb, nkc - 1)] = jnp.full((tk, tq), -jnp.inf, F32)

    def over_chunks(fn, init):
        acc = init
        for j in range(nkc):
            acc = lax.cond(j < nb, functools.partial(fn, j), lambda a: a, acc)
        return acc

    def count_where(pred_fn):
        def walk(n_chunks):
            def run():
                a = jnp.zeros((COUNT_ROWS, tq), F32)
                for j in range(n_chunks):
                    hit = jnp.where(pred_fn(j, sc_ref[j]), 1.0, 0.0)
                    a = a + jnp.sum(hit.reshape(tk // COUNT_ROWS, COUNT_ROWS, tq), axis=0)
                return a
            return run

        extents = sorted({min(n, nkc) for n in range(2, nkc + 2, 2)})
        a = lax.switch((nb - 1) // 2, [walk(n) for n in extents])
        return jnp.sum(a, axis=0, keepdims=True)

    def search():
        kf = float(topk)
        smax = jnp.max(over_chunks(lambda j, a: jnp.maximum(a, _fold_rows(sc_ref[j], jnp.max)),
                                   jnp.full((SUBLANES, tq), -jnp.inf, F32)), axis=0, keepdims=True)
        smin = jnp.min(over_chunks(
            lambda j, a: jnp.minimum(a, _fold_rows(jnp.where(sc_ref[j] == -jnp.inf, jnp.inf, sc_ref[j]), jnp.min)),
            jnp.full((SUBLANES, tq), jnp.inf, F32)), axis=0, keepdims=True)

        def count_ge(t):
            return count_where(lambda j, x: x >= t)

        def midpoint(lo, hi):
            return jnp.where(hi == jnp.inf, smax, 0.5 * (lo + hi))

        def undecided(lo, hi, c_lo, mid):
            return jnp.logical_and(c_lo != kf, jnp.logical_and(mid > lo, mid < hi))

        def step(state):
            lo, hi, c_lo, c_hi, mid = state
            upd = undecided(lo, hi, c_lo, mid)
            cnt = count_ge(mid)
            up = jnp.logical_and(upd, cnt >= kf)
            dn = jnp.logical_and(upd, cnt < kf)
            lo = jnp.where(up, mid, lo)
            c_lo = jnp.where(up, cnt, c_lo)
            hi = jnp.where(dn, mid, hi)
            c_hi = jnp.where(dn, cnt, c_hi)
            return lo, hi, c_lo, c_hi, midpoint(lo, hi)

        def any_undecided(state):
            lo, hi, c_lo, _, mid = state
            return jnp.max(jnp.where(undecided(lo, hi, c_lo, mid), 1.0, 0.0))

        def cond(carry):
            return jnp.logical_and(carry[0] < 200, carry[1] > 0.0)

        def body(carry):
            state = step(step(carry[2]))
            return carry[0] + 1, any_undecided(state), state

        lo0 = smin
        hi0 = jnp.full((1, tq), jnp.inf, F32)
        c_lo0 = (lax.broadcasted_iota(jnp.int32, (1, tq), 1) + (i * tq + 1)).astype(F32)
        c_hi0 = jnp.zeros((1, tq), F32)
        state = (lo0, hi0, c_lo0, c_hi0, midpoint(lo0, hi0))
        state = lax.fori_loop(0, BISECT_FIXED_STEPS, lambda _, s: step(s), state)
        _, _, (lo, hi, c_lo, c_hi, _) = lax.while_loop(
            cond, body, (jnp.int32(0), any_undecided(state), state))

        def tie_search():
            need = kf - c_hi

            def tie_body(_, carry):
                jlo, jhi = carry
                jm = (jlo + jhi) // 2
                cnt = count_where(lambda j, x: jnp.logical_and(jnp.logical_and(x >= lo, x < hi),
                                                               kofs + j * tk <= jm))
                ok = cnt >= need
                return jnp.where(ok, jlo, jm), jnp.where(ok, jm, jhi)

            n_bits = int(math.ceil(math.log2(seq))) + 1
            _, jmax = lax.fori_loop(0, n_bits, tie_body,
                                    (jnp.full((1, tq), -1, jnp.int32), jnp.full((1, tq), seq - 1, jnp.int32)))
            return jmax

        any_tie = jnp.max(jnp.where(c_lo != kf, 1.0, 0.0)) > 0.0
        jmax = lax.cond(any_tie, tie_search, lambda: jnp.full((1, tq), seq - 1, jnp.int32))
        return lo, hi, jmax

    def keep_all():
        return (jnp.full((1, tq), -jnp.inf, F32), jnp.full((1, tq), jnp.inf, F32),
                jnp.full((1, tq), seq - 1, jnp.int32))

    lo, hi, jmax = lax.cond((i + 1) * tq > topk, search, keep_all)

    acc_ref[...] = jnp.zeros(acc_ref.shape, F32)

    def attend_body(near, j, carry):
        ms, ls = carry
        start = pl.multiple_of(j * tk, tk)
        kc = kn_ref[pl.ds(start, tk), :]
        vt = vt_ref[j]
        sc = sc_ref[j]
        kpos = kofs + start
        keep = jnp.logical_or(sc >= hi, jnp.logical_and(sc >= lo, kpos <= jmax))
        if near:
            keep = jnp.logical_and(keep, kpos <= qpos)
        s_all = _dot(kc, qt_ref[...])
        new_ms, new_ls, ps, alphas = [], [], [], []
        for h in range(N_HEADS):
            s = s_all[:, h * tq:(h + 1) * tq]
            if near:
                s = s + jnp.concatenate(
                    [jnp.where(j * tiles + t == i, band_ref[h, tq:2 * tq, :],
                               jnp.where(j * tiles + t == i - 1, band_ref[h, 0:tq, :], 0.0))
                     for t in range(tiles)], axis=0)
            s = jnp.where(keep, s, NEG_BIG)
            m_new = jnp.maximum(ms[h], jnp.max(s, axis=0, keepdims=True))
            p = jnp.exp(s - m_new)
            alphas.append(jnp.exp(ms[h] - m_new))
            new_ls.append(alphas[h] * ls[h] + jnp.sum(p, axis=0, keepdims=True))
            ps.append(p.astype(BF16))
            new_ms.append(m_new)
        acc_ref[...] = (jnp.concatenate(alphas, axis=1) * acc_ref[...]
                        + _dot(vt, jnp.concatenate(ps, axis=1)))
        return tuple(new_ms), tuple(new_ls)

    init = (tuple(jnp.full((1, tq), NEG_BIG, F32) for _ in range(N_HEADS)),
            tuple(jnp.zeros((1, tq), F32) for _ in range(N_HEADS)))
    n_far = jnp.maximum((i - 1) * tq // tk, 0)
    carry = lax.fori_loop(0, n_far, functools.partial(attend_body, False), init)
    _, ls = lax.fori_loop(n_far, nb, functools.partial(attend_body, True), carry)
    for h in range(N_HEADS):
        out_t = acc_ref[:, h * tq:(h + 1) * tq] / ls[h]
        o_ref[:, h * HEAD_DIM:(h + 1) * HEAD_DIM] = out_t.T.astype(o_ref.dtype)


def dsa_attention(p, sm, cqg, wuq_t, wqi_t, qg_col, kg, rel_bias, batch, seq):
    nq = seq // DSA_TQ
    nkc = seq // DSA_TK
    topk = min(DSA_TOPK, seq // 4)
    kern = functools.partial(_dsa_t_kernel, topk=topk)
    return pl.pallas_call(
        kern,
        grid=(batch, nq),
        in_specs=[
            pl.BlockSpec((DSA_TQ, DSA_Q_RANK), lambda b, i: (b * nq + i, COL_CQ // DSA_Q_RANK)),
            pl.BlockSpec((seq, HEAD_DIM), lambda b, i: (b, COL_DK // HEAD_DIM)),
            pl.BlockSpec((seq, HEAD_DIM), lambda b, i: (b, COL_DV // HEAD_DIM)),
            pl.BlockSpec((DSA_TQ, SM_W), lambda b, i: (b * nq + i, 0)),
            pl.BlockSpec((seq, SM_W), lambda b, i: (b, 0)),
            pl.BlockSpec((1, DSA_Q_RANK), lambda b, i: (0, 0)),
            pl.BlockSpec((N_HEADS * HEAD_DIM, DSA_Q_RANK), lambda b, i: (0, 0)),
            pl.BlockSpec((IDX_HEADS * IDX_DIM, DSA_Q_RANK), lambda b, i: (0, 0)),
            pl.BlockSpec((HEAD_DIM, 1), lambda b, i: (0, 0)),
            pl.BlockSpec((1, HEAD_DIM), lambda b, i: (0, 0)),
            pl.BlockSpec(memory_space=pltpu.SMEM),
        ],
        out_specs=pl.BlockSpec((DSA_TQ, MIX_W), lambda b, i: (b * nq + i, 0)),
        out_shape=jax.ShapeDtypeStruct((batch * seq, MIX_W), BF16),
        scratch_shapes=[
            pltpu.VMEM((seq, HEAD_DIM), BF16),
            pltpu.VMEM((seq, IDX_DIM), BF16),
            pltpu.VMEM((nkc, HEAD_DIM, DSA_TK), BF16),
            pltpu.VMEM((N_HEADS, BAND_W, DSA_TQ), F32),
            pltpu.VMEM((nkc, DSA_TK, DSA_TQ), F32),
            pltpu.VMEM((HEAD_DIM, N_HEADS * DSA_TQ), BF16),
            pltpu.VMEM((IDX_DIM, IDX_HEADS * DSA_TQ), BF16),
            pltpu.VMEM((HEAD_DIM, N_HEADS * DSA_TQ), F32),
        ],
        compiler_params=_cparams(("arbitrary", "arbitrary")),
        name="dsa_attention",
    )(p, p, p, sm, sm, cqg, wuq_t, wqi_t, qg_col, kg, rel_bias)


CONV_PAD = 8


def _causal_conv(x_ref, xp_ref, first, w_ref, b_ref, ext_ref):
    ext_ref[0:CONV_PAD, :] = xp_ref[CHUNK - CONV_PAD:CHUNK, :].astype(F32) * first
    ext_ref[CONV_PAD:CONV_PAD + CHUNK, :] = x_ref[...].astype(F32)
    acc = b_ref[...] + ext_ref[CONV_PAD:CONV_PAD + CHUNK, :] * w_ref[SSD_CONV - 1:SSD_CONV, :]
    for d in range(1, SSD_CONV):
        acc = acc + ext_ref[CONV_PAD - d:CONV_PAD - d + CHUNK, :] * w_ref[SSD_CONV - 1 - d:SSD_CONV - d, :]
    return _silu(acc)


def _ssd_kernel(z_ref, xs_ref, bc_ref, xsp_ref, bcp_ref, sm_ref, cwx_ref, cbx_ref, cwb_ref, cbb_ref,
                dtb_ref, alog_ref, dsk_ref, ng_ref, o_ref, prev_ref, y_ref, extx_ref, extb_ref):
    c = CHUNK
    n = pl.program_id(1)

    @pl.when(n == 0)
    def _():
        prev_ref[...] = jnp.zeros_like(prev_ref)

    first = (n > 0).astype(F32)
    xs = _causal_conv(xs_ref, xsp_ref, first, cwx_ref, cbx_ref, extx_ref)
    bc = _causal_conv(bc_ref, bcp_ref, first, cwb_ref, cbb_ref, extb_ref)

    dt_t = _softplus(sm_ref[...].T + dtb_ref[...])
    cs_t = _cumsum_lanes(dt_t * (-jnp.exp(alog_ref[...])))
    cs = cs_t.T
    ii = lax.broadcasted_iota(jnp.int32, (c, c), 0)
    jj = lax.broadcasted_iota(jnp.int32, (c, c), 1)
    tril = ii >= jj
    pair_w = 2 * SSD_HEAD_DIM
    first_head = jj < SSD_HEAD_DIM
    first_head_row = lax.broadcasted_iota(jnp.int32, (1, pair_w), 1) < SSD_HEAD_DIM
    gn = SSD_GROUPS * SSD_STATE
    hpg = SSD_HEADS // SSD_GROUPS
    for g in range(SSD_GROUPS):
        bg = bc[:, g * SSD_STATE:(g + 1) * SSD_STATE]
        cg = bc[:, gn + g * SSD_STATE:gn + (g + 1) * SSD_STATE].astype(BF16)
        cb = _dot_nt(cg, bg.astype(BF16))
        bg_t = bg.T
        y_off = _dot(cg, prev_ref[g].astype(BF16))
        for pr in range(hpg // 2):
            cols = slice((g * hpg + 2 * pr) * SSD_HEAD_DIM, (g * hpg + 2 * pr + 2) * SSD_HEAD_DIM)
            rcols = slice(2 * pr * SSD_HEAD_DIM, (2 * pr + 2) * SSD_HEAD_DIM)
            x_pair = xs[:, cols]
            x_bf = x_pair.astype(BF16)
            y_diag, st, exp_a, exp_last = [], [], [], []
            for k in range(2):
                row = SM_DT + g * hpg + 2 * pr + k
                a_row = cs_t[row:row + 1, :]
                dt_row = dt_t[row:row + 1, :]
                last = cs_t[row:row + 1, c - 1:c]
                a_col = jnp.broadcast_to(cs[:, row:row + 1], (c, c))
                seg = jnp.where(tril, jnp.exp(jnp.where(tril, a_col - a_row, 0.0)), 0.0)
                y_diag.append(_dot((cb * seg * dt_row).astype(BF16), x_bf))
                st.append(_dot((bg_t * (dt_row * jnp.exp(last - a_row))).astype(BF16), x_bf))
                exp_a.append(jnp.exp(a_col))
                exp_last.append(jnp.exp(last))
            y_ref[:, cols] = (jnp.where(first_head, y_diag[0], y_diag[1])
                              + y_off[:, rcols] * jnp.where(first_head, exp_a[0], exp_a[1])
                              + dsk_ref[:, cols] * x_pair)
            prev_ref[g, :, rcols] = (jnp.where(first_head_row, exp_last[0], exp_last[1]) * prev_ref[g, :, rcols]
                                     + jnp.where(first_head, st[0], st[1]))
    gated = y_ref[...] * _silu(z_ref[...].astype(F32))
    gw = SSD_INNER // SSD_GROUPS
    for g in range(SSD_GROUPS):
        sl = slice(g * gw, (g + 1) * gw)
        o_ref[:, sl] = _rms(gated[:, sl], ng_ref[:, sl]).astype(o_ref.dtype)


def ssd_mixer(p, sm, cw, cb, dtb_col, alog_col, dskip_row, ng, batch, seq):
    n = seq // CHUNK
    bcw = 2 * SSD_GROUPS * SSD_STATE

    def cur(width, colbase):
        return pl.BlockSpec((CHUNK, width), lambda b, i: (b * n + i, colbase // width))

    def prv(width, colbase):
        return pl.BlockSpec((CHUNK, width), lambda b, i: (b * n + jnp.maximum(i - 1, 0), colbase // width))

    def const(shape):
        return pl.BlockSpec(shape, lambda b, i: (0, 0))

    return pl.pallas_call(
        _ssd_kernel,
        grid=(batch, n),
        in_specs=[
            cur(SSD_INNER, COL_Z), cur(SSD_INNER, COL_XS), cur(bcw, COL_BC),
            prv(SSD_INNER, COL_XS), prv(bcw, COL_BC),
            pl.BlockSpec((CHUNK, SM_W), lambda b, i: (b * n + i, 0)),
            const((SSD_CONV, SSD_INNER)), const((1, SSD_INNER)),
            const((SSD_CONV, bcw)), const((1, bcw)),
            const((SM_W, 1)), const((SM_W, 1)),
            const((1, SSD_INNER)), const((1, SSD_INNER)),
        ],
        out_specs=pl.BlockSpec((CHUNK, SSD_INNER), lambda b, i: (b * n + i, 0)),
        out_shape=jax.ShapeDtypeStruct((batch * seq, SSD_INNER), BF16),
        scratch_shapes=[
            pltpu.VMEM((SSD_GROUPS, SSD_STATE, SSD_INNER // SSD_GROUPS), F32),
            pltpu.VMEM((CHUNK, SSD_INNER), F32),
            pltpu.VMEM((CONV_PAD + CHUNK, SSD_INNER), F32),
            pltpu.VMEM((CONV_PAD + CHUNK, bcw), F32),
        ],
        compiler_params=_cparams(("parallel", "arbitrary")),
        name="ssd_mixer",
    )(p, p, p, p, p, sm, cw[:, :SSD_INNER], cb[:, :SSD_INNER], cw[:, SSD_INNER:], cb[:, SSD_INNER:],
      dtb_col, alog_col, dskip_row, ng)


MERGE_TM = 256


def _merge_kernel(x_ref, gl_ref, gb_ref, oret_ref, ofox_ref, odsa_ref, ossd_ref, wbr_ref, wout_ref, o_ref):
    branches = (oret_ref, ofox_ref, odsa_ref, ossd_ref)
    merged = None
    row0 = 0
    for bi, br in enumerate(branches):
        width = br.shape[1]
        sl = slice(bi * D_MODEL, (bi + 1) * D_MODEL)
        gate = 1.0 / (1.0 + jnp.exp(-(gl_ref[:, sl].astype(F32) + gb_ref[:, sl])))
        term = gate * _dot(br[...], wbr_ref[row0:row0 + width, :])
        merged = term if merged is None else merged + term
        row0 += width
    o_ref[...] = x_ref[...] + _dot(merged.astype(BF16), wout_ref[...])


def merge_project(x, p, gate_b, o_ret, o_fox, o_dsa, o_ssd, w_br, w_out, layer):
    m = x.shape[0]
    tm = MERGE_TM

    def rows(width):
        return pl.BlockSpec((tm, width), lambda i: (i, 0))

    def const(shape):
        return pl.BlockSpec(shape, lambda i: (0, 0), pipeline_mode=pl.Buffered(1))

    def stacked(w):
        return pl.BlockSpec((None,) + w.shape[1:], lambda i: (layer, 0, 0), pipeline_mode=pl.Buffered(1))

    return pl.pallas_call(
        _merge_kernel,
        grid=(m // tm,),
        in_specs=[
            rows(D_MODEL), rows(N_BRANCH * D_MODEL), const((1, N_BRANCH * D_MODEL)),
            rows(MIX_W), rows(MIX_W), rows(MIX_W), rows(SSD_INNER),
            stacked(w_br), stacked(w_out),
        ],
        out_specs=rows(D_MODEL),
        out_shape=jax.ShapeDtypeStruct(x.shape, x.dtype),
        compiler_params=_cparams(("parallel",)),
        name="merge_project",
    )(x, p, gate_b, o_ret, o_fox, o_dsa, o_ssd, w_br, w_out)


FFN_TM = 1024
FFN_TF = 512


def _ffn_kernel(x_ref, g_ref, w1_ref, w2_ref, o_ref, h_ref):
    @pl.when(pl.program_id(1) == 0)
    def _():
        h_ref[...] = _rms(x_ref[...], g_ref[...]).astype(BF16)
        o_ref[...] = x_ref[...]

    a = jnp.maximum(_dot(h_ref[...], w1_ref[...]), 0.0)
    o_ref[...] += _dot((a * a).astype(BF16), w2_ref[...])


def ffn(x, g, w1, w2, layer):
    m, d = x.shape
    dff = w1.shape[2]
    tm, tf = min(FFN_TM, m), FFN_TF
    return pl.pallas_call(
        _ffn_kernel,
        grid=(m // tm, dff // tf),
        in_specs=[
            pl.BlockSpec((tm, d), lambda i, f: (i, 0), pipeline_mode=pl.Buffered(1)),
            pl.BlockSpec((1, d), lambda i, f: (0, 0)),
            pl.BlockSpec((None, d, tf), lambda i, f: (layer, 0, f)),
            pl.BlockSpec((None, tf, d), lambda i, f: (layer, f, 0)),
        ],
        out_specs=pl.BlockSpec((tm, d), lambda i, f: (i, 0)),
        out_shape=jax.ShapeDtypeStruct(x.shape, x.dtype),
        scratch_shapes=[pltpu.VMEM((tm, d), BF16)],
        compiler_params=_cparams(("parallel", "arbitrary")),
        name="ffn",
    )(x, g, w1, w2)


SRC_RET = 0
SRC_FOX = SRC_RET + 4 * MIX_W
SRC_FF = SRC_FOX + 3 * MIX_W
SRC_CQ = SRC_FF + N_HEADS
SRC_DK = SRC_CQ + DSA_Q_RANK
SRC_IK = SRC_DK + 2 * HEAD_DIM
SRC_IW = SRC_IK + IDX_DIM
SRC_Z = SRC_IW + IDX_HEADS
SRC_DT = SRC_Z + 2 * SSD_INNER + 2 * SSD_GROUPS * SSD_STATE
SRC_GATE = SRC_DT + SSD_HEADS
IN_TOTAL = SRC_GATE + N_BRANCH * D_MODEL
MAIN_RUNS = ((COL_GATE, SRC_GATE), (COL_RET, SRC_RET), (COL_Z, SRC_Z), (COL_CQ, SRC_CQ),
             (COL_FOX, SRC_FOX), (COL_DK, SRC_DK))
RELAYOUT_W = 512
RELAYOUT_TILES = RELAYOUT_W // LANES


def _relayout_tables():
    starts, shifts = [], []
    for blk in range(N_MAIN // RELAYOUT_W):
        o = blk * RELAYOUT_W
        dst, src = [r for r in MAIN_RUNS if r[0] <= o][-1]
        col = src + (o - dst)
        starts.append(col // LANES)
        shifts.append(col % LANES)
    return jnp.asarray(starts, jnp.int32), jnp.asarray(shifts, jnp.int32)


def _relayout_kernel(start_ref, shift_ref, *refs):
    del start_ref
    tiles, o_ref = refs[:-1], refs[-1]
    shift = shift_ref[pl.program_id(1)]
    amount = lax.rem(LANES - shift, LANES)
    lane = lax.broadcasted_iota(jnp.int32, tiles[0].shape, 1)
    rolled = [pltpu.roll(t[...], amount, 1) for t in tiles]
    for k in range(RELAYOUT_TILES):
        piece = jnp.where(lane < LANES - shift, rolled[k], rolled[k + 1])
        o_ref[:, k * LANES:(k + 1) * LANES] = piece.astype(o_ref.dtype)


def relayout_main(w_in):
    depth, d, n_src = w_in.shape
    last = (n_src - 1) // LANES
    starts, shifts = _relayout_tables()

    def tile(k):
        return pl.BlockSpec((None, d, LANES), lambda l, b, st, sh: (l, 0, jnp.minimum(st[b] + k, last)))

    return pl.pallas_call(
        _relayout_kernel,
        grid_spec=pltpu.PrefetchScalarGridSpec(
            num_scalar_prefetch=2,
            grid=(depth, N_MAIN // RELAYOUT_W),
            in_specs=[tile(k) for k in range(RELAYOUT_TILES + 1)],
            out_specs=pl.BlockSpec((None, d, RELAYOUT_W), lambda l, b, st, sh: (l, 0, b)),
        ),
        out_shape=jax.ShapeDtypeStruct((depth, d, N_MAIN), BF16),
        compiler_params=_cparams(("parallel", "arbitrary")),
        name="relayout_main",
    )(starts, shifts, *([w_in] * (RELAYOUT_TILES + 1)))


SMALL_PIECES = ((SRC_DT, SM_DT, SSD_HEADS), (SRC_FF, SM_F, N_HEADS), (SRC_IW, SM_IW, IDX_HEADS),
                (SRC_IK, SM_IK, IDX_DIM))


def _relayout_small_kernel(*refs):
    tiles, o_ref = refs[:-1], refs[-1]
    lane = lax.broadcasted_iota(jnp.int32, o_ref.shape, 1)
    out = jnp.zeros(o_ref.shape, F32)
    for t, (src, dst, width) in zip(tiles, SMALL_PIECES):
        moved = pltpu.roll(t[...], (dst - src % LANES) % LANES, 1)
        out = jnp.where(jnp.logical_and(lane >= dst, lane < dst + width), moved, out)
    o_ref[...] = out.astype(o_ref.dtype)


def relayout_small(w_in):
    depth, d, _ = w_in.shape
    for src, _, width in SMALL_PIECES:
        assert src // LANES == (src + width - 1) // LANES

    def tile(src):
        return pl.BlockSpec((None, d, LANES), lambda l: (l, 0, src // LANES))

    return pl.pallas_call(
        _relayout_small_kernel,
        grid=(depth,),
        in_specs=[tile(src) for src, _, _ in SMALL_PIECES],
        out_specs=pl.BlockSpec((None, d, SM_W), lambda l: (l, 0, 0)),
        out_shape=jax.ShapeDtypeStruct((depth, d, SM_W), BF16),
        compiler_params=_cparams(("parallel",)),
        name="relayout_small",
    )(*([w_in] * len(SMALL_PIECES)))


def _pad_to(v, offset, total):
    return jnp.zeros((total,), v.dtype).at[offset:offset + v.shape[0]].set(v)


def _rotary_tables(seq):
    half = HEAD_DIM // 2
    inv = 1.0 / (10000.0 ** (jnp.arange(half, dtype=F32) / half))
    ang = jnp.arange(seq, dtype=F32)[:, None] * inv[None, :]
    cos, sin = jnp.cos(ang), jnp.sin(ang)
    return jnp.concatenate([cos, cos], axis=1), jnp.concatenate([-sin, sin], axis=1)


def kernel(x, norm1_g, w_in, gate_b, fox_f_b, fox_qn_g, fox_kn_g, dsa_cq_g, dsa_w_uq, dsa_w_qidx, dsa_qn_g,
           dsa_kn_g, rel_bias, ssd_conv_w, ssd_conv_b, ssd_dt_bias, ssd_a_log, ssd_d, ssd_norm_g, w_br, w_out,
           norm2_g, w_ff1, w_ff2):
    batch, seq, d = x.shape
    tokens = batch * seq
    xt = x.reshape(tokens, d)
    cos, sin = _rotary_tables(seq)
    tm = min(1024, tokens)
    w_main = relayout_main(w_in)
    w_small = relayout_small(w_in)
    w_br_bf, w_out_bf = w_br.astype(BF16), w_out.astype(BF16)
    w_ff1_bf, w_ff2_bf = w_ff1.astype(BF16), w_ff2.astype(BF16)
    for l in range(DEPTH):
        g1 = norm1_g[l][None, :]
        p = norm_matmul(xt, g1, w_main, l, BF16, tm, 1024)
        sm = norm_matmul(xt, g1, w_small, l, F32, tm, SM_W)

        o_ret = retention(p, cos, sin, batch, seq)

        fb_row = _pad_to(fox_f_b[l], SM_F, SM_W)[None, :]
        fcol, frow = fox_prep(sm, fb_row, batch, seq)
        o_fox = fox_attention(p, fcol, frow, fox_qn_g[l][None, :], fox_kn_g[l][None, :], batch, seq)

        o_dsa = dsa_attention(p, sm, dsa_cq_g[l][None, :], dsa_w_uq[l].T.astype(BF16), dsa_w_qidx[l].T.astype(BF16),
                              dsa_qn_g[l][:, None], dsa_kn_g[l][None, :], rel_bias, batch, seq)

        o_ssd = ssd_mixer(p, sm, ssd_conv_w[l], ssd_conv_b[l][None, :],
                          _pad_to(ssd_dt_bias[l], SM_DT, SM_W)[:, None], _pad_to(ssd_a_log[l], SM_DT, SM_W)[:, None],
                          jnp.repeat(ssd_d[l], SSD_HEAD_DIM)[None, :], ssd_norm_g[l][None, :], batch, seq)

        xt = merge_project(xt, p, gate_b[l][None, :], o_ret, o_fox, o_dsa, o_ssd, w_br_bf, w_out_bf, l)
        xt = ffn(xt, norm2_g[l][None, :], w_ff1_bf, w_ff2_bf, l)
    return xt.reshape(batch, seq, d)
```

```python
import functools
import math

import jax
import jax.numpy as jnp
from jax import lax
from jax.experimental import pallas as pl
from jax.experimental.pallas import tpu as pltpu

F32 = jnp.float32
BF16 = jnp.bfloat16

D_MODEL = 2048
DEPTH = 4
HEAD_DIM = 128
N_HEADS = 4
DSA_Q_RANK = 512
IDX_HEADS = 16
IDX_DIM = 64
DSA_TOPK = 256
SSD_HEADS = 16
SSD_HEAD_DIM = 64
SSD_GROUPS = 2
SSD_STATE = 128
SSD_CONV = 4
SSD_INNER = SSD_HEADS * SSD_HEAD_DIM
D_FF = 4 * D_MODEL
N_BUCKETS = 32
MAX_DISTANCE = 128
CHUNK = 128
EPS = 1e-6
N_BRANCH = 4
MIX_W = N_HEADS * HEAD_DIM

COL_GATE = 0
COL_RET = COL_GATE + N_BRANCH * D_MODEL
COL_Z = COL_RET + 4 * MIX_W
COL_XS = COL_Z + SSD_INNER
COL_BC = COL_XS + SSD_INNER
COL_CQ = COL_BC + 2 * SSD_GROUPS * SSD_STATE
COL_FOX = COL_CQ + DSA_Q_RANK
COL_DK = COL_FOX + 3 * MIX_W
COL_DV = COL_DK + HEAD_DIM
N_MAIN_USED = COL_DV + HEAD_DIM
N_MAIN = 15360
SM_DT = 0
SM_F = 16
SM_IW = 32
SM_IK = 64
SM_W = 128

LANES = 128
VMEM_LIMIT = 56 * 1024 * 1024
NEG_BIG = -1e30


def _cparams(sem):
    return pltpu.CompilerParams(dimension_semantics=sem, vmem_limit_bytes=VMEM_LIMIT)


def _dot(a, b):
    return jnp.dot(a, b, preferred_element_type=F32)


def _dot_nt(a, b):
    return lax.dot_general(a, b, (((1,), (1,)), ((), ())), preferred_element_type=F32)


def _dot_tn(a, b):
    return lax.dot_general(a, b, (((0,), (0,)), ((), ())), preferred_element_type=F32)


def _rms(x, g):
    return x * lax.rsqrt(jnp.mean(x * x, axis=-1, keepdims=True) + EPS) * g


def _silu(x):
    return x / (1.0 + jnp.exp(-x))


def _softplus(x):
    return jnp.maximum(x, 0.0) + jnp.log1p(jnp.exp(-jnp.abs(x)))


def _cumsum_lanes(x):
    lane = lax.broadcasted_iota(jnp.int32, x.shape, 1)
    d = 1
    while d < x.shape[1]:
        x = x + jnp.where(lane >= d, pltpu.roll(x, d, 1), 0.0)
        d *= 2
    return x


def _norm_matmul_kernel(x_ref, g_ref, w_ref, o_ref, h_ref):
    @pl.when(pl.program_id(1) == 0)
    def _():
        h_ref[...] = _rms(x_ref[...], g_ref[...]).astype(BF16)

    o_ref[...] = _dot(h_ref[...], w_ref[...]).astype(o_ref.dtype)


def norm_matmul(x, g, w, layer, out_dtype, tm, tn):
    m, d = x.shape
    n = w.shape[2]
    return pl.pallas_call(
        _norm_matmul_kernel,
        grid=(m // tm, n // tn),
        in_specs=[
            pl.BlockSpec((tm, d), lambda i, j: (i, 0)),
            pl.BlockSpec((1, d), lambda i, j: (0, 0)),
            pl.BlockSpec((None, d, tn), lambda i, j: (layer, 0, j)),
        ],
        out_specs=pl.BlockSpec((tm, tn), lambda i, j: (i, j)),
        out_shape=jax.ShapeDtypeStruct((m, n), out_dtype),
        scratch_shapes=[pltpu.VMEM((tm, d), BF16)],
        compiler_params=_cparams(("parallel", "arbitrary")),
        name="norm_matmul",
    )(x, g, w)


def _retention_kernel(q_ref, k_ref, v_ref, g_ref, cos_ref, sin_ref, o_ref, state_ref):
    c = CHUNK

    @pl.when(pl.program_id(1) == 0)
    def _():
        state_ref[...] = jnp.zeros_like(state_ref)

    cos = cos_ref[...]
    sin = sin_ref[...]
    ii = lax.broadcasted_iota(jnp.int32, (c, c), 0)
    jj = lax.broadcasted_iota(jnp.int32, (c, c), 1)
    rel = (ii - jj).astype(F32)
    i_col = lax.broadcasted_iota(jnp.int32, (c, 1), 0).astype(F32)
    for h in range(N_HEADS):
        lg = math.log1p(-(2.0 ** (-5.0 - h)))
        sl = slice(h * HEAD_DIM, (h + 1) * HEAD_DIM)
        q = q_ref[:, sl].astype(F32)
        k = k_ref[:, sl].astype(F32)
        v = v_ref[:, sl]
        qr = q * cos + pltpu.roll(q, HEAD_DIM // 2, 1) * sin
        kr = (k * cos + pltpu.roll(k, HEAD_DIM // 2, 1) * sin) * (HEAD_DIM ** -0.5)
        decay = jnp.where(rel >= 0, jnp.exp(lg * jnp.maximum(rel, 0.0)), 0.0)
        scores = _dot_nt(qr.astype(BF16), kr.astype(BF16)) * decay
        y = _dot(scores.astype(BF16), v)
        q_dec = jnp.exp(lg * (i_col + 1.0))
        k_dec = jnp.exp(lg * (c - 1.0 - i_col))
        st = state_ref[h]
        y = y + _dot((qr * q_dec).astype(BF16), st.astype(BF16))
        kv = _dot_tn((kr * k_dec).astype(BF16), v)
        state_ref[h] = math.exp(lg * c) * st + kv
        yc = y - jnp.mean(y, axis=-1, keepdims=True)
        yn = yc * lax.rsqrt(jnp.mean(yc * yc, axis=-1, keepdims=True) + EPS)
        o_ref[:, sl] = (_silu(g_ref[:, sl].astype(F32)) * yn).astype(o_ref.dtype)


def retention(p, cos, sin, batch, seq):
    n = seq // CHUNK
    base = COL_RET // MIX_W

    def col(j):
        return pl.BlockSpec((CHUNK, MIX_W), lambda b, i: (b * n + i, base + j))

    tab = pl.BlockSpec((CHUNK, HEAD_DIM), lambda b, i: (i, 0))
    return pl.pallas_call(
        _retention_kernel,
        grid=(batch, n),
        in_specs=[col(0), col(1), col(2), col(3), tab, tab],
        out_specs=pl.BlockSpec((CHUNK, MIX_W), lambda b, i: (b * n + i, 0)),
        out_shape=jax.ShapeDtypeStruct((batch * seq, MIX_W), BF16),
        scratch_shapes=[pltpu.VMEM((N_HEADS, HEAD_DIM, HEAD_DIM), F32)],
        compiler_params=_cparams(("parallel", "arbitrary")),
        name="retention",
    )(p, p, p, p, cos, sin)


def _fox_prep_kernel(sm_ref, fb_ref, fcol_ref, frow_ref, carry_ref):
    @pl.when(pl.program_id(1) == 0)
    def _():
        carry_ref[...] = jnp.zeros_like(carry_ref)

    t = sm_ref[...] + fb_ref[...]
    lf = jnp.minimum(t, 0.0) - jnp.log1p(jnp.exp(-jnp.abs(t)))
    cs = _cumsum_lanes(lf.T) + carry_ref[...]
    carry_ref[...] = cs[:, LANES - 1:LANES]
    frow_ref[0, 0] = cs[SM_F:SM_F + 8, :]
    fcol_ref[...] = cs.T


def fox_prep(sm, fb_row, batch, seq):
    n = seq // CHUNK
    return pl.pallas_call(
        _fox_prep_kernel,
        grid=(batch, n),
        in_specs=[
            pl.BlockSpec((CHUNK, SM_W), lambda b, i: (b * n + i, 0)),
            pl.BlockSpec((1, SM_W), lambda b, i: (0, 0)),
        ],
        out_specs=[
            pl.BlockSpec((CHUNK, SM_W), lambda b, i: (b * n + i, 0)),
            pl.BlockSpec((1, 1, 8, CHUNK), lambda b, i: (b, i, 0, 0)),
        ],
        out_shape=[
            jax.ShapeDtypeStruct((batch * seq, SM_W), F32),
            jax.ShapeDtypeStruct((batch, n, 8, CHUNK), F32),
        ],
        scratch_shapes=[pltpu.VMEM((SM_W, 1), F32)],
        compiler_params=_cparams(("parallel", "arbitrary")),
        name="fox_prep",
    )(sm, fb_row)


FOX_T = 256


def _fox_kernel(q_ref, k_ref, v_ref, fcol_ref, frow_ref, qg_ref, kg_ref, o_ref,
                kn_ref, vt_ref, fb_ref, qt_ref, acc_ref):
    i = pl.program_id(1)
    t = FOX_T
    nkc = vt_ref.shape[0]
    sub = t // CHUNK

    @pl.when(i == 0)
    def _():
        for h in range(N_HEADS):
            sl = slice(h * HEAD_DIM, (h + 1) * HEAD_DIM)
            kn_ref[:, sl] = _rms(k_ref[:, sl].astype(F32), kg_ref[...]).astype(BF16)
            fb_ref[h] = jnp.broadcast_to(fcol_ref[:, SM_F + h:SM_F + h + 1], fb_ref.shape[1:])
            for j in range(nkc):
                for c in range(sub):
                    rows = slice(j * t + c * CHUNK, j * t + (c + 1) * CHUNK)
                    vt_ref[j, h, :, c * CHUNK:(c + 1) * CHUNK] = v_ref[rows, sl].astype(F32).T.astype(BF16)

    fqs = []
    for h in range(N_HEADS):
        sl = slice(h * HEAD_DIM, (h + 1) * HEAD_DIM)
        qn = _rms(q_ref[:, sl].astype(F32), qg_ref[...]) * (HEAD_DIM ** -0.5)
        qt_ref[h] = jnp.concatenate([qn[c * CHUNK:(c + 1) * CHUNK, :].T for c in range(sub)], axis=1).astype(BF16)
        fqs.append(jnp.concatenate([frow_ref[0, i * sub + c, h:h + 1, :] for c in range(sub)], axis=1))
        acc_ref[h] = jnp.zeros(acc_ref.shape[1:], F32)

    kofs = lax.broadcasted_iota(jnp.int32, (t, t), 0)
    qofs = lax.broadcasted_iota(jnp.int32, (t, t), 1)

    def body(diag, j, carry):
        ms, ls = carry
        start = pl.multiple_of(j * t, t)
        scores = [_dot(kn_ref[pl.ds(start, t), h * HEAD_DIM:(h + 1) * HEAD_DIM], qt_ref[h])
                  for h in range(N_HEADS)]
        new_ms, new_ls, ps, alphas = [], [], [], []
        for h in range(N_HEADS):
            fk = fb_ref[h, pl.ds(start, t), :]
            s = scores[h] + fqs[h] - jnp.concatenate([fk] * (t // LANES), axis=1)
            if diag:
                s = jnp.where(kofs <= qofs, s, NEG_BIG)
            m_new = jnp.maximum(ms[h], jnp.max(s, axis=0, keepdims=True))
            p = jnp.exp(s - m_new)
            alpha = jnp.exp(ms[h] - m_new)
            new_ls.append(alpha * ls[h] + jnp.sum(p, axis=0, keepdims=True))
            ps.append(p.astype(BF16))
            alphas.append(alpha)
            new_ms.append(m_new)
        for h in range(N_HEADS):
            acc_ref[h] = alphas[h] * acc_ref[h] + _dot(vt_ref[j, h], ps[h])
        return tuple(new_ms), tuple(new_ls)

    init = (tuple(jnp.full((1, t), NEG_BIG, F32) for _ in range(N_HEADS)),
            tuple(jnp.zeros((1, t), F32) for _ in range(N_HEADS)))
    carry = lax.fori_loop(0, i, functools.partial(body, False), init)
    _, ls = body(True, i, carry)
    for h in range(N_HEADS):
        out_t = acc_ref[h] / ls[h]
        for c in range(sub):
            o_ref[c * CHUNK:(c + 1) * CHUNK, h * HEAD_DIM:(h + 1) * HEAD_DIM] = (
                out_t[:, c * CHUNK:(c + 1) * CHUNK].T.astype(o_ref.dtype))


def fox_attention(p, fcol, frow, qg, kg, batch, seq):
    nq = seq // FOX_T
    base = COL_FOX // MIX_W
    return pl.pallas_call(
        _fox_kernel,
        grid=(batch, nq),
        in_specs=[
            pl.BlockSpec((FOX_T, MIX_W), lambda b, i: (b * nq + i, base)),
            pl.BlockSpec((seq, MIX_W), lambda b, i: (b, base + 1)),
            pl.BlockSpec((seq, MIX_W), lambda b, i: (b, base + 2)),
            pl.BlockSpec((seq, SM_W), lambda b, i: (b, 0)),
            pl.BlockSpec((1, seq // CHUNK, 8, CHUNK), lambda b, i: (b, 0, 0, 0)),
            pl.BlockSpec((1, HEAD_DIM), lambda b, i: (0, 0)),
            pl.BlockSpec((1, HEAD_DIM), lambda b, i: (0, 0)),
        ],
        out_specs=pl.BlockSpec((FOX_T, MIX_W), lambda b, i: (b * nq + i, 0)),
        out_shape=jax.ShapeDtypeStruct((batch * seq, MIX_W), BF16),
        scratch_shapes=[
            pltpu.VMEM((seq, MIX_W), BF16),
            pltpu.VMEM((nq, N_HEADS, HEAD_DIM, FOX_T), BF16),
            pltpu.VMEM((N_HEADS, seq, LANES), F32),
            pltpu.VMEM((N_HEADS, HEAD_DIM, FOX_T), BF16),
            pltpu.VMEM((N_HEADS, HEAD_DIM, FOX_T), F32),
        ],
        compiler_params=_cparams(("parallel", "arbitrary")),
        name="fox_attention",
    )(p, p, p, fcol, frow, qg, kg)


DSA_TQ = 128
DSA_TK = 256
BAND_W = 2 * DSA_TQ


def _t5_bucket(dist):
    max_exact = N_BUCKETS // 2
    d = jnp.maximum(dist, 0)
    log_ratio = jnp.log(jnp.maximum(d, 1).astype(F32) / max_exact) / math.log(MAX_DISTANCE / max_exact)
    large = jnp.minimum(max_exact + (log_ratio * (N_BUCKETS - max_exact)).astype(jnp.int32), N_BUCKETS - 1)
    return jnp.where(d < max_exact, d, large)


def _dsa_kernel(cq_ref, k_ref, v_ref, smq_ref, smk_ref, cqg_ref, wuq_ref, wqi_ref, qg_ref, kg_ref, rb_ref,
                o_ref, kn_ref, ki_ref, band_ref, sc_ref, qh_ref, qi_ref, m_ref, l_ref, acc_ref, *, topk):
    b = pl.program_id(0)
    i = pl.program_id(1)
    tq, tk = DSA_TQ, DSA_TK
    nkc = sc_ref.shape[0]
    seq = nkc * tk
    tiles = tk // tq
    nb = ((i + 1) * tq + tk - 1) // tk

    @pl.when(jnp.logical_and(b == 0, i == 0))
    def _():
        r = lax.broadcasted_iota(jnp.int32, (tq, BAND_W), 0)
        c = lax.broadcasted_iota(jnp.int32, (tq, BAND_W), 1)
        bucket = _t5_bucket(tq + r - c)
        for h in range(N_HEADS):
            far = rb_ref[N_BUCKETS - 1, h]
            acc = jnp.zeros((tq, BAND_W), F32)
            for bk in range(N_BUCKETS - 1):
                acc = jnp.where(bucket == bk, rb_ref[bk, h] - far, acc)
            band_ref[h] = acc

    @pl.when(i == 0)
    def _():
        kn_ref[...] = _rms(k_ref[...].astype(F32), kg_ref[...]).astype(BF16)
        ki_ref[...] = smk_ref[:, SM_IK:SM_IK + IDX_DIM].astype(BF16)

    cq = _rms(cq_ref[...].astype(F32), cqg_ref[...]).astype(BF16)
    qf = _dot(cq, wuq_ref[...])
    for h in range(N_HEADS):
        sl = slice(h * HEAD_DIM, (h + 1) * HEAD_DIM)
        qh_ref[h] = (_rms(qf[:, sl], qg_ref[...]) * (HEAD_DIM ** -0.5)).astype(BF16)
    q_idx = (_dot(cq, wqi_ref[...]) * (IDX_DIM ** -0.5)).astype(BF16)
    for h in range(IDX_HEADS):
        qi_ref[h] = q_idx[:, h * IDX_DIM:(h + 1) * IDX_DIM]

    qpos = lax.broadcasted_iota(jnp.int32, (tq, tk), 0) + i * tq
    col = lax.broadcasted_iota(jnp.int32, (tq, tk), 1)

    def score_body(j, _):
        start = pl.multiple_of(j * tk, tk)
        kj = ki_ref[pl.ds(start, tk), :]
        w_h = smq_ref[:, SM_IW:SM_IW + IDX_HEADS] * (IDX_HEADS ** -0.5)
        acc = jnp.zeros((tq, tk), F32)
        for h in range(IDX_HEADS):
            acc = acc + w_h[:, h:h + 1] * jnp.maximum(_dot_nt(qi_ref[h], kj), 0.0)
        sc_ref[j] = jnp.where(col + start <= qpos, acc, -jnp.inf)
        return 0

    lax.fori_loop(0, nb, score_body, 0)

    def over_chunks(fn, init):
        acc = init
        for j in range(nkc):
            acc = lax.cond(j < nb, functools.partial(fn, j), lambda a: a, acc)
        return acc

    def lane_tiles(x):
        return [x[:, t * tq:(t + 1) * tq] for t in range(tiles)]

    def row_total(x):
        return jnp.sum(x, axis=1, keepdims=True)

    def search():
        kf = float(topk)

        def max_fn(j, a):
            for x in lane_tiles(sc_ref[j]):
                a = jnp.maximum(a, x)
            return a

        def min_fn(j, a):
            for x in lane_tiles(sc_ref[j]):
                a = jnp.minimum(a, jnp.where(x == -jnp.inf, jnp.inf, x))
            return a

        smax = jnp.max(over_chunks(max_fn, jnp.full((tq, tq), -jnp.inf, F32)), axis=1, keepdims=True)
        smin = jnp.min(over_chunks(min_fn, jnp.full((tq, tq), jnp.inf, F32)), axis=1, keepdims=True)

        def count_ge(t):
            tb = jnp.broadcast_to(t, (tq, tq))

            def fn(j, a):
                for x in lane_tiles(sc_ref[j]):
                    a = a + jnp.where(x >= tb, 1.0, 0.0)
                return a

            return row_total(over_chunks(fn, jnp.zeros((tq, tq), F32)))

        def midpoint(lo, hi):
            return jnp.where(hi == jnp.inf, smax, 0.5 * (lo + hi))

        def undecided(lo, hi, c_lo, mid):
            return jnp.logical_and(c_lo != kf, jnp.logical_and(mid > lo, mid < hi))

        def cond(carry):
            return jnp.logical_and(carry[0] < 400, carry[1] > 0.0)

        def body(carry):
            it, _, lo, hi, c_lo, c_hi, mid = carry
            upd = undecided(lo, hi, c_lo, mid)
            cnt = count_ge(mid)
            up = jnp.logical_and(upd, cnt >= kf)
            dn = jnp.logical_and(upd, cnt < kf)
            lo = jnp.where(up, mid, lo)
            c_lo = jnp.where(up, cnt, c_lo)
            hi = jnp.where(dn, mid, hi)
            c_hi = jnp.where(dn, cnt, c_hi)
            mid = midpoint(lo, hi)
            active = jnp.max(jnp.where(undecided(lo, hi, c_lo, mid), 1.0, 0.0))
            return it + 1, active, lo, hi, c_lo, c_hi, mid

        lo0 = smin
        hi0 = jnp.full((tq, 1), jnp.inf, F32)
        c_lo0 = count_ge(lo0)
        c_hi0 = jnp.zeros((tq, 1), F32)
        mid0 = midpoint(lo0, hi0)
        act0 = jnp.max(jnp.where(undecided(lo0, hi0, c_lo0, mid0), 1.0, 0.0))
        _, _, lo, hi, c_lo, c_hi, _ = lax.while_loop(
            cond, body, (jnp.int32(0), act0, lo0, hi0, c_lo0, c_hi0, mid0))

        def tie_search():
            need = kf - c_hi
            lo_b = jnp.broadcast_to(lo, (tq, tq))
            hi_b = jnp.broadcast_to(hi, (tq, tq))
            lane = lax.broadcasted_iota(jnp.int32, (tq, tq), 1)

            def tie_body(_, carry):
                jlo, jhi = carry
                jm = (jlo + jhi) // 2
                jm_b = jnp.broadcast_to(jm, (tq, tq))

                def fn(j, a):
                    for t, x in enumerate(lane_tiles(sc_ref[j])):
                        hit = jnp.logical_and(jnp.logical_and(x >= lo_b, x < hi_b), lane + (j * tk + t * tq) <= jm_b)
                        a = a + jnp.where(hit, 1.0, 0.0)
                    return a

                ok = row_total(over_chunks(fn, jnp.zeros((tq, tq), F32))) >= need
                return jnp.where(ok, jlo, jm), jnp.where(ok, jm, jhi)

            n_bits = int(math.ceil(math.log2(seq))) + 1
            _, jmax = lax.fori_loop(0, n_bits, tie_body,
                                    (jnp.full((tq, 1), -1, jnp.int32), jnp.full((tq, 1), seq - 1, jnp.int32)))
            return jmax

        any_tie = jnp.max(jnp.where(c_lo != kf, 1.0, 0.0)) > 0.0
        jmax = lax.cond(any_tie, tie_search, lambda: jnp.full((tq, 1), seq - 1, jnp.int32))
        return lo, hi, jmax

    def keep_all():
        return (jnp.full((tq, 1), -jnp.inf, F32), jnp.full((tq, 1), jnp.inf, F32),
                jnp.full((tq, 1), seq - 1, jnp.int32))

    lo, hi, jmax = lax.cond((i + 1) * tq > topk, search, keep_all)

    m_ref[...] = jnp.full(m_ref.shape, NEG_BIG, F32)
    l_ref[...] = jnp.zeros(l_ref.shape, F32)
    acc_ref[...] = jnp.zeros(acc_ref.shape, F32)

    def attend_body(j, _):
        start = pl.multiple_of(j * tk, tk)
        ks = kn_ref[pl.ds(start, tk), :]
        vs = v_ref[pl.ds(start, tk), :]
        sc = sc_ref[j]
        kpos = col + start
        keep = jnp.logical_or(sc >= hi, jnp.logical_and(sc >= lo, kpos <= jmax))
        keep = jnp.logical_and(keep, kpos <= qpos)
        for h in range(N_HEADS):
            bias = jnp.concatenate(
                [jnp.where(j * tiles + t == i, band_ref[h, :, tq:2 * tq],
                           jnp.where(j * tiles + t == i - 1, band_ref[h, :, 0:tq], 0.0))
                 for t in range(tiles)], axis=1)
            s = jnp.where(keep, _dot_nt(qh_ref[h], ks) + bias, NEG_BIG)
            m_old = m_ref[h]
            m_new = jnp.maximum(m_old, jnp.max(s, axis=1, keepdims=True))
            p = jnp.exp(s - m_new)
            alpha = jnp.exp(m_old - m_new)
            l_ref[h] = alpha * l_ref[h] + jnp.sum(p, axis=1, keepdims=True)
            acc_ref[h] = alpha * acc_ref[h] + _dot(p.astype(BF16), vs)
            m_ref[h] = m_new
        return 0

    lax.fori_loop(0, nb, attend_body, 0)
    for h in range(N_HEADS):
        o_ref[:, h * HEAD_DIM:(h + 1) * HEAD_DIM] = (acc_ref[h] / l_ref[h]).astype(o_ref.dtype)


SUBLANES = 8
COUNT_ROWS = 64
BISECT_FIXED_STEPS = 16


def _fold_rows(x, op):
    return op(x.reshape(x.shape[0] // SUBLANES, SUBLANES, x.shape[1]), axis=0)


def _dsa_t_kernel(cq_ref, k_ref, v_ref, smq_ref, smk_ref, cqg_ref, wuq_ref, wqi_ref, qg_ref, kg_ref, rb_ref,
                  o_ref, kn_ref, ki_ref, vt_ref, band_ref, sc_ref, qt_ref, xi_ref, acc_ref, s_ref, *, topk):
    b = pl.program_id(0)
    i = pl.program_id(1)
    tq, tk = DSA_TQ, DSA_TK
    nkc = sc_ref.shape[0]
    seq = nkc * tk
    tiles = tk // tq
    nb = ((i + 1) * tq + tk - 1) // tk

    @pl.when(jnp.logical_and(b == 0, i == 0))
    def _():
        c = lax.broadcasted_iota(jnp.int32, (BAND_W, tq), 0)
        r = lax.broadcasted_iota(jnp.int32, (BAND_W, tq), 1)
        bucket = _t5_bucket(tq + r - c)
        for h in range(N_HEADS):
            far = rb_ref[N_BUCKETS - 1, h]
            acc = jnp.zeros((BAND_W, tq), F32)
            for bk in range(N_BUCKETS - 1):
                acc = jnp.where(bucket == bk, rb_ref[bk, h] - far, acc)
            band_ref[h] = acc

    @pl.when(i == 0)
    def _():
        kn_ref[...] = _rms(k_ref[...].astype(F32), kg_ref[...]).astype(BF16)
        ki_ref[...] = smk_ref[:, SM_IK:SM_IK + IDX_DIM].astype(BF16)
        for j in range(nkc):
            for t in range(tiles):
                rows = slice(j * tk + t * tq, j * tk + (t + 1) * tq)
                vt_ref[j, :, t * tq:(t + 1) * tq] = v_ref[rows, :].astype(F32).T.astype(BF16)

    cq_t = _rms(cq_ref[...].astype(F32), cqg_ref[...]).T.astype(BF16)
    q_t = _dot(wuq_ref[...], cq_t)
    g_col = jnp.broadcast_to(qg_ref[...], (HEAD_DIM, tq))
    for h in range(N_HEADS):
        x = q_t[h * HEAD_DIM:(h + 1) * HEAD_DIM, :]
        inv = lax.rsqrt(jnp.mean(x * x, axis=0, keepdims=True) + EPS)
        qt_ref[:, h * tq:(h + 1) * tq] = (x * inv * g_col * (HEAD_DIM ** -0.5)).astype(BF16)
    qi_t = (_dot(wqi_ref[...], cq_t) * (IDX_DIM ** -0.5)).astype(BF16)
    for h in range(IDX_HEADS):
        xi_ref[:, h * tq:(h + 1) * tq] = qi_t[h * IDX_DIM:(h + 1) * IDX_DIM, :]
    w_rows = smq_ref[...].T[SM_IW:SM_IW + IDX_HEADS, :] * (IDX_HEADS ** -0.5)

    kofs = lax.broadcasted_iota(jnp.int32, (tk, tq), 0)
    qpos = lax.broadcasted_iota(jnp.int32, (tk, tq), 1) + i * tq

    def score_body(j, _):
        start = pl.multiple_of(j * tk, tk)
        kj = ki_ref[pl.ds(start, tk), :]
        acc = jnp.zeros((tk, tq), F32)
        for h2 in range(IDX_HEADS // 2):
            r = _dot(kj, xi_ref[:, 2 * h2 * tq:(2 * h2 + 2) * tq])
            for h in (2 * h2, 2 * h2 + 1):
                acc = acc + w_rows[h:h + 1, :] * jnp.maximum(r[:, (h - 2 * h2) * tq:(h - 2 * h2 + 1) * tq], 0.0)
        sc_ref[j] = jnp.where(kofs + start <= qpos, acc, -jnp.inf)
        return 0

    lax.fori_loop(0, nb, score_body, 0)

    @pl.when(nb % 2 == 1)
    def _():
        sc_ref[jnp.minimum(nb, nkc - 1)] = jnp.full((tk, tq), -jnp.inf, F32)

    def over_chunks(fn, init):
        acc = init
        for j in range(nkc):
            acc = lax.cond(j < nb, functools.partial(fn, j), lambda a: a, acc)
        return acc

    def count_where(pred_fn):
        def walk(n_chunks):
            def run():
                a = jnp.zeros((COUNT_ROWS, tq), F32)
                for j in range(n_chunks):
                    hit = jnp.where(pred_fn(j, sc_ref[j]), 1.0, 0.0)
                    a = a + jnp.sum(hit.reshape(tk // COUNT_ROWS, COUNT_ROWS, tq), axis=0)
                return a
            return run

        extents = sorted({min(n, nkc) for n in range(2, nkc + 2, 2)})
        a = lax.switch((nb - 1) // 2, [walk(n) for n in extents])
        return jnp.sum(a, axis=0, keepdims=True)

    def search():
        kf = float(topk)
        smax = jnp.max(over_chunks(lambda j, a: jnp.maximum(a, _fold_rows(sc_ref[j], jnp.max)),
                                   jnp.full((SUBLANES, tq), -jnp.inf, F32)), axis=0, keepdims=True)
        smin = jnp.min(over_chunks(
            lambda j, a: jnp.minimum(a, _fold_rows(jnp.where(sc_ref[j] == -jnp.inf, jnp.inf, sc_ref[j]), jnp.min)),
            jnp.full((SUBLANES, tq), jnp.inf, F32)), axis=0, keepdims=True)

        def count_ge(t):
            return count_where(lambda j, x: x >= t)

        def midpoint(lo, hi):
            return jnp.where(hi == jnp.inf, smax, 0.5 * (lo + hi))

        def undecided(lo, hi, c_lo, mid):
            return jnp.logical_and(c_lo != kf, jnp.logical_and(mid > lo, mid < hi))

        def step(state):
            lo, hi, c_lo, c_hi, mid = state
            upd = undecided(lo, hi, c_lo, mid)
            cnt = count_ge(mid)
            up = jnp.logical_and(upd, cnt >= kf)
            dn = jnp.logical_and(upd, cnt < kf)
            lo = jnp.where(up, mid, lo)
            c_lo = jnp.where(up, cnt, c_lo)
            hi = jnp.where(dn, mid, hi)
            c_hi = jnp.where(dn, cnt, c_hi)
            return lo, hi, c_lo, c_hi, midpoint(lo, hi)

        def any_undecided(state):
            lo, hi, c_lo, _, mid = state
            return jnp.max(jnp.where(undecided(lo, hi, c_lo, mid), 1.0, 0.0))

        def cond(carry):
            return jnp.logical_and(carry[0] < 200, carry[1] > 0.0)

        def body(carry):
            state = step(step(carry[2]))
            return carry[0] + 1, any_undecided(state), state

        lo0 = smin
        hi0 = jnp.full((1, tq), jnp.inf, F32)
        c_lo0 = (lax.broadcasted_iota(jnp.int32, (1, tq), 1) + (i * tq + 1)).astype(F32)
        c_hi0 = jnp.zeros((1, tq), F32)
        state = (lo0, hi0, c_lo0, c_hi0, midpoint(lo0, hi0))
        state = lax.fori_loop(0, BISECT_FIXED_STEPS, lambda _, s: step(s), state)
        _, _, (lo, hi, c_lo, c_hi, _) = lax.while_loop(
            cond, body, (jnp.int32(0), any_undecided(state), state))

        def tie_search():
            need = kf - c_hi

            def tie_body(_, carry):
                jlo, jhi = carry
                jm = (jlo + jhi) // 2
                cnt = count_where(lambda j, x: jnp.logical_and(jnp.logical_and(x >= lo, x < hi),
                                                               kofs + j * tk <= jm))
                ok = cnt >= need
                return jnp.where(ok, jlo, jm), jnp.where(ok, jm, jhi)

            n_bits = int(math.ceil(math.log2(seq))) + 1
            _, jmax = lax.fori_loop(0, n_bits, tie_body,
                                    (jnp.full((1, tq), -1, jnp.int32), jnp.full((1, tq), seq - 1, jnp.int32)))
            return jmax

        any_tie = jnp.max(jnp.where(c_lo != kf, 1.0, 0.0)) > 0.0
        jmax = lax.cond(any_tie, tie_search, lambda: jnp.full((1, tq), seq - 1, jnp.int32))
        return lo, hi, jmax

    def keep_all():
        return (jnp.full((1, tq), -jnp.inf, F32), jnp.full((1, tq), jnp.inf, F32),
                jnp.full((1, tq), seq - 1, jnp.int32))

    lo, hi, jmax = lax.cond((i + 1) * tq > topk, search, keep_all)

    def attend(n_chunks):
        def run():
            mx = [jnp.full((SUBLANES, tq), NEG_BIG, F32) for _ in range(N_HEADS)]
            for j in range(n_chunks):
                near = j >= n_chunks - 3
                s_all = _dot(kn_ref[j * tk:(j + 1) * tk, :], qt_ref[...])
                sc = sc_ref[j]
                kpos = kofs + j * tk
                keep = jnp.logical_or(sc >= hi, jnp.logical_and(sc >= lo, kpos <= jmax))
                if near:
                    keep = jnp.logical_and(keep, kpos <= qpos)
                for h in range(N_HEADS):
                    hs = slice(h * tq, (h + 1) * tq)
                    s = s_all[:, hs]
                    if near:
                        s = s + jnp.concatenate(
                            [jnp.where(j * tiles + t == i, band_ref[h, tq:2 * tq, :],
                                       jnp.where(j * tiles + t == i - 1, band_ref[h, 0:tq, :], 0.0))
                             for t in range(tiles)], axis=0)
                    s = jnp.where(keep, s, NEG_BIG)
                    s_ref[j, :, hs] = s
                    mx[h] = jnp.maximum(mx[h], _fold_rows(s, jnp.max))
            m_all = jnp.concatenate([jnp.max(x, axis=0, keepdims=True) for x in mx], axis=1)
            acc = jnp.zeros(acc_ref.shape, F32)
            l_part = jnp.zeros((SUBLANES, N_HEADS * tq), F32)
            for j in range(n_chunks):
                p = jnp.exp(s_ref[j] - m_all)
                l_part = l_part + _fold_rows(p, jnp.sum)
                acc = acc + _dot(vt_ref[j], p.astype(BF16))
            acc_ref[...] = acc
            return jnp.sum(l_part, axis=0, keepdims=True)
        return run

    extents = sorted({min(n, nkc) for n in range(2, nkc + 2, 2)})
    l_all = lax.switch((nb - 1) // 2, [attend(n) for n in extents])
    for h in range(N_HEADS):
        hs = slice(h * tq, (h + 1) * tq)
        out_t = acc_ref[:, hs] / l_all[:, hs]
        o_ref[:, h * HEAD_DIM:(h + 1) * HEAD_DIM] = out_t.T.astype(o_ref.dtype)


def dsa_attention(p, sm, cqg, wuq_t, wqi_t, qg_col, kg, rel_bias, batch, seq):
    nq = seq // DSA_TQ
    nkc = seq // DSA_TK
    topk = min(DSA_TOPK, seq // 4)
    kern = functools.partial(_dsa_t_kernel, topk=topk)
    return pl.pallas_call(
        kern,
        grid=(batch, nq),
        in_specs=[
            pl.BlockSpec((DSA_TQ, DSA_Q_RANK), lambda b, i: (b * nq + i, COL_CQ // DSA_Q_RANK)),
            pl.BlockSpec((seq, HEAD_DIM), lambda b, i: (b, COL_DK // HEAD_DIM)),
            pl.BlockSpec((seq, HEAD_DIM), lambda b, i: (b, COL_DV // HEAD_DIM)),
            pl.BlockSpec((DSA_TQ, SM_W), lambda b, i: (b * nq + i, 0)),
            pl.BlockSpec((seq, SM_W), lambda b, i: (b, 0)),
            pl.BlockSpec((1, DSA_Q_RANK), lambda b, i: (0, 0)),
            pl.BlockSpec((N_HEADS * HEAD_DIM, DSA_Q_RANK), lambda b, i: (0, 0)),
            pl.BlockSpec((IDX_HEADS * IDX_DIM, DSA_Q_RANK), lambda b, i: (0, 0)),
            pl.BlockSpec((HEAD_DIM, 1), lambda b, i: (0, 0)),
            pl.BlockSpec((1, HEAD_DIM), lambda b, i: (0, 0)),
            pl.BlockSpec(memory_space=pltpu.SMEM),
        ],
        out_specs=pl.BlockSpec((DSA_TQ, MIX_W), lambda b, i: (b * nq + i, 0)),
        out_shape=jax.ShapeDtypeStruct((batch * seq, MIX_W), BF16),
        scratch_shapes=[
            pltpu.VMEM((seq, HEAD_DIM), BF16),
            pltpu.VMEM((seq, IDX_DIM), BF16),
            pltpu.VMEM((nkc, HEAD_DIM, DSA_TK), BF16),
            pltpu.VMEM((N_HEADS, BAND_W, DSA_TQ), F32),
            pltpu.VMEM((nkc, DSA_TK, DSA_TQ), F32),
            pltpu.VMEM((HEAD_DIM, N_HEADS * DSA_TQ), BF16),
            pltpu.VMEM((IDX_DIM, IDX_HEADS * DSA_TQ), BF16),
            pltpu.VMEM((HEAD_DIM, N_HEADS * DSA_TQ), F32),
            pltpu.VMEM((nkc, DSA_TK, N_HEADS * DSA_TQ), F32),
        ],
        compiler_params=_cparams(("arbitrary", "arbitrary")),
        name="dsa_attention",
    )(p, p, p, sm, sm, cqg, wuq_t, wqi_t, qg_col, kg, rel_bias)


CONV_PAD = 8


def _causal_conv(x_ref, xp_ref, first, w_ref, b_ref, ext_ref):
    ext_ref[0:CONV_PAD, :] = xp_ref[CHUNK - CONV_PAD:CHUNK, :].astype(F32) * first
    ext_ref[CONV_PAD:CONV_PAD + CHUNK, :] = x_ref[...].astype(F32)
    acc = b_ref[...] + ext_ref[CONV_PAD:CONV_PAD + CHUNK, :] * w_ref[SSD_CONV - 1:SSD_CONV, :]
    for d in range(1, SSD_CONV):
        acc = acc + ext_ref[CONV_PAD - d:CONV_PAD - d + CHUNK, :] * w_ref[SSD_CONV - 1 - d:SSD_CONV - d, :]
    return _silu(acc)


def _ssd_kernel(z_ref, xs_ref, bc_ref, xsp_ref, bcp_ref, sm_ref, cwx_ref, cbx_ref, cwb_ref, cbb_ref,
                dtb_ref, alog_ref, dsk_ref, ng_ref, o_ref, prev_ref, y_ref, extx_ref, extb_ref):
    c = CHUNK
    n = pl.program_id(1)

    @pl.when(n == 0)
    def _():
        prev_ref[...] = jnp.zeros_like(prev_ref)

    first = (n > 0).astype(F32)
    xs = _causal_conv(xs_ref, xsp_ref, first, cwx_ref, cbx_ref, extx_ref)
    bc = _causal_conv(bc_ref, bcp_ref, first, cwb_ref, cbb_ref, extb_ref)

    dt_t = _softplus(sm_ref[...].T + dtb_ref[...])
    cs_t = _cumsum_lanes(dt_t * (-jnp.exp(alog_ref[...])))
    cs = cs_t.T
    ii = lax.broadcasted_iota(jnp.int32, (c, c), 0)
    jj = lax.broadcasted_iota(jnp.int32, (c, c), 1)
    tril = ii >= jj
    pair_w = 2 * SSD_HEAD_DIM
    first_head = jj < SSD_HEAD_DIM
    first_head_row = lax.broadcasted_iota(jnp.int32, (1, pair_w), 1) < SSD_HEAD_DIM
    gn = SSD_GROUPS * SSD_STATE
    hpg = SSD_HEADS // SSD_GROUPS
    for g in range(SSD_GROUPS):
        bg = bc[:, g * SSD_STATE:(g + 1) * SSD_STATE]
        cg = bc[:, gn + g * SSD_STATE:gn + (g + 1) * SSD_STATE].astype(BF16)
        cb = _dot_nt(cg, bg.astype(BF16))
        bg_t = bg.T
        y_off = _dot(cg, prev_ref[g].astype(BF16))
        for pr in range(hpg // 2):
            cols = slice((g * hpg + 2 * pr) * SSD_HEAD_DIM, (g * hpg + 2 * pr + 2) * SSD_HEAD_DIM)
            rcols = slice(2 * pr * SSD_HEAD_DIM, (2 * pr + 2) * SSD_HEAD_DIM)
            x_pair = xs[:, cols]
            x_bf = x_pair.astype(BF16)
            y_diag, st, exp_a, exp_last = [], [], [], []
            for k in range(2):
                row = SM_DT + g * hpg + 2 * pr + k
                a_row = cs_t[row:row + 1, :]
                dt_row = dt_t[row:row + 1, :]
                last = cs_t[row:row + 1, c - 1:c]
                a_col = jnp.broadcast_to(cs[:, row:row + 1], (c, c))
                seg = jnp.where(tril, jnp.exp(jnp.where(tril, a_col - a_row, 0.0)), 0.0)
                y_diag.append(_dot((cb * seg * dt_row).astype(BF16), x_bf))
                st.append(_dot((bg_t * (dt_row * jnp.exp(last - a_row))).astype(BF16), x_bf))
                exp_a.append(jnp.exp(a_col))
                exp_last.append(jnp.exp(last))
            y_ref[:, cols] = (jnp.where(first_head, y_diag[0], y_diag[1])
                              + y_off[:, rcols] * jnp.where(first_head, exp_a[0], exp_a[1])
                              + dsk_ref[:, cols] * x_pair)
            prev_ref[g, :, rcols] = (jnp.where(first_head_row, exp_last[0], exp_last[1]) * prev_ref[g, :, rcols]
                                     + jnp.where(first_head, st[0], st[1]))
    gated = y_ref[...] * _silu(z_ref[...].astype(F32))
    gw = SSD_INNER // SSD_GROUPS
    for g in range(SSD_GROUPS):
        sl = slice(g * gw, (g + 1) * gw)
        o_ref[:, sl] = _rms(gated[:, sl], ng_ref[:, sl]).astype(o_ref.dtype)


def ssd_mixer(p, sm, cw, cb, dtb_col, alog_col, dskip_row, ng, batch, seq):
    n = seq // CHUNK
    bcw = 2 * SSD_GROUPS * SSD_STATE

    def cur(width, colbase):
        return pl.BlockSpec((CHUNK, width), lambda b, i: (b * n + i, colbase // width))

    def prv(width, colbase):
        return pl.BlockSpec((CHUNK, width), lambda b, i: (b * n + jnp.maximum(i - 1, 0), colbase // width))

    def const(shape):
        return pl.BlockSpec(shape, lambda b, i: (0, 0))

    return pl.pallas_call(
        _ssd_kernel,
        grid=(batch, n),
        in_specs=[
            cur(SSD_INNER, COL_Z), cur(SSD_INNER, COL_XS), cur(bcw, COL_BC),
            prv(SSD_INNER, COL_XS), prv(bcw, COL_BC),
            pl.BlockSpec((CHUNK, SM_W), lambda b, i: (b * n + i, 0)),
            const((SSD_CONV, SSD_INNER)), const((1, SSD_INNER)),
            const((SSD_CONV, bcw)), const((1, bcw)),
            const((SM_W, 1)), const((SM_W, 1)),
            const((1, SSD_INNER)), const((1, SSD_INNER)),
        ],
        out_specs=pl.BlockSpec((CHUNK, SSD_INNER), lambda b, i: (b * n + i, 0)),
        out_shape=jax.ShapeDtypeStruct((batch * seq, SSD_INNER), BF16),
        scratch_shapes=[
            pltpu.VMEM((SSD_GROUPS, SSD_STATE, SSD_INNER // SSD_GROUPS), F32),
            pltpu.VMEM((CHUNK, SSD_INNER), F32),
            pltpu.VMEM((CONV_PAD + CHUNK, SSD_INNER), F32),
            pltpu.VMEM((CONV_PAD + CHUNK, bcw), F32),
        ],
        compiler_params=_cparams(("parallel", "arbitrary")),
        name="ssd_mixer",
    )(p, p, p, p, p, sm, cw[:, :SSD_INNER], cb[:, :SSD_INNER], cw[:, SSD_INNER:], cb[:, SSD_INNER:],
      dtb_col, alog_col, dskip_row, ng)


MERGE_TM = 256


def _merge_kernel(x_ref, gl_ref, gb_ref, oret_ref, ofox_ref, odsa_ref, ossd_ref, wbr_ref, wout_ref, o_ref):
    branches = (oret_ref, ofox_ref, odsa_ref, ossd_ref)
    merged = None
    row0 = 0
    for bi, br in enumerate(branches):
        width = br.shape[1]
        sl = slice(bi * D_MODEL, (bi + 1) * D_MODEL)
        gate = 1.0 / (1.0 + jnp.exp(-(gl_ref[:, sl].astype(F32) + gb_ref[:, sl])))
        term = gate * _dot(br[...], wbr_ref[row0:row0 + width, :])
        merged = term if merged is None else merged + term
        row0 += width
    o_ref[...] = x_ref[...] + _dot(merged.astype(BF16), wout_ref[...])


def merge_project(x, p, gate_b, o_ret, o_fox, o_dsa, o_ssd, w_br, w_out, layer):
    m = x.shape[0]
    tm = MERGE_TM

    def rows(width):
        return pl.BlockSpec((tm, width), lambda i: (i, 0))

    def const(shape):
        return pl.BlockSpec(shape, lambda i: (0, 0), pipeline_mode=pl.Buffered(1))

    def stacked(w):
        return pl.BlockSpec((None,) + w.shape[1:], lambda i: (layer, 0, 0), pipeline_mode=pl.Buffered(1))

    return pl.pallas_call(
        _merge_kernel,
        grid=(m // tm,),
        in_specs=[
            rows(D_MODEL), rows(N_BRANCH * D_MODEL), const((1, N_BRANCH * D_MODEL)),
            rows(MIX_W), rows(MIX_W), rows(MIX_W), rows(SSD_INNER),
            stacked(w_br), stacked(w_out),
        ],
        out_specs=rows(D_MODEL),
        out_shape=jax.ShapeDtypeStruct(x.shape, x.dtype),
        compiler_params=_cparams(("parallel",)),
        name="merge_project",
    )(x, p, gate_b, o_ret, o_fox, o_dsa, o_ssd, w_br, w_out)


FFN_TM = 1024
FFN_TF = 512


def _ffn_kernel(x_ref, g_ref, w1_ref, w2_ref, o_ref, h_ref):
    @pl.when(pl.program_id(1) == 0)
    def _():
        h_ref[...] = _rms(x_ref[...], g_ref[...]).astype(BF16)
        o_ref[...] = x_ref[...]

    a = jnp.maximum(_dot(h_ref[...], w1_ref[...]), 0.0)
    o_ref[...] += _dot((a * a).astype(BF16), w2_ref[...])


def ffn(x, g, w1, w2, layer):
    m, d = x.shape
    dff = w1.shape[2]
    tm, tf = min(FFN_TM, m), FFN_TF
    return pl.pallas_call(
        _ffn_kernel,
        grid=(m // tm, dff // tf),
        in_specs=[
            pl.BlockSpec((tm, d), lambda i, f: (i, 0), pipeline_mode=pl.Buffered(1)),
            pl.BlockSpec((1, d), lambda i, f: (0, 0)),
            pl.BlockSpec((None, d, tf), lambda i, f: (layer, 0, f)),
            pl.BlockSpec((None, tf, d), lambda i, f: (layer, f, 0)),
        ],
        out_specs=pl.BlockSpec((tm, d), lambda i, f: (i, 0)),
        out_shape=jax.ShapeDtypeStruct(x.shape, x.dtype),
        scratch_shapes=[pltpu.VMEM((tm, d), BF16)],
        compiler_params=_cparams(("parallel", "arbitrary")),
        name="ffn",
    )(x, g, w1, w2)


SRC_RET = 0
SRC_FOX = SRC_RET + 4 * MIX_W
SRC_FF = SRC_FOX + 3 * MIX_W
SRC_CQ = SRC_FF + N_HEADS
SRC_DK = SRC_CQ + DSA_Q_RANK
SRC_IK = SRC_DK + 2 * HEAD_DIM
SRC_IW = SRC_IK + IDX_DIM
SRC_Z = SRC_IW + IDX_HEADS
SRC_DT = SRC_Z + 2 * SSD_INNER + 2 * SSD_GROUPS * SSD_STATE
SRC_GATE = SRC_DT + SSD_HEADS
IN_TOTAL = SRC_GATE + N_BRANCH * D_MODEL
MAIN_RUNS = ((COL_GATE, SRC_GATE), (COL_RET, SRC_RET), (COL_Z, SRC_Z), (COL_CQ, SRC_CQ),
             (COL_FOX, SRC_FOX), (COL_DK, SRC_DK))
RELAYOUT_W = 512
RELAYOUT_TILES = RELAYOUT_W // LANES


def _relayout_tables():
    starts, shifts = [], []
    for blk in range(N_MAIN // RELAYOUT_W):
        o = blk * RELAYOUT_W
        dst, src = [r for r in MAIN_RUNS if r[0] <= o][-1]
        col = src + (o - dst)
        starts.append(col // LANES)
        shifts.append(col % LANES)
    return jnp.asarray(starts, jnp.int32), jnp.asarray(shifts, jnp.int32)


def _relayout_kernel(start_ref, shift_ref, *refs):
    del start_ref
    tiles, o_ref = refs[:-1], refs[-1]
    shift = shift_ref[pl.program_id(1)]
    amount = lax.rem(LANES - shift, LANES)
    lane = lax.broadcasted_iota(jnp.int32, tiles[0].shape, 1)
    rolled = [pltpu.roll(t[...], amount, 1) for t in tiles]
    for k in range(RELAYOUT_TILES):
        piece = jnp.where(lane < LANES - shift, rolled[k], rolled[k + 1])
        o_ref[:, k * LANES:(k + 1) * LANES] = piece.astype(o_ref.dtype)


def relayout_main(w_in):
    depth, d, n_src = w_in.shape
    last = (n_src - 1) // LANES
    starts, shifts = _relayout_tables()

    def tile(k):
        return pl.BlockSpec((None, d, LANES), lambda l, b, st, sh: (l, 0, jnp.minimum(st[b] + k, last)))

    return pl.pallas_call(
        _relayout_kernel,
        grid_spec=pltpu.PrefetchScalarGridSpec(
            num_scalar_prefetch=2,
            grid=(depth, N_MAIN // RELAYOUT_W),
            in_specs=[tile(k) for k in range(RELAYOUT_TILES + 1)],
            out_specs=pl.BlockSpec((None, d, RELAYOUT_W), lambda l, b, st, sh: (l, 0, b)),
        ),
        out_shape=jax.ShapeDtypeStruct((depth, d, N_MAIN), BF16),
        compiler_params=_cparams(("parallel", "arbitrary")),
        name="relayout_main",
    )(starts, shifts, *([w_in] * (RELAYOUT_TILES + 1)))


SMALL_PIECES = ((SRC_DT, SM_DT, SSD_HEADS), (SRC_FF, SM_F, N_HEADS), (SRC_IW, SM_IW, IDX_HEADS),
                (SRC_IK, SM_IK, IDX_DIM))


def _relayout_small_kernel(*refs):
    tiles, o_ref = refs[:-1], refs[-1]
    lane = lax.broadcasted_iota(jnp.int32, o_ref.shape, 1)
    out = jnp.zeros(o_ref.shape, F32)
    for t, (src, dst, width) in zip(tiles, SMALL_PIECES):
        moved = pltpu.roll(t[...], (dst - src % LANES) % LANES, 1)
        out = jnp.where(jnp.logical_and(lane >= dst, lane < dst + width), moved, out)
    o_ref[...] = out.astype(o_ref.dtype)


def relayout_small(w_in):
    depth, d, _ = w_in.shape
    for src, _, width in SMALL_PIECES:
        assert src // LANES == (src + width - 1) // LANES

    def tile(src):
        return pl.BlockSpec((None, d, LANES), lambda l: (l, 0, src // LANES))

    return pl.pallas_call(
        _relayout_small_kernel,
        grid=(depth,),
        in_specs=[tile(src) for src, _, _ in SMALL_PIECES],
        out_specs=pl.BlockSpec((None, d, SM_W), lambda l: (l, 0, 0)),
        out_shape=jax.ShapeDtypeStruct((depth, d, SM_W), BF16),
        compiler_params=_cparams(("parallel",)),
        name="relayout_small",
    )(*([w_in] * len(SMALL_PIECES)))


def _pad_to(v, offset, total):
    return jnp.zeros((total,), v.dtype).at[offset:offset + v.shape[0]].set(v)


def _rotary_tables(seq):
    half = HEAD_DIM // 2
    inv = 1.0 / (10000.0 ** (jnp.arange(half, dtype=F32) / half))
    ang = jnp.arange(seq, dtype=F32)[:, None] * inv[None, :]
    cos, sin = jnp.cos(ang), jnp.sin(ang)
    return jnp.concatenate([cos, cos], axis=1), jnp.concatenate([-sin, sin], axis=1)


def kernel(x, norm1_g, w_in, gate_b, fox_f_b, fox_qn_g, fox_kn_g, dsa_cq_g, dsa_w_uq, dsa_w_qidx, dsa_qn_g,
           dsa_kn_g, rel_bias, ssd_conv_w, ssd_conv_b, ssd_dt_bias, ssd_a_log, ssd_d, ssd_norm_g, w_br, w_out,
           norm2_g, w_ff1, w_ff2):
    batch, seq, d = x.shape
    tokens = batch * seq
    xt = x.reshape(tokens, d)
    cos, sin = _rotary_tables(seq)
    tm = min(1024, tokens)
    w_main = relayout_main(w_in)
    w_small = relayout_small(w_in)
    w_br_bf, w_out_bf = w_br.astype(BF16), w_out.astype(BF16)
    w_ff1_bf, w_ff2_bf = w_ff1.astype(BF16), w_ff2.astype(BF16)
    for l in range(DEPTH):
        g1 = norm1_g[l][None, :]
        p = norm_matmul(xt, g1, w_main, l, BF16, tm, 1024)
        sm = norm_matmul(xt, g1, w_small, l, F32, tm, SM_W)

        o_ret = retention(p, cos, sin, batch, seq)

        fb_row = _pad_to(fox_f_b[l], SM_F, SM_W)[None, :]
        fcol, frow = fox_prep(sm, fb_row, batch, seq)
        o_fox = fox_attention(p, fcol, frow, fox_qn_g[l][None, :], fox_kn_g[l][None, :], batch, seq)

        o_dsa = dsa_attention(p, sm, dsa_cq_g[l][None, :], dsa_w_uq[l].T.astype(BF16), dsa_w_qidx[l].T.astype(BF16),
                              dsa_qn_g[l][:, None], dsa_kn_g[l][None, :], rel_bias, batch, seq)

        o_ssd = ssd_mixer(p, sm, ssd_conv_w[l], ssd_conv_b[l][None, :],
                          _pad_to(ssd_dt_bias[l], SM_DT, SM_W)[:, None], _pad_to(ssd_a_log[l], SM_DT, SM_W)[:, None],
                          jnp.repeat(ssd_d[l], SSD_HEAD_DIM)[None, :], ssd_norm_g[l][None, :], batch, seq)

        xt = merge_project(xt, p, gate_b[l][None, :], o_ret, o_fox, o_dsa, o_ssd, w_br_bf, w_out_bf, l)
        xt = ffn(xt, norm2_g[l][None, :], w_ff1_bf, w_ff2_bf, l)
    return xt.reshape(batch, seq, d)
```

```python
import functools
import math

import jax
import jax.numpy as jnp
from jax import lax
from jax.experimental import pallas as pl
from jax.experimental.pallas import tpu as pltpu

F32 = jnp.float32
BF16 = jnp.bfloat16

D_MODEL = 2048
DEPTH = 4
HEAD_DIM = 128
N_HEADS = 4
DSA_Q_RANK = 512
IDX_HEADS = 16
IDX_DIM = 64
DSA_TOPK = 256
SSD_HEADS = 16
SSD_HEAD_DIM = 64
SSD_GROUPS = 2
SSD_STATE = 128
SSD_CONV = 4
SSD_INNER = SSD_HEADS * SSD_HEAD_DIM
D_FF = 4 * D_MODEL
N_BUCKETS = 32
MAX_DISTANCE = 128
CHUNK = 128
EPS = 1e-6
N_BRANCH = 4
MIX_W = N_HEADS * HEAD_DIM

COL_GATE = 0
COL_RET = COL_GATE + N_BRANCH * D_MODEL
COL_Z = COL_RET + 4 * MIX_W
COL_XS = COL_Z + SSD_INNER
COL_BC = COL_XS + SSD_INNER
COL_CQ = COL_BC + 2 * SSD_GROUPS * SSD_STATE
COL_FOX = COL_CQ + DSA_Q_RANK
COL_DK = COL_FOX + 3 * MIX_W
COL_DV = COL_DK + HEAD_DIM
N_MAIN_USED = COL_DV + HEAD_DIM
N_MAIN = 15360
SM_DT = 0
SM_F = 16
SM_IW = 32
SM_IK = 64
SM_W = 128

LANES = 128
VMEM_LIMIT = 56 * 1024 * 1024
NEG_BIG = -1e30


def _cparams(sem):
    return pltpu.CompilerParams(dimension_semantics=sem, vmem_limit_bytes=VMEM_LIMIT)


def _dot(a, b):
    return jnp.dot(a, b, preferred_element_type=F32)


def _dot_nt(a, b):
    return lax.dot_general(a, b, (((1,), (1,)), ((), ())), preferred_element_type=F32)


def _dot_tn(a, b):
    return lax.dot_general(a, b, (((0,), (0,)), ((), ())), preferred_element_type=F32)


def _rms(x, g):
    return x * lax.rsqrt(jnp.mean(x * x, axis=-1, keepdims=True) + EPS) * g


def _silu(x):
    return x / (1.0 + jnp.exp(-x))


def _softplus(x):
    return jnp.maximum(x, 0.0) + jnp.log1p(jnp.exp(-jnp.abs(x)))


def _cumsum_lanes(x):
    lane = lax.broadcasted_iota(jnp.int32, x.shape, 1)
    d = 1
    while d < x.shape[1]:
        x = x + jnp.where(lane >= d, pltpu.roll(x, d, 1), 0.0)
        d *= 2
    return x


def _norm_matmul_kernel(x_ref, g_ref, w_ref, o_ref, h_ref):
    @pl.when(pl.program_id(1) == 0)
    def _():
        h_ref[...] = _rms(x_ref[...], g_ref[...]).astype(BF16)

    o_ref[...] = _dot_nt(h_ref[...], w_ref[...]).astype(o_ref.dtype)


def norm_matmul(x, g, w_t, layer, out_dtype, tm, tn):
    m, d = x.shape
    n = w_t.shape[1]
    return pl.pallas_call(
        _norm_matmul_kernel,
        grid=(m // tm, n // tn),
        in_specs=[
            pl.BlockSpec((tm, d), lambda i, j: (i, 0)),
            pl.BlockSpec((1, d), lambda i, j: (0, 0)),
            pl.BlockSpec((None, tn, d), lambda i, j: (layer, j, 0)),
        ],
        out_specs=pl.BlockSpec((tm, tn), lambda i, j: (i, j)),
        out_shape=jax.ShapeDtypeStruct((m, n), out_dtype),
        scratch_shapes=[pltpu.VMEM((tm, d), BF16)],
        compiler_params=_cparams(("parallel", "arbitrary")),
        name="norm_matmul",
    )(x, g, w_t)


def _retention_kernel(q_ref, k_ref, v_ref, g_ref, cos_ref, sin_ref, o_ref, state_ref):
    c = CHUNK

    @pl.when(pl.program_id(1) == 0)
    def _():
        state_ref[...] = jnp.zeros_like(state_ref)

    cos = cos_ref[...]
    sin = sin_ref[...]
    ii = lax.broadcasted_iota(jnp.int32, (c, c), 0)
    jj = lax.broadcasted_iota(jnp.int32, (c, c), 1)
    rel = (ii - jj).astype(F32)
    i_col = lax.broadcasted_iota(jnp.int32, (c, 1), 0).astype(F32)
    for h in range(N_HEADS):
        lg = math.log1p(-(2.0 ** (-5.0 - h)))
        sl = slice(h * HEAD_DIM, (h + 1) * HEAD_DIM)
        q = q_ref[:, sl].astype(F32)
        k = k_ref[:, sl].astype(F32)
        v = v_ref[:, sl]
        qr = q * cos + pltpu.roll(q, HEAD_DIM // 2, 1) * sin
        kr = (k * cos + pltpu.roll(k, HEAD_DIM // 2, 1) * sin) * (HEAD_DIM ** -0.5)
        decay = jnp.where(rel >= 0, jnp.exp(lg * jnp.maximum(rel, 0.0)), 0.0)
        scores = _dot_nt(qr.astype(BF16), kr.astype(BF16)) * decay
        y = _dot(scores.astype(BF16), v)
        q_dec = jnp.exp(lg * (i_col + 1.0))
        k_dec = jnp.exp(lg * (c - 1.0 - i_col))
        st = state_ref[h]
        y = y + _dot((qr * q_dec).astype(BF16), st.astype(BF16))
        kv = _dot_tn((kr * k_dec).astype(BF16), v)
        state_ref[h] = math.exp(lg * c) * st + kv
        yc = y - jnp.mean(y, axis=-1, keepdims=True)
        yn = yc * lax.rsqrt(jnp.mean(yc * yc, axis=-1, keepdims=True) + EPS)
        o_ref[:, sl] = (_silu(g_ref[:, sl].astype(F32)) * yn).astype(o_ref.dtype)


def retention(p, cos, sin, batch, seq):
    n = seq // CHUNK
    base = COL_RET // MIX_W

    def col(j):
        return pl.BlockSpec((CHUNK, MIX_W), lambda b, i: (b * n + i, base + j))

    tab = pl.BlockSpec((CHUNK, HEAD_DIM), lambda b, i: (i, 0))
    return pl.pallas_call(
        _retention_kernel,
        grid=(batch, n),
        in_specs=[col(0), col(1), col(2), col(3), tab, tab],
        out_specs=pl.BlockSpec((CHUNK, MIX_W), lambda b, i: (b * n + i, 0)),
        out_shape=jax.ShapeDtypeStruct((batch * seq, MIX_W), BF16),
        scratch_shapes=[pltpu.VMEM((N_HEADS, HEAD_DIM, HEAD_DIM), F32)],
        compiler_params=_cparams(("parallel", "arbitrary")),
        name="retention",
    )(p, p, p, p, cos, sin)


def _fox_prep_kernel(sm_ref, fb_ref, fcol_ref, frow_ref, carry_ref):
    @pl.when(pl.program_id(1) == 0)
    def _():
        carry_ref[...] = jnp.zeros_like(carry_ref)

    t = sm_ref[...] + fb_ref[...]
    lf = jnp.minimum(t, 0.0) - jnp.log1p(jnp.exp(-jnp.abs(t)))
    cs = _cumsum_lanes(lf.T) + carry_ref[...]
    carry_ref[...] = cs[:, LANES - 1:LANES]
    frow_ref[0, 0] = cs[SM_F:SM_F + 8, :]
    fcol_ref[...] = cs.T


def fox_prep(sm, fb_row, batch, seq):
    n = seq // CHUNK
    return pl.pallas_call(
        _fox_prep_kernel,
        grid=(batch, n),
        in_specs=[
            pl.BlockSpec((CHUNK, SM_W), lambda b, i: (b * n + i, 0)),
            pl.BlockSpec((1, SM_W), lambda b, i: (0, 0)),
        ],
        out_specs=[
            pl.BlockSpec((CHUNK, SM_W), lambda b, i: (b * n + i, 0)),
            pl.BlockSpec((1, 1, 8, CHUNK), lambda b, i: (b, i, 0, 0)),
        ],
        out_shape=[
            jax.ShapeDtypeStruct((batch * seq, SM_W), F32),
            jax.ShapeDtypeStruct((batch, n, 8, CHUNK), F32),
        ],
        scratch_shapes=[pltpu.VMEM((SM_W, 1), F32)],
        compiler_params=_cparams(("parallel", "arbitrary")),
        name="fox_prep",
    )(sm, fb_row)


FOX_T = 256


def _fox_kernel(q_ref, k_ref, v_ref, fcol_ref, frow_ref, qg_ref, kg_ref, o_ref,
                kn_ref, vt_ref, fb_ref, qt_ref, acc_ref):
    i = pl.program_id(1)
    t = FOX_T
    nkc = vt_ref.shape[0]
    sub = t // CHUNK

    @pl.when(i == 0)
    def _():
        for h in range(N_HEADS):
            sl = slice(h * HEAD_DIM, (h + 1) * HEAD_DIM)
            kn_ref[:, sl] = _rms(k_ref[:, sl].astype(F32), kg_ref[...]).astype(BF16)
            fb_ref[h] = jnp.broadcast_to(fcol_ref[:, SM_F + h:SM_F + h + 1], fb_ref.shape[1:])
            for j in range(nkc):
                for c in range(sub):
                    rows = slice(j * t + c * CHUNK, j * t + (c + 1) * CHUNK)
                    vt_ref[j, h, :, c * CHUNK:(c + 1) * CHUNK] = v_ref[rows, sl].astype(F32).T.astype(BF16)

    fqs = []
    for h in range(N_HEADS):
        sl = slice(h * HEAD_DIM, (h + 1) * HEAD_DIM)
        qn = _rms(q_ref[:, sl].astype(F32), qg_ref[...]) * (HEAD_DIM ** -0.5)
        qt_ref[h] = jnp.concatenate([qn[c * CHUNK:(c + 1) * CHUNK, :].T for c in range(sub)], axis=1).astype(BF16)
        fqs.append(jnp.concatenate([frow_ref[0, i * sub + c, h:h + 1, :] for c in range(sub)], axis=1))
        acc_ref[h] = jnp.zeros(acc_ref.shape[1:], F32)

    kofs = lax.broadcasted_iota(jnp.int32, (t, t), 0)
    qofs = lax.broadcasted_iota(jnp.int32, (t, t), 1)

    def body(diag, j, carry):
        ms, ls = carry
        start = pl.multiple_of(j * t, t)
        scores = [_dot(kn_ref[pl.ds(start, t), h * HEAD_DIM:(h + 1) * HEAD_DIM], qt_ref[h])
                  for h in range(N_HEADS)]
        new_ms, new_ls, ps, alphas = [], [], [], []
        for h in range(N_HEADS):
            fk = fb_ref[h, pl.ds(start, t), :]
            s = scores[h] + fqs[h] - jnp.concatenate([fk] * (t // LANES), axis=1)
            if diag:
                s = jnp.where(kofs <= qofs, s, NEG_BIG)
            m_new = jnp.maximum(ms[h], jnp.max(s, axis=0, keepdims=True))
            p = jnp.exp(s - m_new)
            alpha = jnp.exp(ms[h] - m_new)
            new_ls.append(alpha * ls[h] + jnp.sum(p, axis=0, keepdims=True))
            ps.append(p.astype(BF16))
            alphas.append(alpha)
            new_ms.append(m_new)
        for h in range(N_HEADS):
            acc_ref[h] = alphas[h] * acc_ref[h] + _dot(vt_ref[j, h], ps[h])
        return tuple(new_ms), tuple(new_ls)

    init = (tuple(jnp.full((1, t), NEG_BIG, F32) for _ in range(N_HEADS)),
            tuple(jnp.zeros((1, t), F32) for _ in range(N_HEADS)))
    carry = lax.fori_loop(0, i, functools.partial(body, False), init)
    _, ls = body(True, i, carry)
    for h in range(N_HEADS):
        out_t = acc_ref[h] / ls[h]
        for c in range(sub):
            o_ref[c * CHUNK:(c + 1) * CHUNK, h * HEAD_DIM:(h + 1) * HEAD_DIM] = (
                out_t[:, c * CHUNK:(c + 1) * CHUNK].T.astype(o_ref.dtype))


def fox_attention(p, fcol, frow, qg, kg, batch, seq):
    nq = seq // FOX_T
    base = COL_FOX // MIX_W
    return pl.pallas_call(
        _fox_kernel,
        grid=(batch, nq),
        in_specs=[
            pl.BlockSpec((FOX_T, MIX_W), lambda b, i: (b * nq + i, base)),
            pl.BlockSpec((seq, MIX_W), lambda b, i: (b, base + 1)),
            pl.BlockSpec((seq, MIX_W), lambda b, i: (b, base + 2)),
            pl.BlockSpec((seq, SM_W), lambda b, i: (b, 0)),
            pl.BlockSpec((1, seq // CHUNK, 8, CHUNK), lambda b, i: (b, 0, 0, 0)),
            pl.BlockSpec((1, HEAD_DIM), lambda b, i: (0, 0)),
            pl.BlockSpec((1, HEAD_DIM), lambda b, i: (0, 0)),
        ],
        out_specs=pl.BlockSpec((FOX_T, MIX_W), lambda b, i: (b * nq + i, 0)),
        out_shape=jax.ShapeDtypeStruct((batch * seq, MIX_W), BF16),
        scratch_shapes=[
            pltpu.VMEM((seq, MIX_W), BF16),
            pltpu.VMEM((nq, N_HEADS, HEAD_DIM, FOX_T), BF16),
            pltpu.VMEM((N_HEADS, seq, LANES), F32),
            pltpu.VMEM((N_HEADS, HEAD_DIM, FOX_T), BF16),
            pltpu.VMEM((N_HEADS, HEAD_DIM, FOX_T), F32),
        ],
        compiler_params=_cparams(("parallel", "arbitrary")),
        name="fox_attention",
    )(p, p, p, fcol, frow, qg, kg)


DSA_TQ = 128
DSA_TK = 256
BAND_W = 2 * DSA_TQ


def _t5_bucket(dist):
    max_exact = N_BUCKETS // 2
    d = jnp.maximum(dist, 0)
    log_ratio = jnp.log(jnp.maximum(d, 1).astype(F32) / max_exact) / math.log(MAX_DISTANCE / max_exact)
    large = jnp.minimum(max_exact + (log_ratio * (N_BUCKETS - max_exact)).astype(jnp.int32), N_BUCKETS - 1)
    return jnp.where(d < max_exact, d, large)


def _dsa_kernel(cq_ref, k_ref, v_ref, smq_ref, smk_ref, cqg_ref, wuq_ref, wqi_ref, qg_ref, kg_ref, rb_ref,
                o_ref, kn_ref, ki_ref, band_ref, sc_ref, qh_ref, qi_ref, m_ref, l_ref, acc_ref, *, topk):
    b = pl.program_id(0)
    i = pl.program_id(1)
    tq, tk = DSA_TQ, DSA_TK
    nkc = sc_ref.shape[0]
    seq = nkc * tk
    tiles = tk // tq
    nb = ((i + 1) * tq + tk - 1) // tk

    @pl.when(jnp.logical_and(b == 0, i == 0))
    def _():
        r = lax.broadcasted_iota(jnp.int32, (tq, BAND_W), 0)
        c = lax.broadcasted_iota(jnp.int32, (tq, BAND_W), 1)
        bucket = _t5_bucket(tq + r - c)
        for h in range(N_HEADS):
            far = rb_ref[N_BUCKETS - 1, h]
            acc = jnp.zeros((tq, BAND_W), F32)
            for bk in range(N_BUCKETS - 1):
                acc = jnp.where(bucket == bk, rb_ref[bk, h] - far, acc)
            band_ref[h] = acc

    @pl.when(i == 0)
    def _():
        kn_ref[...] = _rms(k_ref[...].astype(F32), kg_ref[...]).astype(BF16)
        ki_ref[...] = smk_ref[:, SM_IK:SM_IK + IDX_DIM].astype(BF16)

    cq = _rms(cq_ref[...].astype(F32), cqg_ref[...]).astype(BF16)
    qf = _dot(cq, wuq_ref[...])
    for h in range(N_HEADS):
        sl = slice(h * HEAD_DIM, (h + 1) * HEAD_DIM)
        qh_ref[h] = (_rms(qf[:, sl], qg_ref[...]) * (HEAD_DIM ** -0.5)).astype(BF16)
    q_idx = (_dot(cq, wqi_ref[...]) * (IDX_DIM ** -0.5)).astype(BF16)
    for h in range(IDX_HEADS):
        qi_ref[h] = q_idx[:, h * IDX_DIM:(h + 1) * IDX_DIM]

    qpos = lax.broadcasted_iota(jnp.int32, (tq, tk), 0) + i * tq
    col = lax.broadcasted_iota(jnp.int32, (tq, tk), 1)

    def score_body(j, _):
        start = pl.multiple_of(j * tk, tk)
        kj = ki_ref[pl.ds(start, tk), :]
        w_h = smq_ref[:, SM_IW:SM_IW + IDX_HEADS] * (IDX_HEADS ** -0.5)
        acc = jnp.zeros((tq, tk), F32)
        for h in range(IDX_HEADS):
            acc = acc + w_h[:, h:h + 1] * jnp.maximum(_dot_nt(qi_ref[h], kj), 0.0)
        sc_ref[j] = jnp.where(col + start <= qpos, acc, -jnp.inf)
        return 0

    lax.fori_loop(0, nb, score_body, 0)

    def over_chunks(fn, init):
        acc = init
        for j in range(nkc):
            acc = lax.cond(j < nb, functools.partial(fn, j), lambda a: a, acc)
        return acc

    def lane_tiles(x):
        return [x[:, t * tq:(t + 1) * tq] for t in range(tiles)]

    def row_total(x):
        return jnp.sum(x, axis=1, keepdims=True)

    def search():
        kf = float(topk)

        def max_fn(j, a):
            for x in lane_tiles(sc_ref[j]):
                a = jnp.maximum(a, x)
            return a

        def min_fn(j, a):
            for x in lane_tiles(sc_ref[j]):
                a = jnp.minimum(a, jnp.where(x == -jnp.inf, jnp.inf, x))
            return a

        smax = jnp.max(over_chunks(max_fn, jnp.full((tq, tq), -jnp.inf, F32)), axis=1, keepdims=True)
        smin = jnp.min(over_chunks(min_fn, jnp.full((tq, tq), jnp.inf, F32)), axis=1, keepdims=True)

        def count_ge(t):
            tb = jnp.broadcast_to(t, (tq, tq))

            def fn(j, a):
                for x in lane_tiles(sc_ref[j]):
                    a = a + jnp.where(x >= tb, 1.0, 0.0)
                return a

            return row_total(over_chunks(fn, jnp.zeros((tq, tq), F32)))

        def midpoint(lo, hi):
            return jnp.where(hi == jnp.inf, smax, 0.5 * (lo + hi))

        def undecided(lo, hi, c_lo, mid):
            return jnp.logical_and(c_lo != kf, jnp.logical_and(mid > lo, mid < hi))

        def cond(carry):
            return jnp.logical_and(carry[0] < 400, carry[1] > 0.0)

        def body(carry):
            it, _, lo, hi, c_lo, c_hi, mid = carry
            upd = undecided(lo, hi, c_lo, mid)
            cnt = count_ge(mid)
            up = jnp.logical_and(upd, cnt >= kf)
            dn = jnp.logical_and(upd, cnt < kf)
            lo = jnp.where(up, mid, lo)
            c_lo = jnp.where(up, cnt, c_lo)
            hi = jnp.where(dn, mid, hi)
            c_hi = jnp.where(dn, cnt, c_hi)
            mid = midpoint(lo, hi)
            active = jnp.max(jnp.where(undecided(lo, hi, c_lo, mid), 1.0, 0.0))
            return it + 1, active, lo, hi, c_lo, c_hi, mid

        lo0 = smin
        hi0 = jnp.full((tq, 1), jnp.inf, F32)
        c_lo0 = count_ge(lo0)
        c_hi0 = jnp.zeros((tq, 1), F32)
        mid0 = midpoint(lo0, hi0)
        act0 = jnp.max(jnp.where(undecided(lo0, hi0, c_lo0, mid0), 1.0, 0.0))
        _, _, lo, hi, c_lo, c_hi, _ = lax.while_loop(
            cond, body, (jnp.int32(0), act0, lo0, hi0, c_lo0, c_hi0, mid0))

        def tie_search():
            need = kf - c_hi
            lo_b = jnp.broadcast_to(lo, (tq, tq))
            hi_b = jnp.broadcast_to(hi, (tq, tq))
            lane = lax.broadcasted_iota(jnp.int32, (tq, tq), 1)

            def tie_body(_, carry):
                jlo, jhi = carry
                jm = (jlo + jhi) // 2
                jm_b = jnp.broadcast_to(jm, (tq, tq))

                def fn(j, a):
                    for t, x in enumerate(lane_tiles(sc_ref[j])):
                        hit = jnp.logical_and(jnp.logical_and(x >= lo_b, x < hi_b), lane + (j * tk + t * tq) <= jm_b)
                        a = a + jnp.where(hit, 1.0, 0.0)
                    return a

                ok = row_total(over_chunks(fn, jnp.zeros((tq, tq), F32))) >= need
                return jnp.where(ok, jlo, jm), jnp.where(ok, jm, jhi)

            n_bits = int(math.ceil(math.log2(seq))) + 1
            _, jmax = lax.fori_loop(0, n_bits, tie_body,
                                    (jnp.full((tq, 1), -1, jnp.int32), jnp.full((tq, 1), seq - 1, jnp.int32)))
            return jmax

        any_tie = jnp.max(jnp.where(c_lo != kf, 1.0, 0.0)) > 0.0
        jmax = lax.cond(any_tie, tie_search, lambda: jnp.full((tq, 1), seq - 1, jnp.int32))
        return lo, hi, jmax

    def keep_all():
        return (jnp.full((tq, 1), -jnp.inf, F32), jnp.full((tq, 1), jnp.inf, F32),
                jnp.full((tq, 1), seq - 1, jnp.int32))

    lo, hi, jmax = lax.cond((i + 1) * tq > topk, search, keep_all)

    m_ref[...] = jnp.full(m_ref.shape, NEG_BIG, F32)
    l_ref[...] = jnp.zeros(l_ref.shape, F32)
    acc_ref[...] = jnp.zeros(acc_ref.shape, F32)

    def attend_body(j, _):
        start = pl.multiple_of(j * tk, tk)
        ks = kn_ref[pl.ds(start, tk), :]
        vs = v_ref[pl.ds(start, tk), :]
        sc = sc_ref[j]
        kpos = col + start
        keep = jnp.logical_or(sc >= hi, jnp.logical_and(sc >= lo, kpos <= jmax))
        keep = jnp.logical_and(keep, kpos <= qpos)
        for h in range(N_HEADS):
            bias = jnp.concatenate(
                [jnp.where(j * tiles + t == i, band_ref[h, :, tq:2 * tq],
                           jnp.where(j * tiles + t == i - 1, band_ref[h, :, 0:tq], 0.0))
                 for t in range(tiles)], axis=1)
            s = jnp.where(keep, _dot_nt(qh_ref[h], ks) + bias, NEG_BIG)
            m_old = m_ref[h]
            m_new = jnp.maximum(m_old, jnp.max(s, axis=1, keepdims=True))
            p = jnp.exp(s - m_new)
            alpha = jnp.exp(m_old - m_new)
            l_ref[h] = alpha * l_ref[h] + jnp.sum(p, axis=1, keepdims=True)
            acc_ref[h] = alpha * acc_ref[h] + _dot(p.astype(BF16), vs)
            m_ref[h] = m_new
        return 0

    lax.fori_loop(0, nb, attend_body, 0)
    for h in range(N_HEADS):
        o_ref[:, h * HEAD_DIM:(h + 1) * HEAD_DIM] = (acc_ref[h] / l_ref[h]).astype(o_ref.dtype)


SUBLANES = 8
COUNT_ROWS = 64
BISECT_FIXED_STEPS = 16


def _fold_rows(x, op):
    return op(x.reshape(x.shape[0] // SUBLANES, SUBLANES, x.shape[1]), axis=0)


def _dsa_t_kernel(cq_ref, k_ref, v_ref, smq_ref, smk_ref, cqg_ref, wuq_ref, wqi_ref, qg_ref, kg_ref, rb_ref,
                  o_ref, kn_ref, ki_ref, vt_ref, band_ref, sc_ref, qt_ref, xi_ref, acc_ref, s_ref, *, topk):
    b = pl.program_id(0)
    i = pl.program_id(1)
    tq, tk = DSA_TQ, DSA_TK
    nkc = sc_ref.shape[0]
    seq = nkc * tk
    tiles = tk // tq
    nb = ((i + 1) * tq + tk - 1) // tk

    @pl.when(jnp.logical_and(b == 0, i == 0))
    def _():
        c = lax.broadcasted_iota(jnp.int32, (BAND_W, tq), 0)
        r = lax.broadcasted_iota(jnp.int32, (BAND_W, tq), 1)
        bucket = _t5_bucket(tq + r - c)
        for h in range(N_HEADS):
            far = rb_ref[N_BUCKETS - 1, h]
            acc = jnp.zeros((BAND_W, tq), F32)
            for bk in range(N_BUCKETS - 1):
                acc = jnp.where(bucket == bk, rb_ref[bk, h] - far, acc)
            band_ref[h] = acc

    @pl.when(i == 0)
    def _():
        kn_ref[...] = _rms(k_ref[...].astype(F32), kg_ref[...]).astype(BF16)
        ki_ref[...] = smk_ref[:, SM_IK:SM_IK + IDX_DIM].astype(BF16)
        for j in range(nkc):
            for t in range(tiles):
                rows = slice(j * tk + t * tq, j * tk + (t + 1) * tq)
                vt_ref[j, :, t * tq:(t + 1) * tq] = v_ref[rows, :].astype(F32).T.astype(BF16)

    cq_t = _rms(cq_ref[...].astype(F32), cqg_ref[...]).T.astype(BF16)
    q_t = _dot(wuq_ref[...], cq_t)
    g_col = jnp.broadcast_to(qg_ref[...], (HEAD_DIM, tq))
    for h in range(N_HEADS):
        x = q_t[h * HEAD_DIM:(h + 1) * HEAD_DIM, :]
        inv = lax.rsqrt(jnp.mean(x * x, axis=0, keepdims=True) + EPS)
        qt_ref[:, h * tq:(h + 1) * tq] = (x * inv * g_col * (HEAD_DIM ** -0.5)).astype(BF16)
    qi_t = (_dot(wqi_ref[...], cq_t) * (IDX_DIM ** -0.5)).astype(BF16)
    for h in range(IDX_HEADS):
        xi_ref[:, h * tq:(h + 1) * tq] = qi_t[h * IDX_DIM:(h + 1) * IDX_DIM, :]
    w_rows = smq_ref[...].T[SM_IW:SM_IW + IDX_HEADS, :] * (IDX_HEADS ** -0.5)

    kofs = lax.broadcasted_iota(jnp.int32, (tk, tq), 0)
    qpos = lax.broadcasted_iota(jnp.int32, (tk, tq), 1) + i * tq

    def score_body(j, _):
        start = pl.multiple_of(j * tk, tk)
        kj = ki_ref[pl.ds(start, tk), :]
        acc = jnp.zeros((tk, tq), F32)
        for h2 in range(IDX_HEADS // 2):
            r = _dot(kj, xi_ref[:, 2 * h2 * tq:(2 * h2 + 2) * tq])
            for h in (2 * h2, 2 * h2 + 1):
                acc = acc + w_rows[h:h + 1, :] * jnp.maximum(r[:, (h - 2 * h2) * tq:(h - 2 * h2 + 1) * tq], 0.0)
        sc_ref[j] = jnp.where(kofs + start <= qpos, acc, -jnp.inf)
        return 0

    lax.fori_loop(0, nb, score_body, 0)

    @pl.when(nb % 2 == 1)
    def _():
        sc_ref[jnp.minimum(nb, nkc - 1)] = jnp.full((tk, tq), -jnp.inf, F32)

    def over_chunks(fn, init):
        acc = init
        for j in range(nkc):
            acc = lax.cond(j < nb, functools.partial(fn, j), lambda a: a, acc)
        return acc

    def count_where(pred_fn):
        def walk(n_chunks):
            def run():
                a = jnp.zeros((COUNT_ROWS, tq), F32)
                for j in range(n_chunks):
                    hit = jnp.where(pred_fn(j, sc_ref[j]), 1.0, 0.0)
                    a = a + jnp.sum(hit.reshape(tk // COUNT_ROWS, COUNT_ROWS, tq), axis=0)
                return a
            return run

        extents = sorted({min(n, nkc) for n in range(2, nkc + 2, 2)})
        a = lax.switch((nb - 1) // 2, [walk(n) for n in extents])
        return jnp.sum(a, axis=0, keepdims=True)

    def search():
        kf = float(topk)
        smax = jnp.max(over_chunks(lambda j, a: jnp.maximum(a, _fold_rows(sc_ref[j], jnp.max)),
                                   jnp.full((SUBLANES, tq), -jnp.inf, F32)), axis=0, keepdims=True)
        smin = jnp.min(over_chunks(
            lambda j, a: jnp.minimum(a, _fold_rows(jnp.where(sc_ref[j] == -jnp.inf, jnp.inf, sc_ref[j]), jnp.min)),
            jnp.full((SUBLANES, tq), jnp.inf, F32)), axis=0, keepdims=True)

        def count_ge(t):
            return count_where(lambda j, x: x >= t)

        def midpoint(lo, hi):
            return jnp.where(hi == jnp.inf, smax, 0.5 * (lo + hi))

        def undecided(lo, hi, c_lo, mid):
            return jnp.logical_and(c_lo != kf, jnp.logical_and(mid > lo, mid < hi))

        def step(state):
            lo, hi, c_lo, c_hi, mid = state
            upd = undecided(lo, hi, c_lo, mid)
            cnt = count_ge(mid)
            up = jnp.logical_and(upd, cnt >= kf)
            dn = jnp.logical_and(upd, cnt < kf)
            lo = jnp.where(up, mid, lo)
            c_lo = jnp.where(up, cnt, c_lo)
            hi = jnp.where(dn, mid, hi)
            c_hi = jnp.where(dn, cnt, c_hi)
            return lo, hi, c_lo, c_hi, midpoint(lo, hi)

        def any_undecided(state):
            lo, hi, c_lo, _, mid = state
            return jnp.max(jnp.where(undecided(lo, hi, c_lo, mid), 1.0, 0.0))

        def cond(carry):
            return jnp.logical_and(carry[0] < 200, carry[1] > 0.0)

        def body(carry):
            state = step(step(carry[2]))
            return carry[0] + 1, any_undecided(state), state

        lo0 = smin
        hi0 = jnp.full((1, tq), jnp.inf, F32)
        c_lo0 = (lax.broadcasted_iota(jnp.int32, (1, tq), 1) + (i * tq + 1)).astype(F32)
        c_hi0 = jnp.zeros((1, tq), F32)
        state = (lo0, hi0, c_lo0, c_hi0, midpoint(lo0, hi0))
        state = lax.fori_loop(0, BISECT_FIXED_STEPS, lambda _, s: step(s), state)
        _, _, (lo, hi, c_lo, c_hi, _) = lax.while_loop(
            cond, body, (jnp.int32(0), any_undecided(state), state))

        def tie_search():
            need = kf - c_hi

            def tie_body(_, carry):
                jlo, jhi = carry
                jm = (jlo + jhi) // 2
                cnt = count_where(lambda j, x: jnp.logical_and(jnp.logical_and(x >= lo, x < hi),
                                                               kofs + j * tk <= jm))
                ok = cnt >= need
                return jnp.where(ok, jlo, jm), jnp.where(ok, jm, jhi)

            n_bits = int(math.ceil(math.log2(seq))) + 1
            _, jmax = lax.fori_loop(0, n_bits, tie_body,
                                    (jnp.full((1, tq), -1, jnp.int32), jnp.full((1, tq), seq - 1, jnp.int32)))
            return jmax

        any_tie = jnp.max(jnp.where(c_lo != kf, 1.0, 0.0)) > 0.0
        jmax = lax.cond(any_tie, tie_search, lambda: jnp.full((1, tq), seq - 1, jnp.int32))
        return lo, hi, jmax

    def keep_all():
        return (jnp.full((1, tq), -jnp.inf, F32), jnp.full((1, tq), jnp.inf, F32),
                jnp.full((1, tq), seq - 1, jnp.int32))

    lo, hi, jmax = lax.cond((i + 1) * tq > topk, search, keep_all)

    def attend(n_chunks):
        def run():
            mx = [jnp.full((SUBLANES, tq), NEG_BIG, F32) for _ in range(N_HEADS)]
            for j in range(n_chunks):
                near = j >= n_chunks - 3
                s_all = _dot(kn_ref[j * tk:(j + 1) * tk, :], qt_ref[...])
                sc = sc_ref[j]
                kpos = kofs + j * tk
                keep = jnp.logical_or(sc >= hi, jnp.logical_and(sc >= lo, kpos <= jmax))
                if near:
                    keep = jnp.logical_and(keep, kpos <= qpos)
                for h in range(N_HEADS):
                    hs = slice(h * tq, (h + 1) * tq)
                    s = s_all[:, hs]
                    if near:
                        s = s + jnp.concatenate(
                            [jnp.where(j * tiles + t == i, band_ref[h, tq:2 * tq, :],
                                       jnp.where(j * tiles + t == i - 1, band_ref[h, 0:tq, :], 0.0))
                             for t in range(tiles)], axis=0)
                    s = jnp.where(keep, s, NEG_BIG)
                    s_ref[j, :, hs] = s
                    mx[h] = jnp.maximum(mx[h], _fold_rows(s, jnp.max))
            m_all = jnp.concatenate([jnp.max(x, axis=0, keepdims=True) for x in mx], axis=1)
            acc = jnp.zeros(acc_ref.shape, F32)
            l_part = jnp.zeros((SUBLANES, N_HEADS * tq), F32)
            for j in range(n_chunks):
                p = jnp.exp(s_ref[j] - m_all)
                l_part = l_part + _fold_rows(p, jnp.sum)
                acc = acc + _dot(vt_ref[j], p.astype(BF16))
            acc_ref[...] = acc
            return jnp.sum(l_part, axis=0, keepdims=True)
        return run

    extents = sorted({min(n, nkc) for n in range(2, nkc + 2, 2)})
    l_all = lax.switch((nb - 1) // 2, [attend(n) for n in extents])
    for h in range(N_HEADS):
        hs = slice(h * tq, (h + 1) * tq)
        out_t = acc_ref[:, hs] / l_all[:, hs]
        o_ref[:, h * HEAD_DIM:(h + 1) * HEAD_DIM] = out_t.T.astype(o_ref.dtype)


def dsa_attention(p, sm, cqg, wuq_t, wqi_t, qg_col, kg, rel_bias, batch, seq):
    nq = seq // DSA_TQ
    nkc = seq // DSA_TK
    topk = min(DSA_TOPK, seq // 4)
    kern = functools.partial(_dsa_t_kernel, topk=topk)
    return pl.pallas_call(
        kern,
        grid=(batch, nq),
        in_specs=[
            pl.BlockSpec((DSA_TQ, DSA_Q_RANK), lambda b, i: (b * nq + i, COL_CQ // DSA_Q_RANK)),
            pl.BlockSpec((seq, HEAD_DIM), lambda b, i: (b, COL_DK // HEAD_DIM)),
            pl.BlockSpec((seq, HEAD_DIM), lambda b, i: (b, COL_DV // HEAD_DIM)),
            pl.BlockSpec((DSA_TQ, SM_W), lambda b, i: (b * nq + i, 0)),
            pl.BlockSpec((seq, SM_W), lambda b, i: (b, 0)),
            pl.BlockSpec((1, DSA_Q_RANK), lambda b, i: (0, 0)),
            pl.BlockSpec((N_HEADS * HEAD_DIM, DSA_Q_RANK), lambda b, i: (0, 0)),
            pl.BlockSpec((IDX_HEADS * IDX_DIM, DSA_Q_RANK), lambda b, i: (0, 0)),
            pl.BlockSpec((HEAD_DIM, 1), lambda b, i: (0, 0)),
            pl.BlockSpec((1, HEAD_DIM), lambda b, i: (0, 0)),
            pl.BlockSpec(memory_space=pltpu.SMEM),
        ],
        out_specs=pl.BlockSpec((DSA_TQ, MIX_W), lambda b, i: (b * nq + i, 0)),
        out_shape=jax.ShapeDtypeStruct((batch * seq, MIX_W), BF16),
        scratch_shapes=[
            pltpu.VMEM((seq, HEAD_DIM), BF16),
            pltpu.VMEM((seq, IDX_DIM), BF16),
            pltpu.VMEM((nkc, HEAD_DIM, DSA_TK), BF16),
            pltpu.VMEM((N_HEADS, BAND_W, DSA_TQ), F32),
            pltpu.VMEM((nkc, DSA_TK, DSA_TQ), F32),
            pltpu.VMEM((HEAD_DIM, N_HEADS * DSA_TQ), BF16),
            pltpu.VMEM((IDX_DIM, IDX_HEADS * DSA_TQ), BF16),
            pltpu.VMEM((HEAD_DIM, N_HEADS * DSA_TQ), F32),
            pltpu.VMEM((nkc, DSA_TK, N_HEADS * DSA_TQ), F32),
        ],
        compiler_params=_cparams(("arbitrary", "arbitrary")),
        name="dsa_attention",
    )(p, p, p, sm, sm, cqg, wuq_t, wqi_t, qg_col, kg, rel_bias)


CONV_PAD = 8


def _causal_conv(x_ref, xp_ref, first, w_ref, b_ref, ext_ref):
    ext_ref[0:CONV_PAD, :] = xp_ref[CHUNK - CONV_PAD:CHUNK, :].astype(F32) * first
    ext_ref[CONV_PAD:CONV_PAD + CHUNK, :] = x_ref[...].astype(F32)
    acc = b_ref[...] + ext_ref[CONV_PAD:CONV_PAD + CHUNK, :] * w_ref[SSD_CONV - 1:SSD_CONV, :]
    for d in range(1, SSD_CONV):
        acc = acc + ext_ref[CONV_PAD - d:CONV_PAD - d + CHUNK, :] * w_ref[SSD_CONV - 1 - d:SSD_CONV - d, :]
    return _silu(acc)


def _ssd_kernel(z_ref, xs_ref, bc_ref, xsp_ref, bcp_ref, sm_ref, cwx_ref, cbx_ref, cwb_ref, cbb_ref,
                dtb_ref, alog_ref, dsk_ref, ng_ref, o_ref, prev_ref, y_ref, extx_ref, extb_ref):
    c = CHUNK
    n = pl.program_id(1)

    @pl.when(n == 0)
    def _():
        prev_ref[...] = jnp.zeros_like(prev_ref)

    first = (n > 0).astype(F32)
    xs = _causal_conv(xs_ref, xsp_ref, first, cwx_ref, cbx_ref, extx_ref)
    bc = _causal_conv(bc_ref, bcp_ref, first, cwb_ref, cbb_ref, extb_ref)

    dt_t = _softplus(sm_ref[...].T + dtb_ref[...])
    cs_t = _cumsum_lanes(dt_t * (-jnp.exp(alog_ref[...])))
    cs = cs_t.T
    ii = lax.broadcasted_iota(jnp.int32, (c, c), 0)
    jj = lax.broadcasted_iota(jnp.int32, (c, c), 1)
    tril = ii >= jj
    pair_w = 2 * SSD_HEAD_DIM
    first_head = jj < SSD_HEAD_DIM
    first_head_row = lax.broadcasted_iota(jnp.int32, (1, pair_w), 1) < SSD_HEAD_DIM
    gn = SSD_GROUPS * SSD_STATE
    hpg = SSD_HEADS // SSD_GROUPS
    for g in range(SSD_GROUPS):
        bg = bc[:, g * SSD_STATE:(g + 1) * SSD_STATE]
        cg = bc[:, gn + g * SSD_STATE:gn + (g + 1) * SSD_STATE].astype(BF16)
        cb = _dot_nt(cg, bg.astype(BF16))
        bg_t = bg.T
        y_off = _dot(cg, prev_ref[g].astype(BF16))
        for pr in range(hpg // 2):
            cols = slice((g * hpg + 2 * pr) * SSD_HEAD_DIM, (g * hpg + 2 * pr + 2) * SSD_HEAD_DIM)
            rcols = slice(2 * pr * SSD_HEAD_DIM, (2 * pr + 2) * SSD_HEAD_DIM)
            x_pair = xs[:, cols]
            x_bf = x_pair.astype(BF16)
            y_diag, st, exp_a, exp_last = [], [], [], []
            for k in range(2):
                row = SM_DT + g * hpg + 2 * pr + k
                a_row = cs_t[row:row + 1, :]
                dt_row = dt_t[row:row + 1, :]
                last = cs_t[row:row + 1, c - 1:c]
                a_col = jnp.broadcast_to(cs[:, row:row + 1], (c, c))
                seg = jnp.where(tril, jnp.exp(jnp.where(tril, a_col - a_row, 0.0)), 0.0)
                y_diag.append(_dot((cb * seg * dt_row).astype(BF16), x_bf))
                st.append(_dot((bg_t * (dt_row * jnp.exp(last - a_row))).astype(BF16), x_bf))
                exp_a.append(jnp.exp(a_col))
                exp_last.append(jnp.exp(last))
            y_ref[:, cols] = (jnp.where(first_head, y_diag[0], y_diag[1])
                              + y_off[:, rcols] * jnp.where(first_head, exp_a[0], exp_a[1])
                              + dsk_ref[:, cols] * x_pair)
            prev_ref[g, :, rcols] = (jnp.where(first_head_row, exp_last[0], exp_last[1]) * prev_ref[g, :, rcols]
                                     + jnp.where(first_head, st[0], st[1]))
    gated = y_ref[...] * _silu(z_ref[...].astype(F32))
    gw = SSD_INNER // SSD_GROUPS
    for g in range(SSD_GROUPS):
        sl = slice(g * gw, (g + 1) * gw)
        o_ref[:, sl] = _rms(gated[:, sl], ng_ref[:, sl]).astype(o_ref.dtype)


def ssd_mixer(p, sm, cw, cb, dtb_col, alog_col, dskip_row, ng, batch, seq):
    n = seq // CHUNK
    bcw = 2 * SSD_GROUPS * SSD_STATE

    def cur(width, colbase):
        return pl.BlockSpec((CHUNK, width), lambda b, i: (b * n + i, colbase // width))

    def prv(width, colbase):
        return pl.BlockSpec((CHUNK, width), lambda b, i: (b * n + jnp.maximum(i - 1, 0), colbase // width))

    def const(shape):
        return pl.BlockSpec(shape, lambda b, i: (0, 0))

    return pl.pallas_call(
        _ssd_kernel,
        grid=(batch, n),
        in_specs=[
            cur(SSD_INNER, COL_Z), cur(SSD_INNER, COL_XS), cur(bcw, COL_BC),
            prv(SSD_INNER, COL_XS), prv(bcw, COL_BC),
            pl.BlockSpec((CHUNK, SM_W), lambda b, i: (b * n + i, 0)),
            const((SSD_CONV, SSD_INNER)), const((1, SSD_INNER)),
            const((SSD_CONV, bcw)), const((1, bcw)),
            const((SM_W, 1)), const((SM_W, 1)),
            const((1, SSD_INNER)), const((1, SSD_INNER)),
        ],
        out_specs=pl.BlockSpec((CHUNK, SSD_INNER), lambda b, i: (b * n + i, 0)),
        out_shape=jax.ShapeDtypeStruct((batch * seq, SSD_INNER), BF16),
        scratch_shapes=[
            pltpu.VMEM((SSD_GROUPS, SSD_STATE, SSD_INNER // SSD_GROUPS), F32),
            pltpu.VMEM((CHUNK, SSD_INNER), F32),
            pltpu.VMEM((CONV_PAD + CHUNK, SSD_INNER), F32),
            pltpu.VMEM((CONV_PAD + CHUNK, bcw), F32),
        ],
        compiler_params=_cparams(("parallel", "arbitrary")),
        name="ssd_mixer",
    )(p, p, p, p, p, sm, cw[:, :SSD_INNER], cb[:, :SSD_INNER], cw[:, SSD_INNER:], cb[:, SSD_INNER:],
      dtb_col, alog_col, dskip_row, ng)


MERGE_TM = 256


def _merge_kernel(x_ref, gl_ref, gb_ref, oret_ref, ofox_ref, odsa_ref, ossd_ref, wbr_ref, wout_ref, o_ref):
    branches = (oret_ref, ofox_ref, odsa_ref, ossd_ref)
    merged = None
    row0 = 0
    for bi, br in enumerate(branches):
        width = br.shape[1]
        sl = slice(bi * D_MODEL, (bi + 1) * D_MODEL)
        gate = 1.0 / (1.0 + jnp.exp(-(gl_ref[:, sl].astype(F32) + gb_ref[:, sl])))
        term = gate * _dot(br[...], wbr_ref[row0:row0 + width, :])
        merged = term if merged is None else merged + term
        row0 += width
    o_ref[...] = x_ref[...] + _dot(merged.astype(BF16), wout_ref[...])


def merge_project(x, p, gate_b, o_ret, o_fox, o_dsa, o_ssd, w_br, w_out, layer):
    m = x.shape[0]
    tm = MERGE_TM

    def rows(width):
        return pl.BlockSpec((tm, width), lambda i: (i, 0))

    def const(shape):
        return pl.BlockSpec(shape, lambda i: (0, 0), pipeline_mode=pl.Buffered(1))

    def stacked(w):
        return pl.BlockSpec((None,) + w.shape[1:], lambda i: (layer, 0, 0), pipeline_mode=pl.Buffered(1))

    return pl.pallas_call(
        _merge_kernel,
        grid=(m // tm,),
        in_specs=[
            rows(D_MODEL), rows(N_BRANCH * D_MODEL), const((1, N_BRANCH * D_MODEL)),
            rows(MIX_W), rows(MIX_W), rows(MIX_W), rows(SSD_INNER),
            stacked(w_br), stacked(w_out),
        ],
        out_specs=rows(D_MODEL),
        out_shape=jax.ShapeDtypeStruct(x.shape, x.dtype),
        compiler_params=_cparams(("parallel",)),
        name="merge_project",
    )(x, p, gate_b, o_ret, o_fox, o_dsa, o_ssd, w_br, w_out)


FFN_TM = 1024
FFN_TF = 512


def _ffn_kernel(x_ref, g_ref, w1_ref, w2_ref, o_ref, h_ref):
    @pl.when(pl.program_id(1) == 0)
    def _():
        h_ref[...] = _rms(x_ref[...], g_ref[...]).astype(BF16)
        o_ref[...] = x_ref[...]

    a = jnp.maximum(_dot(h_ref[...], w1_ref[...]), 0.0)
    o_ref[...] += _dot((a * a).astype(BF16), w2_ref[...])


def ffn(x, g, w1, w2, layer):
    m, d = x.shape
    dff = w1.shape[2]
    tm, tf = min(FFN_TM, m), FFN_TF
    return pl.pallas_call(
        _ffn_kernel,
        grid=(m // tm, dff // tf),
        in_specs=[
            pl.BlockSpec((tm, d), lambda i, f: (i, 0), pipeline_mode=pl.Buffered(1)),
            pl.BlockSpec((1, d), lambda i, f: (0, 0)),
            pl.BlockSpec((None, d, tf), lambda i, f: (layer, 0, f)),
            pl.BlockSpec((None, tf, d), lambda i, f: (layer, f, 0)),
        ],
        out_specs=pl.BlockSpec((tm, d), lambda i, f: (i, 0)),
        out_shape=jax.ShapeDtypeStruct(x.shape, x.dtype),
        scratch_shapes=[pltpu.VMEM((tm, d), BF16)],
        compiler_params=_cparams(("parallel", "arbitrary")),
        name="ffn",
    )(x, g, w1, w2)


SRC_RET = 0
SRC_FOX = SRC_RET + 4 * MIX_W
SRC_FF = SRC_FOX + 3 * MIX_W
SRC_CQ = SRC_FF + N_HEADS
SRC_DK = SRC_CQ + DSA_Q_RANK
SRC_IK = SRC_DK + 2 * HEAD_DIM
SRC_IW = SRC_IK + IDX_DIM
SRC_Z = SRC_IW + IDX_HEADS
SRC_DT = SRC_Z + 2 * SSD_INNER + 2 * SSD_GROUPS * SSD_STATE
SRC_GATE = SRC_DT + SSD_HEADS
IN_TOTAL = SRC_GATE + N_BRANCH * D_MODEL
MAIN_RUNS = ((COL_GATE, SRC_GATE), (COL_RET, SRC_RET), (COL_Z, SRC_Z), (COL_CQ, SRC_CQ),
             (COL_FOX, SRC_FOX), (COL_DK, SRC_DK))
RELAYOUT_W = 512
RELAYOUT_TILES = RELAYOUT_W // LANES


def _relayout_tables():
    starts, shifts = [], []
    for blk in range(N_MAIN // RELAYOUT_W):
        o = blk * RELAYOUT_W
        dst, src = [r for r in MAIN_RUNS if r[0] <= o][-1]
        col = src + (o - dst)
        starts.append(col // LANES)
        shifts.append(col % LANES)
    return jnp.asarray(starts, jnp.int32), jnp.asarray(shifts, jnp.int32)


def _relayout_kernel(start_ref, shift_ref, *refs):
    del start_ref
    tiles, o_ref = refs[:-1], refs[-1]
    shift = shift_ref[pl.program_id(1)]
    amount = lax.rem(LANES - shift, LANES)
    lane = lax.broadcasted_iota(jnp.int32, tiles[0].shape, 1)
    rolled = [pltpu.roll(t[...], amount, 1) for t in tiles]
    for k in range(RELAYOUT_TILES):
        piece = jnp.where(lane < LANES - shift, rolled[k], rolled[k + 1])
        o_ref[:, k * LANES:(k + 1) * LANES] = piece.astype(o_ref.dtype)


def relayout_main(w_in):
    depth, d, n_src = w_in.shape
    last = (n_src - 1) // LANES
    starts, shifts = _relayout_tables()

    def tile(k):
        return pl.BlockSpec((None, d, LANES), lambda l, b, st, sh: (l, 0, jnp.minimum(st[b] + k, last)))

    return pl.pallas_call(
        _relayout_kernel,
        grid_spec=pltpu.PrefetchScalarGridSpec(
            num_scalar_prefetch=2,
            grid=(depth, N_MAIN // RELAYOUT_W),
            in_specs=[tile(k) for k in range(RELAYOUT_TILES + 1)],
            out_specs=pl.BlockSpec((None, d, RELAYOUT_W), lambda l, b, st, sh: (l, 0, b)),
        ),
        out_shape=jax.ShapeDtypeStruct((depth, d, N_MAIN), BF16),
        compiler_params=_cparams(("parallel", "arbitrary")),
        name="relayout_main",
    )(starts, shifts, *([w_in] * (RELAYOUT_TILES + 1)))


SMALL_PIECES = ((SRC_DT, SM_DT, SSD_HEADS), (SRC_FF, SM_F, N_HEADS), (SRC_IW, SM_IW, IDX_HEADS),
                (SRC_IK, SM_IK, IDX_DIM))


def _relayout_small_kernel(*refs):
    tiles, o_ref = refs[:-1], refs[-1]
    lane = lax.broadcasted_iota(jnp.int32, o_ref.shape, 1)
    out = jnp.zeros(o_ref.shape, F32)
    for t, (src, dst, width) in zip(tiles, SMALL_PIECES):
        moved = pltpu.roll(t[...], (dst - src % LANES) % LANES, 1)
        out = jnp.where(jnp.logical_and(lane >= dst, lane < dst + width), moved, out)
    o_ref[...] = out.astype(o_ref.dtype)


def relayout_small(w_in):
    depth, d, _ = w_in.shape
    for src, _, width in SMALL_PIECES:
        assert src // LANES == (src + width - 1) // LANES

    def tile(src):
        return pl.BlockSpec((None, d, LANES), lambda l: (l, 0, src // LANES))

    return pl.pallas_call(
        _relayout_small_kernel,
        grid=(depth,),
        in_specs=[tile(src) for src, _, _ in SMALL_PIECES],
        out_specs=pl.BlockSpec((None, d, SM_W), lambda l: (l, 0, 0)),
        out_shape=jax.ShapeDtypeStruct((depth, d, SM_W), BF16),
        compiler_params=_cparams(("parallel",)),
        name="relayout_small",
    )(*([w_in] * len(SMALL_PIECES)))


def _layout_in_proj_t(w_in):
    w_t = jnp.swapaxes(w_in, 1, 2)
    depth, _, d = w_t.shape
    ends = [dst for dst, _ in MAIN_RUNS[1:]] + [N_MAIN_USED]
    main = [w_t[:, src:src + (end - dst), :] for (dst, src), end in zip(MAIN_RUNS, ends)]
    main.append(jnp.zeros((depth, N_MAIN - N_MAIN_USED, d), w_t.dtype))
    small, lane = [], 0
    for src, dst, width in SMALL_PIECES:
        if dst > lane:
            small.append(jnp.zeros((depth, dst - lane, d), w_t.dtype))
        small.append(w_t[:, src:src + width, :])
        lane = dst + width
    return jnp.concatenate(main, axis=1).astype(BF16), jnp.concatenate(small, axis=1).astype(BF16)


def _pad_to(v, offset, total):
    return jnp.zeros((total,), v.dtype).at[offset:offset + v.shape[0]].set(v)


def _rotary_tables(seq):
    half = HEAD_DIM // 2
    inv = 1.0 / (10000.0 ** (jnp.arange(half, dtype=F32) / half))
    ang = jnp.arange(seq, dtype=F32)[:, None] * inv[None, :]
    cos, sin = jnp.cos(ang), jnp.sin(ang)
    return jnp.concatenate([cos, cos], axis=1), jnp.concatenate([-sin, sin], axis=1)


def kernel(x, norm1_g, w_in, gate_b, fox_f_b, fox_qn_g, fox_kn_g, dsa_cq_g, dsa_w_uq, dsa_w_qidx, dsa_qn_g,
           dsa_kn_g, rel_bias, ssd_conv_w, ssd_conv_b, ssd_dt_bias, ssd_a_log, ssd_d, ssd_norm_g, w_br, w_out,
           norm2_g, w_ff1, w_ff2):
    batch, seq, d = x.shape
    tokens = batch * seq
    xt = x.reshape(tokens, d)
    cos, sin = _rotary_tables(seq)
    tm = min(1024, tokens)
    w_main, w_small = _layout_in_proj_t(w_in)
    w_br_bf, w_out_bf = w_br.astype(BF16), w_out.astype(BF16)
    w_ff1_bf, w_ff2_bf = w_ff1.astype(BF16), w_ff2.astype(BF16)
    for l in range(DEPTH):
        g1 = norm1_g[l][None, :]
        p = norm_matmul(xt, g1, w_main, l, BF16, tm, 1024)
        sm = norm_matmul(xt, g1, w_small, l, F32, tm, SM_W)

        o_ret = retention(p, cos, sin, batch, seq)

        fb_row = _pad_to(fox_f_b[l], SM_F, SM_W)[None, :]
        fcol, frow = fox_prep(sm, fb_row, batch, seq)
        o_fox = fox_attention(p, fcol, frow, fox_qn_g[l][None, :], fox_kn_g[l][None, :], batch, seq)

        o_dsa = dsa_attention(p, sm, dsa_cq_g[l][None, :], dsa_w_uq[l].T.astype(BF16), dsa_w_qidx[l].T.astype(BF16),
                              dsa_qn_g[l][:, None], dsa_kn_g[l][None, :], rel_bias, batch, seq)

        o_ssd = ssd_mixer(p, sm, ssd_conv_w[l], ssd_conv_b[l][None, :],
                          _pad_to(ssd_dt_bias[l], SM_DT, SM_W)[:, None], _pad_to(ssd_a_log[l], SM_DT, SM_W)[:, None],
                          jnp.repeat(ssd_d[l], SSD_HEAD_DIM)[None, :], ssd_norm_g[l][None, :], batch, seq)

        xt = merge_project(xt, p, gate_b[l][None, :], o_ret, o_fox, o_dsa, o_ssd, w_br_bf, w_out_bf, l)
        xt = ffn(xt, norm2_g[l][None, :], w_ff1_bf, w_ff2_bf, l)
    return xt.reshape(batch, seq, d)
```

```python
import functools
import math

import jax
import jax.numpy as jnp
from jax import lax
from jax.experimental import pallas as pl
from jax.experimental.pallas import tpu as pltpu

F32 = jnp.float32
BF16 = jnp.bfloat16

D_MODEL = 2048
DEPTH = 4
HEAD_DIM = 128
N_HEADS = 4
DSA_Q_RANK = 512
IDX_HEADS = 16
IDX_DIM = 64
DSA_TOPK = 256
SSD_HEADS = 16
SSD_HEAD_DIM = 64
SSD_GROUPS = 2
SSD_STATE = 128
SSD_CONV = 4
SSD_INNER = SSD_HEADS * SSD_HEAD_DIM
D_FF = 4 * D_MODEL
N_BUCKETS = 32
MAX_DISTANCE = 128
CHUNK = 128
EPS = 1e-6
N_BRANCH = 4
MIX_W = N_HEADS * HEAD_DIM

COL_GATE = 0
COL_RET = COL_GATE + N_BRANCH * D_MODEL
COL_Z = COL_RET + 4 * MIX_W
COL_XS = COL_Z + SSD_INNER
COL_BC = COL_XS + SSD_INNER
COL_CQ = COL_BC + 2 * SSD_GROUPS * SSD_STATE
COL_FOX = COL_CQ + DSA_Q_RANK
COL_DK = COL_FOX + 3 * MIX_W
COL_DV = COL_DK + HEAD_DIM
N_MAIN_USED = COL_DV + HEAD_DIM
N_MAIN = 15360
MAIN_TN = 1536
SM_DT = 0
SM_F = 16
SM_IW = 32
SM_IK = 64
SM_W = 128

LANES = 128
VMEM_LIMIT = 56 * 1024 * 1024
NEG_BIG = -1e30


def _cparams(sem):
    return pltpu.CompilerParams(dimension_semantics=sem, vmem_limit_bytes=VMEM_LIMIT)


def _dot(a, b):
    return jnp.dot(a, b, preferred_element_type=F32)


def _dot_nt(a, b):
    return lax.dot_general(a, b, (((1,), (1,)), ((), ())), preferred_element_type=F32)


def _dot_tn(a, b):
    return lax.dot_general(a, b, (((0,), (0,)), ((), ())), preferred_element_type=F32)


def _rms(x, g):
    return x * lax.rsqrt(jnp.mean(x * x, axis=-1, keepdims=True) + EPS) * g


def _silu(x):
    return x / (1.0 + jnp.exp(-x))


def _softplus(x):
    return jnp.maximum(x, 0.0) + jnp.log1p(jnp.exp(-jnp.abs(x)))


def _cumsum_lanes(x):
    lane = lax.broadcasted_iota(jnp.int32, x.shape, 1)
    d = 1
    while d < x.shape[1]:
        x = x + jnp.where(lane >= d, pltpu.roll(x, d, 1), 0.0)
        d *= 2
    return x


def _norm_matmul_kernel(x_ref, g_ref, w_ref, o_ref, h_ref):
    @pl.when(pl.program_id(1) == 0)
    def _():
        h_ref[...] = _rms(x_ref[...], g_ref[...]).astype(BF16)

    o_ref[...] = _dot_nt(h_ref[...], w_ref[...]).astype(o_ref.dtype)


def norm_matmul(x, g, w_t, layer, out_dtype, tm, tn):
    m, d = x.shape
    n = w_t.shape[1]
    return pl.pallas_call(
        _norm_matmul_kernel,
        grid=(m // tm, n // tn),
        in_specs=[
            pl.BlockSpec((tm, d), lambda i, j: (i, 0)),
            pl.BlockSpec((1, d), lambda i, j: (0, 0)),
            pl.BlockSpec((None, tn, d), lambda i, j: (layer, j, 0)),
        ],
        out_specs=pl.BlockSpec((tm, tn), lambda i, j: (i, j)),
        out_shape=jax.ShapeDtypeStruct((m, n), out_dtype),
        scratch_shapes=[pltpu.VMEM((tm, d), BF16)],
        compiler_params=_cparams(("parallel", "arbitrary")),
        name="norm_matmul",
    )(x, g, w_t)


def _retention_kernel(q_ref, k_ref, v_ref, g_ref, cos_ref, sin_ref, o_ref, state_ref):
    c = CHUNK

    @pl.when(pl.program_id(1) == 0)
    def _():
        state_ref[...] = jnp.zeros_like(state_ref)

    cos = cos_ref[...]
    sin = sin_ref[...]
    ii = lax.broadcasted_iota(jnp.int32, (c, c), 0)
    jj = lax.broadcasted_iota(jnp.int32, (c, c), 1)
    rel = (ii - jj).astype(F32)
    i_col = lax.broadcasted_iota(jnp.int32, (c, 1), 0).astype(F32)
    for h in range(N_HEADS):
        lg = math.log1p(-(2.0 ** (-5.0 - h)))
        sl = slice(h * HEAD_DIM, (h + 1) * HEAD_DIM)
        q = q_ref[:, sl].astype(F32)
        k = k_ref[:, sl].astype(F32)
        v = v_ref[:, sl]
        qr = q * cos + pltpu.roll(q, HEAD_DIM // 2, 1) * sin
        kr = (k * cos + pltpu.roll(k, HEAD_DIM // 2, 1) * sin) * (HEAD_DIM ** -0.5)
        decay = jnp.where(rel >= 0, jnp.exp(lg * jnp.maximum(rel, 0.0)), 0.0)
        scores = _dot_nt(qr.astype(BF16), kr.astype(BF16)) * decay
        y = _dot(scores.astype(BF16), v)
        q_dec = jnp.exp(lg * (i_col + 1.0))
        k_dec = jnp.exp(lg * (c - 1.0 - i_col))
        st = state_ref[h]
        y = y + _dot((qr * q_dec).astype(BF16), st.astype(BF16))
        kv = _dot_tn((kr * k_dec).astype(BF16), v)
        state_ref[h] = math.exp(lg * c) * st + kv
        yc = y - jnp.mean(y, axis=-1, keepdims=True)
        yn = yc * lax.rsqrt(jnp.mean(yc * yc, axis=-1, keepdims=True) + EPS)
        o_ref[:, sl] = (_silu(g_ref[:, sl].astype(F32)) * yn).astype(o_ref.dtype)


def retention(p, cos, sin, batch, seq):
    n = seq // CHUNK
    base = COL_RET // MIX_W

    def col(j):
        return pl.BlockSpec((CHUNK, MIX_W), lambda b, i: (b * n + i, base + j))

    tab = pl.BlockSpec((CHUNK, HEAD_DIM), lambda b, i: (i, 0))
    return pl.pallas_call(
        _retention_kernel,
        grid=(batch, n),
        in_specs=[col(0), col(1), col(2), col(3), tab, tab],
        out_specs=pl.BlockSpec((CHUNK, MIX_W), lambda b, i: (b * n + i, 0)),
        out_shape=jax.ShapeDtypeStruct((batch * seq, MIX_W), BF16),
        scratch_shapes=[pltpu.VMEM((N_HEADS, HEAD_DIM, HEAD_DIM), F32)],
        compiler_params=_cparams(("parallel", "arbitrary")),
        name="retention",
    )(p, p, p, p, cos, sin)


def _fox_prep_kernel(sm_ref, fb_ref, fcol_ref, frow_ref, carry_ref):
    @pl.when(pl.program_id(1) == 0)
    def _():
        carry_ref[...] = jnp.zeros_like(carry_ref)

    t = sm_ref[...] + fb_ref[...]
    lf = jnp.minimum(t, 0.0) - jnp.log1p(jnp.exp(-jnp.abs(t)))
    cs = _cumsum_lanes(lf.T) + carry_ref[...]
    carry_ref[...] = cs[:, LANES - 1:LANES]
    frow_ref[0, 0] = cs[SM_F:SM_F + 8, :]
    fcol_ref[...] = cs.T


def fox_prep(sm, fb_row, batch, seq):
    n = seq // CHUNK
    return pl.pallas_call(
        _fox_prep_kernel,
        grid=(batch, n),
        in_specs=[
            pl.BlockSpec((CHUNK, SM_W), lambda b, i: (b * n + i, 0)),
            pl.BlockSpec((1, SM_W), lambda b, i: (0, 0)),
        ],
        out_specs=[
            pl.BlockSpec((CHUNK, SM_W), lambda b, i: (b * n + i, 0)),
            pl.BlockSpec((1, 1, 8, CHUNK), lambda b, i: (b, i, 0, 0)),
        ],
        out_shape=[
            jax.ShapeDtypeStruct((batch * seq, SM_W), F32),
            jax.ShapeDtypeStruct((batch, n, 8, CHUNK), F32),
        ],
        scratch_shapes=[pltpu.VMEM((SM_W, 1), F32)],
        compiler_params=_cparams(("parallel", "arbitrary")),
        name="fox_prep",
    )(sm, fb_row)


FOX_T = 256


def _fox_kernel(q_ref, k_ref, v_ref, fcol_ref, frow_ref, qg_ref, kg_ref, o_ref,
                kn_ref, vt_ref, fb_ref, qt_ref, acc_ref):
    i = pl.program_id(1)
    t = FOX_T
    nkc = vt_ref.shape[0]
    sub = t // CHUNK

    @pl.when(i == 0)
    def _():
        for h in range(N_HEADS):
            sl = slice(h * HEAD_DIM, (h + 1) * HEAD_DIM)
            kn_ref[:, sl] = _rms(k_ref[:, sl].astype(F32), kg_ref[...]).astype(BF16)
            fb_ref[h] = jnp.broadcast_to(fcol_ref[:, SM_F + h:SM_F + h + 1], fb_ref.shape[1:])
            for j in range(nkc):
                for c in range(sub):
                    rows = slice(j * t + c * CHUNK, j * t + (c + 1) * CHUNK)
                    vt_ref[j, h, :, c * CHUNK:(c + 1) * CHUNK] = v_ref[rows, sl].astype(F32).T.astype(BF16)

    fqs = []
    for h in range(N_HEADS):
        sl = slice(h * HEAD_DIM, (h + 1) * HEAD_DIM)
        qn = _rms(q_ref[:, sl].astype(F32), qg_ref[...]) * (HEAD_DIM ** -0.5)
        qt_ref[h] = jnp.concatenate([qn[c * CHUNK:(c + 1) * CHUNK, :].T for c in range(sub)], axis=1).astype(BF16)
        fqs.append(jnp.concatenate([frow_ref[0, i * sub + c, h:h + 1, :] for c in range(sub)], axis=1))
        acc_ref[h] = jnp.zeros(acc_ref.shape[1:], F32)

    kofs = lax.broadcasted_iota(jnp.int32, (t, t), 0)
    qofs = lax.broadcasted_iota(jnp.int32, (t, t), 1)

    def body(diag, j, carry):
        ms, ls = carry
        start = pl.multiple_of(j * t, t)
        scores = [_dot(kn_ref[pl.ds(start, t), h * HEAD_DIM:(h + 1) * HEAD_DIM], qt_ref[h])
                  for h in range(N_HEADS)]
        new_ms, new_ls, ps, alphas = [], [], [], []
        for h in range(N_HEADS):
            fk = fb_ref[h, pl.ds(start, t), :]
            s = scores[h] + fqs[h] - jnp.concatenate([fk] * (t // LANES), axis=1)
            if diag:
                s = jnp.where(kofs <= qofs, s, NEG_BIG)
            m_new = jnp.maximum(ms[h], jnp.max(s, axis=0, keepdims=True))
            p = jnp.exp(s - m_new)
            alpha = jnp.exp(ms[h] - m_new)
            new_ls.append(alpha * ls[h] + jnp.sum(p, axis=0, keepdims=True))
            ps.append(p.astype(BF16))
            alphas.append(alpha)
            new_ms.append(m_new)
        for h in range(N_HEADS):
            acc_ref[h] = alphas[h] * acc_ref[h] + _dot(vt_ref[j, h], ps[h])
        return tuple(new_ms), tuple(new_ls)

    init = (tuple(jnp.full((1, t), NEG_BIG, F32) for _ in range(N_HEADS)),
            tuple(jnp.zeros((1, t), F32) for _ in range(N_HEADS)))
    carry = lax.fori_loop(0, i, functools.partial(body, False), init)
    _, ls = body(True, i, carry)
    for h in range(N_HEADS):
        out_t = acc_ref[h] / ls[h]
        for c in range(sub):
            o_ref[c * CHUNK:(c + 1) * CHUNK, h * HEAD_DIM:(h + 1) * HEAD_DIM] = (
                out_t[:, c * CHUNK:(c + 1) * CHUNK].T.astype(o_ref.dtype))


def fox_attention(p, fcol, frow, qg, kg, batch, seq):
    nq = seq // FOX_T
    base = COL_FOX // MIX_W
    return pl.pallas_call(
        _fox_kernel,
        grid=(batch, nq),
        in_specs=[
            pl.BlockSpec((FOX_T, MIX_W), lambda b, i: (b * nq + i, base)),
            pl.BlockSpec((seq, MIX_W), lambda b, i: (b, base + 1)),
            pl.BlockSpec((seq, MIX_W), lambda b, i: (b, base + 2)),
            pl.BlockSpec((seq, SM_W), lambda b, i: (b, 0)),
            pl.BlockSpec((1, seq // CHUNK, 8, CHUNK), lambda b, i: (b, 0, 0, 0)),
            pl.BlockSpec((1, HEAD_DIM), lambda b, i: (0, 0)),
            pl.BlockSpec((1, HEAD_DIM), lambda b, i: (0, 0)),
        ],
        out_specs=pl.BlockSpec((FOX_T, MIX_W), lambda b, i: (b * nq + i, 0)),
        out_shape=jax.ShapeDtypeStruct((batch * seq, MIX_W), BF16),
        scratch_shapes=[
            pltpu.VMEM((seq, MIX_W), BF16),
            pltpu.VMEM((nq, N_HEADS, HEAD_DIM, FOX_T), BF16),
            pltpu.VMEM((N_HEADS, seq, LANES), F32),
            pltpu.VMEM((N_HEADS, HEAD_DIM, FOX_T), BF16),
            pltpu.VMEM((N_HEADS, HEAD_DIM, FOX_T), F32),
        ],
        compiler_params=_cparams(("parallel", "arbitrary")),
        name="fox_attention",
    )(p, p, p, fcol, frow, qg, kg)


DSA_TQ = 128
DSA_TK = 256
BAND_W = 2 * DSA_TQ


def _t5_bucket(dist):
    max_exact = N_BUCKETS // 2
    d = jnp.maximum(dist, 0)
    log_ratio = jnp.log(jnp.maximum(d, 1).astype(F32) / max_exact) / math.log(MAX_DISTANCE / max_exact)
    large = jnp.minimum(max_exact + (log_ratio * (N_BUCKETS - max_exact)).astype(jnp.int32), N_BUCKETS - 1)
    return jnp.where(d < max_exact, d, large)


def _dsa_kernel(cq_ref, k_ref, v_ref, smq_ref, smk_ref, cqg_ref, wuq_ref, wqi_ref, qg_ref, kg_ref, rb_ref,
                o_ref, kn_ref, ki_ref, band_ref, sc_ref, qh_ref, qi_ref, m_ref, l_ref, acc_ref, *, topk):
    b = pl.program_id(0)
    i = pl.program_id(1)
    tq, tk = DSA_TQ, DSA_TK
    nkc = sc_ref.shape[0]
    seq = nkc * tk
    tiles = tk // tq
    nb = ((i + 1) * tq + tk - 1) // tk

    @pl.when(jnp.logical_and(b == 0, i == 0))
    def _():
        r = lax.broadcasted_iota(jnp.int32, (tq, BAND_W), 0)
        c = lax.broadcasted_iota(jnp.int32, (tq, BAND_W), 1)
        bucket = _t5_bucket(tq + r - c)
        for h in range(N_HEADS):
            far = rb_ref[N_BUCKETS - 1, h]
            acc = jnp.zeros((tq, BAND_W), F32)
            for bk in range(N_BUCKETS - 1):
                acc = jnp.where(bucket == bk, rb_ref[bk, h] - far, acc)
            band_ref[h] = acc

    @pl.when(i == 0)
    def _():
        kn_ref[...] = _rms(k_ref[...].astype(F32), kg_ref[...]).astype(BF16)
        ki_ref[...] = smk_ref[:, SM_IK:SM_IK + IDX_DIM].astype(BF16)

    cq = _rms(cq_ref[...].astype(F32), cqg_ref[...]).astype(BF16)
    qf = _dot(cq, wuq_ref[...])
    for h in range(N_HEADS):
        sl = slice(h * HEAD_DIM, (h + 1) * HEAD_DIM)
        qh_ref[h] = (_rms(qf[:, sl], qg_ref[...]) * (HEAD_DIM ** -0.5)).astype(BF16)
    q_idx = (_dot(cq, wqi_ref[...]) * (IDX_DIM ** -0.5)).astype(BF16)
    for h in range(IDX_HEADS):
        qi_ref[h] = q_idx[:, h * IDX_DIM:(h + 1) * IDX_DIM]

    qpos = lax.broadcasted_iota(jnp.int32, (tq, tk), 0) + i * tq
    col = lax.broadcasted_iota(jnp.int32, (tq, tk), 1)

    def score_body(j, _):
        start = pl.multiple_of(j * tk, tk)
        kj = ki_ref[pl.ds(start, tk), :]
        w_h = smq_ref[:, SM_IW:SM_IW + IDX_HEADS] * (IDX_HEADS ** -0.5)
        acc = jnp.zeros((tq, tk), F32)
        for h in range(IDX_HEADS):
            acc = acc + w_h[:, h:h + 1] * jnp.maximum(_dot_nt(qi_ref[h], kj), 0.0)
        sc_ref[j] = jnp.where(col + start <= qpos, acc, -jnp.inf)
        return 0

    lax.fori_loop(0, nb, score_body, 0)

    def over_chunks(fn, init):
        acc = init
        for j in range(nkc):
            acc = lax.cond(j < nb, functools.partial(fn, j), lambda a: a, acc)
        return acc

    def lane_tiles(x):
        return [x[:, t * tq:(t + 1) * tq] for t in range(tiles)]

    def row_total(x):
        return jnp.sum(x, axis=1, keepdims=True)

    def search():
        kf = float(topk)

        def max_fn(j, a):
            for x in lane_tiles(sc_ref[j]):
                a = jnp.maximum(a, x)
            return a

        def min_fn(j, a):
            for x in lane_tiles(sc_ref[j]):
                a = jnp.minimum(a, jnp.where(x == -jnp.inf, jnp.inf, x))
            return a

        smax = jnp.max(over_chunks(max_fn, jnp.full((tq, tq), -jnp.inf, F32)), axis=1, keepdims=True)
        smin = jnp.min(over_chunks(min_fn, jnp.full((tq, tq), jnp.inf, F32)), axis=1, keepdims=True)

        def count_ge(t):
            tb = jnp.broadcast_to(t, (tq, tq))

            def fn(j, a):
                for x in lane_tiles(sc_ref[j]):
                    a = a + jnp.where(x >= tb, 1.0, 0.0)
                return a

            return row_total(over_chunks(fn, jnp.zeros((tq, tq), F32)))

        def midpoint(lo, hi):
            return jnp.where(hi == jnp.inf, smax, 0.5 * (lo + hi))

        def undecided(lo, hi, c_lo, mid):
            return jnp.logical_and(c_lo != kf, jnp.logical_and(mid > lo, mid < hi))

        def cond(carry):
            return jnp.logical_and(carry[0] < 400, carry[1] > 0.0)

        def body(carry):
            it, _, lo, hi, c_lo, c_hi, mid = carry
            upd = undecided(lo, hi, c_lo, mid)
            cnt = count_ge(mid)
            up = jnp.logical_and(upd, cnt >= kf)
            dn = jnp.logical_and(upd, cnt < kf)
            lo = jnp.where(up, mid, lo)
            c_lo = jnp.where(up, cnt, c_lo)
            hi = jnp.where(dn, mid, hi)
            c_hi = jnp.where(dn, cnt, c_hi)
            mid = midpoint(lo, hi)
            active = jnp.max(jnp.where(undecided(lo, hi, c_lo, mid), 1.0, 0.0))
            return it + 1, active, lo, hi, c_lo, c_hi, mid

        lo0 = smin
        hi0 = jnp.full((tq, 1), jnp.inf, F32)
        c_lo0 = count_ge(lo0)
        c_hi0 = jnp.zeros((tq, 1), F32)
        mid0 = midpoint(lo0, hi0)
        act0 = jnp.max(jnp.where(undecided(lo0, hi0, c_lo0, mid0), 1.0, 0.0))
        _, _, lo, hi, c_lo, c_hi, _ = lax.while_loop(
            cond, body, (jnp.int32(0), act0, lo0, hi0, c_lo0, c_hi0, mid0))

        def tie_search():
            need = kf - c_hi
            lo_b = jnp.broadcast_to(lo, (tq, tq))
            hi_b = jnp.broadcast_to(hi, (tq, tq))
            lane = lax.broadcasted_iota(jnp.int32, (tq, tq), 1)

            def tie_body(_, carry):
                jlo, jhi = carry
                jm = (jlo + jhi) // 2
                jm_b = jnp.broadcast_to(jm, (tq, tq))

                def fn(j, a):
                    for t, x in enumerate(lane_tiles(sc_ref[j])):
                        hit = jnp.logical_and(jnp.logical_and(x >= lo_b, x < hi_b), lane + (j * tk + t * tq) <= jm_b)
                        a = a + jnp.where(hit, 1.0, 0.0)
                    return a

                ok = row_total(over_chunks(fn, jnp.zeros((tq, tq), F32))) >= need
                return jnp.where(ok, jlo, jm), jnp.where(ok, jm, jhi)

            n_bits = int(math.ceil(math.log2(seq))) + 1
            _, jmax = lax.fori_loop(0, n_bits, tie_body,
                                    (jnp.full((tq, 1), -1, jnp.int32), jnp.full((tq, 1), seq - 1, jnp.int32)))
            return jmax

        any_tie = jnp.max(jnp.where(c_lo != kf, 1.0, 0.0)) > 0.0
        jmax = lax.cond(any_tie, tie_search, lambda: jnp.full((tq, 1), seq - 1, jnp.int32))
        return lo, hi, jmax

    def keep_all():
        return (jnp.full((tq, 1), -jnp.inf, F32), jnp.full((tq, 1), jnp.inf, F32),
                jnp.full((tq, 1), seq - 1, jnp.int32))

    lo, hi, jmax = lax.cond((i + 1) * tq > topk, search, keep_all)

    m_ref[...] = jnp.full(m_ref.shape, NEG_BIG, F32)
    l_ref[...] = jnp.zeros(l_ref.shape, F32)
    acc_ref[...] = jnp.zeros(acc_ref.shape, F32)

    def attend_body(j, _):
        start = pl.multiple_of(j * tk, tk)
        ks = kn_ref[pl.ds(start, tk), :]
        vs = v_ref[pl.ds(start, tk), :]
        sc = sc_ref[j]
        kpos = col + start
        keep = jnp.logical_or(sc >= hi, jnp.logical_and(sc >= lo, kpos <= jmax))
        keep = jnp.logical_and(keep, kpos <= qpos)
        for h in range(N_HEADS):
            bias = jnp.concatenate(
                [jnp.where(j * tiles + t == i, band_ref[h, :, tq:2 * tq],
                           jnp.where(j * tiles + t == i - 1, band_ref[h, :, 0:tq], 0.0))
                 for t in range(tiles)], axis=1)
            s = jnp.where(keep, _dot_nt(qh_ref[h], ks) + bias, NEG_BIG)
            m_old = m_ref[h]
            m_new = jnp.maximum(m_old, jnp.max(s, axis=1, keepdims=True))
            p = jnp.exp(s - m_new)
            alpha = jnp.exp(m_old - m_new)
            l_ref[h] = alpha * l_ref[h] + jnp.sum(p, axis=1, keepdims=True)
            acc_ref[h] = alpha * acc_ref[h] + _dot(p.astype(BF16), vs)
            m_ref[h] = m_new
        return 0

    lax.fori_loop(0, nb, attend_body, 0)
    for h in range(N_HEADS):
        o_ref[:, h * HEAD_DIM:(h + 1) * HEAD_DIM] = (acc_ref[h] / l_ref[h]).astype(o_ref.dtype)


SUBLANES = 8
COUNT_ROWS = 64
BISECT_FIXED_STEPS = 16


def _fold_rows(x, op):
    return op(x.reshape(x.shape[0] // SUBLANES, SUBLANES, x.shape[1]), axis=0)


def _dsa_t_kernel(cq_ref, k_ref, v_ref, smq_ref, smk_ref, cqg_ref, wuq_ref, wqi_ref, qg_ref, kg_ref, rb_ref,
                  o_ref, kn_ref, ki_ref, vt_ref, band_ref, sc_ref, qt_ref, xi_ref, acc_ref, s_ref, *, topk):
    b = pl.program_id(0)
    i = pl.program_id(1)
    tq, tk = DSA_TQ, DSA_TK
    nkc = sc_ref.shape[0]
    seq = nkc * tk
    tiles = tk // tq
    nb = ((i + 1) * tq + tk - 1) // tk

    @pl.when(jnp.logical_and(b == 0, i == 0))
    def _():
        c = lax.broadcasted_iota(jnp.int32, (BAND_W, tq), 0)
        r = lax.broadcasted_iota(jnp.int32, (BAND_W, tq), 1)
        bucket = _t5_bucket(tq + r - c)
        for h in range(N_HEADS):
            far = rb_ref[N_BUCKETS - 1, h]
            acc = jnp.zeros((BAND_W, tq), F32)
            for bk in range(N_BUCKETS - 1):
                acc = jnp.where(bucket == bk, rb_ref[bk, h] - far, acc)
            band_ref[h] = acc

    @pl.when(i == 0)
    def _():
        kn_ref[...] = _rms(k_ref[...].astype(F32), kg_ref[...]).astype(BF16)
        ki_ref[...] = smk_ref[:, SM_IK:SM_IK + IDX_DIM].astype(BF16)
        for j in range(nkc):
            for t in range(tiles):
                rows = slice(j * tk + t * tq, j * tk + (t + 1) * tq)
                vt_ref[j, :, t * tq:(t + 1) * tq] = v_ref[rows, :].astype(F32).T.astype(BF16)

    cq_t = _rms(cq_ref[...].astype(F32), cqg_ref[...]).T.astype(BF16)
    q_t = _dot(wuq_ref[...], cq_t)
    g_col = jnp.broadcast_to(qg_ref[...], (HEAD_DIM, tq))
    for h in range(N_HEADS):
        x = q_t[h * HEAD_DIM:(h + 1) * HEAD_DIM, :]
        inv = lax.rsqrt(jnp.mean(x * x, axis=0, keepdims=True) + EPS)
        qt_ref[:, h * tq:(h + 1) * tq] = (x * inv * g_col * (HEAD_DIM ** -0.5)).astype(BF16)
    qi_t = (_dot(wqi_ref[...], cq_t) * (IDX_DIM ** -0.5)).astype(BF16)
    for h in range(IDX_HEADS):
        xi_ref[:, h * tq:(h + 1) * tq] = qi_t[h * IDX_DIM:(h + 1) * IDX_DIM, :]
    w_rows = smq_ref[...].T[SM_IW:SM_IW + IDX_HEADS, :] * (IDX_HEADS ** -0.5)

    kofs = lax.broadcasted_iota(jnp.int32, (tk, tq), 0)
    qpos = lax.broadcasted_iota(jnp.int32, (tk, tq), 1) + i * tq

    def score_body(j, _):
        start = pl.multiple_of(j * tk, tk)
        kj = ki_ref[pl.ds(start, tk), :]
        acc = jnp.zeros((tk, tq), F32)
        for h2 in range(IDX_HEADS // 2):
            r = _dot(kj, xi_ref[:, 2 * h2 * tq:(2 * h2 + 2) * tq])
            for h in (2 * h2, 2 * h2 + 1):
                acc = acc + w_rows[h:h + 1, :] * jnp.maximum(r[:, (h - 2 * h2) * tq:(h - 2 * h2 + 1) * tq], 0.0)
        sc_ref[j] = jnp.where(kofs + start <= qpos, acc, -jnp.inf)
        return 0

    lax.fori_loop(0, nb, score_body, 0)

    @pl.when(nb % 2 == 1)
    def _():
        sc_ref[jnp.minimum(nb, nkc - 1)] = jnp.full((tk, tq), -jnp.inf, F32)

    def over_chunks(fn, init):
        acc = init
        for j in range(nkc):
            acc = lax.cond(j < nb, functools.partial(fn, j), lambda a: a, acc)
        return acc

    def count_where(pred_fn):
        def walk(n_chunks):
            def run():
                a = jnp.zeros((COUNT_ROWS, tq), F32)
                for j in range(n_chunks):
                    hit = jnp.where(pred_fn(j, sc_ref[j]), 1.0, 0.0)
                    a = a + jnp.sum(hit.reshape(tk // COUNT_ROWS, COUNT_ROWS, tq), axis=0)
                return a
            return run

        extents = sorted({min(n, nkc) for n in range(2, nkc + 2, 2)})
        a = lax.switch((nb - 1) // 2, [walk(n) for n in extents])
        return jnp.sum(a, axis=0, keepdims=True)

    def search():
        kf = float(topk)
        smax = jnp.max(over_chunks(lambda j, a: jnp.maximum(a, _fold_rows(sc_ref[j], jnp.max)),
                                   jnp.full((SUBLANES, tq), -jnp.inf, F32)), axis=0, keepdims=True)
        smin = jnp.min(over_chunks(
            lambda j, a: jnp.minimum(a, _fold_rows(jnp.where(sc_ref[j] == -jnp.inf, jnp.inf, sc_ref[j]), jnp.min)),
            jnp.full((SUBLANES, tq), jnp.inf, F32)), axis=0, keepdims=True)

        def count_ge(t):
            return count_where(lambda j, x: x >= t)

        def midpoint(lo, hi):
            return jnp.where(hi == jnp.inf, smax, 0.5 * (lo + hi))

        def undecided(lo, hi, c_lo, mid):
            return jnp.logical_and(c_lo != kf, jnp.logical_and(mid > lo, mid < hi))

        def step(state):
            lo, hi, c_lo, c_hi, mid = state
            upd = undecided(lo, hi, c_lo, mid)
            cnt = count_ge(mid)
            up = jnp.logical_and(upd, cnt >= kf)
            dn = jnp.logical_and(upd, cnt < kf)
            lo = jnp.where(up, mid, lo)
            c_lo = jnp.where(up, cnt, c_lo)
            hi = jnp.where(dn, mid, hi)
            c_hi = jnp.where(dn, cnt, c_hi)
            return lo, hi, c_lo, c_hi, midpoint(lo, hi)

        def any_undecided(state):
            lo, hi, c_lo, _, mid = state
            return jnp.max(jnp.where(undecided(lo, hi, c_lo, mid), 1.0, 0.0))

        def cond(carry):
            return jnp.logical_and(carry[0] < 200, carry[1] > 0.0)

        def body(carry):
            state = step(step(carry[2]))
            return carry[0] + 1, any_undecided(state), state

        lo0 = smin
        hi0 = jnp.full((1, tq), jnp.inf, F32)
        c_lo0 = (lax.broadcasted_iota(jnp.int32, (1, tq), 1) + (i * tq + 1)).astype(F32)
        c_hi0 = jnp.zeros((1, tq), F32)
        state = (lo0, hi0, c_lo0, c_hi0, midpoint(lo0, hi0))
        state = lax.fori_loop(0, BISECT_FIXED_STEPS, lambda _, s: step(s), state)
        _, _, (lo, hi, c_lo, c_hi, _) = lax.while_loop(
            cond, body, (jnp.int32(0), any_undecided(state), state))

        def tie_search():
            need = kf - c_hi

            def tie_body(_, carry):
                jlo, jhi = carry
                jm = (jlo + jhi) // 2
                cnt = count_where(lambda j, x: jnp.logical_and(jnp.logical_and(x >= lo, x < hi),
                                                               kofs + j * tk <= jm))
                ok = cnt >= need
                return jnp.where(ok, jlo, jm), jnp.where(ok, jm, jhi)

            n_bits = int(math.ceil(math.log2(seq))) + 1
            _, jmax = lax.fori_loop(0, n_bits, tie_body,
                                    (jnp.full((1, tq), -1, jnp.int32), jnp.full((1, tq), seq - 1, jnp.int32)))
            return jmax

        any_tie = jnp.max(jnp.where(c_lo != kf, 1.0, 0.0)) > 0.0
        jmax = lax.cond(any_tie, tie_search, lambda: jnp.full((1, tq), seq - 1, jnp.int32))
        return lo, hi, jmax

    def keep_all():
        return (jnp.full((1, tq), -jnp.inf, F32), jnp.full((1, tq), jnp.inf, F32),
                jnp.full((1, tq), seq - 1, jnp.int32))

    lo, hi, jmax = lax.cond((i + 1) * tq > topk, search, keep_all)

    def attend(n_chunks):
        def run():
            mx = [jnp.full((SUBLANES, tq), NEG_BIG, F32) for _ in range(N_HEADS)]
            for j in range(n_chunks):
                near = j >= n_chunks - 3
                s_all = _dot(kn_ref[j * tk:(j + 1) * tk, :], qt_ref[...])
                sc = sc_ref[j]
                kpos = kofs + j * tk
                keep = jnp.logical_or(sc >= hi, jnp.logical_and(sc >= lo, kpos <= jmax))
                if near:
                    keep = jnp.logical_and(keep, kpos <= qpos)
                for h in range(N_HEADS):
                    hs = slice(h * tq, (h + 1) * tq)
                    s = s_all[:, hs]
                    if near:
                        s = s + jnp.concatenate(
                            [jnp.where(j * tiles + t == i, band_ref[h, tq:2 * tq, :],
                                       jnp.where(j * tiles + t == i - 1, band_ref[h, 0:tq, :], 0.0))
                             for t in range(tiles)], axis=0)
                    s = jnp.where(keep, s, NEG_BIG)
                    s_ref[j, :, hs] = s
                    mx[h] = jnp.maximum(mx[h], _fold_rows(s, jnp.max))
            m_all = jnp.concatenate([jnp.max(x, axis=0, keepdims=True) for x in mx], axis=1)
            acc = jnp.zeros(acc_ref.shape, F32)
            l_part = jnp.zeros((SUBLANES, N_HEADS * tq), F32)
            for j in range(n_chunks):
                p = jnp.exp(s_ref[j] - m_all)
                l_part = l_part + _fold_rows(p, jnp.sum)
                acc = acc + _dot(vt_ref[j], p.astype(BF16))
            acc_ref[...] = acc
            return jnp.sum(l_part, axis=0, keepdims=True)
        return run

    extents = sorted({min(n, nkc) for n in range(2, nkc + 2, 2)})
    l_all = lax.switch((nb - 1) // 2, [attend(n) for n in extents])
    for h in range(N_HEADS):
        hs = slice(h * tq, (h + 1) * tq)
        out_t = acc_ref[:, hs] / l_all[:, hs]
        o_ref[:, h * HEAD_DIM:(h + 1) * HEAD_DIM] = out_t.T.astype(o_ref.dtype)


def dsa_attention(p, sm, cqg, wuq_t, wqi_t, qg_col, kg, rel_bias, batch, seq):
    nq = seq // DSA_TQ
    nkc = seq // DSA_TK
    topk = min(DSA_TOPK, seq // 4)
    kern = functools.partial(_dsa_t_kernel, topk=topk)
    return pl.pallas_call(
        kern,
        grid=(batch, nq),
        in_specs=[
            pl.BlockSpec((DSA_TQ, DSA_Q_RANK), lambda b, i: (b * nq + i, COL_CQ // DSA_Q_RANK)),
            pl.BlockSpec((seq, HEAD_DIM), lambda b, i: (b, COL_DK // HEAD_DIM)),
            pl.BlockSpec((seq, HEAD_DIM), lambda b, i: (b, COL_DV // HEAD_DIM)),
            pl.BlockSpec((DSA_TQ, SM_W), lambda b, i: (b * nq + i, 0)),
            pl.BlockSpec((seq, SM_W), lambda b, i: (b, 0)),
            pl.BlockSpec((1, DSA_Q_RANK), lambda b, i: (0, 0)),
            pl.BlockSpec((N_HEADS * HEAD_DIM, DSA_Q_RANK), lambda b, i: (0, 0)),
            pl.BlockSpec((IDX_HEADS * IDX_DIM, DSA_Q_RANK), lambda b, i: (0, 0)),
            pl.BlockSpec((HEAD_DIM, 1), lambda b, i: (0, 0)),
            pl.BlockSpec((1, HEAD_DIM), lambda b, i: (0, 0)),
            pl.BlockSpec(memory_space=pltpu.SMEM),
        ],
        out_specs=pl.BlockSpec((DSA_TQ, MIX_W), lambda b, i: (b * nq + i, 0)),
        out_shape=jax.ShapeDtypeStruct((batch * seq, MIX_W), BF16),
        scratch_shapes=[
            pltpu.VMEM((seq, HEAD_DIM), BF16),
            pltpu.VMEM((seq, IDX_DIM), BF16),
            pltpu.VMEM((nkc, HEAD_DIM, DSA_TK), BF16),
            pltpu.VMEM((N_HEADS, BAND_W, DSA_TQ), F32),
            pltpu.VMEM((nkc, DSA_TK, DSA_TQ), F32),
            pltpu.VMEM((HEAD_DIM, N_HEADS * DSA_TQ), BF16),
            pltpu.VMEM((IDX_DIM, IDX_HEADS * DSA_TQ), BF16),
            pltpu.VMEM((HEAD_DIM, N_HEADS * DSA_TQ), F32),
            pltpu.VMEM((nkc, DSA_TK, N_HEADS * DSA_TQ), F32),
        ],
        compiler_params=_cparams(("arbitrary", "arbitrary")),
        name="dsa_attention",
    )(p, p, p, sm, sm, cqg, wuq_t, wqi_t, qg_col, kg, rel_bias)


CONV_PAD = 8


def _causal_conv(x_ref, xp_ref, first, w_ref, b_ref, ext_ref):
    ext_ref[0:CONV_PAD, :] = xp_ref[CHUNK - CONV_PAD:CHUNK, :].astype(F32) * first
    ext_ref[CONV_PAD:CONV_PAD + CHUNK, :] = x_ref[...].astype(F32)
    acc = b_ref[...] + ext_ref[CONV_PAD:CONV_PAD + CHUNK, :] * w_ref[SSD_CONV - 1:SSD_CONV, :]
    for d in range(1, SSD_CONV):
        acc = acc + ext_ref[CONV_PAD - d:CONV_PAD - d + CHUNK, :] * w_ref[SSD_CONV - 1 - d:SSD_CONV - d, :]
    return _silu(acc)


def _ssd_kernel(z_ref, xs_ref, bc_ref, xsp_ref, bcp_ref, sm_ref, cwx_ref, cbx_ref, cwb_ref, cbb_ref,
                dtb_ref, alog_ref, dsk_ref, ng_ref, o_ref, prev_ref, y_ref, extx_ref, extb_ref):
    c = CHUNK
    n = pl.program_id(1)

    @pl.when(n == 0)
    def _():
        prev_ref[...] = jnp.zeros_like(prev_ref)

    first = (n > 0).astype(F32)
    xs = _causal_conv(xs_ref, xsp_ref, first, cwx_ref, cbx_ref, extx_ref)
    bc = _causal_conv(bc_ref, bcp_ref, first, cwb_ref, cbb_ref, extb_ref)

    dt_t = _softplus(sm_ref[...].T + dtb_ref[...])
    cs_t = _cumsum_lanes(dt_t * (-jnp.exp(alog_ref[...])))
    cs = cs_t.T
    ii = lax.broadcasted_iota(jnp.int32, (c, c), 0)
    jj = lax.broadcasted_iota(jnp.int32, (c, c), 1)
    tril = ii >= jj
    pair_w = 2 * SSD_HEAD_DIM
    first_head = jj < SSD_HEAD_DIM
    first_head_row = lax.broadcasted_iota(jnp.int32, (1, pair_w), 1) < SSD_HEAD_DIM
    gn = SSD_GROUPS * SSD_STATE
    hpg = SSD_HEADS // SSD_GROUPS
    for g in range(SSD_GROUPS):
        bg = bc[:, g * SSD_STATE:(g + 1) * SSD_STATE]
        cg = bc[:, gn + g * SSD_STATE:gn + (g + 1) * SSD_STATE].astype(BF16)
        cb = _dot_nt(cg, bg.astype(BF16))
        bg_t = bg.T
        y_off = _dot(cg, prev_ref[g].astype(BF16))
        for pr in range(hpg // 2):
            cols = slice((g * hpg + 2 * pr) * SSD_HEAD_DIM, (g * hpg + 2 * pr + 2) * SSD_HEAD_DIM)
            rcols = slice(2 * pr * SSD_HEAD_DIM, (2 * pr + 2) * SSD_HEAD_DIM)
            x_pair = xs[:, cols]
            x_bf = x_pair.astype(BF16)
            y_diag, st, exp_a, exp_last = [], [], [], []
            for k in range(2):
                row = SM_DT + g * hpg + 2 * pr + k
                a_row = cs_t[row:row + 1, :]
                dt_row = dt_t[row:row + 1, :]
                last = cs_t[row:row + 1, c - 1:c]
                a_col = jnp.broadcast_to(cs[:, row:row + 1], (c, c))
                seg = jnp.where(tril, jnp.exp(jnp.where(tril, a_col - a_row, 0.0)), 0.0)
                y_diag.append(_dot((cb * seg * dt_row).astype(BF16), x_bf))
                st.append(_dot((bg_t * (dt_row * jnp.exp(last - a_row))).astype(BF16), x_bf))
                exp_a.append(jnp.exp(a_col))
                exp_last.append(jnp.exp(last))
            y_ref[:, cols] = (jnp.where(first_head, y_diag[0], y_diag[1])
                              + y_off[:, rcols] * jnp.where(first_head, exp_a[0], exp_a[1])
                              + dsk_ref[:, cols] * x_pair)
            prev_ref[g, :, rcols] = (jnp.where(first_head_row, exp_last[0], exp_last[1]) * prev_ref[g, :, rcols]
                                     + jnp.where(first_head, st[0], st[1]))
    gated = y_ref[...] * _silu(z_ref[...].astype(F32))
    gw = SSD_INNER // SSD_GROUPS
    for g in range(SSD_GROUPS):
        sl = slice(g * gw, (g + 1) * gw)
        o_ref[:, sl] = _rms(gated[:, sl], ng_ref[:, sl]).astype(o_ref.dtype)


def ssd_mixer(p, sm, cw, cb, dtb_col, alog_col, dskip_row, ng, batch, seq):
    n = seq // CHUNK
    bcw = 2 * SSD_GROUPS * SSD_STATE

    def cur(width, colbase):
        return pl.BlockSpec((CHUNK, width), lambda b, i: (b * n + i, colbase // width))

    def prv(width, colbase):
        return pl.BlockSpec((CHUNK, width), lambda b, i: (b * n + jnp.maximum(i - 1, 0), colbase // width))

    def const(shape):
        return pl.BlockSpec(shape, lambda b, i: (0, 0))

    return pl.pallas_call(
        _ssd_kernel,
        grid=(batch, n),
        in_specs=[
            cur(SSD_INNER, COL_Z), cur(SSD_INNER, COL_XS), cur(bcw, COL_BC),
            prv(SSD_INNER, COL_XS), prv(bcw, COL_BC),
            pl.BlockSpec((CHUNK, SM_W), lambda b, i: (b * n + i, 0)),
            const((SSD_CONV, SSD_INNER)), const((1, SSD_INNER)),
            const((SSD_CONV, bcw)), const((1, bcw)),
            const((SM_W, 1)), const((SM_W, 1)),
            const((1, SSD_INNER)), const((1, SSD_INNER)),
        ],
        out_specs=pl.BlockSpec((CHUNK, SSD_INNER), lambda b, i: (b * n + i, 0)),
        out_shape=jax.ShapeDtypeStruct((batch * seq, SSD_INNER), BF16),
        scratch_shapes=[
            pltpu.VMEM((SSD_GROUPS, SSD_STATE, SSD_INNER // SSD_GROUPS), F32),
            pltpu.VMEM((CHUNK, SSD_INNER), F32),
            pltpu.VMEM((CONV_PAD + CHUNK, SSD_INNER), F32),
            pltpu.VMEM((CONV_PAD + CHUNK, bcw), F32),
        ],
        compiler_params=_cparams(("parallel", "arbitrary")),
        name="ssd_mixer",
    )(p, p, p, p, p, sm, cw[:, :SSD_INNER], cb[:, :SSD_INNER], cw[:, SSD_INNER:], cb[:, SSD_INNER:],
      dtb_col, alog_col, dskip_row, ng)


MERGE_TM = 256


def _merge_kernel(x_ref, gl_ref, gb_ref, oret_ref, ofox_ref, odsa_ref, ossd_ref, wbr_ref, wout_ref, o_ref):
    branches = (oret_ref, ofox_ref, odsa_ref, ossd_ref)
    merged = None
    row0 = 0
    for bi, br in enumerate(branches):
        width = br.shape[1]
        sl = slice(bi * D_MODEL, (bi + 1) * D_MODEL)
        gate = 1.0 / (1.0 + jnp.exp(-(gl_ref[:, sl].astype(F32) + gb_ref[:, sl])))
        term = gate * _dot(br[...], wbr_ref[row0:row0 + width, :])
        merged = term if merged is None else merged + term
        row0 += width
    o_ref[...] = x_ref[...] + _dot(merged.astype(BF16), wout_ref[...])


def merge_project(x, p, gate_b, o_ret, o_fox, o_dsa, o_ssd, w_br, w_out, layer):
    m = x.shape[0]
    tm = MERGE_TM

    def rows(width):
        return pl.BlockSpec((tm, width), lambda i: (i, 0))

    def const(shape):
        return pl.BlockSpec(shape, lambda i: (0, 0), pipeline_mode=pl.Buffered(1))

    def stacked(w):
        return pl.BlockSpec((None,) + w.shape[1:], lambda i: (layer, 0, 0), pipeline_mode=pl.Buffered(1))

    return pl.pallas_call(
        _merge_kernel,
        grid=(m // tm,),
        in_specs=[
            rows(D_MODEL), rows(N_BRANCH * D_MODEL), const((1, N_BRANCH * D_MODEL)),
            rows(MIX_W), rows(MIX_W), rows(MIX_W), rows(SSD_INNER),
            stacked(w_br), stacked(w_out),
        ],
        out_specs=rows(D_MODEL),
        out_shape=jax.ShapeDtypeStruct(x.shape, x.dtype),
        compiler_params=_cparams(("parallel",)),
        name="merge_project",
    )(x, p, gate_b, o_ret, o_fox, o_dsa, o_ssd, w_br, w_out)


FFN_TM = 1024
FFN_TF = 512


def _ffn_kernel(x_ref, g_ref, w1_ref, w2_ref, o_ref, h_ref):
    @pl.when(pl.program_id(1) == 0)
    def _():
        h_ref[...] = _rms(x_ref[...], g_ref[...]).astype(BF16)
        o_ref[...] = x_ref[...]

    a = jnp.maximum(_dot(h_ref[...], w1_ref[...]), 0.0)
    o_ref[...] += _dot((a * a).astype(BF16), w2_ref[...])


def ffn(x, g, w1, w2, layer):
    m, d = x.shape
    dff = w1.shape[2]
    tm, tf = min(FFN_TM, m), FFN_TF
    return pl.pallas_call(
        _ffn_kernel,
        grid=(m // tm, dff // tf),
        in_specs=[
            pl.BlockSpec((tm, d), lambda i, f: (i, 0), pipeline_mode=pl.Buffered(1)),
            pl.BlockSpec((1, d), lambda i, f: (0, 0)),
            pl.BlockSpec((None, d, tf), lambda i, f: (layer, 0, f)),
            pl.BlockSpec((None, tf, d), lambda i, f: (layer, f, 0)),
        ],
        out_specs=pl.BlockSpec((tm, d), lambda i, f: (i, 0)),
        out_shape=jax.ShapeDtypeStruct(x.shape, x.dtype),
        scratch_shapes=[pltpu.VMEM((tm, d), BF16)],
        compiler_params=_cparams(("parallel", "arbitrary")),
        name="ffn",
    )(x, g, w1, w2)


SRC_RET = 0
SRC_FOX = SRC_RET + 4 * MIX_W
SRC_FF = SRC_FOX + 3 * MIX_W
SRC_CQ = SRC_FF + N_HEADS
SRC_DK = SRC_CQ + DSA_Q_RANK
SRC_IK = SRC_DK + 2 * HEAD_DIM
SRC_IW = SRC_IK + IDX_DIM
SRC_Z = SRC_IW + IDX_HEADS
SRC_DT = SRC_Z + 2 * SSD_INNER + 2 * SSD_GROUPS * SSD_STATE
SRC_GATE = SRC_DT + SSD_HEADS
IN_TOTAL = SRC_GATE + N_BRANCH * D_MODEL
MAIN_RUNS = ((COL_GATE, SRC_GATE), (COL_RET, SRC_RET), (COL_Z, SRC_Z), (COL_CQ, SRC_CQ),
             (COL_FOX, SRC_FOX), (COL_DK, SRC_DK))
RELAYOUT_W = 512
RELAYOUT_TILES = RELAYOUT_W // LANES


def _relayout_tables():
    starts, shifts = [], []
    for blk in range(N_MAIN // RELAYOUT_W):
        o = blk * RELAYOUT_W
        dst, src = [r for r in MAIN_RUNS if r[0] <= o][-1]
        col = src + (o - dst)
        starts.append(col // LANES)
        shifts.append(col % LANES)
    return jnp.asarray(starts, jnp.int32), jnp.asarray(shifts, jnp.int32)


def _relayout_kernel(start_ref, shift_ref, *refs):
    del start_ref
    tiles, o_ref = refs[:-1], refs[-1]
    shift = shift_ref[pl.program_id(1)]
    amount = lax.rem(LANES - shift, LANES)
    lane = lax.broadcasted_iota(jnp.int32, tiles[0].shape, 1)
    rolled = [pltpu.roll(t[...], amount, 1) for t in tiles]
    for k in range(RELAYOUT_TILES):
        piece = jnp.where(lane < LANES - shift, rolled[k], rolled[k + 1])
        o_ref[:, k * LANES:(k + 1) * LANES] = piece.astype(o_ref.dtype)


def relayout_main(w_in):
    depth, d, n_src = w_in.shape
    last = (n_src - 1) // LANES
    starts, shifts = _relayout_tables()

    def tile(k):
        return pl.BlockSpec((None, d, LANES), lambda l, b, st, sh: (l, 0, jnp.minimum(st[b] + k, last)))

    return pl.pallas_call(
        _relayout_kernel,
        grid_spec=pltpu.PrefetchScalarGridSpec(
            num_scalar_prefetch=2,
            grid=(depth, N_MAIN // RELAYOUT_W),
            in_specs=[tile(k) for k in range(RELAYOUT_TILES + 1)],
            out_specs=pl.BlockSpec((None, d, RELAYOUT_W), lambda l, b, st, sh: (l, 0, b)),
        ),
        out_shape=jax.ShapeDtypeStruct((depth, d, N_MAIN), BF16),
        compiler_params=_cparams(("parallel", "arbitrary")),
        name="relayout_main",
    )(starts, shifts, *([w_in] * (RELAYOUT_TILES + 1)))


SMALL_PIECES = ((SRC_DT, SM_DT, SSD_HEADS), (SRC_FF, SM_F, N_HEADS), (SRC_IW, SM_IW, IDX_HEADS),
                (SRC_IK, SM_IK, IDX_DIM))


def _relayout_small_kernel(*refs):
    tiles, o_ref = refs[:-1], refs[-1]
    lane = lax.broadcasted_iota(jnp.int32, o_ref.shape, 1)
    out = jnp.zeros(o_ref.shape, F32)
    for t, (src, dst, width) in zip(tiles, SMALL_PIECES):
        moved = pltpu.roll(t[...], (dst - src % LANES) % LANES, 1)
        out = jnp.where(jnp.logical_and(lane >= dst, lane < dst + width), moved, out)
    o_ref[...] = out.astype(o_ref.dtype)


def relayout_small(w_in):
    depth, d, _ = w_in.shape
    for src, _, width in SMALL_PIECES:
        assert src // LANES == (src + width - 1) // LANES

    def tile(src):
        return pl.BlockSpec((None, d, LANES), lambda l: (l, 0, src // LANES))

    return pl.pallas_call(
        _relayout_small_kernel,
        grid=(depth,),
        in_specs=[tile(src) for src, _, _ in SMALL_PIECES],
        out_specs=pl.BlockSpec((None, d, SM_W), lambda l: (l, 0, 0)),
        out_shape=jax.ShapeDtypeStruct((depth, d, SM_W), BF16),
        compiler_params=_cparams(("parallel",)),
        name="relayout_small",
    )(*([w_in] * len(SMALL_PIECES)))


def _layout_in_proj_t(w_in):
    w_t = jnp.swapaxes(w_in, 1, 2)
    depth, _, d = w_t.shape
    ends = [dst for dst, _ in MAIN_RUNS[1:]] + [N_MAIN_USED]
    main = [w_t[:, src:src + (end - dst), :] for (dst, src), end in zip(MAIN_RUNS, ends)]
    main.append(jnp.zeros((depth, N_MAIN - N_MAIN_USED, d), w_t.dtype))
    small, lane = [], 0
    for src, dst, width in SMALL_PIECES:
        if dst > lane:
            small.append(jnp.zeros((depth, dst - lane, d), w_t.dtype))
        small.append(w_t[:, src:src + width, :])
        lane = dst + width
    return jnp.concatenate(main, axis=1).astype(BF16), jnp.concatenate(small, axis=1).astype(BF16)


def _pad_to(v, offset, total):
    return jnp.zeros((total,), v.dtype).at[offset:offset + v.shape[0]].set(v)


def _rotary_tables(seq):
    half = HEAD_DIM // 2
    inv = 1.0 / (10000.0 ** (jnp.arange(half, dtype=F32) / half))
    ang = jnp.arange(seq, dtype=F32)[:, None] * inv[None, :]
    cos, sin = jnp.cos(ang), jnp.sin(ang)
    return jnp.concatenate([cos, cos], axis=1), jnp.concatenate([-sin, sin], axis=1)


def kernel(x, norm1_g, w_in, gate_b, fox_f_b, fox_qn_g, fox_kn_g, dsa_cq_g, dsa_w_uq, dsa_w_qidx, dsa_qn_g,
           dsa_kn_g, rel_bias, ssd_conv_w, ssd_conv_b, ssd_dt_bias, ssd_a_log, ssd_d, ssd_norm_g, w_br, w_out,
           norm2_g, w_ff1, w_ff2):
    batch, seq, d = x.shape
    tokens = batch * seq
    xt = x.reshape(tokens, d)
    cos, sin = _rotary_tables(seq)
    tm = min(1024, tokens)
    w_main, w_small = _layout_in_proj_t(w_in)
    w_br_bf, w_out_bf = w_br.astype(BF16), w_out.astype(BF16)
    w_ff1_bf, w_ff2_bf = w_ff1.astype(BF16), w_ff2.astype(BF16)
    for l in range(DEPTH):
        g1 = norm1_g[l][None, :]
        p = norm_matmul(xt, g1, w_main, l, BF16, tm, MAIN_TN)
        sm = norm_matmul(xt, g1, w_small, l, F32, tm, SM_W)

        o_ret = retention(p, cos, sin, batch, seq)

        fb_row = _pad_to(fox_f_b[l], SM_F, SM_W)[None, :]
        fcol, frow = fox_prep(sm, fb_row, batch, seq)
        o_fox = fox_attention(p, fcol, frow, fox_qn_g[l][None, :], fox_kn_g[l][None, :], batch, seq)

        o_dsa = dsa_attention(p, sm, dsa_cq_g[l][None, :], dsa_w_uq[l].T.astype(BF16), dsa_w_qidx[l].T.astype(BF16),
                              dsa_qn_g[l][:, None], dsa_kn_g[l][None, :], rel_bias, batch, seq)

        o_ssd = ssd_mixer(p, sm, ssd_conv_w[l], ssd_conv_b[l][None, :],
                          _pad_to(ssd_dt_bias[l], SM_DT, SM_W)[:, None], _pad_to(ssd_a_log[l], SM_DT, SM_W)[:, None],
                          jnp.repeat(ssd_d[l], SSD_HEAD_DIM)[None, :], ssd_norm_g[l][None, :], batch, seq)

        xt = merge_project(xt, p, gate_b[l][None, :], o_ret, o_fox, o_dsa, o_ssd, w_br_bf, w_out_bf, l)
        xt = ffn(xt, norm2_g[l][None, :], w_ff1_bf, w_ff2_bf, l)
    return xt.reshape(batch, seq, d)
```

```python
import functools
import math

import jax
import jax.numpy as jnp
from jax import lax
from jax.experimental import pallas as pl
from jax.experimental.pallas import tpu as pltpu

F32 = jnp.float32
BF16 = jnp.bfloat16

D_MODEL = 2048
DEPTH = 4
HEAD_DIM = 128
N_HEADS = 4
DSA_Q_RANK = 512
IDX_HEADS = 16
IDX_DIM = 64
DSA_TOPK = 256
SSD_HEADS = 16
SSD_HEAD_DIM = 64
SSD_GROUPS = 2
SSD_STATE = 128
SSD_CONV = 4
SSD_INNER = SSD_HEADS * SSD_HEAD_DIM
D_FF = 4 * D_MODEL
N_BUCKETS = 32
MAX_DISTANCE = 128
CHUNK = 128
EPS = 1e-6
N_BRANCH = 4
MIX_W = N_HEADS * HEAD_DIM

COL_GATE = 0
COL_RET = COL_GATE + N_BRANCH * D_MODEL
COL_Z = COL_RET + 4 * MIX_W
COL_XS = COL_Z + SSD_INNER
COL_BC = COL_XS + SSD_INNER
COL_CQ = COL_BC + 2 * SSD_GROUPS * SSD_STATE
COL_FOX = COL_CQ + DSA_Q_RANK
COL_DK = COL_FOX + 3 * MIX_W
COL_DV = COL_DK + HEAD_DIM
N_MAIN_USED = COL_DV + HEAD_DIM
N_MAIN = 15360
MAIN_TN = 1536
SM_DT = 0
SM_F = 16
SM_IW = 32
SM_IK = 64
SM_W = 128

LANES = 128
SUBLANES = 8
VMEM_LIMIT = 56 * 1024 * 1024
NEG_BIG = -1e30


def _cparams(sem):
    return pltpu.CompilerParams(dimension_semantics=sem, vmem_limit_bytes=VMEM_LIMIT)


def _dot(a, b):
    return jnp.dot(a, b, preferred_element_type=F32)


def _dot_nt(a, b):
    return lax.dot_general(a, b, (((1,), (1,)), ((), ())), preferred_element_type=F32)


def _rms(x, g):
    return x * lax.rsqrt(jnp.mean(x * x, axis=-1, keepdims=True) + EPS) * g


def _silu(x):
    return x / (1.0 + jnp.exp(-x))


def _softplus(x):
    return jnp.maximum(x, 0.0) + jnp.log1p(jnp.exp(-jnp.abs(x)))


def _cumsum_lanes(x):
    lane = lax.broadcasted_iota(jnp.int32, x.shape, 1)
    d = 1
    while d < x.shape[1]:
        x = x + jnp.where(lane >= d, pltpu.roll(x, d, 1), 0.0)
        d *= 2
    return x


def _fold_rows(x, op):
    return op(x.reshape(x.shape[0] // SUBLANES, SUBLANES, x.shape[1]), axis=0)


def _norm_matmul_kernel(x_ref, g_ref, w_ref, o_ref, h_ref):
    @pl.when(pl.program_id(1) == 0)
    def _():
        h_ref[...] = _rms(x_ref[...], g_ref[...]).astype(BF16)

    o_ref[...] = _dot_nt(h_ref[...], w_ref[...]).astype(o_ref.dtype)


def norm_matmul(x, g, w_t, layer, out_dtype, tm, tn):
    m, d = x.shape
    n = w_t.shape[1]
    return pl.pallas_call(
        _norm_matmul_kernel,
        grid=(m // tm, n // tn),
        in_specs=[
            pl.BlockSpec((tm, d), lambda i, j: (i, 0)),
            pl.BlockSpec((1, d), lambda i, j: (0, 0)),
            pl.BlockSpec((None, tn, d), lambda i, j: (layer, j, 0)),
        ],
        out_specs=pl.BlockSpec((tm, tn), lambda i, j: (i, j)),
        out_shape=jax.ShapeDtypeStruct((m, n), out_dtype),
        scratch_shapes=[pltpu.VMEM((tm, d), BF16)],
        compiler_params=_cparams(("parallel", "arbitrary")),
        name="norm_matmul",
    )(x, g, w_t)


def _retention_kernel(q_ref, k_ref, v_ref, g_ref, cos_ref, sin_ref, o_ref, state_ref):
    c = CHUNK

    @pl.when(pl.program_id(1) == 0)
    def _():
        state_ref[...] = jnp.zeros_like(state_ref)

    cos = cos_ref[...]
    sin = sin_ref[...]
    ii = lax.broadcasted_iota(jnp.int32, (c, c), 0)
    jj = lax.broadcasted_iota(jnp.int32, (c, c), 1)
    rel = (ii - jj).astype(F32)
    i_col = lax.broadcasted_iota(jnp.int32, (c, 1), 0).astype(F32)
    for h in range(N_HEADS):
        lg = math.log1p(-(2.0 ** (-5.0 - h)))
        sl = slice(h * HEAD_DIM, (h + 1) * HEAD_DIM)
        q = q_ref[:, sl].astype(F32)
        k = k_ref[:, sl].astype(F32)
        v = v_ref[:, sl]
        qr = q * cos + pltpu.roll(q, HEAD_DIM // 2, 1) * sin
        kr = (k * cos + pltpu.roll(k, HEAD_DIM // 2, 1) * sin) * (HEAD_DIM ** -0.5)
        decay = jnp.where(rel >= 0, jnp.exp(lg * jnp.maximum(rel, 0.0)), 0.0)
        scores = _dot_nt(qr.astype(BF16), kr.astype(BF16)) * decay
        y = _dot(scores.astype(BF16), v)
        q_dec = jnp.exp(lg * (i_col + 1.0))
        k_dec = jnp.exp(lg * (c - 1.0 - i_col))
        st = state_ref[h]
        y = y + _dot((qr * q_dec).astype(BF16), st.astype(BF16))
        kv = lax.dot_general((kr * k_dec).astype(BF16), v, (((0,), (0,)), ((), ())), preferred_element_type=F32)
        state_ref[h] = math.exp(lg * c) * st + kv
        yc = y - jnp.mean(y, axis=-1, keepdims=True)
        yn = yc * lax.rsqrt(jnp.mean(yc * yc, axis=-1, keepdims=True) + EPS)
        o_ref[:, sl] = (_silu(g_ref[:, sl].astype(F32)) * yn).astype(o_ref.dtype)


def retention(p, cos, sin, batch, seq):
    n = seq // CHUNK
    base = COL_RET // MIX_W

    def col(j):
        return pl.BlockSpec((CHUNK, MIX_W), lambda b, i: (b * n + i, base + j))

    tab = pl.BlockSpec((CHUNK, HEAD_DIM), lambda b, i: (i, 0))
    return pl.pallas_call(
        _retention_kernel,
        grid=(batch, n),
        in_specs=[col(0), col(1), col(2), col(3), tab, tab],
        out_specs=pl.BlockSpec((CHUNK, MIX_W), lambda b, i: (b * n + i, 0)),
        out_shape=jax.ShapeDtypeStruct((batch * seq, MIX_W), BF16),
        scratch_shapes=[pltpu.VMEM((N_HEADS, HEAD_DIM, HEAD_DIM), F32)],
        compiler_params=_cparams(("parallel", "arbitrary")),
        name="retention",
    )(p, p, p, p, cos, sin)


def _fox_prep_kernel(sm_ref, fb_ref, fcol_ref, frow_ref, carry_ref):
    @pl.when(pl.program_id(1) == 0)
    def _():
        carry_ref[...] = jnp.zeros_like(carry_ref)

    t = sm_ref[...] + fb_ref[...]
    lf = jnp.minimum(t, 0.0) - jnp.log1p(jnp.exp(-jnp.abs(t)))
    cs = _cumsum_lanes(lf.T) + carry_ref[...]
    carry_ref[...] = cs[:, LANES - 1:LANES]
    frow_ref[0, 0] = cs[SM_F:SM_F + 8, :]
    fcol_ref[...] = cs.T


def fox_prep(sm, fb_row, batch, seq):
    n = seq // CHUNK
    return pl.pallas_call(
        _fox_prep_kernel,
        grid=(batch, n),
        in_specs=[
            pl.BlockSpec((CHUNK, SM_W), lambda b, i: (b * n + i, 0)),
            pl.BlockSpec((1, SM_W), lambda b, i: (0, 0)),
        ],
        out_specs=[
            pl.BlockSpec((CHUNK, SM_W), lambda b, i: (b * n + i, 0)),
            pl.BlockSpec((1, 1, 8, CHUNK), lambda b, i: (b, i, 0, 0)),
        ],
        out_shape=[
            jax.ShapeDtypeStruct((batch * seq, SM_W), F32),
            jax.ShapeDtypeStruct((batch, n, 8, CHUNK), F32),
        ],
        scratch_shapes=[pltpu.VMEM((SM_W, 1), F32)],
        compiler_params=_cparams(("parallel", "arbitrary")),
        name="fox_prep",
    )(sm, fb_row)


FOX_T = 256


def _fox_kernel(q_ref, k_ref, v_ref, fcol_ref, frow_ref, qg_ref, kg_ref, o_ref,
                kn_ref, vt_ref, fb_ref, qt_ref, acc_ref):
    i = pl.program_id(1)
    t = FOX_T
    nkc = vt_ref.shape[0]
    sub = t // CHUNK

    @pl.when(i == 0)
    def _():
        for h in range(N_HEADS):
            sl = slice(h * HEAD_DIM, (h + 1) * HEAD_DIM)
            kn_ref[:, sl] = _rms(k_ref[:, sl].astype(F32), kg_ref[...]).astype(BF16)
            fb_ref[h] = jnp.broadcast_to(fcol_ref[:, SM_F + h:SM_F + h + 1], fb_ref.shape[1:])
            for j in range(nkc):
                for c in range(sub):
                    rows = slice(j * t + c * CHUNK, j * t + (c + 1) * CHUNK)
                    vt_ref[j, h, :, c * CHUNK:(c + 1) * CHUNK] = v_ref[rows, sl].astype(F32).T.astype(BF16)

    fqs = []
    for h in range(N_HEADS):
        sl = slice(h * HEAD_DIM, (h + 1) * HEAD_DIM)
        qn = _rms(q_ref[:, sl].astype(F32), qg_ref[...]) * (HEAD_DIM ** -0.5)
        qt_ref[h] = jnp.concatenate([qn[c * CHUNK:(c + 1) * CHUNK, :].T for c in range(sub)], axis=1).astype(BF16)
        fqs.append(jnp.concatenate([frow_ref[0, i * sub + c, h:h + 1, :] for c in range(sub)], axis=1))
        acc_ref[h] = jnp.zeros(acc_ref.shape[1:], F32)

    kofs = lax.broadcasted_iota(jnp.int32, (t, t), 0)
    qofs = lax.broadcasted_iota(jnp.int32, (t, t), 1)

    def body(diag, j, carry):
        ms, ls = carry
        start = pl.multiple_of(j * t, t)
        scores = [_dot(kn_ref[pl.ds(start, t), h * HEAD_DIM:(h + 1) * HEAD_DIM], qt_ref[h])
                  for h in range(N_HEADS)]
        new_ms, new_ls, ps, alphas = [], [], [], []
        for h in range(N_HEADS):
            fk = fb_ref[h, pl.ds(start, t), :]
            s = scores[h] + fqs[h] - jnp.concatenate([fk] * (t // LANES), axis=1)
            if diag:
                s = jnp.where(kofs <= qofs, s, NEG_BIG)
            m_new = jnp.maximum(ms[h], jnp.max(s, axis=0, keepdims=True))
            p = jnp.exp(s - m_new)
            alpha = jnp.exp(ms[h] - m_new)
            new_ls.append(alpha * ls[h] + jnp.sum(p, axis=0, keepdims=True))
            ps.append(p.astype(BF16))
            alphas.append(alpha)
            new_ms.append(m_new)
        for h in range(N_HEADS):
            acc_ref[h] = alphas[h] * acc_ref[h] + _dot(vt_ref[j, h], ps[h])
        return tuple(new_ms), tuple(new_ls)

    init = (tuple(jnp.full((1, t), NEG_BIG, F32) for _ in range(N_HEADS)),
            tuple(jnp.zeros((1, t), F32) for _ in range(N_HEADS)))
    carry = lax.fori_loop(0, i, functools.partial(body, False), init)
    _, ls = body(True, i, carry)
    for h in range(N_HEADS):
        out_t = acc_ref[h] / ls[h]
        for c in range(sub):
            o_ref[c * CHUNK:(c + 1) * CHUNK, h * HEAD_DIM:(h + 1) * HEAD_DIM] = (
                out_t[:, c * CHUNK:(c + 1) * CHUNK].T.astype(o_ref.dtype))


def fox_attention(p, fcol, frow, qg, kg, batch, seq):
    nq = seq // FOX_T
    base = COL_FOX // MIX_W
    return pl.pallas_call(
        _fox_kernel,
        grid=(batch, nq),
        in_specs=[
            pl.BlockSpec((FOX_T, MIX_W), lambda b, i: (b * nq + i, base)),
            pl.BlockSpec((seq, MIX_W), lambda b, i: (b, base + 1)),
            pl.BlockSpec((seq, MIX_W), lambda b, i: (b, base + 2)),
            pl.BlockSpec((seq, SM_W), lambda b, i: (b, 0)),
            pl.BlockSpec((1, seq // CHUNK, 8, CHUNK), lambda b, i: (b, 0, 0, 0)),
            pl.BlockSpec((1, HEAD_DIM), lambda b, i: (0, 0)),
            pl.BlockSpec((1, HEAD_DIM), lambda b, i: (0, 0)),
        ],
        out_specs=pl.BlockSpec((FOX_T, MIX_W), lambda b, i: (b * nq + i, 0)),
        out_shape=jax.ShapeDtypeStruct((batch * seq, MIX_W), BF16),
        scratch_shapes=[
            pltpu.VMEM((seq, MIX_W), BF16),
            pltpu.VMEM((nq, N_HEADS, HEAD_DIM, FOX_T), BF16),
            pltpu.VMEM((N_HEADS, seq, LANES), F32),
            pltpu.VMEM((N_HEADS, HEAD_DIM, FOX_T), BF16),
            pltpu.VMEM((N_HEADS, HEAD_DIM, FOX_T), F32),
        ],
        compiler_params=_cparams(("parallel", "arbitrary")),
        name="fox_attention",
    )(p, p, p, fcol, frow, qg, kg)


DSA_TQ = 128
DSA_TK = 256
BAND_W = 2 * DSA_TQ
COUNT_ROWS = 64
BISECT_FIXED_STEPS = 16


def _t5_bucket(dist):
    max_exact = N_BUCKETS // 2
    d = jnp.maximum(dist, 0)
    log_ratio = jnp.log(jnp.maximum(d, 1).astype(F32) / max_exact) / math.log(MAX_DISTANCE / max_exact)
    large = jnp.minimum(max_exact + (log_ratio * (N_BUCKETS - max_exact)).astype(jnp.int32), N_BUCKETS - 1)
    return jnp.where(d < max_exact, d, large)


def _dsa_kernel(cq_ref, k_ref, v_ref, smq_ref, smk_ref, cqg_ref, wuq_ref, wqi_ref, qg_ref, kg_ref, rb_ref,
                o_ref, kn_ref, ki_ref, vt_ref, band_ref, sc_ref, qt_ref, xi_ref, acc_ref, s_ref, *, topk):
    b = pl.program_id(0)
    i = pl.program_id(1)
    tq, tk = DSA_TQ, DSA_TK
    nkc = sc_ref.shape[0]
    seq = nkc * tk
    tiles = tk // tq
    nb = ((i + 1) * tq + tk - 1) // tk

    @pl.when(jnp.logical_and(b == 0, i == 0))
    def _():
        c = lax.broadcasted_iota(jnp.int32, (BAND_W, tq), 0)
        r = lax.broadcasted_iota(jnp.int32, (BAND_W, tq), 1)
        bucket = _t5_bucket(tq + r - c)
        for h in range(N_HEADS):
            far = rb_ref[N_BUCKETS - 1, h]
            acc = jnp.zeros((BAND_W, tq), F32)
            for bk in range(N_BUCKETS - 1):
                acc = jnp.where(bucket == bk, rb_ref[bk, h] - far, acc)
            band_ref[h] = acc

    @pl.when(i == 0)
    def _():
        kn_ref[...] = _rms(k_ref[...].astype(F32), kg_ref[...]).astype(BF16)
        ki_ref[...] = smk_ref[:, SM_IK:SM_IK + IDX_DIM].astype(BF16)
        for j in range(nkc):
            for t in range(tiles):
                rows = slice(j * tk + t * tq, j * tk + (t + 1) * tq)
                vt_ref[j, :, t * tq:(t + 1) * tq] = v_ref[rows, :].astype(F32).T.astype(BF16)

    cq_t = _rms(cq_ref[...].astype(F32), cqg_ref[...]).T.astype(BF16)
    q_t = _dot(wuq_ref[...], cq_t)
    g_col = jnp.broadcast_to(qg_ref[...], (HEAD_DIM, tq))
    for h in range(N_HEADS):
        x = q_t[h * HEAD_DIM:(h + 1) * HEAD_DIM, :]
        inv = lax.rsqrt(jnp.mean(x * x, axis=0, keepdims=True) + EPS)
        qt_ref[:, h * tq:(h + 1) * tq] = (x * inv * g_col * (HEAD_DIM ** -0.5)).astype(BF16)
    qi_t = (_dot(wqi_ref[...], cq_t) * (IDX_DIM ** -0.5)).astype(BF16)
    for h in range(IDX_HEADS):
        xi_ref[:, h * tq:(h + 1) * tq] = qi_t[h * IDX_DIM:(h + 1) * IDX_DIM, :]
    w_rows = smq_ref[...].T[SM_IW:SM_IW + IDX_HEADS, :] * (IDX_HEADS ** -0.5)

    kofs = lax.broadcasted_iota(jnp.int32, (tk, tq), 0)
    qpos = lax.broadcasted_iota(jnp.int32, (tk, tq), 1) + i * tq

    def score_body(j, _):
        start = pl.multiple_of(j * tk, tk)
        kj = ki_ref[pl.ds(start, tk), :]
        acc = jnp.zeros((tk, tq), F32)
        for h2 in range(IDX_HEADS // 2):
            r = _dot(kj, xi_ref[:, 2 * h2 * tq:(2 * h2 + 2) * tq])
            for h in (2 * h2, 2 * h2 + 1):
                acc = acc + w_rows[h:h + 1, :] * jnp.maximum(r[:, (h - 2 * h2) * tq:(h - 2 * h2 + 1) * tq], 0.0)
        sc_ref[j] = jnp.where(kofs + start <= qpos, acc, -jnp.inf)
        return 0

    lax.fori_loop(0, nb, score_body, 0)

    @pl.when(nb % 2 == 1)
    def _():
        sc_ref[jnp.minimum(nb, nkc - 1)] = jnp.full((tk, tq), -jnp.inf, F32)

    extents = sorted({min(n, nkc) for n in range(2, nkc + 2, 2)})

    def over_chunks(fn, init):
        acc = init
        for j in range(nkc):
            acc = lax.cond(j < nb, functools.partial(fn, j), lambda a: a, acc)
        return acc

    def count_where(pred_fn):
        def walk(n_chunks):
            def run():
                a = jnp.zeros((COUNT_ROWS, tq), F32)
                for j in range(n_chunks):
                    hit = jnp.where(pred_fn(j, sc_ref[j]), 1.0, 0.0)
                    a = a + jnp.sum(hit.reshape(tk // COUNT_ROWS, COUNT_ROWS, tq), axis=0)
                return a
            return run

        a = lax.switch((nb - 1) // 2, [walk(n) for n in extents])
        return jnp.sum(a, axis=0, keepdims=True)

    def search():
        kf = float(topk)
        smax = jnp.max(over_chunks(lambda j, a: jnp.maximum(a, _fold_rows(sc_ref[j], jnp.max)),
                                   jnp.full((SUBLANES, tq), -jnp.inf, F32)), axis=0, keepdims=True)
        smin = jnp.min(over_chunks(
            lambda j, a: jnp.minimum(a, _fold_rows(jnp.where(sc_ref[j] == -jnp.inf, jnp.inf, sc_ref[j]), jnp.min)),
            jnp.full((SUBLANES, tq), jnp.inf, F32)), axis=0, keepdims=True)

        def count_ge(t):
            return count_where(lambda j, x: x >= t)

        def midpoint(lo, hi):
            return jnp.where(hi == jnp.inf, smax, 0.5 * (lo + hi))

        def undecided(lo, hi, c_lo, mid):
            return jnp.logical_and(c_lo != kf, jnp.logical_and(mid > lo, mid < hi))

        def step(state):
            lo, hi, c_lo, c_hi, mid = state
            upd = undecided(lo, hi, c_lo, mid)
            cnt = count_ge(mid)
            up = jnp.logical_and(upd, cnt >= kf)
            dn = jnp.logical_and(upd, cnt < kf)
            lo = jnp.where(up, mid, lo)
            c_lo = jnp.where(up, cnt, c_lo)
            hi = jnp.where(dn, mid, hi)
            c_hi = jnp.where(dn, cnt, c_hi)
            return lo, hi, c_lo, c_hi, midpoint(lo, hi)

        def any_undecided(state):
            lo, hi, c_lo, _, mid = state
            return jnp.max(jnp.where(undecided(lo, hi, c_lo, mid), 1.0, 0.0))

        def cond(carry):
            return jnp.logical_and(carry[0] < 200, carry[1] > 0.0)

        def body(carry):
            state = step(step(carry[2]))
            return carry[0] + 1, any_undecided(state), state

        lo0 = smin
        hi0 = jnp.full((1, tq), jnp.inf, F32)
        c_lo0 = (lax.broadcasted_iota(jnp.int32, (1, tq), 1) + (i * tq + 1)).astype(F32)
        c_hi0 = jnp.zeros((1, tq), F32)
        state = (lo0, hi0, c_lo0, c_hi0, midpoint(lo0, hi0))
        state = lax.fori_loop(0, BISECT_FIXED_STEPS, lambda _, s: step(s), state)
        _, _, (lo, hi, c_lo, c_hi, _) = lax.while_loop(
            cond, body, (jnp.int32(0), any_undecided(state), state))

        def tie_search():
            need = kf - c_hi

            def tie_body(_, carry):
                jlo, jhi = carry
                jm = (jlo + jhi) // 2
                cnt = count_where(lambda j, x: jnp.logical_and(jnp.logical_and(x >= lo, x < hi),
                                                               kofs + j * tk <= jm))
                ok = cnt >= need
                return jnp.where(ok, jlo, jm), jnp.where(ok, jm, jhi)

            n_bits = int(math.ceil(math.log2(seq))) + 1
            _, jmax = lax.fori_loop(0, n_bits, tie_body,
                                    (jnp.full((1, tq), -1, jnp.int32), jnp.full((1, tq), seq - 1, jnp.int32)))
            return jmax

        any_tie = jnp.max(jnp.where(c_lo != kf, 1.0, 0.0)) > 0.0
        jmax = lax.cond(any_tie, tie_search, lambda: jnp.full((1, tq), seq - 1, jnp.int32))
        return lo, hi, jmax

    def keep_all():
        return (jnp.full((1, tq), -jnp.inf, F32), jnp.full((1, tq), jnp.inf, F32),
                jnp.full((1, tq), seq - 1, jnp.int32))

    lo, hi, jmax = lax.cond((i + 1) * tq > topk, search, keep_all)

    def attend(n_chunks):
        def run():
            mx = [jnp.full((SUBLANES, tq), NEG_BIG, F32) for _ in range(N_HEADS)]
            for j in range(n_chunks):
                near = j >= n_chunks - 3
                s_all = _dot(kn_ref[j * tk:(j + 1) * tk, :], qt_ref[...])
                sc = sc_ref[j]
                kpos = kofs + j * tk
                keep = jnp.logical_or(sc >= hi, jnp.logical_and(sc >= lo, kpos <= jmax))
                if near:
                    keep = jnp.logical_and(keep, kpos <= qpos)
                for h in range(N_HEADS):
                    hs = slice(h * tq, (h + 1) * tq)
                    s = s_all[:, hs]
                    if near:
                        s = s + jnp.concatenate(
                            [jnp.where(j * tiles + t == i, band_ref[h, tq:2 * tq, :],
                                       jnp.where(j * tiles + t == i - 1, band_ref[h, 0:tq, :], 0.0))
                             for t in range(tiles)], axis=0)
                    s = jnp.where(keep, s, NEG_BIG)
                    s_ref[j, :, hs] = s
                    mx[h] = jnp.maximum(mx[h], _fold_rows(s, jnp.max))
            m_all = jnp.concatenate([jnp.max(x, axis=0, keepdims=True) for x in mx], axis=1)
            acc = jnp.zeros(acc_ref.shape, F32)
            l_part = jnp.zeros((SUBLANES, N_HEADS * tq), F32)
            for j in range(n_chunks):
                p = jnp.exp(s_ref[j] - m_all)
                l_part = l_part + _fold_rows(p, jnp.sum)
                acc = acc + _dot(vt_ref[j], p.astype(BF16))
            acc_ref[...] = acc
            return jnp.sum(l_part, axis=0, keepdims=True)
        return run

    l_all = lax.switch((nb - 1) // 2, [attend(n) for n in extents])
    for h in range(N_HEADS):
        hs = slice(h * tq, (h + 1) * tq)
        out_t = acc_ref[:, hs] / l_all[:, hs]
        o_ref[:, h * HEAD_DIM:(h + 1) * HEAD_DIM] = out_t.T.astype(o_ref.dtype)


def dsa_attention(p, sm, cqg, wuq_t, wqi_t, qg_col, kg, rel_bias, batch, seq):
    nq = seq // DSA_TQ
    nkc = seq // DSA_TK
    topk = min(DSA_TOPK, seq // 4)
    kern = functools.partial(_dsa_kernel, topk=topk)
    return pl.pallas_call(
        kern,
        grid=(batch, nq),
        in_specs=[
            pl.BlockSpec((DSA_TQ, DSA_Q_RANK), lambda b, i: (b * nq + i, COL_CQ // DSA_Q_RANK)),
            pl.BlockSpec((seq, HEAD_DIM), lambda b, i: (b, COL_DK // HEAD_DIM)),
            pl.BlockSpec((seq, HEAD_DIM), lambda b, i: (b, COL_DV // HEAD_DIM)),
            pl.BlockSpec((DSA_TQ, SM_W), lambda b, i: (b * nq + i, 0)),
            pl.BlockSpec((seq, SM_W), lambda b, i: (b, 0)),
            pl.BlockSpec((1, DSA_Q_RANK), lambda b, i: (0, 0)),
            pl.BlockSpec((N_HEADS * HEAD_DIM, DSA_Q_RANK), lambda b, i: (0, 0)),
            pl.BlockSpec((IDX_HEADS * IDX_DIM, DSA_Q_RANK), lambda b, i: (0, 0)),
            pl.BlockSpec((HEAD_DIM, 1), lambda b, i: (0, 0)),
            pl.BlockSpec((1, HEAD_DIM), lambda b, i: (0, 0)),
            pl.BlockSpec(memory_space=pltpu.SMEM),
        ],
        out_specs=pl.BlockSpec((DSA_TQ, MIX_W), lambda b, i: (b * nq + i, 0)),
        out_shape=jax.ShapeDtypeStruct((batch * seq, MIX_W), BF16),
        scratch_shapes=[
            pltpu.VMEM((seq, HEAD_DIM), BF16),
            pltpu.VMEM((seq, IDX_DIM), BF16),
            pltpu.VMEM((nkc, HEAD_DIM, DSA_TK), BF16),
            pltpu.VMEM((N_HEADS, BAND_W, DSA_TQ), F32),
            pltpu.VMEM((nkc, DSA_TK, DSA_TQ), F32),
            pltpu.VMEM((HEAD_DIM, N_HEADS * DSA_TQ), BF16),
            pltpu.VMEM((IDX_DIM, IDX_HEADS * DSA_TQ), BF16),
            pltpu.VMEM((HEAD_DIM, N_HEADS * DSA_TQ), F32),
            pltpu.VMEM((nkc, DSA_TK, N_HEADS * DSA_TQ), F32),
        ],
        compiler_params=_cparams(("arbitrary", "arbitrary")),
        name="dsa_attention",
    )(p, p, p, sm, sm, cqg, wuq_t, wqi_t, qg_col, kg, rel_bias)


CONV_PAD = 8


def _causal_conv(x_ref, xp_ref, first, w_ref, b_ref, ext_ref):
    ext_ref[0:CONV_PAD, :] = xp_ref[CHUNK - CONV_PAD:CHUNK, :].astype(F32) * first
    ext_ref[CONV_PAD:CONV_PAD + CHUNK, :] = x_ref[...].astype(F32)
    acc = b_ref[...] + ext_ref[CONV_PAD:CONV_PAD + CHUNK, :] * w_ref[SSD_CONV - 1:SSD_CONV, :]
    for d in range(1, SSD_CONV):
        acc = acc + ext_ref[CONV_PAD - d:CONV_PAD - d + CHUNK, :] * w_ref[SSD_CONV - 1 - d:SSD_CONV - d, :]
    return _silu(acc)


def _ssd_kernel(z_ref, xs_ref, bc_ref, xsp_ref, bcp_ref, sm_ref, cwx_ref, cbx_ref, cwb_ref, cbb_ref,
                dtb_ref, alog_ref, dsk_ref, ng_ref, o_ref, prev_ref, y_ref, extx_ref, extb_ref):
    c = CHUNK
    n = pl.program_id(1)

    @pl.when(n == 0)
    def _():
        prev_ref[...] = jnp.zeros_like(prev_ref)

    first = (n > 0).astype(F32)
    xs = _causal_conv(xs_ref, xsp_ref, first, cwx_ref, cbx_ref, extx_ref)
    bc = _causal_conv(bc_ref, bcp_ref, first, cwb_ref, cbb_ref, extb_ref)

    dt_t = _softplus(sm_ref[...].T + dtb_ref[...])
    cs_t = _cumsum_lanes(dt_t * (-jnp.exp(alog_ref[...])))
    cs = cs_t.T
    ii = lax.broadcasted_iota(jnp.int32, (c, c), 0)
    jj = lax.broadcasted_iota(jnp.int32, (c, c), 1)
    tril = ii >= jj
    pair_w = 2 * SSD_HEAD_DIM
    first_head = jj < SSD_HEAD_DIM
    first_head_row = lax.broadcasted_iota(jnp.int32, (1, pair_w), 1) < SSD_HEAD_DIM
    gn = SSD_GROUPS * SSD_STATE
    hpg = SSD_HEADS // SSD_GROUPS
    for g in range(SSD_GROUPS):
        bg = bc[:, g * SSD_STATE:(g + 1) * SSD_STATE]
        cg = bc[:, gn + g * SSD_STATE:gn + (g + 1) * SSD_STATE].astype(BF16)
        cb = _dot_nt(cg, bg.astype(BF16))
        bg_t = bg.T
        y_off = _dot(cg, prev_ref[g].astype(BF16))
        for pr in range(hpg // 2):
            cols = slice((g * hpg + 2 * pr) * SSD_HEAD_DIM, (g * hpg + 2 * pr + 2) * SSD_HEAD_DIM)
            rcols = slice(2 * pr * SSD_HEAD_DIM, (2 * pr + 2) * SSD_HEAD_DIM)
            x_pair = xs[:, cols]
            x_bf = x_pair.astype(BF16)
            y_diag, st, exp_a, exp_last = [], [], [], []
            for k in range(2):
                row = SM_DT + g * hpg + 2 * pr + k
                a_row = cs_t[row:row + 1, :]
                dt_row = dt_t[row:row + 1, :]
                last = cs_t[row:row + 1, c - 1:c]
                a_col = jnp.broadcast_to(cs[:, row:row + 1], (c, c))
                seg = jnp.where(tril, jnp.exp(jnp.where(tril, a_col - a_row, 0.0)), 0.0)
                y_diag.append(_dot((cb * seg * dt_row).astype(BF16), x_bf))
                st.append(_dot((bg_t * (dt_row * jnp.exp(last - a_row))).astype(BF16), x_bf))
                exp_a.append(jnp.exp(a_col))
                exp_last.append(jnp.exp(last))
            y_ref[:, cols] = (jnp.where(first_head, y_diag[0], y_diag[1])
                              + y_off[:, rcols] * jnp.where(first_head, exp_a[0], exp_a[1])
                              + dsk_ref[:, cols] * x_pair)
            prev_ref[g, :, rcols] = (jnp.where(first_head_row, exp_last[0], exp_last[1]) * prev_ref[g, :, rcols]
                                     + jnp.where(first_head, st[0], st[1]))
    gated = y_ref[...] * _silu(z_ref[...].astype(F32))
    gw = SSD_INNER // SSD_GROUPS
    for g in range(SSD_GROUPS):
        sl = slice(g * gw, (g + 1) * gw)
        o_ref[:, sl] = _rms(gated[:, sl], ng_ref[:, sl]).astype(o_ref.dtype)


def ssd_mixer(p, sm, cw, cb, dtb_col, alog_col, dskip_row, ng, batch, seq):
    n = seq // CHUNK
    bcw = 2 * SSD_GROUPS * SSD_STATE

    def cur(width, colbase):
        return pl.BlockSpec((CHUNK, width), lambda b, i: (b * n + i, colbase // width))

    def prv(width, colbase):
        return pl.BlockSpec((CHUNK, width), lambda b, i: (b * n + jnp.maximum(i - 1, 0), colbase // width))

    def const(shape):
        return pl.BlockSpec(shape, lambda b, i: (0, 0))

    return pl.pallas_call(
        _ssd_kernel,
        grid=(batch, n),
        in_specs=[
            cur(SSD_INNER, COL_Z), cur(SSD_INNER, COL_XS), cur(bcw, COL_BC),
            prv(SSD_INNER, COL_XS), prv(bcw, COL_BC),
            pl.BlockSpec((CHUNK, SM_W), lambda b, i: (b * n + i, 0)),
            const((SSD_CONV, SSD_INNER)), const((1, SSD_INNER)),
            const((SSD_CONV, bcw)), const((1, bcw)),
            const((SM_W, 1)), const((SM_W, 1)),
            const((1, SSD_INNER)), const((1, SSD_INNER)),
        ],
        out_specs=pl.BlockSpec((CHUNK, SSD_INNER), lambda b, i: (b * n + i, 0)),
        out_shape=jax.ShapeDtypeStruct((batch * seq, SSD_INNER), BF16),
        scratch_shapes=[
            pltpu.VMEM((SSD_GROUPS, SSD_STATE, SSD_INNER // SSD_GROUPS), F32),
            pltpu.VMEM((CHUNK, SSD_INNER), F32),
            pltpu.VMEM((CONV_PAD + CHUNK, SSD_INNER), F32),
            pltpu.VMEM((CONV_PAD + CHUNK, bcw), F32),
        ],
        compiler_params=_cparams(("parallel", "arbitrary")),
        name="ssd_mixer",
    )(p, p, p, p, p, sm, cw[:, :SSD_INNER], cb[:, :SSD_INNER], cw[:, SSD_INNER:], cb[:, SSD_INNER:],
      dtb_col, alog_col, dskip_row, ng)


MERGE_TM = 256


def _merge_kernel(x_ref, gl_ref, gb_ref, oret_ref, ofox_ref, odsa_ref, ossd_ref, wbr_ref, wout_ref, o_ref):
    branches = (oret_ref, ofox_ref, odsa_ref, ossd_ref)
    merged = None
    row0 = 0
    for bi, br in enumerate(branches):
        width = br.shape[1]
        sl = slice(bi * D_MODEL, (bi + 1) * D_MODEL)
        gate = 1.0 / (1.0 + jnp.exp(-(gl_ref[:, sl].astype(F32) + gb_ref[:, sl])))
        term = gate * _dot(br[...], wbr_ref[row0:row0 + width, :])
        merged = term if merged is None else merged + term
        row0 += width
    o_ref[...] = x_ref[...] + _dot(merged.astype(BF16), wout_ref[...])


def merge_project(x, p, gate_b, o_ret, o_fox, o_dsa, o_ssd, w_br, w_out, layer):
    m = x.shape[0]
    tm = MERGE_TM

    def rows(width):
        return pl.BlockSpec((tm, width), lambda i: (i, 0))

    def const(shape):
        return pl.BlockSpec(shape, lambda i: (0, 0), pipeline_mode=pl.Buffered(1))

    def stacked(w):
        return pl.BlockSpec((None,) + w.shape[1:], lambda i: (layer, 0, 0), pipeline_mode=pl.Buffered(1))

    return pl.pallas_call(
        _merge_kernel,
        grid=(m // tm,),
        in_specs=[
            rows(D_MODEL), rows(N_BRANCH * D_MODEL), const((1, N_BRANCH * D_MODEL)),
            rows(MIX_W), rows(MIX_W), rows(MIX_W), rows(SSD_INNER),
            stacked(w_br), stacked(w_out),
        ],
        out_specs=rows(D_MODEL),
        out_shape=jax.ShapeDtypeStruct(x.shape, x.dtype),
        compiler_params=_cparams(("parallel",)),
        name="merge_project",
    )(x, p, gate_b, o_ret, o_fox, o_dsa, o_ssd, w_br, w_out)


FFN_TM = 1024
FFN_TF = 512


def _ffn_kernel(x_ref, g_ref, w1_ref, w2_ref, o_ref, h_ref):
    @pl.when(pl.program_id(1) == 0)
    def _():
        h_ref[...] = _rms(x_ref[...], g_ref[...]).astype(BF16)
        o_ref[...] = x_ref[...]

    a = jnp.maximum(_dot(h_ref[...], w1_ref[...]), 0.0)
    o_ref[...] += _dot((a * a).astype(BF16), w2_ref[...])


def ffn(x, g, w1, w2, layer):
    m, d = x.shape
    dff = w1.shape[2]
    tm, tf = min(FFN_TM, m), FFN_TF
    return pl.pallas_call(
        _ffn_kernel,
        grid=(m // tm, dff // tf),
        in_specs=[
            pl.BlockSpec((tm, d), lambda i, f: (i, 0)),
            pl.BlockSpec((1, d), lambda i, f: (0, 0)),
            pl.BlockSpec((None, d, tf), lambda i, f: (layer, 0, f)),
            pl.BlockSpec((None, tf, d), lambda i, f: (layer, f, 0)),
        ],
        out_specs=pl.BlockSpec((tm, d), lambda i, f: (i, 0)),
        out_shape=jax.ShapeDtypeStruct(x.shape, x.dtype),
        scratch_shapes=[pltpu.VMEM((tm, d), BF16)],
        compiler_params=_cparams(("parallel", "arbitrary")),
        name="ffn",
    )(x, g, w1, w2)


SRC_RET = 0
SRC_FOX = SRC_RET + 4 * MIX_W
SRC_FF = SRC_FOX + 3 * MIX_W
SRC_CQ = SRC_FF + N_HEADS
SRC_DK = SRC_CQ + DSA_Q_RANK
SRC_IK = SRC_DK + 2 * HEAD_DIM
SRC_IW = SRC_IK + IDX_DIM
SRC_Z = SRC_IW + IDX_HEADS
SRC_DT = SRC_Z + 2 * SSD_INNER + 2 * SSD_GROUPS * SSD_STATE
SRC_GATE = SRC_DT + SSD_HEADS
MAIN_RUNS = ((COL_GATE, SRC_GATE), (COL_RET, SRC_RET), (COL_Z, SRC_Z), (COL_CQ, SRC_CQ),
             (COL_FOX, SRC_FOX), (COL_DK, SRC_DK))
SMALL_PIECES = ((SRC_DT, SM_DT, SSD_HEADS), (SRC_FF, SM_F, N_HEADS), (SRC_IW, SM_IW, IDX_HEADS),
                (SRC_IK, SM_IK, IDX_DIM))


def _layout_in_proj_t(w_in):
    w_t = jnp.swapaxes(w_in, 1, 2)
    depth, _, d = w_t.shape
    ends = [dst for dst, _ in MAIN_RUNS[1:]] + [N_MAIN_USED]
    main = [w_t[:, src:src + (end - dst), :] for (dst, src), end in zip(MAIN_RUNS, ends)]
    main.append(jnp.zeros((depth, N_MAIN - N_MAIN_USED, d), w_t.dtype))
    small, lane = [], 0
    for src, dst, width in SMALL_PIECES:
        if dst > lane:
            small.append(jnp.zeros((depth, dst - lane, d), w_t.dtype))
        small.append(w_t[:, src:src + width, :])
        lane = dst + width
    return jnp.concatenate(main, axis=1).astype(BF16), jnp.concatenate(small, axis=1).astype(BF16)


def _pad_to(v, offset, total):
    return jnp.zeros((total,), v.dtype).at[offset:offset + v.shape[0]].set(v)


def _rotary_tables(seq):
    half = HEAD_DIM // 2
    inv = 1.0 / (10000.0 ** (jnp.arange(half, dtype=F32) / half))
    ang = jnp.arange(seq, dtype=F32)[:, None] * inv[None, :]
    cos, sin = jnp.cos(ang), jnp.sin(ang)
    return jnp.concatenate([cos, cos], axis=1), jnp.concatenate([-sin, sin], axis=1)


def kernel(x, norm1_g, w_in, gate_b, fox_f_b, fox_qn_g, fox_kn_g, dsa_cq_g, dsa_w_uq, dsa_w_qidx, dsa_qn_g,
           dsa_kn_g, rel_bias, ssd_conv_w, ssd_conv_b, ssd_dt_bias, ssd_a_log, ssd_d, ssd_norm_g, w_br, w_out,
           norm2_g, w_ff1, w_ff2):
    batch, seq, d = x.shape
    tokens = batch * seq
    xt = x.reshape(tokens, d)
    cos, sin = _rotary_tables(seq)
    tm = min(1024, tokens)
    w_main, w_small = _layout_in_proj_t(w_in)
    w_br_bf, w_out_bf = w_br.astype(BF16), w_out.astype(BF16)
    w_ff1_bf, w_ff2_bf = w_ff1.astype(BF16), w_ff2.astype(BF16)
    for l in range(DEPTH):
        g1 = norm1_g[l][None, :]
        p = norm_matmul(xt, g1, w_main, l, BF16, tm, MAIN_TN)
        sm = norm_matmul(xt, g1, w_small, l, F32, tm, SM_W)

        o_ret = retention(p, cos, sin, batch, seq)

        fb_row = _pad_to(fox_f_b[l], SM_F, SM_W)[None, :]
        fcol, frow = fox_prep(sm, fb_row, batch, seq)
        o_fox = fox_attention(p, fcol, frow, fox_qn_g[l][None, :], fox_kn_g[l][None, :], batch, seq)

        o_dsa = dsa_attention(p, sm, dsa_cq_g[l][None, :], dsa_w_uq[l].T.astype(BF16), dsa_w_qidx[l].T.astype(BF16),
                              dsa_qn_g[l][:, None], dsa_kn_g[l][None, :], rel_bias, batch, seq)

        o_ssd = ssd_mixer(p, sm, ssd_conv_w[l], ssd_conv_b[l][None, :],
                          _pad_to(ssd_dt_bias[l], SM_DT, SM_W)[:, None], _pad_to(ssd_a_log[l], SM_DT, SM_W)[:, None],
                          jnp.repeat(ssd_d[l], SSD_HEAD_DIM)[None, :], ssd_norm_g[l][None, :], batch, seq)

        xt = merge_project(xt, p, gate_b[l][None, :], o_ret, o_fox, o_dsa, o_ssd, w_br_bf, w_out_bf, l)
        xt = ffn(xt, norm2_g[l][None, :], w_ff1_bf, w_ff2_bf, l)
    return xt.reshape(batch, seq, d)
```

```python
import functools
import math

import jax
import jax.numpy as jnp
from jax import lax
from jax.experimental import pallas as pl
from jax.experimental.pallas import tpu as pltpu

F32 = jnp.float32
BF16 = jnp.bfloat16

D_MODEL = 2048
DEPTH = 4
HEAD_DIM = 128
N_HEADS = 4
DSA_Q_RANK = 512
IDX_HEADS = 16
IDX_DIM = 64
DSA_TOPK = 256
SSD_HEADS = 16
SSD_HEAD_DIM = 64
SSD_GROUPS = 2
SSD_STATE = 128
SSD_CONV = 4
SSD_INNER = SSD_HEADS * SSD_HEAD_DIM
D_FF = 4 * D_MODEL
N_BUCKETS = 32
MAX_DISTANCE = 128
CHUNK = 128
EPS = 1e-6
N_BRANCH = 4
MIX_W = N_HEADS * HEAD_DIM

COL_GATE = 0
COL_RET = COL_GATE + N_BRANCH * D_MODEL
COL_Z = COL_RET + 4 * MIX_W
COL_XS = COL_Z + SSD_INNER
COL_BC = COL_XS + SSD_INNER
COL_CQ = COL_BC + 2 * SSD_GROUPS * SSD_STATE
COL_FOX = COL_CQ + DSA_Q_RANK
COL_DK = COL_FOX + 3 * MIX_W
COL_DV = COL_DK + HEAD_DIM
N_MAIN_USED = COL_DV + HEAD_DIM
N_MAIN = 15360
MAIN_TN = 1536
SM_DT = 0
SM_F = 16
SM_IW = 32
SM_IK = 64
SM_W = 128

LANES = 128
SUBLANES = 8
VMEM_LIMIT = 56 * 1024 * 1024
NEG_BIG = -1e30


def _cparams(sem):
    return pltpu.CompilerParams(dimension_semantics=sem, vmem_limit_bytes=VMEM_LIMIT)


def _dot(a, b):
    return jnp.dot(a, b, preferred_element_type=F32)


def _dot_nt(a, b):
    return lax.dot_general(a, b, (((1,), (1,)), ((), ())), preferred_element_type=F32)


def _rms(x, g):
    return x * lax.rsqrt(jnp.mean(x * x, axis=-1, keepdims=True) + EPS) * g


def _silu(x):
    return x / (1.0 + jnp.exp(-x))


def _softplus(x):
    return jnp.maximum(x, 0.0) + jnp.log1p(jnp.exp(-jnp.abs(x)))


def _cumsum_lanes(x):
    lane = lax.broadcasted_iota(jnp.int32, x.shape, 1)
    d = 1
    while d < x.shape[1]:
        x = x + jnp.where(lane >= d, pltpu.roll(x, d, 1), 0.0)
        d *= 2
    return x


def _fold_rows(x, op):
    return op(x.reshape(x.shape[0] // SUBLANES, SUBLANES, x.shape[1]), axis=0)


def _norm_matmul_kernel(x_ref, g_ref, w_ref, o_ref, h_ref):
    @pl.when(pl.program_id(1) == 0)
    def _():
        h_ref[...] = _rms(x_ref[...], g_ref[...]).astype(BF16)

    o_ref[...] = _dot_nt(h_ref[...], w_ref[...]).astype(o_ref.dtype)


def norm_matmul(x, g, w_t, layer, out_dtype, tm, tn):
    m, d = x.shape
    n = w_t.shape[1]
    return pl.pallas_call(
        _norm_matmul_kernel,
        grid=(m // tm, n // tn),
        in_specs=[
            pl.BlockSpec((tm, d), lambda i, j: (i, 0)),
            pl.BlockSpec((1, d), lambda i, j: (0, 0)),
            pl.BlockSpec((None, tn, d), lambda i, j: (layer, j, 0)),
        ],
        out_specs=pl.BlockSpec((tm, tn), lambda i, j: (i, j)),
        out_shape=jax.ShapeDtypeStruct((m, n), out_dtype),
        scratch_shapes=[pltpu.VMEM((tm, d), BF16)],
        compiler_params=_cparams(("parallel", "arbitrary")),
        name="norm_matmul",
    )(x, g, w_t)


def _retention_kernel(q_ref, k_ref, v_ref, g_ref, cos_ref, sin_ref, o_ref, state_ref):
    c = CHUNK

    @pl.when(pl.program_id(1) == 0)
    def _():
        state_ref[...] = jnp.zeros_like(state_ref)

    cos = cos_ref[...]
    sin = sin_ref[...]
    ii = lax.broadcasted_iota(jnp.int32, (c, c), 0)
    jj = lax.broadcasted_iota(jnp.int32, (c, c), 1)
    rel = (ii - jj).astype(F32)
    i_col = lax.broadcasted_iota(jnp.int32, (c, 1), 0).astype(F32)
    for h in range(N_HEADS):
        lg = math.log1p(-(2.0 ** (-5.0 - h)))
        sl = slice(h * HEAD_DIM, (h + 1) * HEAD_DIM)
        q = q_ref[:, sl].astype(F32)
        k = k_ref[:, sl].astype(F32)
        v = v_ref[:, sl]
        qr = q * cos + pltpu.roll(q, HEAD_DIM // 2, 1) * sin
        kr = (k * cos + pltpu.roll(k, HEAD_DIM // 2, 1) * sin) * (HEAD_DIM ** -0.5)
        decay = jnp.where(rel >= 0, jnp.exp(lg * jnp.maximum(rel, 0.0)), 0.0)
        scores = _dot_nt(qr.astype(BF16), kr.astype(BF16)) * decay
        y = _dot(scores.astype(BF16), v)
        q_dec = jnp.exp(lg * (i_col + 1.0))
        k_dec = jnp.exp(lg * (c - 1.0 - i_col))
        st = state_ref[h]
        y = y + _dot((qr * q_dec).astype(BF16), st.astype(BF16))
        kv = lax.dot_general((kr * k_dec).astype(BF16), v, (((0,), (0,)), ((), ())), preferred_element_type=F32)
        state_ref[h] = math.exp(lg * c) * st + kv
        yc = y - jnp.mean(y, axis=-1, keepdims=True)
        yn = yc * lax.rsqrt(jnp.mean(yc * yc, axis=-1, keepdims=True) + EPS)
        o_ref[:, sl] = (_silu(g_ref[:, sl].astype(F32)) * yn).astype(o_ref.dtype)


def retention(p, cos, sin, batch, seq):
    n = seq // CHUNK
    base = COL_RET // MIX_W

    def col(j):
        return pl.BlockSpec((CHUNK, MIX_W), lambda b, i: (b * n + i, base + j))

    tab = pl.BlockSpec((CHUNK, HEAD_DIM), lambda b, i: (i, 0))
    return pl.pallas_call(
        _retention_kernel,
        grid=(batch, n),
        in_specs=[col(0), col(1), col(2), col(3), tab, tab],
        out_specs=pl.BlockSpec((CHUNK, MIX_W), lambda b, i: (b * n + i, 0)),
        out_shape=jax.ShapeDtypeStruct((batch * seq, MIX_W), BF16),
        scratch_shapes=[pltpu.VMEM((N_HEADS, HEAD_DIM, HEAD_DIM), F32)],
        compiler_params=_cparams(("parallel", "arbitrary")),
        name="retention",
    )(p, p, p, p, cos, sin)


FOX_T = 256


def _fox_kernel(q_ref, k_ref, v_ref, sm_ref, fbias_ref, qg_ref, kg_ref, o_ref,
                kn_ref, vt_ref, fb_ref, frow_ref, qt_ref, acc_ref):
    i = pl.program_id(1)
    t = FOX_T
    nkc = vt_ref.shape[0]
    sub = t // CHUNK

    @pl.when(i == 0)
    def _():
        carry = jnp.zeros((SM_W, 1), F32)
        for c in range(frow_ref.shape[0]):
            rows = slice(c * CHUNK, (c + 1) * CHUNK)
            logit = sm_ref[rows, :] + fbias_ref[...]
            log_f = jnp.minimum(logit, 0.0) - jnp.log1p(jnp.exp(-jnp.abs(logit)))
            cs = _cumsum_lanes(log_f.T) + carry
            carry = cs[:, LANES - 1:LANES]
            frow_ref[c] = cs[SM_F:SM_F + SUBLANES, :]
            cs_col = cs.T
            for h in range(N_HEADS):
                fb_ref[h, rows, :] = jnp.broadcast_to(cs_col[:, SM_F + h:SM_F + h + 1], (CHUNK, LANES))
        for h in range(N_HEADS):
            sl = slice(h * HEAD_DIM, (h + 1) * HEAD_DIM)
            kn_ref[:, sl] = _rms(k_ref[:, sl].astype(F32), kg_ref[...]).astype(BF16)
            for j in range(nkc):
                for c in range(sub):
                    rows = slice(j * t + c * CHUNK, j * t + (c + 1) * CHUNK)
                    vt_ref[j, h, :, c * CHUNK:(c + 1) * CHUNK] = v_ref[rows, sl].astype(F32).T.astype(BF16)

    fqs = []
    for h in range(N_HEADS):
        sl = slice(h * HEAD_DIM, (h + 1) * HEAD_DIM)
        qn = _rms(q_ref[:, sl].astype(F32), qg_ref[...]) * (HEAD_DIM ** -0.5)
        qt_ref[h] = jnp.concatenate([qn[c * CHUNK:(c + 1) * CHUNK, :].T for c in range(sub)], axis=1).astype(BF16)
        fqs.append(jnp.concatenate([frow_ref[i * sub + c, h:h + 1, :] for c in range(sub)], axis=1))
        acc_ref[h] = jnp.zeros(acc_ref.shape[1:], F32)

    kofs = lax.broadcasted_iota(jnp.int32, (t, t), 0)
    qofs = lax.broadcasted_iota(jnp.int32, (t, t), 1)

    def body(diag, j, carry):
        ms, ls = carry
        start = pl.multiple_of(j * t, t)
        scores = [_dot(kn_ref[pl.ds(start, t), h * HEAD_DIM:(h + 1) * HEAD_DIM], qt_ref[h])
                  for h in range(N_HEADS)]
        new_ms, new_ls, ps, alphas = [], [], [], []
        for h in range(N_HEADS):
            fk = fb_ref[h, pl.ds(start, t), :]
            s = scores[h] + fqs[h] - jnp.concatenate([fk] * (t // LANES), axis=1)
            if diag:
                s = jnp.where(kofs <= qofs, s, NEG_BIG)
            m_new = jnp.maximum(ms[h], jnp.max(s, axis=0, keepdims=True))
            p = jnp.exp(s - m_new)
            alpha = jnp.exp(ms[h] - m_new)
            new_ls.append(alpha * ls[h] + jnp.sum(p, axis=0, keepdims=True))
            ps.append(p.astype(BF16))
            alphas.append(alpha)
            new_ms.append(m_new)
        for h in range(N_HEADS):
            acc_ref[h] = alphas[h] * acc_ref[h] + _dot(vt_ref[j, h], ps[h])
        return tuple(new_ms), tuple(new_ls)

    init = (tuple(jnp.full((1, t), NEG_BIG, F32) for _ in range(N_HEADS)),
            tuple(jnp.zeros((1, t), F32) for _ in range(N_HEADS)))
    carry = lax.fori_loop(0, i, functools.partial(body, False), init)
    _, ls = body(True, i, carry)
    for h in range(N_HEADS):
        out_t = acc_ref[h] / ls[h]
        for c in range(sub):
            o_ref[c * CHUNK:(c + 1) * CHUNK, h * HEAD_DIM:(h + 1) * HEAD_DIM] = (
                out_t[:, c * CHUNK:(c + 1) * CHUNK].T.astype(o_ref.dtype))


def fox_attention(p, sm, fbias_row, qg, kg, batch, seq):
    nq = seq // FOX_T
    base = COL_FOX // MIX_W
    return pl.pallas_call(
        _fox_kernel,
        grid=(batch, nq),
        in_specs=[
            pl.BlockSpec((FOX_T, MIX_W), lambda b, i: (b * nq + i, base)),
            pl.BlockSpec((seq, MIX_W), lambda b, i: (b, base + 1)),
            pl.BlockSpec((seq, MIX_W), lambda b, i: (b, base + 2)),
            pl.BlockSpec((seq, SM_W), lambda b, i: (b, 0)),
            pl.BlockSpec((1, SM_W), lambda b, i: (0, 0)),
            pl.BlockSpec((1, HEAD_DIM), lambda b, i: (0, 0)),
            pl.BlockSpec((1, HEAD_DIM), lambda b, i: (0, 0)),
        ],
        out_specs=pl.BlockSpec((FOX_T, MIX_W), lambda b, i: (b * nq + i, 0)),
        out_shape=jax.ShapeDtypeStruct((batch * seq, MIX_W), BF16),
        scratch_shapes=[
            pltpu.VMEM((seq, MIX_W), BF16),
            pltpu.VMEM((nq, N_HEADS, HEAD_DIM, FOX_T), BF16),
            pltpu.VMEM((N_HEADS, seq, LANES), F32),
            pltpu.VMEM((seq // CHUNK, SUBLANES, CHUNK), F32),
            pltpu.VMEM((N_HEADS, HEAD_DIM, FOX_T), BF16),
            pltpu.VMEM((N_HEADS, HEAD_DIM, FOX_T), F32),
        ],
        compiler_params=_cparams(("parallel", "arbitrary")),
        name="fox_attention",
    )(p, p, p, sm, fbias_row, qg, kg)


DSA_TQ = 128
DSA_TK = 256
BAND_W = 2 * DSA_TQ
COUNT_ROWS = 64
BISECT_FIXED_STEPS = 16


def _t5_bucket(dist):
    max_exact = N_BUCKETS // 2
    d = jnp.maximum(dist, 0)
    log_ratio = jnp.log(jnp.maximum(d, 1).astype(F32) / max_exact) / math.log(MAX_DISTANCE / max_exact)
    large = jnp.minimum(max_exact + (log_ratio * (N_BUCKETS - max_exact)).astype(jnp.int32), N_BUCKETS - 1)
    return jnp.where(d < max_exact, d, large)


def _dsa_kernel(cq_ref, k_ref, v_ref, smq_ref, smk_ref, cqg_ref, wuq_ref, wqi_ref, qg_ref, kg_ref, rb_ref,
                o_ref, kn_ref, ki_ref, vt_ref, band_ref, sc_ref, qt_ref, xi_ref, acc_ref, s_ref, *, topk):
    b = pl.program_id(0)
    i = pl.program_id(1)
    tq, tk = DSA_TQ, DSA_TK
    nkc = sc_ref.shape[0]
    seq = nkc * tk
    tiles = tk // tq
    nb = ((i + 1) * tq + tk - 1) // tk

    @pl.when(jnp.logical_and(b == 0, i == 0))
    def _():
        c = lax.broadcasted_iota(jnp.int32, (BAND_W, tq), 0)
        r = lax.broadcasted_iota(jnp.int32, (BAND_W, tq), 1)
        bucket = _t5_bucket(tq + r - c)
        for h in range(N_HEADS):
            far = rb_ref[N_BUCKETS - 1, h]
            acc = jnp.zeros((BAND_W, tq), F32)
            for bk in range(N_BUCKETS - 1):
                acc = jnp.where(bucket == bk, rb_ref[bk, h] - far, acc)
            band_ref[h] = acc

    @pl.when(i == 0)
    def _():
        kn_ref[...] = _rms(k_ref[...].astype(F32), kg_ref[...]).astype(BF16)
        ki_ref[...] = smk_ref[:, SM_IK:SM_IK + IDX_DIM].astype(BF16)
        for j in range(nkc):
            for t in range(tiles):
                rows = slice(j * tk + t * tq, j * tk + (t + 1) * tq)
                vt_ref[j, :, t * tq:(t + 1) * tq] = v_ref[rows, :].astype(F32).T.astype(BF16)

    cq_t = _rms(cq_ref[...].astype(F32), cqg_ref[...]).T.astype(BF16)
    q_t = _dot(wuq_ref[...], cq_t)
    g_col = jnp.broadcast_to(qg_ref[...], (HEAD_DIM, tq))
    for h in range(N_HEADS):
        x = q_t[h * HEAD_DIM:(h + 1) * HEAD_DIM, :]
        inv = lax.rsqrt(jnp.mean(x * x, axis=0, keepdims=True) + EPS)
        qt_ref[:, h * tq:(h + 1) * tq] = (x * inv * g_col * (HEAD_DIM ** -0.5)).astype(BF16)
    qi_t = (_dot(wqi_ref[...], cq_t) * (IDX_DIM ** -0.5)).astype(BF16)
    for h in range(IDX_HEADS):
        xi_ref[:, h * tq:(h + 1) * tq] = qi_t[h * IDX_DIM:(h + 1) * IDX_DIM, :]
    w_rows = smq_ref[...].T[SM_IW:SM_IW + IDX_HEADS, :] * (IDX_HEADS ** -0.5)

    kofs = lax.broadcasted_iota(jnp.int32, (tk, tq), 0)
    qpos = lax.broadcasted_iota(jnp.int32, (tk, tq), 1) + i * tq

    def score_body(j, _):
        start = pl.multiple_of(j * tk, tk)
        kj = ki_ref[pl.ds(start, tk), :]
        acc = jnp.zeros((tk, tq), F32)
        for h2 in range(IDX_HEADS // 2):
            r = _dot(kj, xi_ref[:, 2 * h2 * tq:(2 * h2 + 2) * tq])
            for h in (2 * h2, 2 * h2 + 1):
                acc = acc + w_rows[h:h + 1, :] * jnp.maximum(r[:, (h - 2 * h2) * tq:(h - 2 * h2 + 1) * tq], 0.0)
        sc_ref[j] = jnp.where(kofs + start <= qpos, acc, -jnp.inf)
        return 0

    lax.fori_loop(0, nb, score_body, 0)

    @pl.when(nb % 2 == 1)
    def _():
        sc_ref[jnp.minimum(nb, nkc - 1)] = jnp.full((tk, tq), -jnp.inf, F32)

    extents = sorted({min(n, nkc) for n in range(2, nkc + 2, 2)})

    def over_chunks(fn, init):
        acc = init
        for j in range(nkc):
            acc = lax.cond(j < nb, functools.partial(fn, j), lambda a: a, acc)
        return acc

    def count_where(pred_fn):
        def walk(n_chunks):
            def run():
                a = jnp.zeros((COUNT_ROWS, tq), F32)
                for j in range(n_chunks):
                    hit = jnp.where(pred_fn(j, sc_ref[j]), 1.0, 0.0)
                    a = a + jnp.sum(hit.reshape(tk // COUNT_ROWS, COUNT_ROWS, tq), axis=0)
                return a
            return run

        a = lax.switch((nb - 1) // 2, [walk(n) for n in extents])
        return jnp.sum(a, axis=0, keepdims=True)

    def search():
        kf = float(topk)
        smax = jnp.max(over_chunks(lambda j, a: jnp.maximum(a, _fold_rows(sc_ref[j], jnp.max)),
                                   jnp.full((SUBLANES, tq), -jnp.inf, F32)), axis=0, keepdims=True)
        smin = jnp.min(over_chunks(
            lambda j, a: jnp.minimum(a, _fold_rows(jnp.where(sc_ref[j] == -jnp.inf, jnp.inf, sc_ref[j]), jnp.min)),
            jnp.full((SUBLANES, tq), jnp.inf, F32)), axis=0, keepdims=True)

        def count_ge(t):
            return count_where(lambda j, x: x >= t)

        def midpoint(lo, hi):
            return jnp.where(hi == jnp.inf, smax, 0.5 * (lo + hi))

        def undecided(lo, hi, c_lo, mid):
            return jnp.logical_and(c_lo != kf, jnp.logical_and(mid > lo, mid < hi))

        def step(state):
            lo, hi, c_lo, c_hi, mid = state
            upd = undecided(lo, hi, c_lo, mid)
            cnt = count_ge(mid)
            up = jnp.logical_and(upd, cnt >= kf)
            dn = jnp.logical_and(upd, cnt < kf)
            lo = jnp.where(up, mid, lo)
            c_lo = jnp.where(up, cnt, c_lo)
            hi = jnp.where(dn, mid, hi)
            c_hi = jnp.where(dn, cnt, c_hi)
            return lo, hi, c_lo, c_hi, midpoint(lo, hi)

        def any_undecided(state):
            lo, hi, c_lo, _, mid = state
            return jnp.max(jnp.where(undecided(lo, hi, c_lo, mid), 1.0, 0.0))

        def cond(carry):
            return jnp.logical_and(carry[0] < 200, carry[1] > 0.0)

        def body(carry):
            state = step(step(carry[2]))
            return carry[0] + 1, any_undecided(state), state

        lo0 = smin
        hi0 = jnp.full((1, tq), jnp.inf, F32)
        c_lo0 = (lax.broadcasted_iota(jnp.int32, (1, tq), 1) + (i * tq + 1)).astype(F32)
        c_hi0 = jnp.zeros((1, tq), F32)
        state = (lo0, hi0, c_lo0, c_hi0, midpoint(lo0, hi0))
        state = lax.fori_loop(0, BISECT_FIXED_STEPS, lambda _, s: step(s), state)
        _, _, (lo, hi, c_lo, c_hi, _) = lax.while_loop(
            cond, body, (jnp.int32(0), any_undecided(state), state))

        def tie_search():
            need = kf - c_hi

            def tie_body(_, carry):
                jlo, jhi = carry
                jm = (jlo + jhi) // 2
                cnt = count_where(lambda j, x: jnp.logical_and(jnp.logical_and(x >= lo, x < hi),
                                                               kofs + j * tk <= jm))
                ok = cnt >= need
                return jnp.where(ok, jlo, jm), jnp.where(ok, jm, jhi)

            n_bits = int(math.ceil(math.log2(seq))) + 1
            _, jmax = lax.fori_loop(0, n_bits, tie_body,
                                    (jnp.full((1, tq), -1, jnp.int32), jnp.full((1, tq), seq - 1, jnp.int32)))
            return jmax

        any_tie = jnp.max(jnp.where(c_lo != kf, 1.0, 0.0)) > 0.0
        jmax = lax.cond(any_tie, tie_search, lambda: jnp.full((1, tq), seq - 1, jnp.int32))
        return lo, hi, jmax

    def keep_all():
        return (jnp.full((1, tq), -jnp.inf, F32), jnp.full((1, tq), jnp.inf, F32),
                jnp.full((1, tq), seq - 1, jnp.int32))

    lo, hi, jmax = lax.cond((i + 1) * tq > topk, search, keep_all)

    def attend(n_chunks):
        def run():
            mx = [jnp.full((SUBLANES, tq), NEG_BIG, F32) for _ in range(N_HEADS)]
            for j in range(n_chunks):
                near = j >= n_chunks - 3
                s_all = _dot(kn_ref[j * tk:(j + 1) * tk, :], qt_ref[...])
                sc = sc_ref[j]
                kpos = kofs + j * tk
                keep = jnp.logical_or(sc >= hi, jnp.logical_and(sc >= lo, kpos <= jmax))
                if near:
                    keep = jnp.logical_and(keep, kpos <= qpos)
                for h in range(N_HEADS):
                    hs = slice(h * tq, (h + 1) * tq)
                    s = s_all[:, hs]
                    if near:
                        s = s + jnp.concatenate(
                            [jnp.where(j * tiles + t == i, band_ref[h, tq:2 * tq, :],
                                       jnp.where(j * tiles + t == i - 1, band_ref[h, 0:tq, :], 0.0))
                             for t in range(tiles)], axis=0)
                    s = jnp.where(keep, s, NEG_BIG)
                    s_ref[j, :, hs] = s
                    mx[h] = jnp.maximum(mx[h], _fold_rows(s, jnp.max))
            m_all = jnp.concatenate([jnp.max(x, axis=0, keepdims=True) for x in mx], axis=1)
            acc = jnp.zeros(acc_ref.shape, F32)
            l_part = jnp.zeros((SUBLANES, N_HEADS * tq), F32)
            for j in range(n_chunks):
                p = jnp.exp(s_ref[j] - m_all)
                l_part = l_part + _fold_rows(p, jnp.sum)
                acc = acc + _dot(vt_ref[j], p.astype(BF16))
            acc_ref[...] = acc
            return jnp.sum(l_part, axis=0, keepdims=True)
        return run

    l_all = lax.switch((nb - 1) // 2, [attend(n) for n in extents])
    for h in range(N_HEADS):
        hs = slice(h * tq, (h + 1) * tq)
        out_t = acc_ref[:, hs] / l_all[:, hs]
        o_ref[:, h * HEAD_DIM:(h + 1) * HEAD_DIM] = out_t.T.astype(o_ref.dtype)


def dsa_attention(p, sm, cqg, wuq_t, wqi_t, qg_col, kg, rel_bias, batch, seq):
    nq = seq // DSA_TQ
    nkc = seq // DSA_TK
    topk = min(DSA_TOPK, seq // 4)
    kern = functools.partial(_dsa_kernel, topk=topk)
    return pl.pallas_call(
        kern,
        grid=(batch, nq),
        in_specs=[
            pl.BlockSpec((DSA_TQ, DSA_Q_RANK), lambda b, i: (b * nq + i, COL_CQ // DSA_Q_RANK)),
            pl.BlockSpec((seq, HEAD_DIM), lambda b, i: (b, COL_DK // HEAD_DIM)),
            pl.BlockSpec((seq, HEAD_DIM), lambda b, i: (b, COL_DV // HEAD_DIM)),
            pl.BlockSpec((DSA_TQ, SM_W), lambda b, i: (b * nq + i, 0)),
            pl.BlockSpec((seq, SM_W), lambda b, i: (b, 0)),
            pl.BlockSpec((1, DSA_Q_RANK), lambda b, i: (0, 0)),
            pl.BlockSpec((N_HEADS * HEAD_DIM, DSA_Q_RANK), lambda b, i: (0, 0)),
            pl.BlockSpec((IDX_HEADS * IDX_DIM, DSA_Q_RANK), lambda b, i: (0, 0)),
            pl.BlockSpec((HEAD_DIM, 1), lambda b, i: (0, 0)),
            pl.BlockSpec((1, HEAD_DIM), lambda b, i: (0, 0)),
            pl.BlockSpec(memory_space=pltpu.SMEM),
        ],
        out_specs=pl.BlockSpec((DSA_TQ, MIX_W), lambda b, i: (b * nq + i, 0)),
        out_shape=jax.ShapeDtypeStruct((batch * seq, MIX_W), BF16),
        scratch_shapes=[
            pltpu.VMEM((seq, HEAD_DIM), BF16),
            pltpu.VMEM((seq, IDX_DIM), BF16),
            pltpu.VMEM((nkc, HEAD_DIM, DSA_TK), BF16),
            pltpu.VMEM((N_HEADS, BAND_W, DSA_TQ), F32),
            pltpu.VMEM((nkc, DSA_TK, DSA_TQ), F32),
            pltpu.VMEM((HEAD_DIM, N_HEADS * DSA_TQ), BF16),
            pltpu.VMEM((IDX_DIM, IDX_HEADS * DSA_TQ), BF16),
            pltpu.VMEM((HEAD_DIM, N_HEADS * DSA_TQ), F32),
            pltpu.VMEM((nkc, DSA_TK, N_HEADS * DSA_TQ), F32),
        ],
        compiler_params=_cparams(("arbitrary", "arbitrary")),
        name="dsa_attention",
    )(p, p, p, sm, sm, cqg, wuq_t, wqi_t, qg_col, kg, rel_bias)


CONV_PAD = 8


def _causal_conv(x_ref, xp_ref, first, w_ref, b_ref, ext_ref):
    ext_ref[0:CONV_PAD, :] = xp_ref[CHUNK - CONV_PAD:CHUNK, :].astype(F32) * first
    ext_ref[CONV_PAD:CONV_PAD + CHUNK, :] = x_ref[...].astype(F32)
    acc = b_ref[...] + ext_ref[CONV_PAD:CONV_PAD + CHUNK, :] * w_ref[SSD_CONV - 1:SSD_CONV, :]
    for d in range(1, SSD_CONV):
        acc = acc + ext_ref[CONV_PAD - d:CONV_PAD - d + CHUNK, :] * w_ref[SSD_CONV - 1 - d:SSD_CONV - d, :]
    return _silu(acc)


def _ssd_kernel(z_ref, xs_ref, bc_ref, xsp_ref, bcp_ref, sm_ref, cwx_ref, cbx_ref, cwb_ref, cbb_ref,
                dtb_ref, alog_ref, dsk_ref, ng_ref, o_ref, prev_ref, y_ref, extx_ref, extb_ref):
    c = CHUNK
    n = pl.program_id(1)

    @pl.when(n == 0)
    def _():
        prev_ref[...] = jnp.zeros_like(prev_ref)

    first = (n > 0).astype(F32)
    xs = _causal_conv(xs_ref, xsp_ref, first, cwx_ref, cbx_ref, extx_ref)
    bc = _causal_conv(bc_ref, bcp_ref, first, cwb_ref, cbb_ref, extb_ref)

    dt_t = _softplus(sm_ref[...].T + dtb_ref[...])
    cs_t = _cumsum_lanes(dt_t * (-jnp.exp(alog_ref[...])))
    cs = cs_t.T
    ii = lax.broadcasted_iota(jnp.int32, (c, c), 0)
    jj = lax.broadcasted_iota(jnp.int32, (c, c), 1)
    tril = ii >= jj
    pair_w = 2 * SSD_HEAD_DIM
    first_head = jj < SSD_HEAD_DIM
    first_head_row = lax.broadcasted_iota(jnp.int32, (1, pair_w), 1) < SSD_HEAD_DIM
    gn = SSD_GROUPS * SSD_STATE
    hpg = SSD_HEADS // SSD_GROUPS
    for g in range(SSD_GROUPS):
        bg = bc[:, g * SSD_STATE:(g + 1) * SSD_STATE]
        cg = bc[:, gn + g * SSD_STATE:gn + (g + 1) * SSD_STATE].astype(BF16)
        cb = _dot_nt(cg, bg.astype(BF16))
        bg_t = bg.T
        y_off = _dot(cg, prev_ref[g].astype(BF16))
        for pr in range(hpg // 2):
            cols = slice((g * hpg + 2 * pr) * SSD_HEAD_DIM, (g * hpg + 2 * pr + 2) * SSD_HEAD_DIM)
            rcols = slice(2 * pr * SSD_HEAD_DIM, (2 * pr + 2) * SSD_HEAD_DIM)
            x_pair = xs[:, cols]
            x_bf = x_pair.astype(BF16)
            y_diag, st, exp_a, exp_last = [], [], [], []
            for k in range(2):
                row = SM_DT + g * hpg + 2 * pr + k
                a_row = cs_t[row:row + 1, :]
                dt_row = dt_t[row:row + 1, :]
                last = cs_t[row:row + 1, c - 1:c]
                a_col = jnp.broadcast_to(cs[:, row:row + 1], (c, c))
                seg = jnp.where(tril, jnp.exp(jnp.where(tril, a_col - a_row, 0.0)), 0.0)
                y_diag.append(_dot((cb * seg * dt_row).astype(BF16), x_bf))
                st.append(_dot((bg_t * (dt_row * jnp.exp(last - a_row))).astype(BF16), x_bf))
                exp_a.append(jnp.exp(a_col))
                exp_last.append(jnp.exp(last))
            y_ref[:, cols] = (jnp.where(first_head, y_diag[0], y_diag[1])
                              + y_off[:, rcols] * jnp.where(first_head, exp_a[0], exp_a[1])
                              + dsk_ref[:, cols] * x_pair)
            prev_ref[g, :, rcols] = (jnp.where(first_head_row, exp_last[0], exp_last[1]) * prev_ref[g, :, rcols]
                                     + jnp.where(first_head, st[0], st[1]))
    gated = y_ref[...] * _silu(z_ref[...].astype(F32))
    gw = SSD_INNER // SSD_GROUPS
    for g in range(SSD_GROUPS):
        sl = slice(g * gw, (g + 1) * gw)
        o_ref[:, sl] = _rms(gated[:, sl], ng_ref[:, sl]).astype(o_ref.dtype)


def ssd_mixer(p, sm, cw, cb, dtb_col, alog_col, dskip_row, ng, batch, seq):
    n = seq // CHUNK
    bcw = 2 * SSD_GROUPS * SSD_STATE

    def cur(width, colbase):
        return pl.BlockSpec((CHUNK, width), lambda b, i: (b * n + i, colbase // width))

    def prv(width, colbase):
        return pl.BlockSpec((CHUNK, width), lambda b, i: (b * n + jnp.maximum(i - 1, 0), colbase // width))

    def const(shape):
        return pl.BlockSpec(shape, lambda b, i: (0, 0))

    return pl.pallas_call(
        _ssd_kernel,
        grid=(batch, n),
        in_specs=[
            cur(SSD_INNER, COL_Z), cur(SSD_INNER, COL_XS), cur(bcw, COL_BC),
            prv(SSD_INNER, COL_XS), prv(bcw, COL_BC),
            pl.BlockSpec((CHUNK, SM_W), lambda b, i: (b * n + i, 0)),
            const((SSD_CONV, SSD_INNER)), const((1, SSD_INNER)),
            const((SSD_CONV, bcw)), const((1, bcw)),
            const((SM_W, 1)), const((SM_W, 1)),
            const((1, SSD_INNER)), const((1, SSD_INNER)),
        ],
        out_specs=pl.BlockSpec((CHUNK, SSD_INNER), lambda b, i: (b * n + i, 0)),
        out_shape=jax.ShapeDtypeStruct((batch * seq, SSD_INNER), BF16),
        scratch_shapes=[
            pltpu.VMEM((SSD_GROUPS, SSD_STATE, SSD_INNER // SSD_GROUPS), F32),
            pltpu.VMEM((CHUNK, SSD_INNER), F32),
            pltpu.VMEM((CONV_PAD + CHUNK, SSD_INNER), F32),
            pltpu.VMEM((CONV_PAD + CHUNK, bcw), F32),
        ],
        compiler_params=_cparams(("parallel", "arbitrary")),
        name="ssd_mixer",
    )(p, p, p, p, p, sm, cw[:, :SSD_INNER], cb[:, :SSD_INNER], cw[:, SSD_INNER:], cb[:, SSD_INNER:],
      dtb_col, alog_col, dskip_row, ng)


MERGE_TM = 256


def _merge_kernel(x_ref, gl_ref, gb_ref, oret_ref, ofox_ref, odsa_ref, ossd_ref, wbr_ref, wout_ref, o_ref):
    branches = (oret_ref, ofox_ref, odsa_ref, ossd_ref)
    merged = None
    row0 = 0
    for bi, br in enumerate(branches):
        width = br.shape[1]
        sl = slice(bi * D_MODEL, (bi + 1) * D_MODEL)
        gate = 1.0 / (1.0 + jnp.exp(-(gl_ref[:, sl].astype(F32) + gb_ref[:, sl])))
        term = gate * _dot(br[...], wbr_ref[row0:row0 + width, :])
        merged = term if merged is None else merged + term
        row0 += width
    o_ref[...] = x_ref[...] + _dot(merged.astype(BF16), wout_ref[...])


def merge_project(x, p, gate_b, o_ret, o_fox, o_dsa, o_ssd, w_br, w_out, layer):
    m = x.shape[0]
    tm = MERGE_TM

    def rows(width):
        return pl.BlockSpec((tm, width), lambda i: (i, 0))

    def const(shape):
        return pl.BlockSpec(shape, lambda i: (0, 0), pipeline_mode=pl.Buffered(1))

    def stacked(w):
        return pl.BlockSpec((None,) + w.shape[1:], lambda i: (layer, 0, 0), pipeline_mode=pl.Buffered(1))

    return pl.pallas_call(
        _merge_kernel,
        grid=(m // tm,),
        in_specs=[
            rows(D_MODEL), rows(N_BRANCH * D_MODEL), const((1, N_BRANCH * D_MODEL)),
            rows(MIX_W), rows(MIX_W), rows(MIX_W), rows(SSD_INNER),
            stacked(w_br), stacked(w_out),
        ],
        out_specs=rows(D_MODEL),
        out_shape=jax.ShapeDtypeStruct(x.shape, x.dtype),
        compiler_params=_cparams(("parallel",)),
        name="merge_project",
    )(x, p, gate_b, o_ret, o_fox, o_dsa, o_ssd, w_br, w_out)


FFN_TM = 1024
FFN_TF = 512


def _ffn_kernel(x_ref, g_ref, w1_ref, w2_ref, o_ref, h_ref):
    @pl.when(pl.program_id(1) == 0)
    def _():
        h_ref[...] = _rms(x_ref[...], g_ref[...]).astype(BF16)
        o_ref[...] = x_ref[...]

    a = jnp.maximum(_dot(h_ref[...], w1_ref[...]), 0.0)
    o_ref[...] += _dot((a * a).astype(BF16), w2_ref[...])


def ffn(x, g, w1, w2, layer):
    m, d = x.shape
    dff = w1.shape[2]
    tm, tf = min(FFN_TM, m), FFN_TF
    return pl.pallas_call(
        _ffn_kernel,
        grid=(m // tm, dff // tf),
        in_specs=[
            pl.BlockSpec((tm, d), lambda i, f: (i, 0)),
            pl.BlockSpec((1, d), lambda i, f: (0, 0)),
            pl.BlockSpec((None, d, tf), lambda i, f: (layer, 0, f)),
            pl.BlockSpec((None, tf, d), lambda i, f: (layer, f, 0)),
        ],
        out_specs=pl.BlockSpec((tm, d), lambda i, f: (i, 0)),
        out_shape=jax.ShapeDtypeStruct(x.shape, x.dtype),
        scratch_shapes=[pltpu.VMEM((tm, d), BF16)],
        compiler_params=_cparams(("parallel", "arbitrary")),
        name="ffn",
    )(x, g, w1, w2)


SRC_RET = 0
SRC_FOX = SRC_RET + 4 * MIX_W
SRC_FF = SRC_FOX + 3 * MIX_W
SRC_CQ = SRC_FF + N_HEADS
SRC_DK = SRC_CQ + DSA_Q_RANK
SRC_IK = SRC_DK + 2 * HEAD_DIM
SRC_IW = SRC_IK + IDX_DIM
SRC_Z = SRC_IW + IDX_HEADS
SRC_DT = SRC_Z + 2 * SSD_INNER + 2 * SSD_GROUPS * SSD_STATE
SRC_GATE = SRC_DT + SSD_HEADS
MAIN_RUNS = ((COL_GATE, SRC_GATE), (COL_RET, SRC_RET), (COL_Z, SRC_Z), (COL_CQ, SRC_CQ),
             (COL_FOX, SRC_FOX), (COL_DK, SRC_DK))
SMALL_PIECES = ((SRC_DT, SM_DT, SSD_HEADS), (SRC_FF, SM_F, N_HEADS), (SRC_IW, SM_IW, IDX_HEADS),
                (SRC_IK, SM_IK, IDX_DIM))


def _layout_in_proj_t(w_in):
    w_t = jnp.swapaxes(w_in, 1, 2)
    depth, _, d = w_t.shape
    ends = [dst for dst, _ in MAIN_RUNS[1:]] + [N_MAIN_USED]
    main = [w_t[:, src:src + (end - dst), :] for (dst, src), end in zip(MAIN_RUNS, ends)]
    main.append(jnp.zeros((depth, N_MAIN - N_MAIN_USED, d), w_t.dtype))
    small, lane = [], 0
    for src, dst, width in SMALL_PIECES:
        if dst > lane:
            small.append(jnp.zeros((depth, dst - lane, d), w_t.dtype))
        small.append(w_t[:, src:src + width, :])
        lane = dst + width
    return jnp.concatenate(main, axis=1).astype(BF16), jnp.concatenate(small, axis=1).astype(BF16)


def _pad_to(v, offset, total):
    return jnp.zeros((total,), v.dtype).at[offset:offset + v.shape[0]].set(v)


def _rotary_tables(seq):
    half = HEAD_DIM // 2
    inv = 1.0 / (10000.0 ** (jnp.arange(half, dtype=F32) / half))
    ang = jnp.arange(seq, dtype=F32)[:, None] * inv[None, :]
    cos, sin = jnp.cos(ang), jnp.sin(ang)
    return jnp.concatenate([cos, cos], axis=1), jnp.concatenate([-sin, sin], axis=1)


def kernel(x, norm1_g, w_in, gate_b, fox_f_b, fox_qn_g, fox_kn_g, dsa_cq_g, dsa_w_uq, dsa_w_qidx, dsa_qn_g,
           dsa_kn_g, rel_bias, ssd_conv_w, ssd_conv_b, ssd_dt_bias, ssd_a_log, ssd_d, ssd_norm_g, w_br, w_out,
           norm2_g, w_ff1, w_ff2):
    batch, seq, d = x.shape
    tokens = batch * seq
    xt = x.reshape(tokens, d)
    cos, sin = _rotary_tables(seq)
    tm = min(1024, tokens)
    w_main, w_small = _layout_in_proj_t(w_in)
    w_br_bf, w_out_bf = w_br.astype(BF16), w_out.astype(BF16)
    w_ff1_bf, w_ff2_bf = w_ff1.astype(BF16), w_ff2.astype(BF16)
    for l in range(DEPTH):
        g1 = norm1_g[l][None, :]
        p = norm_matmul(xt, g1, w_main, l, BF16, tm, MAIN_TN)
        sm = norm_matmul(xt, g1, w_small, l, F32, tm, SM_W)

        o_ret = retention(p, cos, sin, batch, seq)

        o_fox = fox_attention(p, sm, _pad_to(fox_f_b[l], SM_F, SM_W)[None, :], fox_qn_g[l][None, :],
                              fox_kn_g[l][None, :], batch, seq)

        o_dsa = dsa_attention(p, sm, dsa_cq_g[l][None, :], dsa_w_uq[l].T.astype(BF16), dsa_w_qidx[l].T.astype(BF16),
                              dsa_qn_g[l][:, None], dsa_kn_g[l][None, :], rel_bias, batch, seq)

        o_ssd = ssd_mixer(p, sm, ssd_conv_w[l], ssd_conv_b[l][None, :],
                          _pad_to(ssd_dt_bias[l], SM_DT, SM_W)[:, None], _pad_to(ssd_a_log[l], SM_DT, SM_W)[:, None],
                          jnp.repeat(ssd_d[l], SSD_HEAD_DIM)[None, :], ssd_norm_g[l][None, :], batch, seq)

        xt = merge_project(xt, p, gate_b[l][None, :], o_ret, o_fox, o_dsa, o_ssd, w_br_bf, w_out_bf, l)
        xt = ffn(xt, norm2_g[l][None, :], w_ff1_bf, w_ff2_bf, l)
    return xt.reshape(batch, seq, d)
```

```python
import functools
import math

import jax
import jax.numpy as jnp
from jax import lax
from jax.experimental import pallas as pl
from jax.experimental.pallas import tpu as pltpu

F32 = jnp.float32
BF16 = jnp.bfloat16

D_MODEL = 2048
DEPTH = 4
HEAD_DIM = 128
N_HEADS = 4
DSA_Q_RANK = 512
IDX_HEADS = 16
IDX_DIM = 64
DSA_TOPK = 256
SSD_HEADS = 16
SSD_HEAD_DIM = 64
SSD_GROUPS = 2
SSD_STATE = 128
SSD_CONV = 4
SSD_INNER = SSD_HEADS * SSD_HEAD_DIM
D_FF = 4 * D_MODEL
N_BUCKETS = 32
MAX_DISTANCE = 128
CHUNK = 128
EPS = 1e-6
N_BRANCH = 4
MIX_W = N_HEADS * HEAD_DIM

COL_GATE = 0
COL_RET = COL_GATE + N_BRANCH * D_MODEL
COL_Z = COL_RET + 4 * MIX_W
COL_XS = COL_Z + SSD_INNER
COL_BC = COL_XS + SSD_INNER
COL_CQ = COL_BC + 2 * SSD_GROUPS * SSD_STATE
COL_FOX = COL_CQ + DSA_Q_RANK
COL_DK = COL_FOX + 3 * MIX_W
COL_DV = COL_DK + HEAD_DIM
N_MAIN_USED = COL_DV + HEAD_DIM
N_MAIN = 15360
MAIN_TN = 1536
SM_DT = 0
SM_F = 16
SM_IW = 32
SM_IK = 64
SM_W = 128

LANES = 128
SUBLANES = 8
VMEM_LIMIT = 56 * 1024 * 1024
NEG_BIG = -1e30


def _cparams(sem):
    return pltpu.CompilerParams(dimension_semantics=sem, vmem_limit_bytes=VMEM_LIMIT)


def _dot(a, b):
    return jnp.dot(a, b, preferred_element_type=F32)


def _dot_nt(a, b):
    return lax.dot_general(a, b, (((1,), (1,)), ((), ())), preferred_element_type=F32)


def _rms(x, g):
    return x * lax.rsqrt(jnp.mean(x * x, axis=-1, keepdims=True) + EPS) * g


def _silu(x):
    return x / (1.0 + jnp.exp(-x))


def _softplus(x):
    return jnp.maximum(x, 0.0) + jnp.log1p(jnp.exp(-jnp.abs(x)))


def _cumsum_lanes(x):
    lane = lax.broadcasted_iota(jnp.int32, x.shape, 1)
    d = 1
    while d < x.shape[1]:
        x = x + jnp.where(lane >= d, pltpu.roll(x, d, 1), 0.0)
        d *= 2
    return x


def _fold_rows(x, op):
    return op(x.reshape(x.shape[0] // SUBLANES, SUBLANES, x.shape[1]), axis=0)


def _norm_matmul_kernel(x_ref, g_ref, w_ref, ws_ref, o_ref, os_ref, h_ref):
    @pl.when(pl.program_id(1) == 0)
    def _():
        h_ref[...] = _rms(x_ref[...], g_ref[...]).astype(BF16)
        os_ref[...] = _dot_nt(h_ref[...], ws_ref[...])

    o_ref[...] = _dot_nt(h_ref[...], w_ref[...]).astype(o_ref.dtype)


def norm_matmul(x, g, w_t, ws_t, layer, tm, tn):
    m, d = x.shape
    n = w_t.shape[1]
    return pl.pallas_call(
        _norm_matmul_kernel,
        grid=(m // tm, n // tn),
        in_specs=[
            pl.BlockSpec((tm, d), lambda i, j: (i, 0)),
            pl.BlockSpec((1, d), lambda i, j: (0, 0)),
            pl.BlockSpec((None, tn, d), lambda i, j: (layer, j, 0)),
            pl.BlockSpec((None, SM_W, d), lambda i, j: (layer, 0, 0)),
        ],
        out_specs=[
            pl.BlockSpec((tm, tn), lambda i, j: (i, j)),
            pl.BlockSpec((tm, SM_W), lambda i, j: (i, 0)),
        ],
        out_shape=[jax.ShapeDtypeStruct((m, n), BF16), jax.ShapeDtypeStruct((m, SM_W), F32)],
        scratch_shapes=[pltpu.VMEM((tm, d), BF16)],
        compiler_params=_cparams(("parallel", "arbitrary")),
        name="norm_matmul",
    )(x, g, w_t, ws_t)


def _retention_kernel(q_ref, k_ref, v_ref, g_ref, cos_ref, sin_ref, o_ref, state_ref):
    c = CHUNK

    @pl.when(pl.program_id(1) == 0)
    def _():
        state_ref[...] = jnp.zeros_like(state_ref)

    cos = cos_ref[...]
    sin = sin_ref[...]
    ii = lax.broadcasted_iota(jnp.int32, (c, c), 0)
    jj = lax.broadcasted_iota(jnp.int32, (c, c), 1)
    rel = (ii - jj).astype(F32)
    i_col = lax.broadcasted_iota(jnp.int32, (c, 1), 0).astype(F32)
    for h in range(N_HEADS):
        lg = math.log1p(-(2.0 ** (-5.0 - h)))
        sl = slice(h * HEAD_DIM, (h + 1) * HEAD_DIM)
        q = q_ref[:, sl].astype(F32)
        k = k_ref[:, sl].astype(F32)
        v = v_ref[:, sl]
        qr = q * cos + pltpu.roll(q, HEAD_DIM // 2, 1) * sin
        kr = (k * cos + pltpu.roll(k, HEAD_DIM // 2, 1) * sin) * (HEAD_DIM ** -0.5)
        decay = jnp.where(rel >= 0, jnp.exp(lg * jnp.maximum(rel, 0.0)), 0.0)
        scores = _dot_nt(qr.astype(BF16), kr.astype(BF16)) * decay
        y = _dot(scores.astype(BF16), v)
        q_dec = jnp.exp(lg * (i_col + 1.0))
        k_dec = jnp.exp(lg * (c - 1.0 - i_col))
        st = state_ref[h]
        y = y + _dot((qr * q_dec).astype(BF16), st.astype(BF16))
        kv = lax.dot_general((kr * k_dec).astype(BF16), v, (((0,), (0,)), ((), ())), preferred_element_type=F32)
        state_ref[h] = math.exp(lg * c) * st + kv
        yc = y - jnp.mean(y, axis=-1, keepdims=True)
        yn = yc * lax.rsqrt(jnp.mean(yc * yc, axis=-1, keepdims=True) + EPS)
        o_ref[:, sl] = (_silu(g_ref[:, sl].astype(F32)) * yn).astype(o_ref.dtype)


def retention(p, cos, sin, batch, seq):
    n = seq // CHUNK
    base = COL_RET // MIX_W

    def col(j):
        return pl.BlockSpec((CHUNK, MIX_W), lambda b, i: (b * n + i, base + j))

    tab = pl.BlockSpec((CHUNK, HEAD_DIM), lambda b, i: (i, 0))
    return pl.pallas_call(
        _retention_kernel,
        grid=(batch, n),
        in_specs=[col(0), col(1), col(2), col(3), tab, tab],
        out_specs=pl.BlockSpec((CHUNK, MIX_W), lambda b, i: (b * n + i, 0)),
        out_shape=jax.ShapeDtypeStruct((batch * seq, MIX_W), BF16),
        scratch_shapes=[pltpu.VMEM((N_HEADS, HEAD_DIM, HEAD_DIM), F32)],
        compiler_params=_cparams(("parallel", "arbitrary")),
        name="retention",
    )(p, p, p, p, cos, sin)


FOX_T = 256


def _fox_kernel(q_ref, k_ref, v_ref, sm_ref, fbias_ref, qg_ref, kg_ref, o_ref,
                kn_ref, vt_ref, fb_ref, frow_ref, qt_ref, acc_ref):
    i = pl.program_id(1)
    t = FOX_T
    nkc = vt_ref.shape[0]
    sub = t // CHUNK

    @pl.when(i == 0)
    def _():
        carry = jnp.zeros((SM_W, 1), F32)
        for c in range(frow_ref.shape[0]):
            rows = slice(c * CHUNK, (c + 1) * CHUNK)
            logit = sm_ref[rows, :] + fbias_ref[...]
            log_f = jnp.minimum(logit, 0.0) - jnp.log1p(jnp.exp(-jnp.abs(logit)))
            cs = _cumsum_lanes(log_f.T) + carry
            carry = cs[:, LANES - 1:LANES]
            frow_ref[c] = cs[SM_F:SM_F + SUBLANES, :]
            cs_col = cs.T
            for h in range(N_HEADS):
                fb_ref[h, rows, :] = jnp.broadcast_to(cs_col[:, SM_F + h:SM_F + h + 1], (CHUNK, LANES))
        for h in range(N_HEADS):
            sl = slice(h * HEAD_DIM, (h + 1) * HEAD_DIM)
            kn_ref[:, sl] = _rms(k_ref[:, sl].astype(F32), kg_ref[...]).astype(BF16)
            for j in range(nkc):
                for c in range(sub):
                    rows = slice(j * t + c * CHUNK, j * t + (c + 1) * CHUNK)
                    vt_ref[j, h, :, c * CHUNK:(c + 1) * CHUNK] = v_ref[rows, sl].astype(F32).T.astype(BF16)

    fqs = []
    for h in range(N_HEADS):
        sl = slice(h * HEAD_DIM, (h + 1) * HEAD_DIM)
        qn = _rms(q_ref[:, sl].astype(F32), qg_ref[...]) * (HEAD_DIM ** -0.5)
        qt_ref[h] = jnp.concatenate([qn[c * CHUNK:(c + 1) * CHUNK, :].T for c in range(sub)], axis=1).astype(BF16)
        fqs.append(jnp.concatenate([frow_ref[i * sub + c, h:h + 1, :] for c in range(sub)], axis=1))
        acc_ref[h] = jnp.zeros(acc_ref.shape[1:], F32)

    kofs = lax.broadcasted_iota(jnp.int32, (t, t), 0)
    qofs = lax.broadcasted_iota(jnp.int32, (t, t), 1)

    def body(diag, j, carry):
        ms, ls = carry
        start = pl.multiple_of(j * t, t)
        scores = [_dot(kn_ref[pl.ds(start, t), h * HEAD_DIM:(h + 1) * HEAD_DIM], qt_ref[h])
                  for h in range(N_HEADS)]
        new_ms, new_ls, ps, alphas = [], [], [], []
        for h in range(N_HEADS):
            fk = fb_ref[h, pl.ds(start, t), :]
            s = scores[h] + fqs[h] - jnp.concatenate([fk] * (t // LANES), axis=1)
            if diag:
                s = jnp.where(kofs <= qofs, s, NEG_BIG)
            m_new = jnp.maximum(ms[h], jnp.max(s, axis=0, keepdims=True))
            p = jnp.exp(s - m_new)
            alpha = jnp.exp(ms[h] - m_new)
            new_ls.append(alpha * ls[h] + jnp.sum(p, axis=0, keepdims=True))
            ps.append(p.astype(BF16))
            alphas.append(alpha)
            new_ms.append(m_new)
        for h in range(N_HEADS):
            acc_ref[h] = alphas[h] * acc_ref[h] + _dot(vt_ref[j, h], ps[h])
        return tuple(new_ms), tuple(new_ls)

    init = (tuple(jnp.full((1, t), NEG_BIG, F32) for _ in range(N_HEADS)),
            tuple(jnp.zeros((1, t), F32) for _ in range(N_HEADS)))
    carry = lax.fori_loop(0, i, functools.partial(body, False), init)
    _, ls = body(True, i, carry)
    for h in range(N_HEADS):
        out_t = acc_ref[h] / ls[h]
        for c in range(sub):
            o_ref[c * CHUNK:(c + 1) * CHUNK, h * HEAD_DIM:(h + 1) * HEAD_DIM] = (
                out_t[:, c * CHUNK:(c + 1) * CHUNK].T.astype(o_ref.dtype))


def fox_attention(p, sm, fbias_row, qg, kg, batch, seq):
    nq = seq // FOX_T
    base = COL_FOX // MIX_W
    return pl.pallas_call(
        _fox_kernel,
        grid=(batch, nq),
        in_specs=[
            pl.BlockSpec((FOX_T, MIX_W), lambda b, i: (b * nq + i, base)),
            pl.BlockSpec((seq, MIX_W), lambda b, i: (b, base + 1)),
            pl.BlockSpec((seq, MIX_W), lambda b, i: (b, base + 2)),
            pl.BlockSpec((seq, SM_W), lambda b, i: (b, 0)),
            pl.BlockSpec((1, SM_W), lambda b, i: (0, 0)),
            pl.BlockSpec((1, HEAD_DIM), lambda b, i: (0, 0)),
            pl.BlockSpec((1, HEAD_DIM), lambda b, i: (0, 0)),
        ],
        out_specs=pl.BlockSpec((FOX_T, MIX_W), lambda b, i: (b * nq + i, 0)),
        out_shape=jax.ShapeDtypeStruct((batch * seq, MIX_W), BF16),
        scratch_shapes=[
            pltpu.VMEM((seq, MIX_W), BF16),
            pltpu.VMEM((nq, N_HEADS, HEAD_DIM, FOX_T), BF16),
            pltpu.VMEM((N_HEADS, seq, LANES), F32),
            pltpu.VMEM((seq // CHUNK, SUBLANES, CHUNK), F32),
            pltpu.VMEM((N_HEADS, HEAD_DIM, FOX_T), BF16),
            pltpu.VMEM((N_HEADS, HEAD_DIM, FOX_T), F32),
        ],
        compiler_params=_cparams(("parallel", "arbitrary")),
        name="fox_attention",
    )(p, p, p, sm, fbias_row, qg, kg)


DSA_TQ = 128
DSA_TK = 256
BAND_W = 2 * DSA_TQ
COUNT_ROWS = 64
BISECT_FIXED_STEPS = 16


def _t5_bucket(dist):
    max_exact = N_BUCKETS // 2
    d = jnp.maximum(dist, 0)
    log_ratio = jnp.log(jnp.maximum(d, 1).astype(F32) / max_exact) / math.log(MAX_DISTANCE / max_exact)
    large = jnp.minimum(max_exact + (log_ratio * (N_BUCKETS - max_exact)).astype(jnp.int32), N_BUCKETS - 1)
    return jnp.where(d < max_exact, d, large)


def _dsa_kernel(cq_ref, k_ref, v_ref, smq_ref, smk_ref, cqg_ref, wuq_ref, wqi_ref, qg_ref, kg_ref, rb_ref,
                o_ref, kn_ref, ki_ref, vt_ref, band_ref, sc_ref, qt_ref, xi_ref, acc_ref, s_ref, *, topk):
    b = pl.program_id(0)
    i = pl.program_id(1)
    tq, tk = DSA_TQ, DSA_TK
    nkc = sc_ref.shape[0]
    seq = nkc * tk
    tiles = tk // tq
    nb = ((i + 1) * tq + tk - 1) // tk

    @pl.when(jnp.logical_and(b == 0, i == 0))
    def _():
        c = lax.broadcasted_iota(jnp.int32, (BAND_W, tq), 0)
        r = lax.broadcasted_iota(jnp.int32, (BAND_W, tq), 1)
        bucket = _t5_bucket(tq + r - c)
        for h in range(N_HEADS):
            far = rb_ref[N_BUCKETS - 1, h]
            acc = jnp.zeros((BAND_W, tq), F32)
            for bk in range(N_BUCKETS - 1):
                acc = jnp.where(bucket == bk, rb_ref[bk, h] - far, acc)
            band_ref[h] = acc

    @pl.when(i == 0)
    def _():
        kn_ref[...] = _rms(k_ref[...].astype(F32), kg_ref[...]).astype(BF16)
        ki_ref[...] = smk_ref[:, SM_IK:SM_IK + IDX_DIM].astype(BF16)
        for j in range(nkc):
            for t in range(tiles):
                rows = slice(j * tk + t * tq, j * tk + (t + 1) * tq)
                vt_ref[j, :, t * tq:(t + 1) * tq] = v_ref[rows, :].astype(F32).T.astype(BF16)

    cq_t = _rms(cq_ref[...].astype(F32), cqg_ref[...]).T.astype(BF16)
    q_t = _dot(wuq_ref[...], cq_t)
    g_col = jnp.broadcast_to(qg_ref[...], (HEAD_DIM, tq))
    for h in range(N_HEADS):
        x = q_t[h * HEAD_DIM:(h + 1) * HEAD_DIM, :]
        inv = lax.rsqrt(jnp.mean(x * x, axis=0, keepdims=True) + EPS)
        qt_ref[:, h * tq:(h + 1) * tq] = (x * inv * g_col * (HEAD_DIM ** -0.5)).astype(BF16)
    qi_t = (_dot(wqi_ref[...], cq_t) * (IDX_DIM ** -0.5)).astype(BF16)
    for h in range(IDX_HEADS):
        xi_ref[:, h * tq:(h + 1) * tq] = qi_t[h * IDX_DIM:(h + 1) * IDX_DIM, :]
    w_rows = smq_ref[...].T[SM_IW:SM_IW + IDX_HEADS, :] * (IDX_HEADS ** -0.5)

    kofs = lax.broadcasted_iota(jnp.int32, (tk, tq), 0)
    qpos = lax.broadcasted_iota(jnp.int32, (tk, tq), 1) + i * tq

    def score_body(j, _):
        start = pl.multiple_of(j * tk, tk)
        kj = ki_ref[pl.ds(start, tk), :]
        acc = jnp.zeros((tk, tq), F32)
        for h2 in range(IDX_HEADS // 2):
            r = _dot(kj, xi_ref[:, 2 * h2 * tq:(2 * h2 + 2) * tq])
            for h in (2 * h2, 2 * h2 + 1):
                acc = acc + w_rows[h:h + 1, :] * jnp.maximum(r[:, (h - 2 * h2) * tq:(h - 2 * h2 + 1) * tq], 0.0)
        sc_ref[j] = jnp.where(kofs + start <= qpos, acc, -jnp.inf)
        return 0

    lax.fori_loop(0, nb, score_body, 0)

    @pl.when(nb % 2 == 1)
    def _():
        sc_ref[jnp.minimum(nb, nkc - 1)] = jnp.full((tk, tq), -jnp.inf, F32)

    extents = sorted({min(n, nkc) for n in range(2, nkc + 2, 2)})

    def over_chunks(fn, init):
        acc = init
        for j in range(nkc):
            acc = lax.cond(j < nb, functools.partial(fn, j), lambda a: a, acc)
        return acc

    def count_where(pred_fn):
        def walk(n_chunks):
            def run():
                a = jnp.zeros((COUNT_ROWS, tq), F32)
                for j in range(n_chunks):
                    hit = jnp.where(pred_fn(j, sc_ref[j]), 1.0, 0.0)
                    a = a + jnp.sum(hit.reshape(tk // COUNT_ROWS, COUNT_ROWS, tq), axis=0)
                return a
            return run

        a = lax.switch((nb - 1) // 2, [walk(n) for n in extents])
        return jnp.sum(a, axis=0, keepdims=True)

    def search():
        kf = float(topk)
        smax = jnp.max(over_chunks(lambda j, a: jnp.maximum(a, _fold_rows(sc_ref[j], jnp.max)),
                                   jnp.full((SUBLANES, tq), -jnp.inf, F32)), axis=0, keepdims=True)
        smin = jnp.min(over_chunks(
            lambda j, a: jnp.minimum(a, _fold_rows(jnp.where(sc_ref[j] == -jnp.inf, jnp.inf, sc_ref[j]), jnp.min)),
            jnp.full((SUBLANES, tq), jnp.inf, F32)), axis=0, keepdims=True)

        def count_ge(t):
            return count_where(lambda j, x: x >= t)

        def midpoint(lo, hi):
            return jnp.where(hi == jnp.inf, smax, 0.5 * (lo + hi))

        def undecided(lo, hi, c_lo, mid):
            return jnp.logical_and(c_lo != kf, jnp.logical_and(mid > lo, mid < hi))

        def step(state):
            lo, hi, c_lo, c_hi, mid = state
            upd = undecided(lo, hi, c_lo, mid)
            cnt = count_ge(mid)
            up = jnp.logical_and(upd, cnt >= kf)
            dn = jnp.logical_and(upd, cnt < kf)
            lo = jnp.where(up, mid, lo)
            c_lo = jnp.where(up, cnt, c_lo)
            hi = jnp.where(dn, mid, hi)
            c_hi = jnp.where(dn, cnt, c_hi)
            return lo, hi, c_lo, c_hi, midpoint(lo, hi)

        def any_undecided(state):
            lo, hi, c_lo, _, mid = state
            return jnp.max(jnp.where(undecided(lo, hi, c_lo, mid), 1.0, 0.0))

        def cond(carry):
            return jnp.logical_and(carry[0] < 200, carry[1] > 0.0)

        def body(carry):
            state = step(step(carry[2]))
            return carry[0] + 1, any_undecided(state), state

        lo0 = smin
        hi0 = jnp.full((1, tq), jnp.inf, F32)
        c_lo0 = (lax.broadcasted_iota(jnp.int32, (1, tq), 1) + (i * tq + 1)).astype(F32)
        c_hi0 = jnp.zeros((1, tq), F32)
        state = (lo0, hi0, c_lo0, c_hi0, midpoint(lo0, hi0))
        state = lax.fori_loop(0, BISECT_FIXED_STEPS, lambda _, s: step(s), state)
        _, _, (lo, hi, c_lo, c_hi, _) = lax.while_loop(
            cond, body, (jnp.int32(0), any_undecided(state), state))

        def tie_search():
            need = kf - c_hi

            def tie_body(_, carry):
                jlo, jhi = carry
                jm = (jlo + jhi) // 2
                cnt = count_where(lambda j, x: jnp.logical_and(jnp.logical_and(x >= lo, x < hi),
                                                               kofs + j * tk <= jm))
                ok = cnt >= need
                return jnp.where(ok, jlo, jm), jnp.where(ok, jm, jhi)

            n_bits = int(math.ceil(math.log2(seq))) + 1
            _, jmax = lax.fori_loop(0, n_bits, tie_body,
                                    (jnp.full((1, tq), -1, jnp.int32), jnp.full((1, tq), seq - 1, jnp.int32)))
            return jmax

        any_tie = jnp.max(jnp.where(c_lo != kf, 1.0, 0.0)) > 0.0
        jmax = lax.cond(any_tie, tie_search, lambda: jnp.full((1, tq), seq - 1, jnp.int32))
        return lo, hi, jmax

    def keep_all():
        return (jnp.full((1, tq), -jnp.inf, F32), jnp.full((1, tq), jnp.inf, F32),
                jnp.full((1, tq), seq - 1, jnp.int32))

    lo, hi, jmax = lax.cond((i + 1) * tq > topk, search, keep_all)

    def attend(n_chunks):
        def run():
            mx = [jnp.full((SUBLANES, tq), NEG_BIG, F32) for _ in range(N_HEADS)]
            for j in range(n_chunks):
                near = j >= n_chunks - 3
                s_all = _dot(kn_ref[j * tk:(j + 1) * tk, :], qt_ref[...])
                sc = sc_ref[j]
                kpos = kofs + j * tk
                keep = jnp.logical_or(sc >= hi, jnp.logical_and(sc >= lo, kpos <= jmax))
                if near:
                    keep = jnp.logical_and(keep, kpos <= qpos)
                for h in range(N_HEADS):
                    hs = slice(h * tq, (h + 1) * tq)
                    s = s_all[:, hs]
                    if near:
                        s = s + jnp.concatenate(
                            [jnp.where(j * tiles + t == i, band_ref[h, tq:2 * tq, :],
                                       jnp.where(j * tiles + t == i - 1, band_ref[h, 0:tq, :], 0.0))
                             for t in range(tiles)], axis=0)
                    s = jnp.where(keep, s, NEG_BIG)
                    s_ref[j, :, hs] = s
                    mx[h] = jnp.maximum(mx[h], _fold_rows(s, jnp.max))
            m_all = jnp.concatenate([jnp.max(x, axis=0, keepdims=True) for x in mx], axis=1)
            acc = jnp.zeros(acc_ref.shape, F32)
            l_part = jnp.zeros((SUBLANES, N_HEADS * tq), F32)
            for j in range(n_chunks):
                p = jnp.exp(s_ref[j] - m_all)
                l_part = l_part + _fold_rows(p, jnp.sum)
                acc = acc + _dot(vt_ref[j], p.astype(BF16))
            acc_ref[...] = acc
            return jnp.sum(l_part, axis=0, keepdims=True)
        return run

    l_all = lax.switch((nb - 1) // 2, [attend(n) for n in extents])
    for h in range(N_HEADS):
        hs = slice(h * tq, (h + 1) * tq)
        out_t = acc_ref[:, hs] / l_all[:, hs]
        o_ref[:, h * HEAD_DIM:(h + 1) * HEAD_DIM] = out_t.T.astype(o_ref.dtype)


def dsa_attention(p, sm, cqg, wuq_t, wqi_t, qg_col, kg, rel_bias, batch, seq):
    nq = seq // DSA_TQ
    nkc = seq // DSA_TK
    topk = min(DSA_TOPK, seq // 4)
    kern = functools.partial(_dsa_kernel, topk=topk)
    return pl.pallas_call(
        kern,
        grid=(batch, nq),
        in_specs=[
            pl.BlockSpec((DSA_TQ, DSA_Q_RANK), lambda b, i: (b * nq + i, COL_CQ // DSA_Q_RANK)),
            pl.BlockSpec((seq, HEAD_DIM), lambda b, i: (b, COL_DK // HEAD_DIM)),
            pl.BlockSpec((seq, HEAD_DIM), lambda b, i: (b, COL_DV // HEAD_DIM)),
            pl.BlockSpec((DSA_TQ, SM_W), lambda b, i: (b * nq + i, 0)),
            pl.BlockSpec((seq, SM_W), lambda b, i: (b, 0)),
            pl.BlockSpec((1, DSA_Q_RANK), lambda b, i: (0, 0)),
            pl.BlockSpec((N_HEADS * HEAD_DIM, DSA_Q_RANK), lambda b, i: (0, 0)),
            pl.BlockSpec((IDX_HEADS * IDX_DIM, DSA_Q_RANK), lambda b, i: (0, 0)),
            pl.BlockSpec((HEAD_DIM, 1), lambda b, i: (0, 0)),
            pl.BlockSpec((1, HEAD_DIM), lambda b, i: (0, 0)),
            pl.BlockSpec(memory_space=pltpu.SMEM),
        ],
        out_specs=pl.BlockSpec((DSA_TQ, MIX_W), lambda b, i: (b * nq + i, 0)),
        out_shape=jax.ShapeDtypeStruct((batch * seq, MIX_W), BF16),
        scratch_shapes=[
            pltpu.VMEM((seq, HEAD_DIM), BF16),
            pltpu.VMEM((seq, IDX_DIM), BF16),
            pltpu.VMEM((nkc, HEAD_DIM, DSA_TK), BF16),
            pltpu.VMEM((N_HEADS, BAND_W, DSA_TQ), F32),
            pltpu.VMEM((nkc, DSA_TK, DSA_TQ), F32),
            pltpu.VMEM((HEAD_DIM, N_HEADS * DSA_TQ), BF16),
            pltpu.VMEM((IDX_DIM, IDX_HEADS * DSA_TQ), BF16),
            pltpu.VMEM((HEAD_DIM, N_HEADS * DSA_TQ), F32),
            pltpu.VMEM((nkc, DSA_TK, N_HEADS * DSA_TQ), F32),
        ],
        compiler_params=_cparams(("arbitrary", "arbitrary")),
        name="dsa_attention",
    )(p, p, p, sm, sm, cqg, wuq_t, wqi_t, qg_col, kg, rel_bias)


CONV_PAD = 8


def _causal_conv(x_ref, xp_ref, first, w_ref, b_ref, ext_ref):
    ext_ref[0:CONV_PAD, :] = xp_ref[CHUNK - CONV_PAD:CHUNK, :].astype(F32) * first
    ext_ref[CONV_PAD:CONV_PAD + CHUNK, :] = x_ref[...].astype(F32)
    acc = b_ref[...] + ext_ref[CONV_PAD:CONV_PAD + CHUNK, :] * w_ref[SSD_CONV - 1:SSD_CONV, :]
    for d in range(1, SSD_CONV):
        acc = acc + ext_ref[CONV_PAD - d:CONV_PAD - d + CHUNK, :] * w_ref[SSD_CONV - 1 - d:SSD_CONV - d, :]
    return _silu(acc)


def _ssd_kernel(z_ref, xs_ref, bc_ref, xsp_ref, bcp_ref, sm_ref, cwx_ref, cbx_ref, cwb_ref, cbb_ref,
                dtb_ref, alog_ref, dsk_ref, ng_ref, o_ref, prev_ref, y_ref, extx_ref, extb_ref):
    c = CHUNK
    n = pl.program_id(1)

    @pl.when(n == 0)
    def _():
        prev_ref[...] = jnp.zeros_like(prev_ref)

    first = (n > 0).astype(F32)
    xs = _causal_conv(xs_ref, xsp_ref, first, cwx_ref, cbx_ref, extx_ref)
    bc = _causal_conv(bc_ref, bcp_ref, first, cwb_ref, cbb_ref, extb_ref)

    dt_t = _softplus(sm_ref[...].T + dtb_ref[...])
    cs_t = _cumsum_lanes(dt_t * (-jnp.exp(alog_ref[...])))
    cs = cs_t.T
    ii = lax.broadcasted_iota(jnp.int32, (c, c), 0)
    jj = lax.broadcasted_iota(jnp.int32, (c, c), 1)
    tril = ii >= jj
    pair_w = 2 * SSD_HEAD_DIM
    first_head = jj < SSD_HEAD_DIM
    first_head_row = lax.broadcasted_iota(jnp.int32, (1, pair_w), 1) < SSD_HEAD_DIM
    gn = SSD_GROUPS * SSD_STATE
    hpg = SSD_HEADS // SSD_GROUPS
    for g in range(SSD_GROUPS):
        bg = bc[:, g * SSD_STATE:(g + 1) * SSD_STATE]
        cg = bc[:, gn + g * SSD_STATE:gn + (g + 1) * SSD_STATE].astype(BF16)
        cb = _dot_nt(cg, bg.astype(BF16))
        bg_t = bg.T
        y_off = _dot(cg, prev_ref[g].astype(BF16))
        for pr in range(hpg // 2):
            cols = slice((g * hpg + 2 * pr) * SSD_HEAD_DIM, (g * hpg + 2 * pr + 2) * SSD_HEAD_DIM)
            rcols = slice(2 * pr * SSD_HEAD_DIM, (2 * pr + 2) * SSD_HEAD_DIM)
            x_pair = xs[:, cols]
            x_bf = x_pair.astype(BF16)
            y_diag, st, exp_a, exp_last = [], [], [], []
            for k in range(2):
                row = SM_DT + g * hpg + 2 * pr + k
                a_row = cs_t[row:row + 1, :]
                dt_row = dt_t[row:row + 1, :]
                last = cs_t[row:row + 1, c - 1:c]
                a_col = jnp.broadcast_to(cs[:, row:row + 1], (c, c))
                seg = jnp.where(tril, jnp.exp(jnp.where(tril, a_col - a_row, 0.0)), 0.0)
                y_diag.append(_dot((cb * seg * dt_row).astype(BF16), x_bf))
                st.append(_dot((bg_t * (dt_row * jnp.exp(last - a_row))).astype(BF16), x_bf))
                exp_a.append(jnp.exp(a_col))
                exp_last.append(jnp.exp(last))
            y_ref[:, cols] = (jnp.where(first_head, y_diag[0], y_diag[1])
                              + y_off[:, rcols] * jnp.where(first_head, exp_a[0], exp_a[1])
                              + dsk_ref[:, cols] * x_pair)
            prev_ref[g, :, rcols] = (jnp.where(first_head_row, exp_last[0], exp_last[1]) * prev_ref[g, :, rcols]
                                     + jnp.where(first_head, st[0], st[1]))
    gated = y_ref[...] * _silu(z_ref[...].astype(F32))
    gw = SSD_INNER // SSD_GROUPS
    for g in range(SSD_GROUPS):
        sl = slice(g * gw, (g + 1) * gw)
        o_ref[:, sl] = _rms(gated[:, sl], ng_ref[:, sl]).astype(o_ref.dtype)


def ssd_mixer(p, sm, cw, cb, dtb_col, alog_col, dskip_row, ng, batch, seq):
    n = seq // CHUNK
    bcw = 2 * SSD_GROUPS * SSD_STATE

    def cur(width, colbase):
        return pl.BlockSpec((CHUNK, width), lambda b, i: (b * n + i, colbase // width))

    def prv(width, colbase):
        return pl.BlockSpec((CHUNK, width), lambda b, i: (b * n + jnp.maximum(i - 1, 0), colbase // width))

    def const(shape):
        return pl.BlockSpec(shape, lambda b, i: (0, 0))

    return pl.pallas_call(
        _ssd_kernel,
        grid=(batch, n),
        in_specs=[
            cur(SSD_INNER, COL_Z), cur(SSD_INNER, COL_XS), cur(bcw, COL_BC),
            prv(SSD_INNER, COL_XS), prv(bcw, COL_BC),
            pl.BlockSpec((CHUNK, SM_W), lambda b, i: (b * n + i, 0)),
            const((SSD_CONV, SSD_INNER)), const((1, SSD_INNER)),
            const((SSD_CONV, bcw)), const((1, bcw)),
            const((SM_W, 1)), const((SM_W, 1)),
            const((1, SSD_INNER)), const((1, SSD_INNER)),
        ],
        out_specs=pl.BlockSpec((CHUNK, SSD_INNER), lambda b, i: (b * n + i, 0)),
        out_shape=jax.ShapeDtypeStruct((batch * seq, SSD_INNER), BF16),
        scratch_shapes=[
            pltpu.VMEM((SSD_GROUPS, SSD_STATE, SSD_INNER // SSD_GROUPS), F32),
            pltpu.VMEM((CHUNK, SSD_INNER), F32),
            pltpu.VMEM((CONV_PAD + CHUNK, SSD_INNER), F32),
            pltpu.VMEM((CONV_PAD + CHUNK, bcw), F32),
        ],
        compiler_params=_cparams(("parallel", "arbitrary")),
        name="ssd_mixer",
    )(p, p, p, p, p, sm, cw[:, :SSD_INNER], cb[:, :SSD_INNER], cw[:, SSD_INNER:], cb[:, SSD_INNER:],
      dtb_col, alog_col, dskip_row, ng)


MERGE_TM = 256


def _merge_kernel(x_ref, gl_ref, gb_ref, oret_ref, ofox_ref, odsa_ref, ossd_ref, wbr_ref, wout_ref, o_ref):
    branches = (oret_ref, ofox_ref, odsa_ref, ossd_ref)
    merged = None
    row0 = 0
    for bi, br in enumerate(branches):
        width = br.shape[1]
        sl = slice(bi * D_MODEL, (bi + 1) * D_MODEL)
        gate = 1.0 / (1.0 + jnp.exp(-(gl_ref[:, sl].astype(F32) + gb_ref[:, sl])))
        term = gate * _dot(br[...], wbr_ref[row0:row0 + width, :])
        merged = term if merged is None else merged + term
        row0 += width
    o_ref[...] = x_ref[...] + _dot(merged.astype(BF16), wout_ref[...])


def merge_project(x, p, gate_b, o_ret, o_fox, o_dsa, o_ssd, w_br, w_out, layer):
    m = x.shape[0]
    tm = MERGE_TM

    def rows(width):
        return pl.BlockSpec((tm, width), lambda i: (i, 0))

    def const(shape):
        return pl.BlockSpec(shape, lambda i: (0, 0), pipeline_mode=pl.Buffered(1))

    def stacked(w):
        return pl.BlockSpec((None,) + w.shape[1:], lambda i: (layer, 0, 0), pipeline_mode=pl.Buffered(1))

    return pl.pallas_call(
        _merge_kernel,
        grid=(m // tm,),
        in_specs=[
            rows(D_MODEL), rows(N_BRANCH * D_MODEL), const((1, N_BRANCH * D_MODEL)),
            rows(MIX_W), rows(MIX_W), rows(MIX_W), rows(SSD_INNER),
            stacked(w_br), stacked(w_out),
        ],
        out_specs=rows(D_MODEL),
        out_shape=jax.ShapeDtypeStruct(x.shape, x.dtype),
        compiler_params=_cparams(("parallel",)),
        name="merge_project",
    )(x, p, gate_b, o_ret, o_fox, o_dsa, o_ssd, w_br, w_out)


FFN_TM = 1024
FFN_TF = 512


def _ffn_kernel(x_ref, g_ref, w1_ref, w2_ref, o_ref, h_ref):
    @pl.when(pl.program_id(1) == 0)
    def _():
        h_ref[...] = _rms(x_ref[...], g_ref[...]).astype(BF16)
        o_ref[...] = x_ref[...]

    a = jnp.maximum(_dot(h_ref[...], w1_ref[...]), 0.0)
    o_ref[...] += _dot((a * a).astype(BF16), w2_ref[...])


def ffn(x, g, w1, w2, layer):
    m, d = x.shape
    dff = w1.shape[2]
    tm, tf = min(FFN_TM, m), FFN_TF
    return pl.pallas_call(
        _ffn_kernel,
        grid=(m // tm, dff // tf),
        in_specs=[
            pl.BlockSpec((tm, d), lambda i, f: (i, 0)),
            pl.BlockSpec((1, d), lambda i, f: (0, 0)),
            pl.BlockSpec((None, d, tf), lambda i, f: (layer, 0, f)),
            pl.BlockSpec((None, tf, d), lambda i, f: (layer, f, 0)),
        ],
        out_specs=pl.BlockSpec((tm, d), lambda i, f: (i, 0)),
        out_shape=jax.ShapeDtypeStruct(x.shape, x.dtype),
        scratch_shapes=[pltpu.VMEM((tm, d), BF16)],
        compiler_params=_cparams(("parallel", "arbitrary")),
        name="ffn",
    )(x, g, w1, w2)


SRC_RET = 0
SRC_FOX = SRC_RET + 4 * MIX_W
SRC_FF = SRC_FOX + 3 * MIX_W
SRC_CQ = SRC_FF + N_HEADS
SRC_DK = SRC_CQ + DSA_Q_RANK
SRC_IK = SRC_DK + 2 * HEAD_DIM
SRC_IW = SRC_IK + IDX_DIM
SRC_Z = SRC_IW + IDX_HEADS
SRC_DT = SRC_Z + 2 * SSD_INNER + 2 * SSD_GROUPS * SSD_STATE
SRC_GATE = SRC_DT + SSD_HEADS
MAIN_RUNS = ((COL_GATE, SRC_GATE), (COL_RET, SRC_RET), (COL_Z, SRC_Z), (COL_CQ, SRC_CQ),
             (COL_FOX, SRC_FOX), (COL_DK, SRC_DK))
SMALL_PIECES = ((SRC_DT, SM_DT, SSD_HEADS), (SRC_FF, SM_F, N_HEADS), (SRC_IW, SM_IW, IDX_HEADS),
                (SRC_IK, SM_IK, IDX_DIM))


def _layout_in_proj_t(w_in):
    w_t = jnp.swapaxes(w_in, 1, 2)
    depth, _, d = w_t.shape
    ends = [dst for dst, _ in MAIN_RUNS[1:]] + [N_MAIN_USED]
    main = [w_t[:, src:src + (end - dst), :] for (dst, src), end in zip(MAIN_RUNS, ends)]
    main.append(jnp.zeros((depth, N_MAIN - N_MAIN_USED, d), w_t.dtype))
    small, lane = [], 0
    for src, dst, width in SMALL_PIECES:
        if dst > lane:
            small.append(jnp.zeros((depth, dst - lane, d), w_t.dtype))
        small.append(w_t[:, src:src + width, :])
        lane = dst + width
    return jnp.concatenate(main, axis=1).astype(BF16), jnp.concatenate(small, axis=1).astype(BF16)


def _pad_to(v, offset, total):
    return jnp.zeros((total,), v.dtype).at[offset:offset + v.shape[0]].set(v)


def _rotary_tables(seq):
    half = HEAD_DIM // 2
    inv = 1.0 / (10000.0 ** (jnp.arange(half, dtype=F32) / half))
    ang = jnp.arange(seq, dtype=F32)[:, None] * inv[None, :]
    cos, sin = jnp.cos(ang), jnp.sin(ang)
    return jnp.concatenate([cos, cos], axis=1), jnp.concatenate([-sin, sin], axis=1)


def kernel(x, norm1_g, w_in, gate_b, fox_f_b, fox_qn_g, fox_kn_g, dsa_cq_g, dsa_w_uq, dsa_w_qidx, dsa_qn_g,
           dsa_kn_g, rel_bias, ssd_conv_w, ssd_conv_b, ssd_dt_bias, ssd_a_log, ssd_d, ssd_norm_g, w_br, w_out,
           norm2_g, w_ff1, w_ff2):
    batch, seq, d = x.shape
    tokens = batch * seq
    xt = x.reshape(tokens, d)
    cos, sin = _rotary_tables(seq)
    tm = min(1024, tokens)
    w_main, w_small = _layout_in_proj_t(w_in)
    w_br_bf, w_out_bf = w_br.astype(BF16), w_out.astype(BF16)
    w_ff1_bf, w_ff2_bf = w_ff1.astype(BF16), w_ff2.astype(BF16)
    for l in range(DEPTH):
        g1 = norm1_g[l][None, :]
        p, sm = norm_matmul(xt, g1, w_main, w_small, l, tm, MAIN_TN)

        o_ret = retention(p, cos, sin, batch, seq)

        o_fox = fox_attention(p, sm, _pad_to(fox_f_b[l], SM_F, SM_W)[None, :], fox_qn_g[l][None, :],
                              fox_kn_g[l][None, :], batch, seq)

        o_dsa = dsa_attention(p, sm, dsa_cq_g[l][None, :], dsa_w_uq[l].T.astype(BF16), dsa_w_qidx[l].T.astype(BF16),
                              dsa_qn_g[l][:, None], dsa_kn_g[l][None, :], rel_bias, batch, seq)

        o_ssd = ssd_mixer(p, sm, ssd_conv_w[l], ssd_conv_b[l][None, :],
                          _pad_to(ssd_dt_bias[l], SM_DT, SM_W)[:, None], _pad_to(ssd_a_log[l], SM_DT, SM_W)[:, None],
                          jnp.repeat(ssd_d[l], SSD_HEAD_DIM)[None, :], ssd_norm_g[l][None, :], batch, seq)

        xt = merge_project(xt, p, gate_b[l][None, :], o_ret, o_fox, o_dsa, o_ssd, w_br_bf, w_out_bf, l)
        xt = ffn(xt, norm2_g[l][None, :], w_ff1_bf, w_ff2_bf, l)
    return xt.reshape(batch, seq, d)
```
